```python
import math
import jax, jax.numpy as jnp
from jax import lax
import numpy as np


D_MODEL = 1024
BATCH = 4
SEQ = 8192
DEPTH = 2

GRID_W = 64
CTX_LEN = 256
HEAD_DIM = 64
GROUP_HEADS = 4
GROUP_WIDTH = GROUP_HEADS * HEAD_DIM
N_MIXERS = 4
MIX_WIDTH = N_MIXERS * GROUP_WIDTH
SWA_KV_HEADS = 2
SWA_WINDOW = 128
SWA_BLOCK = 128
MLA_Q_RANK = 256
MLA_KV_RANK = 128
MLA_NOPE = 64
MLA_ROPE = 32
MLA_V = 64
DIFF_QK = 32
DIFF_V = 64
NA_KH = 8
NA_KW = 16
N_GROUPS = 4
EXPERTS_PER_GROUP = 8
N_EXPERTS = N_GROUPS * EXPERTS_PER_GROUP
EXPERT_HIDDEN = 256
TOP_K_IN_GROUP = 2
Q_BLOCK = 128
ROPE_BASE = 10000.0
NORM_EPS = 1e-5
NEG_INF = -1e30
DN_ALPHA = (2 * DEPTH) ** 0.25
DN_BETA = (8 * DEPTH) ** -0.25
SWA_SCALE = HEAD_DIM ** -0.5
MLA_SCALE = (MLA_NOPE + MLA_ROPE) ** -0.5
DIFF_SCALE = DIFF_QK ** -0.5
NA_SCALE = HEAD_DIM ** -0.5
IN_SPLITS = (GROUP_WIDTH, SWA_KV_HEADS * HEAD_DIM, SWA_KV_HEADS * HEAD_DIM,
             MLA_Q_RANK, MLA_KV_RANK, MLA_ROPE,
             GROUP_HEADS * 2 * DIFF_QK, GROUP_HEADS * 2 * DIFF_QK, GROUP_HEADS * DIFF_V,
             GROUP_WIDTH, GROUP_WIDTH, GROUP_WIDTH)
D_IN = sum(IN_SPLITS)
IN_CUTS = tuple(int(v) for v in np.cumsum(IN_SPLITS)[:-1])

kernel_name = 'hymba_style_hybrid_dit_block'


def layer_norm(x, g, b):
    xf = x.astype(jnp.float32)
    mu = jnp.mean(xf, -1, keepdims=True)
    var = jnp.mean(jnp.square(xf - mu), -1, keepdims=True)
    return ((xf - mu) * lax.rsqrt(var + NORM_EPS)).astype(x.dtype) * g + b


def rms_norm(x, g):
    xf = x.astype(jnp.float32)
    return (xf * lax.rsqrt(jnp.mean(xf * xf, -1, keepdims=True) + NORM_EPS)).astype(x.dtype) * g


def rope_1d(x, pos):
    half = x.shape[-1] // 2
    inv = ROPE_BASE ** (-jnp.arange(half, dtype=jnp.float32) / half)
    ang = pos.astype(jnp.float32)[:, None] * inv[None, :]
    bshape = (ang.shape[0],) + (1,) * (x.ndim - 3) + (half,)
    cos = jnp.cos(ang).reshape(bshape).astype(x.dtype)
    sin = jnp.sin(ang).reshape(bshape).astype(x.dtype)
    x1, x2 = x[..., :half], x[..., half:]
    return jnp.concatenate([x1 * cos - x2 * sin, x1 * sin + x2 * cos], -1)


def rope_2d(x, rows, cols):
    r = x.shape[-1] // 2
    return jnp.concatenate([rope_1d(x[..., :r], rows), rope_1d(x[..., r:], cols)], -1)


def attend(q, k, v, scale):
    s = jnp.einsum('bqhd,bkhd->bhqk', q, k).astype(jnp.float32) * scale
    p = jax.nn.softmax(s, axis=-1).astype(v.dtype)
    return jnp.einsum('bhqk,bkhd->bqhd', p, v)


def diff_attend(q1, q2, k1, k2, v, lam, scale):
    a1 = jax.nn.softmax(jnp.einsum('bqhd,bkhd->bhqk', q1, k1).astype(jnp.float32) * scale, axis=-1)
    a2 = jax.nn.softmax(jnp.einsum('bqhd,bkhd->bhqk', q2, k2).astype(jnp.float32) * scale, axis=-1)
    p = (a1 - lam * a2).astype(v.dtype)
    return jnp.einsum('bhqk,bkhd->bqhd', p, v)


def sweep_query_blocks(fn, qs):
    B, L = qs[0].shape[:2]
    nb = L // Q_BLOCK
    blocks = tuple(jnp.moveaxis(q.reshape((B, nb, Q_BLOCK) + q.shape[2:]), 1, 0) for q in qs)
    out = lax.map(lambda a: fn(*a), blocks)
    out = jnp.moveaxis(out, 0, 1)
    return out.reshape((B, L) + out.shape[3:])


def swa_sink_latent(q, k, v, kc, vc, sink):
    B, L, HQ, d = q.shape
    HKV = k.shape[2]
    G = HQ // HKV
    C = kc.shape[1]
    nb = L // SWA_BLOCK
    span = 3 * SWA_BLOCK

    def band(z):
        zp = jnp.pad(z, ((0, 0), (SWA_BLOCK, SWA_BLOCK), (0, 0), (0, 0)))
        zp = zp.reshape(B, nb + 2, SWA_BLOCK, HKV, z.shape[-1])
        return jnp.concatenate([zp[:, :-2], zp[:, 1:-1], zp[:, 2:]], axis=2)

    kb, vb = band(k), band(v)
    qb = q.reshape(B, nb, SWA_BLOCK, HKV, G, d)
    s_band = jnp.einsum('bnqhgd,bnkhd->bnhgqk', qb, kb).astype(jnp.float32) * SWA_SCALE
    s_ctx = jnp.einsum('bnqhgd,bchd->bnhgqc', qb, kc).astype(jnp.float32) * SWA_SCALE
    rel_q = jnp.arange(SWA_BLOCK)[:, None]
    rel_k = jnp.arange(span)[None, :] - SWA_BLOCK
    in_window = jnp.abs(rel_k - rel_q) <= SWA_WINDOW
    k_abs = (jnp.arange(nb) * SWA_BLOCK)[:, None] + rel_k
    in_range = (k_abs >= 0) & (k_abs < L)
    mask = in_window[None] & in_range[:, None, :]
    s_band = jnp.where(mask[None, :, None, None], s_band, NEG_INF)
    sink_logit = jnp.broadcast_to(sink.reshape(HKV, G)[None, None, :, :, None, None].astype(jnp.float32),
                                  s_band.shape[:-1] + (1,))
    p = jax.nn.softmax(jnp.concatenate([s_band, s_ctx, sink_logit], -1), axis=-1)
    p_band = p[..., :span].astype(v.dtype)
    p_ctx = p[..., span:span + C].astype(v.dtype)
    out = (jnp.einsum('bnhgqk,bnkhd->bnqhgd', p_band, vb)
           + jnp.einsum('bnhgqc,bchd->bnqhgd', p_ctx, vc))
    return out.reshape(B, L, HQ * d)


def swa_sink_context(qc, kc, vc, sink):
    B, C, HQ, d = qc.shape
    HKV = kc.shape[2]
    G = HQ // HKV
    qg = qc.reshape(B, C, HKV, G, d)
    s = jnp.einsum('bqhgd,bkhd->bhgqk', qg, kc).astype(jnp.float32) * SWA_SCALE
    sink_logit = jnp.broadcast_to(sink.reshape(HKV, G)[None, :, :, None, None].astype(jnp.float32),
                                  s.shape[:-1] + (1,))
    p = jax.nn.softmax(jnp.concatenate([s, sink_logit], -1), axis=-1)[..., :C].astype(vc.dtype)
    return jnp.einsum('bhgqk,bkhd->bqhgd', p, vc).reshape(B, C, HQ * d)


def neighborhood_latent(q, k, v, kc, vc, rpb):
    B, L, H, d = q.shape
    rows = L // GRID_W
    kh = min(NA_KH, rows)
    qg = q.reshape(B, rows, GRID_W, H, d)
    kg = k.reshape(B, rows, GRID_W, H, d)
    vg = v.reshape(B, rows, GRID_W, H, d)
    col = jnp.arange(GRID_W)
    col_start = jnp.clip(col - NA_KW // 2, 0, GRID_W - NA_KW)
    col_idx = col_start[:, None] + jnp.arange(NA_KW)[None, :]
    col_off = col_idx - col[:, None] + (NA_KW - 1)
    n_win = kh * NA_KW

    def one_row(args):
        r, q_r = args
        r_start = jnp.clip(r - kh // 2, 0, rows - kh)
        k_rows = lax.dynamic_slice_in_dim(kg, r_start, kh, axis=1)
        v_rows = lax.dynamic_slice_in_dim(vg, r_start, kh, axis=1)
        k_win = k_rows[:, :, col_idx]
        v_win = v_rows[:, :, col_idx]
        row_off = r_start + jnp.arange(kh) - r + (NA_KH - 1)
        bias = rpb[:, row_off[None, :, None], col_off[:, None, :]]
        s_win = (jnp.einsum('bqhd,bkqwhd->bhqkw', q_r, k_win).astype(jnp.float32) * NA_SCALE
                 + bias[None].astype(jnp.float32))
        s_ctx = jnp.einsum('bqhd,bchd->bhqc', q_r, kc).astype(jnp.float32) * NA_SCALE
        p = jax.nn.softmax(jnp.concatenate([s_win.reshape(B, H, GRID_W, n_win), s_ctx], -1), axis=-1)
        p_win = p[..., :n_win].reshape(B, H, GRID_W, kh, NA_KW).astype(v.dtype)
        p_ctx = p[..., n_win:].astype(v.dtype)
        return (jnp.einsum('bhqkw,bkqwhd->bqhd', p_win, v_win)
                + jnp.einsum('bhqc,bchd->bqhd', p_ctx, vc))

    out = lax.map(one_row, (jnp.arange(rows, dtype=jnp.int32), jnp.moveaxis(qg, 1, 0)))
    return jnp.moveaxis(out, 0, 1).reshape(B, L, H * d)


def mixing_sublayer(h, hc, w_in, sink, q_norm_g, w_uq, kv_norm_g, w_ukv, lam, lam_init, subln_g,
                    rpb, w_out, need_ctx):
    B, L, _ = h.shape
    C = hc.shape[1]
    t = jnp.arange(L, dtype=jnp.int32)
    rows_pos = t // GRID_W
    cols_pos = t % GRID_W

    def heads(z):
        Bz, N = z.shape[:2]
        p = jnp.split(z @ w_in, IN_CUTS, axis=-1)
        qa = p[0].reshape(Bz, N, GROUP_HEADS, HEAD_DIM)
        ka = p[1].reshape(Bz, N, SWA_KV_HEADS, HEAD_DIM)
        va = p[2].reshape(Bz, N, SWA_KV_HEADS, HEAD_DIM)
        mq = (rms_norm(p[3], q_norm_g) @ w_uq).reshape(Bz, N, GROUP_HEADS, MLA_NOPE + MLA_ROPE)
        mkv = (rms_norm(p[4], kv_norm_g) @ w_ukv).reshape(Bz, N, GROUP_HEADS, MLA_NOPE + MLA_V)
        mkr = p[5].reshape(Bz, N, 1, MLA_ROPE)
        dq = p[6].reshape(Bz, N, GROUP_HEADS, 2, DIFF_QK)
        dk = p[7].reshape(Bz, N, GROUP_HEADS, 2, DIFF_QK)
        dv = p[8].reshape(Bz, N, GROUP_HEADS, DIFF_V)
        nq = p[9].reshape(Bz, N, GROUP_HEADS, HEAD_DIM)
        nk = p[10].reshape(Bz, N, GROUP_HEADS, HEAD_DIM)
        nv = p[11].reshape(Bz, N, GROUP_HEADS, HEAD_DIM)
        return qa, ka, va, mq, mkv, mkr, dq, dk, dv, nq, nk, nv

    def rot(z):
        return rope_2d(z, rows_pos, cols_pos)

    qa, ka, va, mq, mkv, mkr, dq, dk, dv, nq, nk, nv = heads(h)
    qa, ka = rot(qa), rot(ka)
    mq = jnp.concatenate([mq[..., :MLA_NOPE], rot(mq[..., MLA_NOPE:])], -1)
    mk = jnp.concatenate([mkv[..., :MLA_NOPE],
                          jnp.broadcast_to(rot(mkr), (B, L, GROUP_HEADS, MLA_ROPE))], -1)
    mv = mkv[..., MLA_NOPE:]
    dq, dk = rot(dq), rot(dk)

    qa_c, ka_c, va_c, mq_c, mkv_c, mkr_c, dq_c, dk_c, dv_c, nq_c, nk_c, nv_c = heads(hc)
    mk_c = jnp.concatenate([mkv_c[..., :MLA_NOPE],
                            jnp.broadcast_to(mkr_c, (B, C, GROUP_HEADS, MLA_ROPE))], -1)
    mv_c = mkv_c[..., MLA_NOPE:]

    out_a = swa_sink_latent(qa, ka, va, ka_c, va_c, sink)
    mk_all = jnp.concatenate([mk_c, mk], 1)
    mv_all = jnp.concatenate([mv_c, mv], 1)
    out_b = sweep_query_blocks(lambda qb: attend(qb, mk_all, mv_all, MLA_SCALE), (mq,)).reshape(B, L, -1)
    k1_all = jnp.concatenate([dk_c[..., 0, :], dk[..., 0, :]], 1)
    k2_all = jnp.concatenate([dk_c[..., 1, :], dk[..., 1, :]], 1)
    dv_all = jnp.concatenate([dv_c, dv], 1)
    out_c = sweep_query_blocks(
        lambda q1, q2: diff_attend(q1, q2, k1_all, k2_all, dv_all, lam, DIFF_SCALE),
        (dq[..., 0, :], dq[..., 1, :]))
    out_c = (rms_norm(out_c, subln_g) * (1.0 - lam_init)).reshape(B, L, -1)
    out_d = neighborhood_latent(nq, nk, nv, nk_c, nv_c, rpb)
    y = jnp.concatenate([out_a, out_b, out_c, out_d], -1) @ w_out
    if not need_ctx:
        return y, None

    yc_a = swa_sink_context(qa_c, ka_c, va_c, sink)
    yc_b = attend(mq_c, mk_c, mv_c, MLA_SCALE).reshape(B, C, -1)
    yc_c = diff_attend(dq_c[..., 0, :], dq_c[..., 1, :], dk_c[..., 0, :], dk_c[..., 1, :], dv_c, lam, DIFF_SCALE)
    yc_c = (rms_norm(yc_c, subln_g) * (1.0 - lam_init)).reshape(B, C, -1)
    yc_d = attend(nq_c, nk_c, nv_c, NA_SCALE).reshape(B, C, -1)
    yc = jnp.concatenate([yc_a, yc_b, yc_c, yc_d], -1) @ w_out
    return y, yc


def hier_moe(h, w_grp, b_grp, w_rt, b_rt, w_gate, w_up, w_down):
    T = h.shape[0]
    p_grp = jax.nn.softmax((h @ w_grp + b_grp).astype(jnp.float32), axis=-1)
    g_val, g_idx = lax.top_k(p_grp, 1)
    e_logits = (h @ w_rt + b_rt).astype(jnp.float32).reshape(T, N_GROUPS, EXPERTS_PER_GROUP)
    e_sel = jnp.take_along_axis(e_logits, g_idx[:, :, None], axis=1)[:, 0]
    p_exp = jax.nn.softmax(e_sel, axis=-1)
    e_val, e_idx = lax.top_k(p_exp, TOP_K_IN_GROUP)
    e_val = e_val / jnp.sum(e_val, -1, keepdims=True)
    w_exp = jnp.sum(jax.nn.one_hot(e_idx, EXPERTS_PER_GROUP, dtype=jnp.float32) * e_val[..., None], axis=1)
    gates = (jax.nn.one_hot(g_idx[:, 0], N_GROUPS, dtype=jnp.float32)[:, :, None]
             * (g_val * w_exp)[:, None, :]).astype(h.dtype)
    y = jnp.zeros_like(h)
    for g in range(N_GROUPS):
        hid = (jax.nn.silu(jnp.einsum('td,edf->tef', h, w_gate[g]))
               * jnp.einsum('td,edf->tef', h, w_up[g]))
        y = y + jnp.einsum('tef,efd->td', hid * gates[:, g, :, None], w_down[g])
    return y


def setup_inputs(seed: int = 0) -> dict:
    key = jax.random.key(seed)
    ks = jax.random.split(key, 30)

    def nrm(i, shape, scale):
        return jax.random.normal(ks[i], shape, jnp.float32) * scale

    D = D_MODEL
    return {
        'x': nrm(0, (BATCH, SEQ, D), 1.0),
        'c': nrm(1, (BATCH, D), 1.0),
        'ctx': nrm(2, (BATCH, CTX_LEN, D), 1.0),
        'c_ctx': nrm(3, (D,), 1.0),
        'w_mod': nrm(4, (DEPTH, D, 6 * D), 0.5 * D ** -0.5),
        'b_mod': nrm(5, (DEPTH, 6 * D), 0.02),
        'w_in': nrm(6, (DEPTH, D, D_IN), D ** -0.5),
        'attn_sink': nrm(7, (DEPTH, GROUP_HEADS), 0.5),
        'mla_q_norm': 1.0 + nrm(8, (DEPTH, MLA_Q_RANK), 0.02),
        'w_uq': nrm(9, (DEPTH, MLA_Q_RANK, GROUP_HEADS * (MLA_NOPE + MLA_ROPE)), MLA_Q_RANK ** -0.5),
        'mla_kv_norm': 1.0 + nrm(10, (DEPTH, MLA_KV_RANK), 0.02),
        'w_ukv': nrm(11, (DEPTH, MLA_KV_RANK, GROUP_HEADS * (MLA_NOPE + MLA_V)), MLA_KV_RANK ** -0.5),
        'lam_q1': nrm(12, (DEPTH, DIFF_QK), 0.1),
        'lam_k1': nrm(13, (DEPTH, DIFF_QK), 0.1),
        'lam_q2': nrm(14, (DEPTH, DIFF_QK), 0.1),
        'lam_k2': nrm(15, (DEPTH, DIFF_QK), 0.1),
        'diff_subln': 1.0 + nrm(16, (DEPTH, DIFF_V), 0.02),
        'na_rpb': nrm(17, (DEPTH, GROUP_HEADS, 2 * NA_KH - 1, 2 * NA_KW - 1), 0.02),
        'w_out': nrm(18, (DEPTH, MIX_WIDTH, D), MIX_WIDTH ** -0.5 * DN_BETA),
        'ln1_g': 1.0 + nrm(19, (DEPTH, D), 0.02),
        'ln1_b': nrm(20, (DEPTH, D), 0.02),
        'w_group': nrm(21, (DEPTH, D, N_GROUPS), D ** -0.5),
        'b_group': nrm(22, (DEPTH, N_GROUPS), 0.01),
        'w_router': nrm(23, (DEPTH, D, N_EXPERTS), D ** -0.5),
        'b_router': nrm(24, (DEPTH, N_EXPERTS), 0.01),
        'w_gate': nrm(25, (DEPTH, N_GROUPS, EXPERTS_PER_GROUP, D, EXPERT_HIDDEN), D ** -0.5),
        'w_up': nrm(26, (DEPTH, N_GROUPS, EXPERTS_PER_GROUP, D, EXPERT_HIDDEN), D ** -0.5),
        'w_down': nrm(27, (DEPTH, N_GROUPS, EXPERTS_PER_GROUP, EXPERT_HIDDEN, D), EXPERT_HIDDEN ** -0.5 * DN_BETA),
        'ln2_g': 1.0 + nrm(28, (DEPTH, D), 0.02),
        'ln2_b': nrm(29, (DEPTH, D), 0.02),
    }


def reference(x, c, ctx, c_ctx, w_mod, b_mod, w_in, attn_sink, mla_q_norm, w_uq, mla_kv_norm, w_ukv,
              lam_q1, lam_k1, lam_q2, lam_k2, diff_subln, na_rpb, w_out, ln1_g, ln1_b,
              w_group, b_group, w_router, b_router, w_gate, w_up, w_down, ln2_g, ln2_b):
    B, L, D = x.shape
    C = ctx.shape[1]
    xc = ctx
    c_act = jax.nn.silu(c)
    cc_act = jax.nn.silu(c_ctx)
    for l in range(DEPTH):
        need_ctx = l < DEPTH - 1
        mod = c_act @ w_mod[l] + b_mod[l]
        modc = cc_act @ w_mod[l] + b_mod[l]
        sh1, sc1, g1, sh2, sc2, g2 = jnp.split(mod[:, None, :], 6, axis=-1)
        sh1c, sc1c, g1c, sh2c, sc2c, g2c = jnp.split(modc, 6, axis=-1)
        lam_init = 0.8 - 0.6 * math.exp(-0.3 * l)
        lam = (jnp.exp(jnp.sum(lam_q1[l] * lam_k1[l]).astype(jnp.float32))
               - jnp.exp(jnp.sum(lam_q2[l] * lam_k2[l]).astype(jnp.float32)) + lam_init)

        h = x * (1.0 + sc1) + sh1
        hc = xc * (1.0 + sc1c) + sh1c
        y, yc = mixing_sublayer(h, hc, w_in[l], attn_sink[l], mla_q_norm[l], w_uq[l], mla_kv_norm[l],
                                w_ukv[l], lam, lam_init, diff_subln[l], na_rpb[l], w_out[l], need_ctx)
        x = layer_norm(DN_ALPHA * x + g1 * y, ln1_g[l], ln1_b[l])
        h = x * (1.0 + sc2) + sh2

        if need_ctx:
            xc = layer_norm(DN_ALPHA * xc + g1c * yc, ln1_g[l], ln1_b[l])
            hc = xc * (1.0 + sc2c) + sh2c
            tokens = jnp.concatenate([h.reshape(-1, D), hc.reshape(-1, D)], 0)
            f = hier_moe(tokens, w_group[l], b_group[l], w_router[l], b_router[l],
                         w_gate[l], w_up[l], w_down[l])
            f_lat = f[:B * L].reshape(B, L, D)
            f_ctx = f[B * L:].reshape(B, C, D)
            xc = layer_norm(DN_ALPHA * xc + g2c * f_ctx, ln2_g[l], ln2_b[l])
        else:
            f_lat = hier_moe(h.reshape(-1, D), w_group[l], b_group[l], w_router[l], b_router[l],
                             w_gate[l], w_up[l], w_down[l]).reshape(B, L, D)
        x = layer_norm(DN_ALPHA * x + g2 * f_lat, ln2_g[l], ln2_b[l])
    return x
```

```python
import functools
import math

import jax
import jax.numpy as jnp
import numpy as np
from jax import lax
from jax.experimental import pallas as pl
from jax.experimental.pallas import tpu as pltpu

F32 = jnp.float32
BF16 = jnp.bfloat16

D_MODEL = 1024
GRID_W = 64
HEAD_DIM = 64
N_HEADS = 4
GROUP_WIDTH = N_HEADS * HEAD_DIM
SWA_KV_HEADS = 2
SWA_WINDOW = 128
SWA_BLOCK = 128
MLA_Q_RANK = 256
MLA_KV_RANK = 128
MLA_NOPE = 64
MLA_ROPE = 32
MLA_V = 64
MLA_QK_PAD = 128
DIFF_QK = 32
DIFF_V = 64
NA_KH = 8
NA_KW = 16
NA_Q_ROWS = 2
NA_K_ROWS = 10
N_GROUPS = 4
EXPERTS_PER_GROUP = 8
N_EXPERTS = N_GROUPS * EXPERTS_PER_GROUP
EXPERT_HIDDEN = 256
ROUTER_LANES = 128
ROPE_BASE = 10000.0
NORM_EPS = 1e-5
NEG_INF = -1e30
SWA_SCALE = HEAD_DIM ** -0.5
MLA_SCALE = (MLA_NOPE + MLA_ROPE) ** -0.5
DIFF_SCALE = DIFF_QK ** -0.5
NA_SCALE = HEAD_DIM ** -0.5
IN_SPLITS = (GROUP_WIDTH, SWA_KV_HEADS * HEAD_DIM, SWA_KV_HEADS * HEAD_DIM,
             MLA_Q_RANK, MLA_KV_RANK, MLA_ROPE,
             N_HEADS * 2 * DIFF_QK, N_HEADS * 2 * DIFF_QK, N_HEADS * DIFF_V,
             GROUP_WIDTH, GROUP_WIDTH, GROUP_WIDTH)
IN_CUTS = tuple(int(v) for v in np.cumsum(IN_SPLITS)[:-1])

_SEG_LAYOUT = (("qa", 256), ("qa_r", 256), ("ka", 128), ("ka_r", 128), ("va", 128),
               ("mqr", 256), ("mkvr", 128), ("mkr", 128), ("mkr_r", 128),
               ("dq", 256), ("dq_r", 256), ("dk", 256), ("dk_r", 256), ("dv", 256),
               ("nq", 256), ("nk", 256), ("nv", 256))
_SEG = {}
_off = 0
for _name, _w in _SEG_LAYOUT:
    _SEG[_name] = (_off, _off + _w)
    _off += _w
W_ALL_COLS = _off

VMEM_LIMIT_BYTES = 56 * 1024 * 1024


def _cparams(sem):
    return pltpu.CompilerParams(dimension_semantics=sem, vmem_limit_bytes=VMEM_LIMIT_BYTES)


def _dot(a, b):
    return jnp.dot(a, b, preferred_element_type=F32)


def _dot_nt(a, b):
    return lax.dot_general(a, b, (((1,), (1,)), ((), ())), preferred_element_type=F32)


def _rms(x):
    return x * lax.rsqrt(jnp.mean(x * x, axis=-1, keepdims=True) + NORM_EPS)


def _layer_norm(z, g, b):
    mu = jnp.mean(z, axis=-1, keepdims=True)
    zc = z - mu
    var = jnp.mean(zc * zc, axis=-1, keepdims=True)
    return zc * lax.rsqrt(var + NORM_EPS) * g + b


def _mod_kernel(c_ref, w_ref, b_ref, o_ref):
    c = c_ref[...]
    act = c * jax.nn.sigmoid(c)
    o_ref[0] = jnp.dot(act, w_ref[0], preferred_element_type=F32,
                       precision=lax.Precision.HIGHEST) + b_ref[0]


def _modulation(cvec, w_mod, b_mod):
    depth, d, n = w_mod.shape
    tn = 1024
    return pl.pallas_call(
        _mod_kernel,
        grid=(depth, n // tn),
        in_specs=[pl.BlockSpec((8, d), lambda l, j: (0, 0)),
                  pl.BlockSpec((1, d, tn), lambda l, j: (l, 0, j)),
                  pl.BlockSpec((1, 1, tn), lambda l, j: (l, 0, j))],
        out_specs=pl.BlockSpec((1, 8, tn), lambda l, j: (l, 0, j)),
        out_shape=jax.ShapeDtypeStruct((depth, 8, n), F32),
        compiler_params=_cparams(("arbitrary", "arbitrary")),
        name="modulation",
    )(cvec, w_mod, b_mod.reshape(depth, 1, n))


def _inproj_kernel(x_ref, sc_ref, sh_ref, w_ref, cos64_ref, sin64_ref, cos32_ref, sin32_ref,
                   cosm_ref, sinm_ref, qng_ref, wuq_ref, kvng_ref, wukv_ref,
                   qa_o, ka_o, va_o, mq_o, mk_o, mv_o, dq_o, dk_o, dv_o, nq_o, nk_o, nv_o):
    h = (x_ref[0] * (1.0 + sc_ref[0]) + sh_ref[0]).astype(BF16)

    def seg(name):
        a, b = _SEG[name]
        return _dot(h, w_ref[:, a:b])

    def split_heads(val, out_ref, n, width):
        for i in range(n):
            out_ref[0, i] = val[:, i * width:(i + 1) * width].astype(out_ref.dtype)

    cos64 = cos64_ref[...]
    sin64 = sin64_ref[...]
    cos32 = cos32_ref[...]
    sin32 = sin32_ref[...]
    cosm = cosm_ref[...]
    sinm = sinm_ref[...]

    qa = (seg("qa") * cos64 + seg("qa_r") * sin64) * SWA_SCALE
    split_heads(qa, qa_o, N_HEADS, HEAD_DIM)
    ka = seg("ka") * cos64[:, :128] + seg("ka_r") * sin64[:, :128]
    split_heads(ka, ka_o, SWA_KV_HEADS, HEAD_DIM)
    split_heads(seg("va"), va_o, SWA_KV_HEADS, HEAD_DIM)

    qn = (_rms(seg("mqr")) * qng_ref[...]).astype(BF16)
    uq = _dot(qn, wuq_ref[...])
    half = N_HEADS * MLA_QK_PAD
    for i in range(N_HEADS):
        a = i * MLA_QK_PAD
        mq = (uq[:, a:a + MLA_QK_PAD] * cosm + uq[:, half + a:half + a + MLA_QK_PAD] * sinm) * MLA_SCALE
        mq_o[0, i] = mq.astype(BF16)
    kvn = (_rms(seg("mkvr")) * kvng_ref[...]).astype(BF16)
    ukv = _dot(kvn, wukv_ref[...])
    k_rope = seg("mkr") * cosm + seg("mkr_r") * sinm
    for i in range(N_HEADS):
        a = i * MLA_QK_PAD
        mk_o[0, i] = (ukv[:, a:a + MLA_QK_PAD] + k_rope).astype(BF16)
        b = half + i * MLA_V
        mv_o[0, i] = ukv[:, b:b + MLA_V].astype(BF16)

    dq = (seg("dq") * cos32 + seg("dq_r") * sin32) * DIFF_SCALE
    split_heads(dq, dq_o, 2 * N_HEADS, DIFF_QK)
    dk = seg("dk") * cos32 + seg("dk_r") * sin32
    split_heads(dk, dk_o, 2 * N_HEADS, DIFF_QK)
    split_heads(seg("dv"), dv_o, N_HEADS, DIFF_V)

    split_heads(seg("nq") * NA_SCALE, nq_o, N_HEADS, HEAD_DIM)
    split_heads(seg("nk"), nk_o, N_HEADS, HEAD_DIM)
    split_heads(seg("nv"), nv_o, N_HEADS, HEAD_DIM)


_HEAD_OUTS = (("qa", N_HEADS, HEAD_DIM), ("ka", SWA_KV_HEADS, HEAD_DIM), ("va", SWA_KV_HEADS, HEAD_DIM),
              ("mq", N_HEADS, MLA_QK_PAD), ("mk", N_HEADS, MLA_QK_PAD), ("mv", N_HEADS, MLA_V),
              ("dq", 2 * N_HEADS, DIFF_QK), ("dk", 2 * N_HEADS, DIFF_QK), ("dv", N_HEADS, DIFF_V),
              ("nq", N_HEADS, HEAD_DIM), ("nk", N_HEADS, HEAD_DIM), ("nv", N_HEADS, HEAD_DIM))


def _input_projection(x, sc, sh, w_all, tables, qng, wuq, kvng, wukv):
    B, N, D = x.shape
    tm = min(512, N)
    tok = lambda i, b: (b, i, 0)
    vec = lambda i, b: (b, 0, 0)
    tab = lambda i, b: (i, 0)
    const = lambda i, b: (0, 0)
    in_specs = [pl.BlockSpec((1, tm, D), tok),
                pl.BlockSpec((1, 1, D), vec), pl.BlockSpec((1, 1, D), vec),
                pl.BlockSpec(w_all.shape, const)]
    in_specs += [pl.BlockSpec((tm, t.shape[1]), tab) for t in tables]
    in_specs += [pl.BlockSpec(a.shape, const) for a in (qng, wuq, kvng, wukv)]
    out_specs = [pl.BlockSpec((1, n, tm, w), lambda i, b: (b, 0, i, 0)) for _, n, w in _HEAD_OUTS]
    out_shape = [jax.ShapeDtypeStruct((B, n, N, w), BF16) for _, n, w in _HEAD_OUTS]
    outs = pl.pallas_call(
        _inproj_kernel,
        grid=(N // tm, B),
        in_specs=in_specs, out_specs=out_specs, out_shape=out_shape,
        compiler_params=_cparams(("arbitrary", "arbitrary")),
        name="input_projection",
    )(x, sc, sh, w_all, *tables, qng, wuq, kvng, wukv)
    return {name: o for (name, _, _), o in zip(_HEAD_OUTS, outs)}


def _softmax_parts(scores, extra_logit=None):
    m = functools.reduce(jnp.maximum, [jnp.max(s, axis=-1, keepdims=True) for s in scores])
    if extra_logit is not None:
        m = jnp.maximum(m, extra_logit)
    ps = [jnp.exp(s - m) for s in scores]
    denom = functools.reduce(jnp.add, [jnp.sum(p, axis=-1, keepdims=True) for p in ps])
    if extra_logit is not None:
        denom = denom + jnp.exp(extra_logit - m)
    return ps, denom


def _flash(q, ctx_k, ctx_v, k_at, v_at, n_chunks, dv):
    tq = q.shape[0]

    def update(carry, k, v):
        m, l, acc = carry
        s = _dot_nt(q, k)
        m_new = jnp.maximum(m, jnp.max(s, axis=-1, keepdims=True))
        alpha = jnp.exp(m - m_new)
        p = jnp.exp(s - m_new)
        l = alpha * l + jnp.sum(p, axis=-1, keepdims=True)
        acc = alpha * acc + _dot(p.astype(BF16), v)
        return m_new, l, acc

    carry = (jnp.full((tq, 1), NEG_INF, F32), jnp.zeros((tq, 1), F32), jnp.zeros((tq, dv), F32))
    carry = update(carry, ctx_k, ctx_v)
    carry = lax.fori_loop(0, n_chunks, lambda i, cr: update(cr, k_at(i), v_at(i)), carry)
    _, l, acc = carry
    return acc / l


def _lambda_value(lam_ref, lam_init):
    lv = lam_ref[...]
    return (jnp.exp(jnp.sum(lv[0:1] * lv[1:2], axis=-1, keepdims=True))
            - jnp.exp(jnp.sum(lv[2:3] * lv[3:4], axis=-1, keepdims=True)) + lam_init)


def _sub_ln(o, g_ref, lam_init):
    return _rms(o) * g_ref[...] * (1.0 - lam_init)


def _swa_kernel(sink_ref, q_ref, k_ref, v_ref, kc_ref, vc_ref, o_ref):
    L = k_ref.shape[2]
    span = 3 * SWA_BLOCK
    qb = pl.program_id(1)
    start = pl.multiple_of(jnp.clip(qb * SWA_BLOCK - SWA_BLOCK, 0, L - span), SWA_BLOCK)
    k_abs = start + lax.broadcasted_iota(jnp.int32, (SWA_BLOCK, span), 1)
    q_abs = qb * SWA_BLOCK + lax.broadcasted_iota(jnp.int32, (SWA_BLOCK, span), 0)
    in_window = jnp.abs(k_abs - q_abs) <= SWA_WINDOW
    group = N_HEADS // SWA_KV_HEADS
    for hk in range(SWA_KV_HEADS):
        kb = k_ref[0, hk, pl.ds(start, span), :]
        vb = v_ref[0, hk, pl.ds(start, span), :]
        kc = kc_ref[0, hk]
        vc = vc_ref[0, hk]
        for g in range(group):
            h = hk * group + g
            q = q_ref[0, h]
            s_band = jnp.where(in_window, _dot_nt(q, kb), NEG_INF)
            s_ctx = _dot_nt(q, kc)
            (p_band, p_ctx), denom = _softmax_parts([s_band, s_ctx], sink_ref[h])
            o = _dot(p_band.astype(BF16), vb) + _dot(p_ctx.astype(BF16), vc)
            o_ref[0, :, h * HEAD_DIM:(h + 1) * HEAD_DIM] = (o / denom).astype(o_ref.dtype)


def _swa_attention(sink, hd, hc):
    B, _, L, _ = hd["qa"].shape
    C = hc["ka"].shape[2]
    whole = lambda b, i: (b, 0, 0, 0)
    return pl.pallas_call(
        _swa_kernel,
        grid=(B, L // SWA_BLOCK),
        in_specs=[pl.BlockSpec(memory_space=pltpu.SMEM),
                  pl.BlockSpec((1, N_HEADS, SWA_BLOCK, HEAD_DIM), lambda b, i: (b, 0, i, 0)),
                  pl.BlockSpec((1, SWA_KV_HEADS, L, HEAD_DIM), whole),
                  pl.BlockSpec((1, SWA_KV_HEADS, L, HEAD_DIM), whole),
                  pl.BlockSpec((1, SWA_KV_HEADS, C, HEAD_DIM), whole),
                  pl.BlockSpec((1, SWA_KV_HEADS, C, HEAD_DIM), whole)],
        out_specs=pl.BlockSpec((1, SWA_BLOCK, GROUP_WIDTH), lambda b, i: (b, i, 0)),
        out_shape=jax.ShapeDtypeStruct((B, L, GROUP_WIDTH), BF16),
        compiler_params=_cparams(("arbitrary", "arbitrary")),
        name="swa_attention",
    )(sink, hd["qa"], hd["ka"], hd["va"], hc["ka"], hc["va"])


GLOBAL_TQ = 256
GLOBAL_TK = 512


def _chunk_at(ref, lead, tk):
    return lambda i: ref[lead + (pl.ds(pl.multiple_of(i * tk, tk), tk), slice(None))]


def _mla_kernel(q_ref, kc_ref, vc_ref, k_ref, v_ref, o_ref, *, tk):
    n_chunks = k_ref.shape[2] // tk
    o = _flash(q_ref[0, 0], kc_ref[0, 0], vc_ref[0, 0],
               _chunk_at(k_ref, (0, 0), tk), _chunk_at(v_ref, (0, 0), tk), n_chunks, MLA_V)
    o_ref[0, 0] = o.astype(o_ref.dtype)


def _mla_attention(hd, hc):
    B, H, L, dk = hd["mq"].shape
    C = hc["mk"].shape[2]
    tq = min(GLOBAL_TQ, L)
    tk = min(GLOBAL_TK, L)
    whole = lambda b, h, i: (b, h, 0, 0)
    return pl.pallas_call(
        functools.partial(_mla_kernel, tk=tk),
        grid=(B, H, L // tq),
        in_specs=[pl.BlockSpec((1, 1, tq, dk), lambda b, h, i: (b, h, i, 0)),
                  pl.BlockSpec((1, 1, C, dk), whole),
                  pl.BlockSpec((1, 1, C, MLA_V), whole),
                  pl.BlockSpec((1, 1, L, dk), whole),
                  pl.BlockSpec((1, 1, L, MLA_V), whole)],
        out_specs=pl.BlockSpec((1, 1, tq, MLA_V), lambda b, h, i: (b, h, i, 0)),
        out_shape=jax.ShapeDtypeStruct((B, H, L, MLA_V), BF16),
        compiler_params=_cparams(("arbitrary", "arbitrary", "arbitrary")),
        name="mla_attention",
    )(hd["mq"], hc["mk"], hc["mv"], hd["mk"], hd["mv"])


def _diff_kernel(lam_ref, g_ref, q_ref, kc_ref, vc_ref, k_ref, v_ref, o_ref, *, tk, lam_init):
    n_chunks = k_ref.shape[2] // tk
    v_at = _chunk_at(v_ref, (0, 0), tk)
    o1 = _flash(q_ref[0, 0], kc_ref[0, 0], vc_ref[0, 0], _chunk_at(k_ref, (0, 0), tk), v_at, n_chunks, DIFF_V)
    o2 = _flash(q_ref[0, 1], kc_ref[0, 1], vc_ref[0, 0], _chunk_at(k_ref, (0, 1), tk), v_at, n_chunks, DIFF_V)
    o = o1 - _lambda_value(lam_ref, lam_init) * o2
    o_ref[0, 0] = _sub_ln(o, g_ref, lam_init).astype(o_ref.dtype)


def _diff_attention(lam_vecs, subln_g, hd, hc, lam_init):
    B, _, L, dk = hd["dq"].shape
    C = hc["dk"].shape[2]
    tq = min(GLOBAL_TQ, L)
    tk = min(GLOBAL_TK, L)
    whole = lambda b, h, i: (b, h, 0, 0)
    const = lambda b, h, i: (0, 0)
    return pl.pallas_call(
        functools.partial(_diff_kernel, tk=tk, lam_init=lam_init),
        grid=(B, N_HEADS, L // tq),
        in_specs=[pl.BlockSpec(lam_vecs.shape, const),
                  pl.BlockSpec(subln_g.shape, const),
                  pl.BlockSpec((1, 2, tq, dk), lambda b, h, i: (b, h, i, 0)),
                  pl.BlockSpec((1, 2, C, dk), whole),
                  pl.BlockSpec((1, 1, C, DIFF_V), whole),
                  pl.BlockSpec((1, 2, L, dk), whole),
                  pl.BlockSpec((1, 1, L, DIFF_V), whole)],
        out_specs=pl.BlockSpec((1, 1, tq, DIFF_V), lambda b, h, i: (b, h, i, 0)),
        out_shape=jax.ShapeDtypeStruct((B, N_HEADS, L, DIFF_V), BF16),
        compiler_params=_cparams(("arbitrary", "arbitrary", "arbitrary")),
        name="diff_attention",
    )(lam_vecs, subln_g, hd["dq"], hc["dk"], hc["dv"], hd["dk"], hd["dv"])


def _na_plan(rows):
    kh = min(NA_KH, rows)
    q_len = NA_Q_ROWS * GRID_W
    k_len = NA_K_ROWS * GRID_W
    ql = np.arange(q_len)
    kl = np.arange(k_len)
    q_col = ql % GRID_W
    k_col = kl % GRID_W
    col_start = np.clip(q_col - NA_KW // 2, 0, GRID_W - NA_KW)
    col_ok = (k_col[None, :] >= col_start[:, None]) & (k_col[None, :] < col_start[:, None] + NA_KW)
    col_off = k_col[None, :] - q_col[:, None] + (NA_KW - 1)
    patterns, starts, ids = {}, [], []
    for blk in range(rows // NA_Q_ROWS):
        r0 = blk * NA_Q_ROWS
        ks = int(np.clip(r0 - kh // 2, 0, rows - NA_K_ROWS))
        q_row = r0 + ql // GRID_W
        k_row = ks + kl // GRID_W
        r_start = np.clip(q_row - kh // 2, 0, rows - kh)
        row_ok = (k_row[None, :] >= r_start[:, None]) & (k_row[None, :] < r_start[:, None] + kh)
        row_off = k_row[None, :] - q_row[:, None] + (NA_KH - 1)
        ok = row_ok & col_ok
        key = (ks - r0,) + tuple(int(v) for v in (r_start[::GRID_W] - r0))
        if key not in patterns:
            patterns[key] = (len(patterns), np.where(ok, row_off, 0), np.where(ok, col_off, 0), ok)
        starts.append(ks)
        ids.append(patterns[key][0])
    ordered = sorted(patterns.values(), key=lambda p: p[0])
    row_idx = np.stack([p[1] for p in ordered])
    col_idx = np.stack([p[2] for p in ordered])
    ok = np.stack([p[3] for p in ordered])
    return np.asarray(starts, np.int32), np.asarray(ids, np.int32), row_idx, col_idx, ok


def _na_kernel(ks_ref, pid_ref, q_ref, k_ref, v_ref, kc_ref, vc_ref, bias_ref, o_ref):
    del pid_ref
    k_len = NA_K_ROWS * GRID_W
    start = pl.multiple_of(ks_ref[pl.program_id(1)] * GRID_W, GRID_W)
    for h in range(N_HEADS):
        q = q_ref[0, h]
        kw = k_ref[0, h, pl.ds(start, k_len), :]
        vw = v_ref[0, h, pl.ds(start, k_len), :]
        s_win = _dot_nt(q, kw) + bias_ref[0, h]
        s_ctx = _dot_nt(q, kc_ref[0, h])
        (p_win, p_ctx), denom = _softmax_parts([s_win, s_ctx])
        o = _dot(p_win.astype(BF16), vw) + _dot(p_ctx.astype(BF16), vc_ref[0, h])
        o_ref[0, :, h * HEAD_DIM:(h + 1) * HEAD_DIM] = (o / denom).astype(o_ref.dtype)


def _na_attention(rpb, hd, hc):
    B, H, L, _ = hd["nq"].shape
    C = hc["nk"].shape[2]
    rows = L // GRID_W
    starts, ids, row_idx, col_idx, ok = _na_plan(rows)
    bias = jnp.where(ok[:, None], jnp.moveaxis(rpb[:, row_idx, col_idx], 0, 1), NEG_INF)
    q_len = NA_Q_ROWS * GRID_W
    k_len = NA_K_ROWS * GRID_W
    whole = lambda b, i, ks, pid: (b, 0, 0, 0)
    grid_spec = pltpu.PrefetchScalarGridSpec(
        num_scalar_prefetch=2,
        grid=(B, rows // NA_Q_ROWS),
        in_specs=[pl.BlockSpec((1, H, q_len, HEAD_DIM), lambda b, i, ks, pid: (b, 0, i, 0)),
                  pl.BlockSpec((1, H, L, HEAD_DIM), whole),
                  pl.BlockSpec((1, H, L, HEAD_DIM), whole),
                  pl.BlockSpec((1, H, C, HEAD_DIM), whole),
                  pl.BlockSpec((1, H, C, HEAD_DIM), whole),
                  pl.BlockSpec((1, H, q_len, k_len), lambda b, i, ks, pid: (pid[i], 0, 0, 0))],
        out_specs=pl.BlockSpec((1, q_len, GROUP_WIDTH), lambda b, i, ks, pid: (b, i, 0)))
    return pl.pallas_call(
        _na_kernel,
        grid_spec=grid_spec,
        out_shape=jax.ShapeDtypeStruct((B, L, GROUP_WIDTH), BF16),
        compiler_params=_cparams(("arbitrary", "arbitrary")),
        name="neighborhood_attention",
    )(jnp.asarray(starts), jnp.asarray(ids), hd["nq"], hd["nk"], hd["nv"], hc["nk"], hc["nv"], bias)


def _ctx_attn_kernel(sink_ref, lam_ref, g_ref, qa_ref, ka_ref, va_ref, mq_ref, mk_ref, mv_ref,
                     dq_ref, dk_ref, dv_ref, nq_ref, nk_ref, nv_ref,
                     ya_ref, yb_ref, yc_ref, yd_ref, *, lam_init):
    def attend(q, k, v, extra=None):
        (p,), denom = _softmax_parts([_dot_nt(q, k)], extra)
        return _dot(p.astype(BF16), v) / denom

    group = N_HEADS // SWA_KV_HEADS
    lam = _lambda_value(lam_ref, lam_init)
    for h in range(N_HEADS):
        lanes = slice(h * HEAD_DIM, (h + 1) * HEAD_DIM)
        ya_ref[0, :, lanes] = attend(qa_ref[0, h], ka_ref[0, h // group], va_ref[0, h // group],
                                     sink_ref[h]).astype(ya_ref.dtype)
        yb_ref[0, h] = attend(mq_ref[0, h], mk_ref[0, h], mv_ref[0, h]).astype(yb_ref.dtype)
        o = (attend(dq_ref[0, 2 * h], dk_ref[0, 2 * h], dv_ref[0, h])
             - lam * attend(dq_ref[0, 2 * h + 1], dk_ref[0, 2 * h + 1], dv_ref[0, h]))
        yc_ref[0, h] = _sub_ln(o, g_ref, lam_init).astype(yc_ref.dtype)
        yd_ref[0, :, lanes] = attend(nq_ref[0, h], nk_ref[0, h], nv_ref[0, h]).astype(yd_ref.dtype)


def _ctx_attention(sink, lam_vecs, subln_g, hc, lam_init):
    names = ("qa", "ka", "va", "mq", "mk", "mv", "dq", "dk", "dv", "nq", "nk", "nv")
    B, _, C, _ = hc["qa"].shape
    whole4 = lambda b: (b, 0, 0, 0)
    const = lambda b: (0, 0)
    in_specs = [pl.BlockSpec(memory_space=pltpu.SMEM),
                pl.BlockSpec(lam_vecs.shape, const), pl.BlockSpec(subln_g.shape, const)]
    in_specs += [pl.BlockSpec((1,) + hc[n].shape[1:], whole4) for n in names]
    tok = pl.BlockSpec((1, C, GROUP_WIDTH), lambda b: (b, 0, 0))
    hm = pl.BlockSpec((1, N_HEADS, C, HEAD_DIM), whole4)
    return pl.pallas_call(
        functools.partial(_ctx_attn_kernel, lam_init=lam_init),
        grid=(B,),
        in_specs=in_specs,
        out_specs=[tok, hm, hm, tok],
        out_shape=[jax.ShapeDtypeStruct((B, C, GROUP_WIDTH), BF16),
                   jax.ShapeDtypeStruct((B, N_HEADS, C, HEAD_DIM), BF16),
                   jax.ShapeDtypeStruct((B, N_HEADS, C, HEAD_DIM), BF16),
                   jax.ShapeDtypeStruct((B, C, GROUP_WIDTH), BF16)],
        compiler_params=_cparams(("arbitrary",)),
        name="context_attention",
    )(sink, lam_vecs, subln_g, *[hc[n] for n in names])


def _route(r):
    lane = lax.broadcasted_iota(jnp.int32, r.shape, 1)
    lane_f = lane.astype(F32)
    big = float(ROUTER_LANES)
    is_grp = (lane >= N_EXPERTS) & (lane < N_EXPERTS + N_GROUPS)
    g_log = jnp.where(is_grp, r, NEG_INF)
    g_max = jnp.max(g_log, axis=-1, keepdims=True)
    g_val = 1.0 / jnp.sum(jnp.exp(g_log - g_max), axis=-1, keepdims=True)
    g_idx = jnp.min(jnp.where(g_log == g_max, lane_f, big), axis=-1, keepdims=True) - float(N_EXPERTS)
    lane_grp = lax.shift_right_logical(lane, int(math.log2(EXPERTS_PER_GROUP))).astype(F32)
    in_grp = (lane < N_EXPERTS) & (lane_grp == g_idx)
    e_log = jnp.where(in_grp, r, NEG_INF)
    e_max = jnp.max(e_log, axis=-1, keepdims=True)
    i1 = jnp.min(jnp.where(e_log == e_max, lane_f, big), axis=-1, keepdims=True)
    e_rest = jnp.where(lane_f == i1, NEG_INF, e_log)
    e_max2 = jnp.max(e_rest, axis=-1, keepdims=True)
    i2 = jnp.min(jnp.where(e_rest == e_max2, lane_f, big), axis=-1, keepdims=True)
    p2 = jnp.exp(e_max2 - e_max)
    w1 = 1.0 / (1.0 + p2)
    w2 = p2 / (1.0 + p2)
    return g_val * jnp.where(lane_f == i1, w1, jnp.where(lane_f == i2, w2, 0.0))


def _outproj_kernel(x_ref, a_ref, b_ref, c_ref, d_ref, wo_ref, g1_ref, sc2_ref, sh2_ref,
                    lng_ref, lnb_ref, wr_ref, br_ref, x1_ref, h2_ref, gate_ref, *, alpha):
    y = _dot(a_ref[0], wo_ref[0:GROUP_WIDTH, :])
    y += _dot(d_ref[0], wo_ref[3 * GROUP_WIDTH:4 * GROUP_WIDTH, :])
    for h in range(N_HEADS):
        rb = GROUP_WIDTH + h * HEAD_DIM
        rc = 2 * GROUP_WIDTH + h * HEAD_DIM
        y += _dot(b_ref[0, h], wo_ref[rb:rb + HEAD_DIM, :])
        y += _dot(c_ref[0, h], wo_ref[rc:rc + HEAD_DIM, :])
    x1 = _layer_norm(alpha * x_ref[0] + g1_ref[0] * y, lng_ref[...], lnb_ref[...])
    x1_ref[0] = x1
    h2 = x1 * (1.0 + sc2_ref[0]) + sh2_ref[0]
    h2_ref[0] = h2.astype(h2_ref.dtype)
    r = jnp.dot(h2, wr_ref[...], preferred_element_type=F32,
                precision=lax.Precision.HIGHEST) + br_ref[...]
    gate_ref[0] = _route(r)


def _output_projection(x, att, wo, g1, sc2, sh2, ln_g, ln_b, wr, br, alpha):
    B, N, D = x.shape
    tm = min(512, N)
    tok = lambda w: pl.BlockSpec((1, tm, w), lambda b, i: (b, i, 0))
    hm = pl.BlockSpec((1, N_HEADS, tm, HEAD_DIM), lambda b, i: (b, 0, i, 0))
    vec = pl.BlockSpec((1, 1, D), lambda b, i: (b, 0, 0))
    const = lambda a: pl.BlockSpec(a.shape, lambda b, i: (0, 0))
    return pl.pallas_call(
        functools.partial(_outproj_kernel, alpha=alpha),
        grid=(B, N // tm),
        in_specs=[tok(D), tok(GROUP_WIDTH), hm, hm, tok(GROUP_WIDTH), const(wo), vec, vec, vec,
                  const(ln_g), const(ln_b), const(wr), const(br)],
        out_specs=[tok(D), tok(D), tok(ROUTER_LANES)],
        out_shape=[jax.ShapeDtypeStruct((B, N, D), F32), jax.ShapeDtypeStruct((B, N, D), BF16),
                   jax.ShapeDtypeStruct((B, N, ROUTER_LANES), F32)],
        compiler_params=_cparams(("arbitrary", "arbitrary")),
        name="output_projection",
    )(x, att[0], att[1], att[2], att[3], wo, g1, sc2, sh2, ln_g, ln_b, wr, br)


def _moe_kernel(h_ref, gate_ref, x1_ref, wg_ref, wu_ref, wd_ref, g2_ref, lng_ref, lnb_ref,
                o_ref, acc_ref, *, alpha):
    e = pl.program_id(2)

    @pl.when(e == 0)
    def _():
        acc_ref[...] = jnp.zeros_like(acc_ref)

    h = h_ref[0]
    gate_pre = _dot(h, wg_ref[0])
    hid = gate_pre * jax.nn.sigmoid(gate_pre) * _dot(h, wu_ref[0])
    gates = gate_ref[0]
    lane = lax.broadcasted_iota(jnp.int32, gates.shape, 1)
    g_col = jnp.sum(jnp.where(lane == e, gates, 0.0), axis=-1, keepdims=True)
    acc_ref[...] += _dot((hid * g_col).astype(BF16), wd_ref[0])

    @pl.when(e == pl.num_programs(2) - 1)
    def _():
        o_ref[0] = _layer_norm(alpha * x1_ref[0] + g2_ref[0] * acc_ref[...], lng_ref[...], lnb_ref[...])


def _moe(h2, gates, x1, wg, wu, wd, g2, ln_g, ln_b, alpha):
    B, N, D = x1.shape
    tm = min(1024, N)
    n_e, _, hid = wg.shape
    tok = lambda w: pl.BlockSpec((1, tm, w), lambda b, i, e: (b, i, 0))
    const = lambda a: pl.BlockSpec(a.shape, lambda b, i, e: (0, 0))
    return pl.pallas_call(
        functools.partial(_moe_kernel, alpha=alpha),
        grid=(B, N // tm, n_e),
        in_specs=[tok(D), tok(ROUTER_LANES), tok(D),
                  pl.BlockSpec((1, D, hid), lambda b, i, e: (e, 0, 0)),
                  pl.BlockSpec((1, D, hid), lambda b, i, e: (e, 0, 0)),
                  pl.BlockSpec((1, hid, D), lambda b, i, e: (e, 0, 0)),
                  pl.BlockSpec((1, 1, D), lambda b, i, e: (b, 0, 0)),
                  const(ln_g), const(ln_b)],
        out_specs=tok(D),
        out_shape=jax.ShapeDtypeStruct((B, N, D), F32),
        scratch_shapes=[pltpu.VMEM((tm, D), F32)],
        compiler_params=_cparams(("arbitrary", "arbitrary", "arbitrary")),
        name="moe",
    )(h2, gates, x1, wg, wu, wd, g2, ln_g, ln_b)


def _rot_cols(w, d):
    k, n = w.shape
    q = d // 4
    w4 = w.reshape(k, n // d, 4, q)
    return jnp.stack([-w4[:, :, 1], w4[:, :, 0], -w4[:, :, 3], w4[:, :, 2]], axis=2).reshape(k, n)


def _rope_tables(L):
    t = jnp.arange(L, dtype=jnp.int32)
    rows = (t // GRID_W).astype(F32)
    cols = (t % GRID_W).astype(F32)

    def cos_sin(d):
        q = d // 4
        inv = ROPE_BASE ** (-jnp.arange(q, dtype=F32) / q)
        ar = rows[:, None] * inv[None, :]
        ac = cols[:, None] * inv[None, :]
        return (jnp.concatenate([jnp.cos(ar), jnp.cos(ar), jnp.cos(ac), jnp.cos(ac)], -1),
                jnp.concatenate([jnp.sin(ar), jnp.sin(ar), jnp.sin(ac), jnp.sin(ac)], -1))

    c64, s64 = cos_sin(HEAD_DIM)
    c32, s32 = cos_sin(MLA_ROPE)
    ones = jnp.ones((L, MLA_NOPE), F32)
    zeros_n = jnp.zeros((L, MLA_NOPE), F32)
    zeros_p = jnp.zeros((L, MLA_QK_PAD - MLA_NOPE - MLA_ROPE), F32)
    return (jnp.tile(c64, (1, N_HEADS)), jnp.tile(s64, (1, N_HEADS)),
            jnp.tile(c32, (1, 2 * N_HEADS)), jnp.tile(s32, (1, 2 * N_HEADS)),
            jnp.concatenate([ones, c32, zeros_p], -1), jnp.concatenate([zeros_n, s32, zeros_p], -1))


def _identity_tables(C):
    one = jnp.ones((C, GROUP_WIDTH), F32)
    zero = jnp.zeros((C, GROUP_WIDTH), F32)
    pad = MLA_QK_PAD - MLA_NOPE - MLA_ROPE
    cosm = jnp.concatenate([jnp.ones((C, MLA_NOPE + MLA_ROPE), F32), jnp.zeros((C, pad), F32)], -1)
    return one, zero, one, zero, cosm, jnp.zeros((C, MLA_QK_PAD), F32)


def _fused_in_weight(w_in):
    d = w_in.shape[0]
    p = jnp.split(w_in, IN_CUTS, axis=1)
    z = lambda n: jnp.zeros((d, n), w_in.dtype)
    pad_r = MLA_QK_PAD - MLA_NOPE - MLA_ROPE
    parts = {"qa": p[0], "qa_r": _rot_cols(p[0], HEAD_DIM), "ka": p[1], "ka_r": _rot_cols(p[1], HEAD_DIM),
             "va": p[2], "mqr": p[3], "mkvr": p[4],
             "mkr": jnp.concatenate([z(MLA_NOPE), p[5], z(pad_r)], 1),
             "mkr_r": jnp.concatenate([z(MLA_NOPE), _rot_cols(p[5], MLA_ROPE), z(pad_r)], 1),
             "dq": p[6], "dq_r": _rot_cols(p[6], DIFF_QK), "dk": p[7], "dk_r": _rot_cols(p[7], DIFF_QK),
             "dv": p[8], "nq": p[9], "nk": p[10], "nv": p[11]}
    return jnp.concatenate([parts[n] for n, _ in _SEG_LAYOUT], axis=1).astype(BF16)


def _mla_up_weights(w_uq, w_ukv):
    rq = w_uq.shape[0]
    pad_r = MLA_QK_PAD - MLA_NOPE - MLA_ROPE
    wq = w_uq.reshape(rq, N_HEADS, MLA_NOPE + MLA_ROPE)
    zq = lambda n: jnp.zeros((rq, N_HEADS, n), w_uq.dtype)
    rope_rot = _rot_cols(wq[:, :, MLA_NOPE:].reshape(rq, -1), MLA_ROPE).reshape(rq, N_HEADS, MLA_ROPE)
    main = jnp.concatenate([wq, zq(pad_r)], -1).reshape(rq, -1)
    rot = jnp.concatenate([zq(MLA_NOPE), rope_rot, zq(pad_r)], -1).reshape(rq, -1)
    wuq_ext = jnp.concatenate([main, rot], 1).astype(BF16)
    rk = w_ukv.shape[0]
    wkv = w_ukv.reshape(rk, N_HEADS, MLA_NOPE + MLA_V)
    k_part = jnp.concatenate([wkv[:, :, :MLA_NOPE],
                              jnp.zeros((rk, N_HEADS, MLA_QK_PAD - MLA_NOPE), w_ukv.dtype)], -1)
    wukv_ext = jnp.concatenate([k_part.reshape(rk, -1), wkv[:, :, MLA_NOPE:].reshape(rk, -1)], 1).astype(BF16)
    return wuq_ext, wukv_ext


def kernel(x, c, ctx, c_ctx, w_mod, b_mod, w_in, attn_sink, mla_q_norm, w_uq, mla_kv_norm, w_ukv,
           lam_q1, lam_k1, lam_q2, lam_k2, diff_subln, na_rpb, w_out, ln1_g, ln1_b,
           w_group, b_group, w_router, b_router, w_gate, w_up, w_down, ln2_g, ln2_b):
    B, L, D = x.shape
    C = ctx.shape[1]
    depth = w_mod.shape[0]
    alpha = (2 * depth) ** 0.25
    assert D == D_MODEL and B + 1 <= 8 and L % (NA_Q_ROWS * GRID_W) == 0

    cvec = jnp.concatenate([c, c_ctx[None, :], jnp.zeros((8 - B - 1, D), F32)], axis=0)
    mod = _modulation(cvec, w_mod, b_mod)
    lat_tables = _rope_tables(L)
    ctx_tables = _identity_tables(C)
    row = lambda a: a.reshape(1, -1)

    xc = ctx
    for l in range(depth):
        need_ctx = l < depth - 1
        lam_init = 0.8 - 0.6 * math.exp(-0.3 * l)
        chunks = [mod[l, :, i * D:(i + 1) * D] for i in range(6)]
        sh1, sc1, g1, sh2, sc2, g2 = [m[:B, None, :] for m in chunks]
        sh1c, sc1c, g1c, sh2c, sc2c, g2c = [jnp.broadcast_to(m[B:B + 1, None, :], (B, 1, D)) for m in chunks]

        w_all = _fused_in_weight(w_in[l])
        wuq_ext, wukv_ext = _mla_up_weights(w_uq[l], w_ukv[l])
        proj = functools.partial(_input_projection, w_all=w_all, qng=row(mla_q_norm[l]), wuq=wuq_ext,
                                 kvng=row(mla_kv_norm[l]), wukv=wukv_ext)
        hd = proj(x, sc1, sh1, tables=lat_tables)
        hc = proj(xc, sc1c, sh1c, tables=ctx_tables)

        lam_vecs = jnp.stack([lam_q1[l], lam_k1[l], lam_q2[l], lam_k2[l]])
        subln_g = row(diff_subln[l])
        att = (_swa_attention(attn_sink[l], hd, hc), _mla_attention(hd, hc),
               _diff_attention(lam_vecs, subln_g, hd, hc, lam_init), _na_attention(na_rpb[l], hd, hc))

        wo = w_out[l].astype(BF16)
        pad = ROUTER_LANES - N_EXPERTS - N_GROUPS
        wr = jnp.concatenate([w_router[l], w_group[l], jnp.zeros((D, pad), F32)], axis=1)
        br = row(jnp.concatenate([b_router[l], b_group[l], jnp.zeros((pad,), F32)]))
        wg = w_gate[l].reshape(N_EXPERTS, D, EXPERT_HIDDEN).astype(BF16)
        wu = w_up[l].reshape(N_EXPERTS, D, EXPERT_HIDDEN).astype(BF16)
        wd = w_down[l].reshape(N_EXPERTS, EXPERT_HIDDEN, D).astype(BF16)
        post = functools.partial(_output_projection, wo=wo, ln_g=row(ln1_g[l]), ln_b=row(ln1_b[l]),
                                 wr=wr, br=br, alpha=alpha)
        ffn = functools.partial(_moe, wg=wg, wu=wu, wd=wd, ln_g=row(ln2_g[l]), ln_b=row(ln2_b[l]), alpha=alpha)

        x1, h2, gates = post(x, att, g1=g1, sc2=sc2, sh2=sh2)
        x = ffn(h2, gates, x1, g2=g2)
        if need_ctx:
            att_c = _ctx_attention(attn_sink[l], lam_vecs, subln_g, hc, lam_init)
            xc1, hc2, gates_c = post(xc, att_c, g1=g1c, sc2=sc2c, sh2=sh2c)
            xc = ffn(hc2, gates_c, xc1, g2=g2c)
    return x
```

```python
import functools
import math

import jax
import jax.numpy as jnp
import numpy as np
from jax import lax
from jax.experimental import pallas as pl
from jax.experimental.pallas import tpu as pltpu

F32 = jnp.float32
BF16 = jnp.bfloat16

D_MODEL = 1024
GRID_W = 64
HEAD_DIM = 64
N_HEADS = 4
GROUP_WIDTH = N_HEADS * HEAD_DIM
SWA_KV_HEADS = 2
SWA_WINDOW = 128
SWA_BLOCK = 128
MLA_Q_RANK = 256
MLA_KV_RANK = 128
MLA_NOPE = 64
MLA_ROPE = 32
MLA_V = 64
MLA_QK_PAD = 128
DIFF_QK = 32
DIFF_V = 64
NA_KH = 8
NA_KW = 16
NA_Q_ROWS = 2
NA_K_ROWS = 10
N_GROUPS = 4
EXPERTS_PER_GROUP = 8
N_EXPERTS = N_GROUPS * EXPERTS_PER_GROUP
EXPERT_HIDDEN = 256
ROUTER_LANES = 128
ROPE_BASE = 10000.0
NORM_EPS = 1e-5
NEG_INF = -1e30
LOG2E = math.log2(math.e)
SWA_SCALE = HEAD_DIM ** -0.5 * LOG2E
MLA_SCALE = (MLA_NOPE + MLA_ROPE) ** -0.5 * LOG2E
DIFF_SCALE = DIFF_QK ** -0.5 * LOG2E
NA_SCALE = HEAD_DIM ** -0.5 * LOG2E
IN_SPLITS = (GROUP_WIDTH, SWA_KV_HEADS * HEAD_DIM, SWA_KV_HEADS * HEAD_DIM,
             MLA_Q_RANK, MLA_KV_RANK, MLA_ROPE,
             N_HEADS * 2 * DIFF_QK, N_HEADS * 2 * DIFF_QK, N_HEADS * DIFF_V,
             GROUP_WIDTH, GROUP_WIDTH, GROUP_WIDTH)
IN_CUTS = tuple(int(v) for v in np.cumsum(IN_SPLITS)[:-1])

_SEG_LAYOUT = (("qa", 256), ("qa_r", 256), ("ka", 128), ("ka_r", 128), ("va", 128),
               ("mqr", 256), ("mkvr", 128), ("mkr", 128), ("mkr_r", 128),
               ("dq", 256), ("dq_r", 256), ("dk", 256), ("dk_r", 256), ("dv", 256),
               ("nq", 256), ("nk", 256), ("nv", 256))
_SEG = {}
_off = 0
for _name, _w in _SEG_LAYOUT:
    _SEG[_name] = (_off, _off + _w)
    _off += _w
W_ALL_COLS = _off

VMEM_LIMIT_BYTES = 56 * 1024 * 1024


def _cparams(sem):
    return pltpu.CompilerParams(dimension_semantics=sem, vmem_limit_bytes=VMEM_LIMIT_BYTES)


def _dot(a, b):
    return jnp.dot(a, b, preferred_element_type=F32)


def _dot_nt(a, b):
    return lax.dot_general(a, b, (((1,), (1,)), ((), ())), preferred_element_type=F32)


def _rms(x):
    return x * lax.rsqrt(jnp.mean(x * x, axis=-1, keepdims=True) + NORM_EPS)


def _layer_norm(z, g, b):
    mu = jnp.mean(z, axis=-1, keepdims=True)
    zc = z - mu
    var = jnp.mean(zc * zc, axis=-1, keepdims=True)
    return zc * lax.rsqrt(var + NORM_EPS) * g + b


def _mod_kernel(c_ref, w_ref, b_ref, o_ref):
    c = c_ref[...]
    act = c * jax.nn.sigmoid(c)
    o_ref[0] = jnp.dot(act, w_ref[0], preferred_element_type=F32,
                       precision=lax.Precision.HIGHEST) + b_ref[0]


def _modulation(cvec, w_mod, b_mod):
    depth, d, n = w_mod.shape
    tn = 1024
    return pl.pallas_call(
        _mod_kernel,
        grid=(depth, n // tn),
        in_specs=[pl.BlockSpec((8, d), lambda l, j: (0, 0)),
                  pl.BlockSpec((1, d, tn), lambda l, j: (l, 0, j)),
                  pl.BlockSpec((1, 1, tn), lambda l, j: (l, 0, j))],
        out_specs=pl.BlockSpec((1, 8, tn), lambda l, j: (l, 0, j)),
        out_shape=jax.ShapeDtypeStruct((depth, 8, n), F32),
        compiler_params=_cparams(("arbitrary", "arbitrary")),
        name="modulation",
    )(cvec, w_mod, b_mod.reshape(depth, 1, n))


def _inproj_kernel(x_ref, sc_ref, sh_ref, w_ref, cos64_ref, sin64_ref, cos32_ref, sin32_ref,
                   cosm_ref, sinm_ref, qng_ref, wuq_ref, kvng_ref, wukv_ref,
                   qa_o, ka_o, va_o, mq_o, mk_o, mv_o, dq_o, dk_o, dv_o, nq_o, nk_o, nv_o):
    h = (x_ref[0] * (1.0 + sc_ref[0]) + sh_ref[0]).astype(BF16)

    def seg(name):
        a, b = _SEG[name]
        return _dot(h, w_ref[:, a:b])

    def split_heads(val, out_ref, n, width):
        for i in range(n):
            out_ref[0, i] = val[:, i * width:(i + 1) * width].astype(out_ref.dtype)

    cos64 = cos64_ref[...]
    sin64 = sin64_ref[...]
    cos32 = cos32_ref[...]
    sin32 = sin32_ref[...]
    cosm = cosm_ref[...]
    sinm = sinm_ref[...]

    qa = (seg("qa") * cos64 + seg("qa_r") * sin64) * SWA_SCALE
    split_heads(qa, qa_o, N_HEADS, HEAD_DIM)
    ka = seg("ka") * cos64[:, :128] + seg("ka_r") * sin64[:, :128]
    split_heads(ka, ka_o, SWA_KV_HEADS, HEAD_DIM)
    split_heads(seg("va"), va_o, SWA_KV_HEADS, HEAD_DIM)

    qn = (_rms(seg("mqr")) * qng_ref[...]).astype(BF16)
    uq = _dot(qn, wuq_ref[...])
    half = N_HEADS * MLA_QK_PAD
    for i in range(N_HEADS):
        a = i * MLA_QK_PAD
        mq = (uq[:, a:a + MLA_QK_PAD] * cosm + uq[:, half + a:half + a + MLA_QK_PAD] * sinm) * MLA_SCALE
        mq_o[0, i] = mq.astype(BF16)
    kvn = (_rms(seg("mkvr")) * kvng_ref[...]).astype(BF16)
    ukv = _dot(kvn, wukv_ref[...])
    k_rope = seg("mkr") * cosm + seg("mkr_r") * sinm
    for i in range(N_HEADS):
        a = i * MLA_QK_PAD
        mk_o[0, i] = (ukv[:, a:a + MLA_QK_PAD] + k_rope).astype(BF16)
        b = half + i * MLA_V
        mv_o[0, i] = ukv[:, b:b + MLA_V].astype(BF16)

    dq = (seg("dq") * cos32 + seg("dq_r") * sin32) * DIFF_SCALE
    split_heads(dq, dq_o, 2 * N_HEADS, DIFF_QK)
    dk = seg("dk") * cos32 + seg("dk_r") * sin32
    split_heads(dk, dk_o, 2 * N_HEADS, DIFF_QK)
    split_heads(seg("dv"), dv_o, N_HEADS, DIFF_V)

    split_heads(seg("nq") * NA_SCALE, nq_o, N_HEADS, HEAD_DIM)
    split_heads(seg("nk"), nk_o, N_HEADS, HEAD_DIM)
    split_heads(seg("nv"), nv_o, N_HEADS, HEAD_DIM)


_HEAD_OUTS = (("qa", N_HEADS, HEAD_DIM), ("ka", SWA_KV_HEADS, HEAD_DIM), ("va", SWA_KV_HEADS, HEAD_DIM),
              ("mq", N_HEADS, MLA_QK_PAD), ("mk", N_HEADS, MLA_QK_PAD), ("mv", N_HEADS, MLA_V),
              ("dq", 2 * N_HEADS, DIFF_QK), ("dk", 2 * N_HEADS, DIFF_QK), ("dv", N_HEADS, DIFF_V),
              ("nq", N_HEADS, HEAD_DIM), ("nk", N_HEADS, HEAD_DIM), ("nv", N_HEADS, HEAD_DIM))


def _input_projection(x, sc, sh, w_all, tables, qng, wuq, kvng, wukv):
    B, N, D = x.shape
    tm = min(512, N)
    tok = lambda i, b: (b, i, 0)
    vec = lambda i, b: (b, 0, 0)
    tab = lambda i, b: (i, 0)
    const = lambda i, b: (0, 0)
    in_specs = [pl.BlockSpec((1, tm, D), tok),
                pl.BlockSpec((1, 1, D), vec), pl.BlockSpec((1, 1, D), vec),
                pl.BlockSpec(w_all.shape, const)]
    in_specs += [pl.BlockSpec((tm, t.shape[1]), tab) for t in tables]
    in_specs += [pl.BlockSpec(a.shape, const) for a in (qng, wuq, kvng, wukv)]
    out_specs = [pl.BlockSpec((1, n, tm, w), lambda i, b: (b, 0, i, 0)) for _, n, w in _HEAD_OUTS]
    out_shape = [jax.ShapeDtypeStruct((B, n, N, w), BF16) for _, n, w in _HEAD_OUTS]
    outs = pl.pallas_call(
        _inproj_kernel,
        grid=(N // tm, B),
        in_specs=in_specs, out_specs=out_specs, out_shape=out_shape,
        compiler_params=_cparams(("arbitrary", "arbitrary")),
        name="input_projection",
    )(x, sc, sh, w_all, *tables, qng, wuq, kvng, wukv)
    return {name: o for (name, _, _), o in zip(_HEAD_OUTS, outs)}


def _softmax_parts(scores, extra_logit=None):
    m = functools.reduce(jnp.maximum, [jnp.max(s, axis=-1, keepdims=True) for s in scores])
    if extra_logit is not None:
        m = jnp.maximum(m, extra_logit)
    ps = [jnp.exp2(s - m) for s in scores]
    denom = functools.reduce(jnp.add, [jnp.sum(p, axis=-1, keepdims=True) for p in ps])
    if extra_logit is not None:
        denom = denom + jnp.exp2(extra_logit - m)
    return ps, denom


def _flash(qs, ctx_ks, ctx_vs, k_ats, v_ats, n_chunks, dv):
    tq = qs[0].shape[0]

    def update(carry, q, k, v):
        m, l, acc = carry
        s = _dot_nt(q, k)
        m_new = jnp.maximum(m, jnp.max(s, axis=-1, keepdims=True))
        alpha = jnp.exp2(m - m_new)
        p = jnp.exp2(s - m_new)
        l = alpha * l + jnp.sum(p, axis=-1, keepdims=True)
        acc = alpha * acc + _dot(p.astype(BF16), v)
        return m_new, l, acc

    init = (jnp.full((tq, 1), NEG_INF, F32), jnp.zeros((tq, 1), F32), jnp.zeros((tq, dv), F32))
    carries = tuple(update(init, q, k, v) for q, k, v in zip(qs, ctx_ks, ctx_vs))

    def body(i, carries):
        return tuple(update(cr, q, k_at(i), v_at(i))
                     for cr, q, k_at, v_at in zip(carries, qs, k_ats, v_ats))

    carries = lax.fori_loop(0, n_chunks, body, carries, unroll=min(FLASH_UNROLL, n_chunks))
    return [acc / l for _, l, acc in carries]


def _lambda_value(lam_ref, lam_init):
    lv = lam_ref[...]
    return (jnp.exp(jnp.sum(lv[0:1] * lv[1:2], axis=-1, keepdims=True))
            - jnp.exp(jnp.sum(lv[2:3] * lv[3:4], axis=-1, keepdims=True)) + lam_init)


def _sub_ln(o, g_ref, lam_init):
    return _rms(o) * g_ref[...] * (1.0 - lam_init)


def _swa_kernel(sink_ref, q_ref, k_ref, v_ref, kc_ref, vc_ref, o_ref):
    L = k_ref.shape[2]
    span = 3 * SWA_BLOCK
    qb = pl.program_id(1)
    start = pl.multiple_of(jnp.clip(qb * SWA_BLOCK - SWA_BLOCK, 0, L - span), SWA_BLOCK)
    k_abs = start + lax.broadcasted_iota(jnp.int32, (SWA_BLOCK, span), 1)
    q_abs = qb * SWA_BLOCK + lax.broadcasted_iota(jnp.int32, (SWA_BLOCK, span), 0)
    in_window = jnp.abs(k_abs - q_abs) <= SWA_WINDOW
    group = N_HEADS // SWA_KV_HEADS
    for hk in range(SWA_KV_HEADS):
        kb = k_ref[0, hk, pl.ds(start, span), :]
        vb = v_ref[0, hk, pl.ds(start, span), :]
        kc = kc_ref[0, hk]
        vc = vc_ref[0, hk]
        for g in range(group):
            h = hk * group + g
            q = q_ref[0, h]
            s_band = jnp.where(in_window, _dot_nt(q, kb), NEG_INF)
            s_ctx = _dot_nt(q, kc)
            (p_band, p_ctx), denom = _softmax_parts([s_band, s_ctx], sink_ref[h] * LOG2E)
            o = _dot(p_band.astype(BF16), vb) + _dot(p_ctx.astype(BF16), vc)
            o_ref[0, :, h * HEAD_DIM:(h + 1) * HEAD_DIM] = (o / denom).astype(o_ref.dtype)


def _swa_attention(sink, hd, hc):
    B, _, L, _ = hd["qa"].shape
    C = hc["ka"].shape[2]
    whole = lambda b, i: (b, 0, 0, 0)
    return pl.pallas_call(
        _swa_kernel,
        grid=(B, L // SWA_BLOCK),
        in_specs=[pl.BlockSpec(memory_space=pltpu.SMEM),
                  pl.BlockSpec((1, N_HEADS, SWA_BLOCK, HEAD_DIM), lambda b, i: (b, 0, i, 0)),
                  pl.BlockSpec((1, SWA_KV_HEADS, L, HEAD_DIM), whole),
                  pl.BlockSpec((1, SWA_KV_HEADS, L, HEAD_DIM), whole),
                  pl.BlockSpec((1, SWA_KV_HEADS, C, HEAD_DIM), whole),
                  pl.BlockSpec((1, SWA_KV_HEADS, C, HEAD_DIM), whole)],
        out_specs=pl.BlockSpec((1, SWA_BLOCK, GROUP_WIDTH), lambda b, i: (b, i, 0)),
        out_shape=jax.ShapeDtypeStruct((B, L, GROUP_WIDTH), BF16),
        compiler_params=_cparams(("arbitrary", "arbitrary")),
        name="swa_attention",
    )(sink, hd["qa"], hd["ka"], hd["va"], hc["ka"], hc["va"])


GLOBAL_TQ = 512
GLOBAL_TK = 1024
GLOBAL_CHAINS = 2
FLASH_UNROLL = 2


def _chunk_at(ref, lead, tk):
    return lambda i: ref[lead + (pl.ds(pl.multiple_of(i * tk, tk), tk), slice(None))]


def _mla_kernel(q_ref, kc_ref, vc_ref, k_ref, v_ref, o_ref, *, tk):
    n_chunks = k_ref.shape[2] // tk
    heads = range(GLOBAL_CHAINS)
    outs = _flash([q_ref[0, j] for j in heads], [kc_ref[0, j] for j in heads], [vc_ref[0, j] for j in heads],
                  [_chunk_at(k_ref, (0, j), tk) for j in heads], [_chunk_at(v_ref, (0, j), tk) for j in heads],
                  n_chunks, MLA_V)
    for j in heads:
        o_ref[0, j] = outs[j].astype(o_ref.dtype)


def _mla_attention(hd, hc):
    B, H, L, dk = hd["mq"].shape
    C = hc["mk"].shape[2]
    tq = min(GLOBAL_TQ, L)
    tk = min(GLOBAL_TK, L)
    hp = GLOBAL_CHAINS
    whole = lambda b, h, i: (b, h, 0, 0)
    return pl.pallas_call(
        functools.partial(_mla_kernel, tk=tk),
        grid=(B, H // hp, L // tq),
        in_specs=[pl.BlockSpec((1, hp, tq, dk), lambda b, h, i: (b, h, i, 0)),
                  pl.BlockSpec((1, hp, C, dk), whole),
                  pl.BlockSpec((1, hp, C, MLA_V), whole),
                  pl.BlockSpec((1, hp, L, dk), whole),
                  pl.BlockSpec((1, hp, L, MLA_V), whole)],
        out_specs=pl.BlockSpec((1, hp, tq, MLA_V), lambda b, h, i: (b, h, i, 0)),
        out_shape=jax.ShapeDtypeStruct((B, H, L, MLA_V), BF16),
        compiler_params=_cparams(("arbitrary", "arbitrary", "arbitrary")),
        name="mla_attention",
    )(hd["mq"], hc["mk"], hc["mv"], hd["mk"], hd["mv"])


def _diff_kernel(lam_ref, g_ref, q_ref, kc_ref, vc_ref, k_ref, v_ref, o_ref, *, tk, lam_init):
    n_chunks = k_ref.shape[2] // tk
    v_at = _chunk_at(v_ref, (0, 0), tk)
    o1, o2 = _flash([q_ref[0, 0], q_ref[0, 1]], [kc_ref[0, 0], kc_ref[0, 1]], [vc_ref[0, 0]] * 2,
                    [_chunk_at(k_ref, (0, 0), tk), _chunk_at(k_ref, (0, 1), tk)], [v_at] * 2,
                    n_chunks, DIFF_V)
    o = o1 - _lambda_value(lam_ref, lam_init) * o2
    o_ref[0, 0] = _sub_ln(o, g_ref, lam_init).astype(o_ref.dtype)


def _diff_attention(lam_vecs, subln_g, hd, hc, lam_init):
    B, _, L, dk = hd["dq"].shape
    C = hc["dk"].shape[2]
    tq = min(GLOBAL_TQ, L)
    tk = min(GLOBAL_TK, L)
    whole = lambda b, h, i: (b, h, 0, 0)
    const = lambda b, h, i: (0, 0)
    return pl.pallas_call(
        functools.partial(_diff_kernel, tk=tk, lam_init=lam_init),
        grid=(B, N_HEADS, L // tq),
        in_specs=[pl.BlockSpec(lam_vecs.shape, const),
                  pl.BlockSpec(subln_g.shape, const),
                  pl.BlockSpec((1, 2, tq, dk), lambda b, h, i: (b, h, i, 0)),
                  pl.BlockSpec((1, 2, C, dk), whole),
                  pl.BlockSpec((1, 1, C, DIFF_V), whole),
                  pl.BlockSpec((1, 2, L, dk), whole),
                  pl.BlockSpec((1, 1, L, DIFF_V), whole)],
        out_specs=pl.BlockSpec((1, 1, tq, DIFF_V), lambda b, h, i: (b, h, i, 0)),
        out_shape=jax.ShapeDtypeStruct((B, N_HEADS, L, DIFF_V), BF16),
        compiler_params=_cparams(("arbitrary", "arbitrary", "arbitrary")),
        name="diff_attention",
    )(lam_vecs, subln_g, hd["dq"], hc["dk"], hc["dv"], hd["dk"], hd["dv"])


def _na_plan(rows):
    kh = min(NA_KH, rows)
    n_row_off = 2 * NA_KH - 1
    col = np.arange(GRID_W)
    col_start = np.clip(col - NA_KW // 2, 0, GRID_W - NA_KW)
    col_ok = (col[None, :] >= col_start[:, None]) & (col[None, :] < col_start[:, None] + NA_KW)
    col_off = col[None, :] - col[:, None] + (NA_KW - 1)
    col_onehot = ((col_off[None] == np.arange(2 * NA_KW - 1)[:, None, None]) & col_ok[None]).astype(np.float32)
    patterns, starts, ids = {}, [], []
    for blk in range(rows // NA_Q_ROWS):
        r0 = blk * NA_Q_ROWS
        ks = int(np.clip(r0 - kh // 2, 0, rows - NA_K_ROWS))
        q_row = r0 + np.arange(NA_Q_ROWS)
        k_row = ks + np.arange(NA_K_ROWS)
        r_start = np.clip(q_row - kh // 2, 0, rows - kh)
        row_ok = (k_row[None, :] >= r_start[:, None]) & (k_row[None, :] < r_start[:, None] + kh)
        row_off = k_row[None, :] - q_row[:, None] + (NA_KH - 1)
        sel = np.where(row_ok, row_off, n_row_off)
        key = tuple(int(v) for v in sel.reshape(-1))
        if key not in patterns:
            patterns[key] = (len(patterns), sel)
        starts.append(ks)
        ids.append(patterns[key][0])
    row_sel = np.stack([p[1] for p in sorted(patterns.values(), key=lambda p: p[0])])
    return np.asarray(starts, np.int32), np.asarray(ids, np.int32), row_sel, col_onehot, col_ok


def _na_kernel(ks_ref, pid_ref, q_ref, k_ref, v_ref, kc_ref, vc_ref, bias_ref, o_ref):
    del pid_ref
    k_len = NA_K_ROWS * GRID_W
    start = pl.multiple_of(ks_ref[pl.program_id(1)] * GRID_W, GRID_W)
    for h in range(N_HEADS):
        q = q_ref[0, h]
        kw = k_ref[0, h, pl.ds(start, k_len), :]
        vw = v_ref[0, h, pl.ds(start, k_len), :]
        s_win = _dot_nt(q, kw) + bias_ref[0, h]
        s_ctx = _dot_nt(q, kc_ref[0, h])
        (p_win, p_ctx), denom = _softmax_parts([s_win, s_ctx])
        o = _dot(p_win.astype(BF16), vw) + _dot(p_ctx.astype(BF16), vc_ref[0, h])
        o_ref[0, :, h * HEAD_DIM:(h + 1) * HEAD_DIM] = (o / denom).astype(o_ref.dtype)


def _na_attention(rpb, hd, hc):
    B, H, L, _ = hd["nq"].shape
    C = hc["nk"].shape[2]
    rows = L // GRID_W
    starts, ids, row_sel, col_onehot, col_ok = _na_plan(rows)
    q_len = NA_Q_ROWS * GRID_W
    k_len = NA_K_ROWS * GRID_W
    slab = jnp.einsum("hrj,jqk->hrqk", rpb * LOG2E, col_onehot, precision=lax.Precision.HIGHEST)
    slab = jnp.where(col_ok[None, None], slab, NEG_INF)
    slab = jnp.concatenate([slab, jnp.full_like(slab[:, :1], NEG_INF)], axis=1)
    bias = jnp.take(slab, jnp.asarray(row_sel), axis=1)
    bias = jnp.transpose(bias, (1, 0, 2, 4, 3, 5)).reshape(row_sel.shape[0], H, q_len, k_len)
    whole = lambda b, i, ks, pid: (b, 0, 0, 0)
    grid_spec = pltpu.PrefetchScalarGridSpec(
        num_scalar_prefetch=2,
        grid=(B, rows // NA_Q_ROWS),
        in_specs=[pl.BlockSpec((1, H, q_len, HEAD_DIM), lambda b, i, ks, pid: (b, 0, i, 0)),
                  pl.BlockSpec((1, H, L, HEAD_DIM), whole),
                  pl.BlockSpec((1, H, L, HEAD_DIM), whole),
                  pl.BlockSpec((1, H, C, HEAD_DIM), whole),
                  pl.BlockSpec((1, H, C, HEAD_DIM), whole),
                  pl.BlockSpec((1, H, q_len, k_len), lambda b, i, ks, pid: (pid[i], 0, 0, 0))],
        out_specs=pl.BlockSpec((1, q_len, GROUP_WIDTH), lambda b, i, ks, pid: (b, i, 0)))
    return pl.pallas_call(
        _na_kernel,
        grid_spec=grid_spec,
        out_shape=jax.ShapeDtypeStruct((B, L, GROUP_WIDTH), BF16),
        compiler_params=_cparams(("arbitrary", "arbitrary")),
        name="neighborhood_attention",
    )(jnp.asarray(starts), jnp.asarray(ids), hd["nq"], hd["nk"], hd["nv"], hc["nk"], hc["nv"], bias)


def _ctx_attn_kernel(sink_ref, lam_ref, g_ref, qa_ref, ka_ref, va_ref, mq_ref, mk_ref, mv_ref,
                     dq_ref, dk_ref, dv_ref, nq_ref, nk_ref, nv_ref,
                     ya_ref, yb_ref, yc_ref, yd_ref, *, lam_init):
    def attend(q, k, v, extra=None):
        (p,), denom = _softmax_parts([_dot_nt(q, k)], extra)
        return _dot(p.astype(BF16), v) / denom

    group = N_HEADS // SWA_KV_HEADS
    lam = _lambda_value(lam_ref, lam_init)
    for h in range(N_HEADS):
        lanes = slice(h * HEAD_DIM, (h + 1) * HEAD_DIM)
        ya_ref[0, :, lanes] = attend(qa_ref[0, h], ka_ref[0, h // group], va_ref[0, h // group],
                                     sink_ref[h] * LOG2E).astype(ya_ref.dtype)
        yb_ref[0, h] = attend(mq_ref[0, h], mk_ref[0, h], mv_ref[0, h]).astype(yb_ref.dtype)
        o = (attend(dq_ref[0, 2 * h], dk_ref[0, 2 * h], dv_ref[0, h])
             - lam * attend(dq_ref[0, 2 * h + 1], dk_ref[0, 2 * h + 1], dv_ref[0, h]))
        yc_ref[0, h] = _sub_ln(o, g_ref, lam_init).astype(yc_ref.dtype)
        yd_ref[0, :, lanes] = attend(nq_ref[0, h], nk_ref[0, h], nv_ref[0, h]).astype(yd_ref.dtype)


def _ctx_attention(sink, lam_vecs, subln_g, hc, lam_init):
    names = ("qa", "ka", "va", "mq", "mk", "mv", "dq", "dk", "dv", "nq", "nk", "nv")
    B, _, C, _ = hc["qa"].shape
    whole4 = lambda b: (b, 0, 0, 0)
    const = lambda b: (0, 0)
    in_specs = [pl.BlockSpec(memory_space=pltpu.SMEM),
                pl.BlockSpec(lam_vecs.shape, const), pl.BlockSpec(subln_g.shape, const)]
    in_specs += [pl.BlockSpec((1,) + hc[n].shape[1:], whole4) for n in names]
    tok = pl.BlockSpec((1, C, GROUP_WIDTH), lambda b: (b, 0, 0))
    hm = pl.BlockSpec((1, N_HEADS, C, HEAD_DIM), whole4)
    return pl.pallas_call(
        functools.partial(_ctx_attn_kernel, lam_init=lam_init),
        grid=(B,),
        in_specs=in_specs,
        out_specs=[tok, hm, hm, tok],
        out_shape=[jax.ShapeDtypeStruct((B, C, GROUP_WIDTH), BF16),
                   jax.ShapeDtypeStruct((B, N_HEADS, C, HEAD_DIM), BF16),
                   jax.ShapeDtypeStruct((B, N_HEADS, C, HEAD_DIM), BF16),
                   jax.ShapeDtypeStruct((B, C, GROUP_WIDTH), BF16)],
        compiler_params=_cparams(("arbitrary",)),
        name="context_attention",
    )(sink, lam_vecs, subln_g, *[hc[n] for n in names])


def _route(r):
    lane = lax.broadcasted_iota(jnp.int32, r.shape, 1)
    lane_f = lane.astype(F32)
    big = float(ROUTER_LANES)
    is_grp = (lane >= N_EXPERTS) & (lane < N_EXPERTS + N_GROUPS)
    g_log = jnp.where(is_grp, r, NEG_INF)
    g_max = jnp.max(g_log, axis=-1, keepdims=True)
    g_val = 1.0 / jnp.sum(jnp.exp(g_log - g_max), axis=-1, keepdims=True)
    g_idx = jnp.min(jnp.where(g_log == g_max, lane_f, big), axis=-1, keepdims=True) - float(N_EXPERTS)
    lane_grp = lax.shift_right_logical(lane, int(math.log2(EXPERTS_PER_GROUP))).astype(F32)
    in_grp = (lane < N_EXPERTS) & (lane_grp == g_idx)
    e_log = jnp.where(in_grp, r, NEG_INF)
    e_max = jnp.max(e_log, axis=-1, keepdims=True)
    i1 = jnp.min(jnp.where(e_log == e_max, lane_f, big), axis=-1, keepdims=True)
    e_rest = jnp.where(lane_f == i1, NEG_INF, e_log)
    e_max2 = jnp.max(e_rest, axis=-1, keepdims=True)
    i2 = jnp.min(jnp.where(e_rest == e_max2, lane_f, big), axis=-1, keepdims=True)
    p2 = jnp.exp(e_max2 - e_max)
    w1 = 1.0 / (1.0 + p2)
    w2 = p2 / (1.0 + p2)
    return g_val * jnp.where(lane_f == i1, w1, jnp.where(lane_f == i2, w2, 0.0))


def _outproj_kernel(x_ref, a_ref, b_ref, c_ref, d_ref, wo_ref, g1_ref, sc2_ref, sh2_ref,
                    lng_ref, lnb_ref, wr_ref, br_ref, x1_ref, h2_ref, gate_ref, *, alpha):
    y = _dot(a_ref[0], wo_ref[0:GROUP_WIDTH, :])
    y += _dot(d_ref[0], wo_ref[3 * GROUP_WIDTH:4 * GROUP_WIDTH, :])
    for h in range(N_HEADS):
        rb = GROUP_WIDTH + h * HEAD_DIM
        rc = 2 * GROUP_WIDTH + h * HEAD_DIM
        y += _dot(b_ref[0, h], wo_ref[rb:rb + HEAD_DIM, :])
        y += _dot(c_ref[0, h], wo_ref[rc:rc + HEAD_DIM, :])
    x1 = _layer_norm(alpha * x_ref[0] + g1_ref[0] * y, lng_ref[...], lnb_ref[...])
    x1_ref[0] = x1
    h2 = x1 * (1.0 + sc2_ref[0]) + sh2_ref[0]
    h2_ref[0] = h2.astype(h2_ref.dtype)
    r = jnp.dot(h2, wr_ref[...], preferred_element_type=F32,
                precision=lax.Precision.HIGHEST) + br_ref[...]
    gate_ref[0] = _route(r)


def _output_projection(x, att, wo, g1, sc2, sh2, ln_g, ln_b, wr, br, alpha):
    B, N, D = x.shape
    tm = min(512, N)
    tok = lambda w: pl.BlockSpec((1, tm, w), lambda b, i: (b, i, 0))
    hm = pl.BlockSpec((1, N_HEADS, tm, HEAD_DIM), lambda b, i: (b, 0, i, 0))
    vec = pl.BlockSpec((1, 1, D), lambda b, i: (b, 0, 0))
    const = lambda a: pl.BlockSpec(a.shape, lambda b, i: (0, 0))
    return pl.pallas_call(
        functools.partial(_outproj_kernel, alpha=alpha),
        grid=(B, N // tm),
        in_specs=[tok(D), tok(GROUP_WIDTH), hm, hm, tok(GROUP_WIDTH), const(wo), vec, vec, vec,
                  const(ln_g), const(ln_b), const(wr), const(br)],
        out_specs=[tok(D), tok(D), tok(ROUTER_LANES)],
        out_shape=[jax.ShapeDtypeStruct((B, N, D), F32), jax.ShapeDtypeStruct((B, N, D), BF16),
                   jax.ShapeDtypeStruct((B, N, ROUTER_LANES), F32)],
        compiler_params=_cparams(("arbitrary", "arbitrary")),
        name="output_projection",
    )(x, att[0], att[1], att[2], att[3], wo, g1, sc2, sh2, ln_g, ln_b, wr, br)


def _moe_kernel(h_ref, gate_ref, x1_ref, wg_ref, wu_ref, wd_ref, g2_ref, lng_ref, lnb_ref,
                o_ref, acc_ref, *, alpha):
    e = pl.program_id(2)

    @pl.when(e == 0)
    def _():
        acc_ref[...] = jnp.zeros_like(acc_ref)

    h = h_ref[0]
    gate_pre = _dot(h, wg_ref[0])
    hid = gate_pre * jax.nn.sigmoid(gate_pre) * _dot(h, wu_ref[0])
    gates = gate_ref[0]
    lane = lax.broadcasted_iota(jnp.int32, gates.shape, 1)
    g_col = jnp.sum(jnp.where(lane == e, gates, 0.0), axis=-1, keepdims=True)
    acc_ref[...] += _dot((hid * g_col).astype(BF16), wd_ref[0])

    @pl.when(e == pl.num_programs(2) - 1)
    def _():
        o_ref[0] = _layer_norm(alpha * x1_ref[0] + g2_ref[0] * acc_ref[...], lng_ref[...], lnb_ref[...])


def _moe(h2, gates, x1, wg, wu, wd, g2, ln_g, ln_b, alpha):
    B, N, D = x1.shape
    tm = min(1024, N)
    n_e, _, hid = wg.shape
    tok = lambda w: pl.BlockSpec((1, tm, w), lambda b, i, e: (b, i, 0))
    const = lambda a: pl.BlockSpec(a.shape, lambda b, i, e: (0, 0))
    return pl.pallas_call(
        functools.partial(_moe_kernel, alpha=alpha),
        grid=(B, N // tm, n_e),
        in_specs=[tok(D), tok(ROUTER_LANES), tok(D),
                  pl.BlockSpec((1, D, hid), lambda b, i, e: (e, 0, 0)),
                  pl.BlockSpec((1, D, hid), lambda b, i, e: (e, 0, 0)),
                  pl.BlockSpec((1, hid, D), lambda b, i, e: (e, 0, 0)),
                  pl.BlockSpec((1, 1, D), lambda b, i, e: (b, 0, 0)),
                  const(ln_g), const(ln_b)],
        out_specs=tok(D),
        out_shape=jax.ShapeDtypeStruct((B, N, D), F32),
        scratch_shapes=[pltpu.VMEM((tm, D), F32)],
        compiler_params=_cparams(("arbitrary", "arbitrary", "arbitrary")),
        name="moe",
    )(h2, gates, x1, wg, wu, wd, g2, ln_g, ln_b)


def _rot_cols(w, d):
    k, n = w.shape
    q = d // 4
    w4 = w.reshape(k, n // d, 4, q)
    return jnp.stack([-w4[:, :, 1], w4[:, :, 0], -w4[:, :, 3], w4[:, :, 2]], axis=2).reshape(k, n)


def _rope_tables(L):
    t = jnp.arange(L, dtype=jnp.int32)
    rows = (t // GRID_W).astype(F32)
    cols = (t % GRID_W).astype(F32)

    def cos_sin(d):
        q = d // 4
        inv = ROPE_BASE ** (-jnp.arange(q, dtype=F32) / q)
        ar = rows[:, None] * inv[None, :]
        ac = cols[:, None] * inv[None, :]
        return (jnp.concatenate([jnp.cos(ar), jnp.cos(ar), jnp.cos(ac), jnp.cos(ac)], -1),
                jnp.concatenate([jnp.sin(ar), jnp.sin(ar), jnp.sin(ac), jnp.sin(ac)], -1))

    c64, s64 = cos_sin(HEAD_DIM)
    c32, s32 = cos_sin(MLA_ROPE)
    ones = jnp.ones((L, MLA_NOPE), F32)
    zeros_n = jnp.zeros((L, MLA_NOPE), F32)
    zeros_p = jnp.zeros((L, MLA_QK_PAD - MLA_NOPE - MLA_ROPE), F32)
    return (jnp.tile(c64, (1, N_HEADS)), jnp.tile(s64, (1, N_HEADS)),
            jnp.tile(c32, (1, 2 * N_HEADS)), jnp.tile(s32, (1, 2 * N_HEADS)),
            jnp.concatenate([ones, c32, zeros_p], -1), jnp.concatenate([zeros_n, s32, zeros_p], -1))


def _identity_tables(C):
    one = jnp.ones((C, GROUP_WIDTH), F32)
    zero = jnp.zeros((C, GROUP_WIDTH), F32)
    pad = MLA_QK_PAD - MLA_NOPE - MLA_ROPE
    cosm = jnp.concatenate([jnp.ones((C, MLA_NOPE + MLA_ROPE), F32), jnp.zeros((C, pad), F32)], -1)
    return one, zero, one, zero, cosm, jnp.zeros((C, MLA_QK_PAD), F32)


def _fused_in_weight(w_in):
    d = w_in.shape[0]
    p = jnp.split(w_in, IN_CUTS, axis=1)
    z = lambda n: jnp.zeros((d, n), w_in.dtype)
    pad_r = MLA_QK_PAD - MLA_NOPE - MLA_ROPE
    parts = {"qa": p[0], "qa_r": _rot_cols(p[0], HEAD_DIM), "ka": p[1], "ka_r": _rot_cols(p[1], HEAD_DIM),
             "va": p[2], "mqr": p[3], "mkvr": p[4],
             "mkr": jnp.concatenate([z(MLA_NOPE), p[5], z(pad_r)], 1),
             "mkr_r": jnp.concatenate([z(MLA_NOPE), _rot_cols(p[5], MLA_ROPE), z(pad_r)], 1),
             "dq": p[6], "dq_r": _rot_cols(p[6], DIFF_QK), "dk": p[7], "dk_r": _rot_cols(p[7], DIFF_QK),
             "dv": p[8], "nq": p[9], "nk": p[10], "nv": p[11]}
    return jnp.concatenate([parts[n] for n, _ in _SEG_LAYOUT], axis=1).astype(BF16)


def _mla_up_weights(w_uq, w_ukv):
    rq = w_uq.shape[0]
    pad_r = MLA_QK_PAD - MLA_NOPE - MLA_ROPE
    wq = w_uq.reshape(rq, N_HEADS, MLA_NOPE + MLA_ROPE)
    zq = lambda n: jnp.zeros((rq, N_HEADS, n), w_uq.dtype)
    rope_rot = _rot_cols(wq[:, :, MLA_NOPE:].reshape(rq, -1), MLA_ROPE).reshape(rq, N_HEADS, MLA_ROPE)
    main = jnp.concatenate([wq, zq(pad_r)], -1).reshape(rq, -1)
    rot = jnp.concatenate([zq(MLA_NOPE), rope_rot, zq(pad_r)], -1).reshape(rq, -1)
    wuq_ext = jnp.concatenate([main, rot], 1).astype(BF16)
    rk = w_ukv.shape[0]
    wkv = w_ukv.reshape(rk, N_HEADS, MLA_NOPE + MLA_V)
    k_part = jnp.concatenate([wkv[:, :, :MLA_NOPE],
                              jnp.zeros((rk, N_HEADS, MLA_QK_PAD - MLA_NOPE), w_ukv.dtype)], -1)
    wukv_ext = jnp.concatenate([k_part.reshape(rk, -1), wkv[:, :, MLA_NOPE:].reshape(rk, -1)], 1).astype(BF16)
    return wuq_ext, wukv_ext


def kernel(x, c, ctx, c_ctx, w_mod, b_mod, w_in, attn_sink, mla_q_norm, w_uq, mla_kv_norm, w_ukv,
           lam_q1, lam_k1, lam_q2, lam_k2, diff_subln, na_rpb, w_out, ln1_g, ln1_b,
           w_group, b_group, w_router, b_router, w_gate, w_up, w_down, ln2_g, ln2_b):
    B, L, D = x.shape
    C = ctx.shape[1]
    depth = w_mod.shape[0]
    alpha = (2 * depth) ** 0.25
    assert D == D_MODEL and B + 1 <= 8 and L % (NA_Q_ROWS * GRID_W) == 0

    cvec = jnp.concatenate([c, c_ctx[None, :], jnp.zeros((8 - B - 1, D), F32)], axis=0)
    mod = _modulation(cvec, w_mod, b_mod)
    lat_tables = _rope_tables(L)
    ctx_tables = _identity_tables(C)
    row = lambda a: a.reshape(1, -1)

    xc = ctx
    for l in range(depth):
        need_ctx = l < depth - 1
        lam_init = 0.8 - 0.6 * math.exp(-0.3 * l)
        chunks = [mod[l, :, i * D:(i + 1) * D] for i in range(6)]
        sh1, sc1, g1, sh2, sc2, g2 = [m[:B, None, :] for m in chunks]
        sh1c, sc1c, g1c, sh2c, sc2c, g2c = [jnp.broadcast_to(m[B:B + 1, None, :], (B, 1, D)) for m in chunks]

        w_all = _fused_in_weight(w_in[l])
        wuq_ext, wukv_ext = _mla_up_weights(w_uq[l], w_ukv[l])
        proj = functools.partial(_input_projection, w_all=w_all, qng=row(mla_q_norm[l]), wuq=wuq_ext,
                                 kvng=row(mla_kv_norm[l]), wukv=wukv_ext)
        hd = proj(x, sc1, sh1, tables=lat_tables)
        hc = proj(xc, sc1c, sh1c, tables=ctx_tables)

        lam_vecs = jnp.stack([lam_q1[l], lam_k1[l], lam_q2[l], lam_k2[l]])
        subln_g = row(diff_subln[l])
        att = (_swa_attention(attn_sink[l], hd, hc), _mla_attention(hd, hc),
               _diff_attention(lam_vecs, subln_g, hd, hc, lam_init), _na_attention(na_rpb[l], hd, hc))

        wo = w_out[l].astype(BF16)
        pad = ROUTER_LANES - N_EXPERTS - N_GROUPS
        wr = jnp.concatenate([w_router[l], w_group[l], jnp.zeros((D, pad), F32)], axis=1)
        br = row(jnp.concatenate([b_router[l], b_group[l], jnp.zeros((pad,), F32)]))
        wg = w_gate[l].reshape(N_EXPERTS, D, EXPERT_HIDDEN).astype(BF16)
        wu = w_up[l].reshape(N_EXPERTS, D, EXPERT_HIDDEN).astype(BF16)
        wd = w_down[l].reshape(N_EXPERTS, EXPERT_HIDDEN, D).astype(BF16)
        post = functools.partial(_output_projection, wo=wo, ln_g=row(ln1_g[l]), ln_b=row(ln1_b[l]),
                                 wr=wr, br=br, alpha=alpha)
        ffn = functools.partial(_moe, wg=wg, wu=wu, wd=wd, ln_g=row(ln2_g[l]), ln_b=row(ln2_b[l]), alpha=alpha)

        x1, h2, gates = post(x, att, g1=g1, sc2=sc2, sh2=sh2)
        x = ffn(h2, gates, x1, g2=g2)
        if need_ctx:
            att_c = _ctx_attention(attn_sink[l], lam_vecs, subln_g, hc, lam_init)
            xc1, hc2, gates_c = post(xc, att_c, g1=g1c, sc2=sc2c, sh2=sh2c)
            xc = ffn(hc2, gates_c, xc1, g2=g2c)
    return x
```

```python
import functools
import math

import jax
import jax.numpy as jnp
import numpy as np
from jax import lax
from jax.experimental import pallas as pl
from jax.experimental.pallas import tpu as pltpu

F32 = jnp.float32
BF16 = jnp.bfloat16

D_MODEL = 1024
GRID_W = 64
HEAD_DIM = 64
N_HEADS = 4
N_MIXERS = 4
GROUP_WIDTH = N_HEADS * HEAD_DIM
SWA_KV_HEADS = 2
SWA_WINDOW = 128
SWA_BLOCK = 128
MLA_Q_RANK = 256
MLA_KV_RANK = 128
MLA_NOPE = 64
MLA_ROPE = 32
MLA_V = 64
MLA_QK_PAD = 128
DIFF_QK = 32
DIFF_V = 64
NA_KH = 8
NA_KW = 16
NA_Q_ROWS = 4
NA_K_ROWS = NA_Q_ROWS + NA_KH
N_GROUPS = 4
EXPERTS_PER_GROUP = 8
N_EXPERTS = N_GROUPS * EXPERTS_PER_GROUP
EXPERT_HIDDEN = 256
ROUTER_LANES = 128
ROPE_BASE = 10000.0
NORM_EPS = 1e-5
NEG_INF = -1e30
LOG2E = math.log2(math.e)
SWA_SCALE = HEAD_DIM ** -0.5 * LOG2E
MLA_SCALE = (MLA_NOPE + MLA_ROPE) ** -0.5 * LOG2E
DIFF_SCALE = DIFF_QK ** -0.5 * LOG2E
NA_SCALE = HEAD_DIM ** -0.5 * LOG2E
IN_SPLITS = (GROUP_WIDTH, SWA_KV_HEADS * HEAD_DIM, SWA_KV_HEADS * HEAD_DIM,
             MLA_Q_RANK, MLA_KV_RANK, MLA_ROPE,
             N_HEADS * 2 * DIFF_QK, N_HEADS * 2 * DIFF_QK, N_HEADS * DIFF_V,
             GROUP_WIDTH, GROUP_WIDTH, GROUP_WIDTH)
IN_CUTS = tuple(int(v) for v in np.cumsum(IN_SPLITS)[:-1])

_SEG_LAYOUT = (("qa", 256), ("qa_r", 256), ("ka", 128), ("ka_r", 128), ("va", 128),
               ("mqr", 256), ("mkvr", 128), ("mkr", 128), ("mkr_r", 128),
               ("dq", 256), ("dq_r", 256), ("dk", 256), ("dk_r", 256), ("dv", 256),
               ("nq", 256), ("nk", 256), ("nv", 256))
_SEG = {}
_off = 0
for _name, _w in _SEG_LAYOUT:
    _SEG[_name] = (_off, _off + _w)
    _off += _w
W_ALL_COLS = _off

VMEM_LIMIT_BYTES = 56 * 1024 * 1024


def _cparams(sem):
    return pltpu.CompilerParams(dimension_semantics=sem, vmem_limit_bytes=VMEM_LIMIT_BYTES)


def _dot(a, b):
    return jnp.dot(a, b, preferred_element_type=F32)


def _dot_nt(a, b):
    return lax.dot_general(a, b, (((1,), (1,)), ((), ())), preferred_element_type=F32)


def _rms(x):
    return x * lax.rsqrt(jnp.mean(x * x, axis=-1, keepdims=True) + NORM_EPS)


def _layer_norm(z, g, b):
    mu = jnp.mean(z, axis=-1, keepdims=True)
    zc = z - mu
    var = jnp.mean(zc * zc, axis=-1, keepdims=True)
    return zc * lax.rsqrt(var + NORM_EPS) * g + b


def _mod_kernel(c_ref, w_ref, b_ref, o_ref):
    c = c_ref[...]
    act = c * jax.nn.sigmoid(c)
    o_ref[0] = jnp.dot(act, w_ref[0], preferred_element_type=F32,
                       precision=lax.Precision.HIGHEST) + b_ref[0]


def _modulation(cvec, w_mod, b_mod):
    depth, d, n = w_mod.shape
    tn = 1024
    return pl.pallas_call(
        _mod_kernel,
        grid=(depth, n // tn),
        in_specs=[pl.BlockSpec((8, d), lambda l, j: (0, 0)),
                  pl.BlockSpec((1, d, tn), lambda l, j: (l, 0, j)),
                  pl.BlockSpec((1, 1, tn), lambda l, j: (l, 0, j))],
        out_specs=pl.BlockSpec((1, 8, tn), lambda l, j: (l, 0, j)),
        out_shape=jax.ShapeDtypeStruct((depth, 8, n), F32),
        compiler_params=_cparams(("arbitrary", "arbitrary")),
        name="modulation",
    )(cvec, w_mod, b_mod.reshape(depth, 1, n))


def _inproj_kernel(x_ref, sc_ref, sh_ref, w_ref, cos64_ref, sin64_ref, cos32_ref, sin32_ref,
                   cosm_ref, sinm_ref, qng_ref, wuq_ref, kvng_ref, wukv_ref,
                   qa_o, ka_o, va_o, mq_o, mk_o, mv_o, dq_o, dk_o, dv_o, nq_o, nk_o, nv_o):
    h = (x_ref[0] * (1.0 + sc_ref[0]) + sh_ref[0]).astype(BF16)

    def seg(name):
        a, b = _SEG[name]
        return _dot(h, w_ref[:, a:b])

    def split_heads(val, out_ref, n, width):
        for i in range(n):
            out_ref[0, i] = val[:, i * width:(i + 1) * width].astype(out_ref.dtype)

    cos64 = cos64_ref[...]
    sin64 = sin64_ref[...]
    cos32 = cos32_ref[...]
    sin32 = sin32_ref[...]
    cosm = cosm_ref[...]
    sinm = sinm_ref[...]

    qa = (seg("qa") * cos64 + seg("qa_r") * sin64) * SWA_SCALE
    split_heads(qa, qa_o, N_HEADS, HEAD_DIM)
    ka = seg("ka") * cos64[:, :128] + seg("ka_r") * sin64[:, :128]
    split_heads(ka, ka_o, SWA_KV_HEADS, HEAD_DIM)
    split_heads(seg("va"), va_o, SWA_KV_HEADS, HEAD_DIM)

    qn = (_rms(seg("mqr")) * qng_ref[...]).astype(BF16)
    uq = _dot(qn, wuq_ref[...])
    half = N_HEADS * MLA_QK_PAD
    for i in range(N_HEADS):
        a = i * MLA_QK_PAD
        mq = (uq[:, a:a + MLA_QK_PAD] * cosm + uq[:, half + a:half + a + MLA_QK_PAD] * sinm) * MLA_SCALE
        mq_o[0, i] = mq.astype(BF16)
    kvn = (_rms(seg("mkvr")) * kvng_ref[...]).astype(BF16)
    ukv = _dot(kvn, wukv_ref[...])
    k_rope = seg("mkr") * cosm + seg("mkr_r") * sinm
    for i in range(N_HEADS):
        a = i * MLA_QK_PAD
        mk_o[0, i] = (ukv[:, a:a + MLA_QK_PAD] + k_rope).astype(BF16)
    mv_o[0] = ukv[:, half:half + N_HEADS * MLA_V].T.astype(BF16)

    dq = (seg("dq") * cos32 + seg("dq_r") * sin32) * DIFF_SCALE
    split_heads(dq, dq_o, 2 * N_HEADS, DIFF_QK)
    dk = seg("dk") * cos32 + seg("dk_r") * sin32
    split_heads(dk, dk_o, 2 * N_HEADS, DIFF_QK)
    dv_o[0] = seg("dv").T.astype(BF16)

    split_heads(seg("nq") * NA_SCALE, nq_o, N_HEADS, HEAD_DIM)
    split_heads(seg("nk"), nk_o, N_HEADS, HEAD_DIM)
    split_heads(seg("nv"), nv_o, N_HEADS, HEAD_DIM)


_HEAD_OUTS = (("qa", N_HEADS, HEAD_DIM), ("ka", SWA_KV_HEADS, HEAD_DIM), ("va", SWA_KV_HEADS, HEAD_DIM),
              ("mq", N_HEADS, MLA_QK_PAD), ("mk", N_HEADS, MLA_QK_PAD), ("mv", None, N_HEADS * MLA_V),
              ("dq", 2 * N_HEADS, DIFF_QK), ("dk", 2 * N_HEADS, DIFF_QK), ("dv", None, N_HEADS * DIFF_V),
              ("nq", N_HEADS, HEAD_DIM), ("nk", N_HEADS, HEAD_DIM), ("nv", N_HEADS, HEAD_DIM))


def _input_projection(x, sc, sh, w_all, tables, qng, wuq, kvng, wukv):
    B, N, D = x.shape
    tm = min(512, N)
    tok = lambda i, b: (b, i, 0)
    vec = lambda i, b: (b, 0, 0)
    tab = lambda i, b: (i, 0)
    const = lambda i, b: (0, 0)
    in_specs = [pl.BlockSpec((1, tm, D), tok),
                pl.BlockSpec((1, 1, D), vec), pl.BlockSpec((1, 1, D), vec),
                pl.BlockSpec(w_all.shape, const)]
    in_specs += [pl.BlockSpec((tm, t.shape[1]), tab) for t in tables]
    in_specs += [pl.BlockSpec(a.shape, const) for a in (qng, wuq, kvng, wukv)]
    out_specs = [pl.BlockSpec((1, w, tm), lambda i, b: (b, 0, i)) if n is None else
                 pl.BlockSpec((1, n, tm, w), lambda i, b: (b, 0, i, 0)) for _, n, w in _HEAD_OUTS]
    out_shape = [jax.ShapeDtypeStruct((B, w, N) if n is None else (B, n, N, w), BF16) for _, n, w in _HEAD_OUTS]
    outs = pl.pallas_call(
        _inproj_kernel,
        grid=(N // tm, B),
        in_specs=in_specs, out_specs=out_specs, out_shape=out_shape,
        compiler_params=_cparams(("arbitrary", "arbitrary")),
        name="input_projection",
    )(x, sc, sh, w_all, *tables, qng, wuq, kvng, wukv)
    return {name: o for (name, _, _), o in zip(_HEAD_OUTS, outs)}


def _softmax_parts(scores, extra_logit=None):
    m = functools.reduce(jnp.maximum, [jnp.max(s, axis=-1, keepdims=True) for s in scores])
    if extra_logit is not None:
        m = jnp.maximum(m, extra_logit)
    ps = [jnp.exp2(s - m) for s in scores]
    denom = functools.reduce(jnp.add, [jnp.sum(p, axis=-1, keepdims=True) for p in ps])
    if extra_logit is not None:
        denom = denom + jnp.exp2(extra_logit - m)
    return ps, denom


def _flash(qs, ctx_ks, ctx_vts, k_ats, vt_ats, n_chunks, dv):
    tq = qs[0].shape[0]

    def update(carry, q, k, vt):
        m, l, acc = carry
        s = _dot_nt(k, q)
        m_new = jnp.maximum(m, jnp.max(s, axis=0, keepdims=True))
        alpha = jnp.exp2(m - m_new)
        p = jnp.exp2(s - m_new)
        l = alpha * l + jnp.sum(p, axis=0, keepdims=True)
        acc = alpha * acc + _dot(vt, p.astype(BF16))
        return m_new, l, acc

    init = (jnp.full((1, tq), NEG_INF, F32), jnp.zeros((1, tq), F32), jnp.zeros((dv, tq), F32))
    carries = tuple(update(init, q, k, vt) for q, k, vt in zip(qs, ctx_ks, ctx_vts))

    def body(i, carries):
        return tuple(update(cr, q, k_at(i), vt_at(i))
                     for cr, q, k_at, vt_at in zip(carries, qs, k_ats, vt_ats))

    carries = lax.fori_loop(0, n_chunks, body, carries, unroll=min(FLASH_UNROLL, n_chunks))
    return [acc / l for _, l, acc in carries]


def _lambda_value(lam_ref, lam_init):
    lv = lam_ref[...]
    return (jnp.exp(jnp.sum(lv[0:1] * lv[1:2], axis=-1, keepdims=True))
            - jnp.exp(jnp.sum(lv[2:3] * lv[3:4], axis=-1, keepdims=True)) + lam_init)


def _sub_ln(o, g_ref, lam_init):
    return _rms(o) * g_ref[...] * (1.0 - lam_init)


SWA_TQ = 2 * SWA_BLOCK
SWA_SPAN = SWA_TQ + 2 * SWA_WINDOW


def _swa_kernel(sink_ref, q_ref, k_ref, v_ref, kc_ref, vc_ref, o_ref):
    L = k_ref.shape[2]
    qb = pl.program_id(1)
    start = pl.multiple_of(jnp.clip(qb * SWA_TQ - SWA_WINDOW, 0, L - SWA_SPAN), SWA_BLOCK)
    group = N_HEADS // SWA_KV_HEADS
    rows = group * SWA_TQ
    k_abs = start + lax.broadcasted_iota(jnp.int32, (rows, SWA_SPAN), 1)
    row = lax.broadcasted_iota(jnp.int32, (rows, SWA_SPAN), 0)
    q_abs = qb * SWA_TQ + jnp.where(row >= SWA_TQ, row - SWA_TQ, row)
    in_window = jnp.abs(k_abs - q_abs) <= SWA_WINDOW
    for hk in range(SWA_KV_HEADS):
        kb = k_ref[0, hk, pl.ds(start, SWA_SPAN), :]
        vb = v_ref[0, hk, pl.ds(start, SWA_SPAN), :]
        q = jnp.concatenate([q_ref[0, hk * group + g] for g in range(group)], axis=0)
        sink = jnp.concatenate([jnp.full((SWA_TQ, 1), sink_ref[hk * group + g], F32) for g in range(group)],
                               axis=0) * LOG2E
        s_band = jnp.where(in_window, _dot_nt(q, kb), NEG_INF)
        s_ctx = _dot_nt(q, kc_ref[0, hk])
        (p_band, p_ctx), denom = _softmax_parts([s_band, s_ctx], sink)
        o = (_dot(p_band.astype(BF16), vb) + _dot(p_ctx.astype(BF16), vc_ref[0, hk])) / denom
        for g in range(group):
            h = hk * group + g
            o_ref[0, :, h * HEAD_DIM:(h + 1) * HEAD_DIM] = o[g * SWA_TQ:(g + 1) * SWA_TQ].astype(o_ref.dtype)


def _swa_attention(sink, hd, hc):
    B, _, L, _ = hd["qa"].shape
    C = hc["ka"].shape[2]
    assert L % SWA_TQ == 0 and L >= SWA_SPAN
    whole = lambda b, i: (b, 0, 0, 0)
    return pl.pallas_call(
        _swa_kernel,
        grid=(B, L // SWA_TQ),
        in_specs=[pl.BlockSpec(memory_space=pltpu.SMEM),
                  pl.BlockSpec((1, N_HEADS, SWA_TQ, HEAD_DIM), lambda b, i: (b, 0, i, 0)),
                  pl.BlockSpec((1, SWA_KV_HEADS, L, HEAD_DIM), whole),
                  pl.BlockSpec((1, SWA_KV_HEADS, L, HEAD_DIM), whole),
                  pl.BlockSpec((1, SWA_KV_HEADS, C, HEAD_DIM), whole),
                  pl.BlockSpec((1, SWA_KV_HEADS, C, HEAD_DIM), whole)],
        out_specs=pl.BlockSpec((1, SWA_TQ, GROUP_WIDTH), lambda b, i: (b, i, 0)),
        out_shape=jax.ShapeDtypeStruct((B, L, GROUP_WIDTH), BF16),
        compiler_params=_cparams(("arbitrary", "arbitrary")),
        name="swa_attention",
    )(sink, hd["qa"], hd["ka"], hd["va"], hc["ka"], hc["va"])


GLOBAL_TQ = 512
GLOBAL_TK = 1024
HEADS_PER_STEP = 2
FLASH_UNROLL = 2


def _chunk_at(ref, lead, tk):
    return lambda i: ref[lead + (pl.ds(pl.multiple_of(i * tk, tk), tk), slice(None))]


def _vt_chunk_at(ref, head, tk, dv):
    return lambda i: ref[0, head * dv:(head + 1) * dv, pl.ds(pl.multiple_of(i * tk, tk), tk)]


def _mla_kernel(q_ref, kc_ref, vtc_ref, k_ref, vt_ref, o_ref, *, tk):
    n_chunks = k_ref.shape[2] // tk
    heads = range(HEADS_PER_STEP)
    outs = _flash([q_ref[0, j] for j in heads], [kc_ref[0, j] for j in heads],
                  [vtc_ref[0, j * MLA_V:(j + 1) * MLA_V, :] for j in heads],
                  [_chunk_at(k_ref, (0, j), tk) for j in heads],
                  [_vt_chunk_at(vt_ref, j, tk, MLA_V) for j in heads], n_chunks, MLA_V)
    o_ref[0] = jnp.concatenate(outs, axis=0).T.astype(o_ref.dtype)


def _mla_attention(hd, hc):
    B, H, L, dk = hd["mq"].shape
    C = hc["mk"].shape[2]
    tq = min(GLOBAL_TQ, L)
    tk = min(GLOBAL_TK, L)
    hp = HEADS_PER_STEP
    whole = lambda b, h, i: (b, h, 0, 0)
    whole_t = lambda b, h, i: (b, h, 0)
    return pl.pallas_call(
        functools.partial(_mla_kernel, tk=tk),
        grid=(B, H // hp, L // tq),
        in_specs=[pl.BlockSpec((1, hp, tq, dk), lambda b, h, i: (b, h, i, 0)),
                  pl.BlockSpec((1, hp, C, dk), whole),
                  pl.BlockSpec((1, hp * MLA_V, C), whole_t),
                  pl.BlockSpec((1, hp, L, dk), whole),
                  pl.BlockSpec((1, hp * MLA_V, L), whole_t)],
        out_specs=pl.BlockSpec((1, tq, hp * MLA_V), lambda b, h, i: (b, i, h)),
        out_shape=jax.ShapeDtypeStruct((B, L, H * MLA_V), BF16),
        compiler_params=_cparams(("arbitrary", "arbitrary", "arbitrary")),
        name="mla_attention",
    )(hd["mq"], hc["mk"], hc["mv"], hd["mk"], hd["mv"])


def _diff_kernel(lam_ref, g_ref, q_ref, kc_ref, vtc_ref, k_ref, vt_ref, o_ref, *, tk, lam_init):
    n_chunks = k_ref.shape[2] // tk
    lam = _lambda_value(lam_ref, lam_init)
    heads = []
    for h in range(HEADS_PER_STEP):
        branches = (2 * h, 2 * h + 1)
        vtc = vtc_ref[0, h * DIFF_V:(h + 1) * DIFF_V, :]
        o1, o2 = _flash([q_ref[0, j] for j in branches], [kc_ref[0, j] for j in branches], [vtc] * 2,
                        [_chunk_at(k_ref, (0, j), tk) for j in branches],
                        [_vt_chunk_at(vt_ref, h, tk, DIFF_V)] * 2, n_chunks, DIFF_V)
        o = o1 - lam * o2
        o = o * lax.rsqrt(jnp.mean(o * o, axis=0, keepdims=True) + NORM_EPS)
        heads.append(o * g_ref[...] * (1.0 - lam_init))
    o_ref[0] = jnp.concatenate(heads, axis=0).T.astype(o_ref.dtype)


def _diff_attention(lam_vecs, subln_g_col, hd, hc, lam_init):
    B, _, L, dk = hd["dq"].shape
    C = hc["dk"].shape[2]
    tq = min(GLOBAL_TQ, L)
    tk = min(GLOBAL_TK, L)
    hp = HEADS_PER_STEP
    whole = lambda b, h, i: (b, h, 0, 0)
    whole_t = lambda b, h, i: (b, h, 0)
    const = lambda b, h, i: (0, 0)
    return pl.pallas_call(
        functools.partial(_diff_kernel, tk=tk, lam_init=lam_init),
        grid=(B, N_HEADS // hp, L // tq),
        in_specs=[pl.BlockSpec(lam_vecs.shape, const),
                  pl.BlockSpec(subln_g_col.shape, const),
                  pl.BlockSpec((1, 2 * hp, tq, dk), lambda b, h, i: (b, h, i, 0)),
                  pl.BlockSpec((1, 2 * hp, C, dk), whole),
                  pl.BlockSpec((1, hp * DIFF_V, C), whole_t),
                  pl.BlockSpec((1, 2 * hp, L, dk), whole),
                  pl.BlockSpec((1, hp * DIFF_V, L), whole_t)],
        out_specs=pl.BlockSpec((1, tq, hp * DIFF_V), lambda b, h, i: (b, i, h)),
        out_shape=jax.ShapeDtypeStruct((B, L, N_HEADS * DIFF_V), BF16),
        compiler_params=_cparams(("arbitrary", "arbitrary", "arbitrary")),
        name="diff_attention",
    )(lam_vecs, subln_g_col, hd["dq"], hc["dk"], hc["dv"], hd["dk"], hd["dv"])


def _na_plan(rows):
    kh = min(NA_KH, rows)
    n_row_off = 2 * NA_KH - 1
    col = np.arange(GRID_W)
    col_start = np.clip(col - NA_KW // 2, 0, GRID_W - NA_KW)
    col_ok = (col[None, :] >= col_start[:, None]) & (col[None, :] < col_start[:, None] + NA_KW)
    col_off = col[None, :] - col[:, None] + (NA_KW - 1)
    col_onehot = ((col_off[None] == np.arange(2 * NA_KW - 1)[:, None, None]) & col_ok[None]).astype(np.float32)
    patterns, starts, ids = {}, [], []
    for blk in range(rows // NA_Q_ROWS):
        r0 = blk * NA_Q_ROWS
        ks = int(np.clip(r0 - kh // 2, 0, rows - NA_K_ROWS))
        q_row = r0 + np.arange(NA_Q_ROWS)
        k_row = ks + np.arange(NA_K_ROWS)
        r_start = np.clip(q_row - kh // 2, 0, rows - kh)
        row_ok = (k_row[None, :] >= r_start[:, None]) & (k_row[None, :] < r_start[:, None] + kh)
        row_off = k_row[None, :] - q_row[:, None] + (NA_KH - 1)
        sel = np.where(row_ok, row_off, n_row_off)
        key = tuple(int(v) for v in sel.reshape(-1))
        if key not in patterns:
            patterns[key] = (len(patterns), sel)
        starts.append(ks)
        ids.append(patterns[key][0])
    row_sel = np.stack([p[1] for p in sorted(patterns.values(), key=lambda p: p[0])])
    return np.asarray(starts, np.int32), np.asarray(ids, np.int32), row_sel, col_onehot, col_ok


def _na_kernel(ks_ref, pid_ref, q_ref, k_ref, v_ref, kc_ref, vc_ref, bias_ref, o_ref):
    del pid_ref
    k_len = NA_K_ROWS * GRID_W
    start = pl.multiple_of(ks_ref[pl.program_id(1)] * GRID_W, GRID_W)
    for h in range(N_HEADS):
        q = q_ref[0, h]
        kw = k_ref[0, h, pl.ds(start, k_len), :]
        vw = v_ref[0, h, pl.ds(start, k_len), :]
        s_win = _dot_nt(q, kw) + bias_ref[0, h]
        s_ctx = _dot_nt(q, kc_ref[0, h])
        (p_win, p_ctx), denom = _softmax_parts([s_win, s_ctx])
        o = _dot(p_win.astype(BF16), vw) + _dot(p_ctx.astype(BF16), vc_ref[0, h])
        o_ref[0, :, h * HEAD_DIM:(h + 1) * HEAD_DIM] = (o / denom).astype(o_ref.dtype)


def _na_attention(rpb, hd, hc):
    B, H, L, _ = hd["nq"].shape
    C = hc["nk"].shape[2]
    rows = L // GRID_W
    starts, ids, row_sel, col_onehot, col_ok = _na_plan(rows)
    q_len = NA_Q_ROWS * GRID_W
    k_len = NA_K_ROWS * GRID_W
    slab = jnp.einsum("hrj,jqk->hrqk", rpb * LOG2E, col_onehot, precision=lax.Precision.HIGHEST)
    slab = jnp.where(col_ok[None, None], slab, NEG_INF)
    slab = jnp.concatenate([slab, jnp.full_like(slab[:, :1], NEG_INF)], axis=1)
    bias = jnp.take(slab, jnp.asarray(row_sel), axis=1)
    bias = jnp.transpose(bias, (1, 0, 2, 4, 3, 5)).reshape(row_sel.shape[0], H, q_len, k_len)
    whole = lambda b, i, ks, pid: (b, 0, 0, 0)
    grid_spec = pltpu.PrefetchScalarGridSpec(
        num_scalar_prefetch=2,
        grid=(B, rows // NA_Q_ROWS),
        in_specs=[pl.BlockSpec((1, H, q_len, HEAD_DIM), lambda b, i, ks, pid: (b, 0, i, 0)),
                  pl.BlockSpec((1, H, L, HEAD_DIM), whole),
                  pl.BlockSpec((1, H, L, HEAD_DIM), whole),
                  pl.BlockSpec((1, H, C, HEAD_DIM), whole),
                  pl.BlockSpec((1, H, C, HEAD_DIM), whole),
                  pl.BlockSpec((1, H, q_len, k_len), lambda b, i, ks, pid: (pid[i], 0, 0, 0))],
        out_specs=pl.BlockSpec((1, q_len, GROUP_WIDTH), lambda b, i, ks, pid: (b, i, 0)))
    return pl.pallas_call(
        _na_kernel,
        grid_spec=grid_spec,
        out_shape=jax.ShapeDtypeStruct((B, L, GROUP_WIDTH), BF16),
        compiler_params=_cparams(("arbitrary", "arbitrary")),
        name="neighborhood_attention",
    )(jnp.asarray(starts), jnp.asarray(ids), hd["nq"], hd["nk"], hd["nv"], hc["nk"], hc["nv"], bias)


def _ctx_attn_kernel(sink_ref, lam_ref, g_ref, qa_ref, ka_ref, va_ref, mq_ref, mk_ref, mv_ref,
                     dq_ref, dk_ref, dv_ref, nq_ref, nk_ref, nv_ref,
                     ya_ref, yb_ref, yc_ref, yd_ref, *, lam_init):
    def attend(q, k, v, extra=None, v_transposed=False):
        (p,), denom = _softmax_parts([_dot_nt(q, k)], extra)
        p = p.astype(BF16)
        return (_dot_nt(p, v) if v_transposed else _dot(p, v)) / denom

    group = N_HEADS // SWA_KV_HEADS
    lam = _lambda_value(lam_ref, lam_init)
    for h in range(N_HEADS):
        lanes = slice(h * HEAD_DIM, (h + 1) * HEAD_DIM)
        ya_ref[0, :, lanes] = attend(qa_ref[0, h], ka_ref[0, h // group], va_ref[0, h // group],
                                     sink_ref[h] * LOG2E).astype(ya_ref.dtype)
        mvt = mv_ref[0, h * MLA_V:(h + 1) * MLA_V, :]
        dvt = dv_ref[0, h * DIFF_V:(h + 1) * DIFF_V, :]
        yb_ref[0, :, lanes] = attend(mq_ref[0, h], mk_ref[0, h], mvt, v_transposed=True).astype(yb_ref.dtype)
        o = (attend(dq_ref[0, 2 * h], dk_ref[0, 2 * h], dvt, v_transposed=True)
             - lam * attend(dq_ref[0, 2 * h + 1], dk_ref[0, 2 * h + 1], dvt, v_transposed=True))
        yc_ref[0, :, lanes] = _sub_ln(o, g_ref, lam_init).astype(yc_ref.dtype)
        yd_ref[0, :, lanes] = attend(nq_ref[0, h], nk_ref[0, h], nv_ref[0, h]).astype(yd_ref.dtype)


def _ctx_attention(sink, lam_vecs, subln_g, hc, lam_init):
    names = ("qa", "ka", "va", "mq", "mk", "mv", "dq", "dk", "dv", "nq", "nk", "nv")
    B, _, C, _ = hc["qa"].shape
    whole4 = lambda b: (b, 0, 0, 0)
    const = lambda b: (0, 0)
    in_specs = [pl.BlockSpec(memory_space=pltpu.SMEM),
                pl.BlockSpec(lam_vecs.shape, const), pl.BlockSpec(subln_g.shape, const)]
    in_specs += [pl.BlockSpec((1,) + hc[n].shape[1:], whole4 if hc[n].ndim == 4 else (lambda b: (b, 0, 0)))
                 for n in names]
    tok = pl.BlockSpec((1, C, GROUP_WIDTH), lambda b: (b, 0, 0))
    return pl.pallas_call(
        functools.partial(_ctx_attn_kernel, lam_init=lam_init),
        grid=(B,),
        in_specs=in_specs,
        out_specs=[tok] * N_MIXERS,
        out_shape=[jax.ShapeDtypeStruct((B, C, GROUP_WIDTH), BF16)] * N_MIXERS,
        compiler_params=_cparams(("arbitrary",)),
        name="context_attention",
    )(sink, lam_vecs, subln_g, *[hc[n] for n in names])


def _route(r):
    lane = lax.broadcasted_iota(jnp.int32, r.shape, 1)
    lane_f = lane.astype(F32)
    big = float(ROUTER_LANES)
    is_grp = (lane >= N_EXPERTS) & (lane < N_EXPERTS + N_GROUPS)
    g_log = jnp.where(is_grp, r, NEG_INF)
    g_max = jnp.max(g_log, axis=-1, keepdims=True)
    g_val = 1.0 / jnp.sum(jnp.exp(g_log - g_max), axis=-1, keepdims=True)
    g_idx = jnp.min(jnp.where(g_log == g_max, lane_f, big), axis=-1, keepdims=True) - float(N_EXPERTS)
    lane_grp = lax.shift_right_logical(lane, int(math.log2(EXPERTS_PER_GROUP))).astype(F32)
    in_grp = (lane < N_EXPERTS) & (lane_grp == g_idx)
    e_log = jnp.where(in_grp, r, NEG_INF)
    e_max = jnp.max(e_log, axis=-1, keepdims=True)
    i1 = jnp.min(jnp.where(e_log == e_max, lane_f, big), axis=-1, keepdims=True)
    e_rest = jnp.where(lane_f == i1, NEG_INF, e_log)
    e_max2 = jnp.max(e_rest, axis=-1, keepdims=True)
    i2 = jnp.min(jnp.where(e_rest == e_max2, lane_f, big), axis=-1, keepdims=True)
    p2 = jnp.exp(e_max2 - e_max)
    w1 = 1.0 / (1.0 + p2)
    w2 = p2 / (1.0 + p2)
    return g_val * jnp.where(lane_f == i1, w1, jnp.where(lane_f == i2, w2, 0.0))


def _split_bf16(v):
    hi = v.astype(BF16)
    return hi, (v - hi.astype(F32)).astype(BF16)


def _outproj_kernel(x_ref, a_ref, b_ref, c_ref, d_ref, wo_ref, g1_ref, sc2_ref, sh2_ref,
                    lng_ref, lnb_ref, wr_ref, br_ref, x1_ref, h2_ref, gate_ref, *, alpha):
    y = _dot(a_ref[0], wo_ref[0:GROUP_WIDTH, :])
    for i, m_ref in enumerate((b_ref, c_ref, d_ref), start=1):
        y += _dot(m_ref[0], wo_ref[i * GROUP_WIDTH:(i + 1) * GROUP_WIDTH, :])
    x1 = _layer_norm(alpha * x_ref[0] + g1_ref[0] * y, lng_ref[...], lnb_ref[...])
    x1_ref[0] = x1
    h2 = x1 * (1.0 + sc2_ref[0]) + sh2_ref[0]
    h2_ref[0] = h2.astype(h2_ref.dtype)
    h_hi, h_lo = _split_bf16(h2)
    r = (_dot(h_hi, wr_ref[0]) + _dot(h_lo, wr_ref[0]) + _dot(h_hi, wr_ref[1])) + br_ref[...]
    gate_ref[0] = _route(r)


def _output_projection(x, att, wo, g1, sc2, sh2, ln_g, ln_b, wr, br, alpha):
    B, N, D = x.shape
    tm = min(512, N)
    tok = lambda w: pl.BlockSpec((1, tm, w), lambda b, i: (b, i, 0))
    vec = pl.BlockSpec((1, 1, D), lambda b, i: (b, 0, 0))
    const = lambda a: pl.BlockSpec(a.shape, lambda b, i: (0,) * a.ndim)
    return pl.pallas_call(
        functools.partial(_outproj_kernel, alpha=alpha),
        grid=(B, N // tm),
        in_specs=[tok(D)] + [tok(GROUP_WIDTH)] * N_MIXERS + [const(wo), vec, vec, vec,
                  const(ln_g), const(ln_b), const(wr), const(br)],
        out_specs=[tok(D), tok(D), tok(ROUTER_LANES)],
        out_shape=[jax.ShapeDtypeStruct((B, N, D), F32), jax.ShapeDtypeStruct((B, N, D), BF16),
                   jax.ShapeDtypeStruct((B, N, ROUTER_LANES), F32)],
        compiler_params=_cparams(("arbitrary", "arbitrary")),
        name="output_projection",
    )(x, att[0], att[1], att[2], att[3], wo, g1, sc2, sh2, ln_g, ln_b, wr, br)


def _moe_kernel(h_ref, gate_ref, x1_ref, wg_ref, wu_ref, wd_ref, g2_ref, lng_ref, lnb_ref,
                o_ref, acc_ref, *, alpha):
    e = pl.program_id(2)

    @pl.when(e == 0)
    def _():
        acc_ref[...] = jnp.zeros_like(acc_ref)

    h = h_ref[0]
    gate_pre = _dot(h, wg_ref[0])
    hid = gate_pre * jax.nn.sigmoid(gate_pre) * _dot(h, wu_ref[0])
    gates = gate_ref[0]
    lane = lax.broadcasted_iota(jnp.int32, gates.shape, 1)
    g_col = jnp.sum(jnp.where(lane == e, gates, 0.0), axis=-1, keepdims=True)
    acc_ref[...] += _dot((hid * g_col).astype(BF16), wd_ref[0])

    @pl.when(e == pl.num_programs(2) - 1)
    def _():
        o_ref[0] = _layer_norm(alpha * x1_ref[0] + g2_ref[0] * acc_ref[...], lng_ref[...], lnb_ref[...])


def _moe(h2, gates, x1, wg, wu, wd, g2, ln_g, ln_b, alpha):
    B, N, D = x1.shape
    tm = min(1024, N)
    n_e, _, hid = wg.shape
    tok = lambda w: pl.BlockSpec((1, tm, w), lambda b, i, e: (b, i, 0))
    const = lambda a: pl.BlockSpec(a.shape, lambda b, i, e: (0, 0))
    return pl.pallas_call(
        functools.partial(_moe_kernel, alpha=alpha),
        grid=(B, N // tm, n_e),
        in_specs=[tok(D), tok(ROUTER_LANES), tok(D),
                  pl.BlockSpec((1, D, hid), lambda b, i, e: (e, 0, 0)),
                  pl.BlockSpec((1, D, hid), lambda b, i, e: (e, 0, 0)),
                  pl.BlockSpec((1, hid, D), lambda b, i, e: (e, 0, 0)),
                  pl.BlockSpec((1, 1, D), lambda b, i, e: (b, 0, 0)),
                  const(ln_g), const(ln_b)],
        out_specs=tok(D),
        out_shape=jax.ShapeDtypeStruct((B, N, D), F32),
        scratch_shapes=[pltpu.VMEM((tm, D), F32)],
        compiler_params=_cparams(("arbitrary", "arbitrary", "arbitrary")),
        name="moe",
    )(h2, gates, x1, wg, wu, wd, g2, ln_g, ln_b)


def _rot_cols(w, d):
    k, n = w.shape
    q = d // 4
    w4 = w.reshape(k, n // d, 4, q)
    return jnp.stack([-w4[:, :, 1], w4[:, :, 0], -w4[:, :, 3], w4[:, :, 2]], axis=2).reshape(k, n)


def _rope_tables(L):
    t = jnp.arange(L, dtype=jnp.int32)
    rows = (t // GRID_W).astype(F32)
    cols = (t % GRID_W).astype(F32)

    def cos_sin(d):
        q = d // 4
        inv = ROPE_BASE ** (-jnp.arange(q, dtype=F32) / q)
        ar = rows[:, None] * inv[None, :]
        ac = cols[:, None] * inv[None, :]
        return (jnp.concatenate([jnp.cos(ar), jnp.cos(ar), jnp.cos(ac), jnp.cos(ac)], -1),
                jnp.concatenate([jnp.sin(ar), jnp.sin(ar), jnp.sin(ac), jnp.sin(ac)], -1))

    c64, s64 = cos_sin(HEAD_DIM)
    c32, s32 = cos_sin(MLA_ROPE)
    ones = jnp.ones((L, MLA_NOPE), F32)
    zeros_n = jnp.zeros((L, MLA_NOPE), F32)
    zeros_p = jnp.zeros((L, MLA_QK_PAD - MLA_NOPE - MLA_ROPE), F32)
    return (jnp.tile(c64, (1, N_HEADS)), jnp.tile(s64, (1, N_HEADS)),
            jnp.tile(c32, (1, 2 * N_HEADS)), jnp.tile(s32, (1, 2 * N_HEADS)),
            jnp.concatenate([ones, c32, zeros_p], -1), jnp.concatenate([zeros_n, s32, zeros_p], -1))


def _identity_tables(C):
    one = jnp.ones((C, GROUP_WIDTH), F32)
    zero = jnp.zeros((C, GROUP_WIDTH), F32)
    pad = MLA_QK_PAD - MLA_NOPE - MLA_ROPE
    cosm = jnp.concatenate([jnp.ones((C, MLA_NOPE + MLA_ROPE), F32), jnp.zeros((C, pad), F32)], -1)
    return one, zero, one, zero, cosm, jnp.zeros((C, MLA_QK_PAD), F32)


def _fused_in_weight(w_in):
    d = w_in.shape[0]
    p = jnp.split(w_in, IN_CUTS, axis=1)
    z = lambda n: jnp.zeros((d, n), w_in.dtype)
    pad_r = MLA_QK_PAD - MLA_NOPE - MLA_ROPE
    parts = {"qa": p[0], "qa_r": _rot_cols(p[0], HEAD_DIM), "ka": p[1], "ka_r": _rot_cols(p[1], HEAD_DIM),
             "va": p[2], "mqr": p[3], "mkvr": p[4],
             "mkr": jnp.concatenate([z(MLA_NOPE), p[5], z(pad_r)], 1),
             "mkr_r": jnp.concatenate([z(MLA_NOPE), _rot_cols(p[5], MLA_ROPE), z(pad_r)], 1),
             "dq": p[6], "dq_r": _rot_cols(p[6], DIFF_QK), "dk": p[7], "dk_r": _rot_cols(p[7], DIFF_QK),
             "dv": p[8], "nq": p[9], "nk": p[10], "nv": p[11]}
    return jnp.concatenate([parts[n] for n, _ in _SEG_LAYOUT], axis=1).astype(BF16)


def _mla_up_weights(w_uq, w_ukv):
    rq = w_uq.shape[0]
    pad_r = MLA_QK_PAD - MLA_NOPE - MLA_ROPE
    wq = w_uq.reshape(rq, N_HEADS, MLA_NOPE + MLA_ROPE)
    zq = lambda n: jnp.zeros((rq, N_HEADS, n), w_uq.dtype)
    rope_rot = _rot_cols(wq[:, :, MLA_NOPE:].reshape(rq, -1), MLA_ROPE).reshape(rq, N_HEADS, MLA_ROPE)
    main = jnp.concatenate([wq, zq(pad_r)], -1).reshape(rq, -1)
    rot = jnp.concatenate([zq(MLA_NOPE), rope_rot, zq(pad_r)], -1).reshape(rq, -1)
    wuq_ext = jnp.concatenate([main, rot], 1).astype(BF16)
    rk = w_ukv.shape[0]
    wkv = w_ukv.reshape(rk, N_HEADS, MLA_NOPE + MLA_V)
    k_part = jnp.concatenate([wkv[:, :, :MLA_NOPE],
                              jnp.zeros((rk, N_HEADS, MLA_QK_PAD - MLA_NOPE), w_ukv.dtype)], -1)
    wukv_ext = jnp.concatenate([k_part.reshape(rk, -1), wkv[:, :, MLA_NOPE:].reshape(rk, -1)], 1).astype(BF16)
    return wuq_ext, wukv_ext


def kernel(x, c, ctx, c_ctx, w_mod, b_mod, w_in, attn_sink, mla_q_norm, w_uq, mla_kv_norm, w_ukv,
           lam_q1, lam_k1, lam_q2, lam_k2, diff_subln, na_rpb, w_out, ln1_g, ln1_b,
           w_group, b_group, w_router, b_router, w_gate, w_up, w_down, ln2_g, ln2_b):
    B, L, D = x.shape
    C = ctx.shape[1]
    depth = w_mod.shape[0]
    alpha = (2 * depth) ** 0.25
    assert D == D_MODEL and B + 1 <= 8 and L % (NA_Q_ROWS * GRID_W) == 0

    cvec = jnp.concatenate([c, c_ctx[None, :], jnp.zeros((8 - B - 1, D), F32)], axis=0)
    mod = _modulation(cvec, w_mod, b_mod)
    lat_tables = _rope_tables(L)
    ctx_tables = _identity_tables(C)
    row = lambda a: a.reshape(1, -1)

    xc = ctx
    for l in range(depth):
        need_ctx = l < depth - 1
        lam_init = 0.8 - 0.6 * math.exp(-0.3 * l)
        chunks = [mod[l, :, i * D:(i + 1) * D] for i in range(6)]
        sh1, sc1, g1, sh2, sc2, g2 = [m[:B, None, :] for m in chunks]
        sh1c, sc1c, g1c, sh2c, sc2c, g2c = [jnp.broadcast_to(m[B:B + 1, None, :], (B, 1, D)) for m in chunks]

        w_all = _fused_in_weight(w_in[l])
        wuq_ext, wukv_ext = _mla_up_weights(w_uq[l], w_ukv[l])
        proj = functools.partial(_input_projection, w_all=w_all, qng=row(mla_q_norm[l]), wuq=wuq_ext,
                                 kvng=row(mla_kv_norm[l]), wukv=wukv_ext)
        hd = proj(x, sc1, sh1, tables=lat_tables)
        hc = proj(xc, sc1c, sh1c, tables=ctx_tables)

        lam_vecs = jnp.stack([lam_q1[l], lam_k1[l], lam_q2[l], lam_k2[l]])
        subln_g = row(diff_subln[l])
        att = (_swa_attention(attn_sink[l], hd, hc), _mla_attention(hd, hc),
               _diff_attention(lam_vecs, diff_subln[l].reshape(-1, 1), hd, hc, lam_init),
               _na_attention(na_rpb[l], hd, hc))

        wo = w_out[l].astype(BF16)
        pad = ROUTER_LANES - N_EXPERTS - N_GROUPS
        wr = jnp.stack(_split_bf16(jnp.concatenate([w_router[l], w_group[l], jnp.zeros((D, pad), F32)], axis=1)))
        br = row(jnp.concatenate([b_router[l], b_group[l], jnp.zeros((pad,), F32)]))
        wg = w_gate[l].reshape(N_EXPERTS, D, EXPERT_HIDDEN).astype(BF16)
        wu = w_up[l].reshape(N_EXPERTS, D, EXPERT_HIDDEN).astype(BF16)
        wd = w_down[l].reshape(N_EXPERTS, EXPERT_HIDDEN, D).astype(BF16)
        post = functools.partial(_output_projection, wo=wo, ln_g=row(ln1_g[l]), ln_b=row(ln1_b[l]),
                                 wr=wr, br=br, alpha=alpha)
        ffn = functools.partial(_moe, wg=wg, wu=wu, wd=wd, ln_g=row(ln2_g[l]), ln_b=row(ln2_b[l]), alpha=alpha)

        x1, h2, gates = post(x, att, g1=g1, sc2=sc2, sh2=sh2)
        x = ffn(h2, gates, x1, g2=g2)
        if need_ctx:
            att_c = _ctx_attention(attn_sink[l], lam_vecs, subln_g, hc, lam_init)
            xc1, hc2, gates_c = post(xc, att_c, g1=g1c, sc2=sc2c, sh2=sh2c)
            xc = ffn(hc2, gates_c, xc1, g2=g2c)
    return x
```

```python
import functools
import math

import jax
import jax.numpy as jnp
import numpy as np
from jax import lax
from jax.experimental import pallas as pl
from jax.experimental.pallas import tpu as pltpu

F32 = jnp.float32
BF16 = jnp.bfloat16

D_MODEL = 1024
GRID_W = 64
HEAD_DIM = 64
N_HEADS = 4
N_MIXERS = 4
GROUP_WIDTH = N_HEADS * HEAD_DIM
SWA_KV_HEADS = 2
SWA_WINDOW = 128
SWA_BLOCK = 128
MLA_Q_RANK = 256
MLA_KV_RANK = 128
MLA_NOPE = 64
MLA_ROPE = 32
MLA_V = 64
MLA_QK_PAD = 128
DIFF_QK = 32
DIFF_V = 64
NA_KH = 8
NA_KW = 16
NA_Q_ROWS = 4
NA_K_ROWS = NA_Q_ROWS + NA_KH
N_GROUPS = 4
EXPERTS_PER_GROUP = 8
N_EXPERTS = N_GROUPS * EXPERTS_PER_GROUP
EXPERT_HIDDEN = 256
ROUTER_LANES = 128
ROPE_BASE = 10000.0
NORM_EPS = 1e-5
NEG_INF = -1e30
LOG2E = math.log2(math.e)
SWA_SCALE = HEAD_DIM ** -0.5 * LOG2E
MLA_SCALE = (MLA_NOPE + MLA_ROPE) ** -0.5 * LOG2E
DIFF_SCALE = DIFF_QK ** -0.5 * LOG2E
NA_SCALE = HEAD_DIM ** -0.5 * LOG2E
IN_SPLITS = (GROUP_WIDTH, SWA_KV_HEADS * HEAD_DIM, SWA_KV_HEADS * HEAD_DIM,
             MLA_Q_RANK, MLA_KV_RANK, MLA_ROPE,
             N_HEADS * 2 * DIFF_QK, N_HEADS * 2 * DIFF_QK, N_HEADS * DIFF_V,
             GROUP_WIDTH, GROUP_WIDTH, GROUP_WIDTH)
IN_CUTS = tuple(int(v) for v in np.cumsum(IN_SPLITS)[:-1])

_SEG_LAYOUT = (("qa", 256), ("qa_r", 256), ("ka", 128), ("ka_r", 128), ("va", 128),
               ("mqr", 256), ("mkvr", 128), ("mkr", 128), ("mkr_r", 128),
               ("dq", 256), ("dq_r", 256), ("dk", 256), ("dk_r", 256), ("dv", 256),
               ("nq", 256), ("nk", 256), ("nv", 256))
_SEG = {}
_off = 0
for _name, _w in _SEG_LAYOUT:
    _SEG[_name] = (_off, _off + _w)
    _off += _w
W_ALL_COLS = _off

VMEM_LIMIT_BYTES = 56 * 1024 * 1024


def _cparams(sem):
    return pltpu.CompilerParams(dimension_semantics=sem, vmem_limit_bytes=VMEM_LIMIT_BYTES)


def _dot(a, b):
    return jnp.dot(a, b, preferred_element_type=F32)


def _dot_nt(a, b):
    return lax.dot_general(a, b, (((1,), (1,)), ((), ())), preferred_element_type=F32)


def _rms(x):
    return x * lax.rsqrt(jnp.mean(x * x, axis=-1, keepdims=True) + NORM_EPS)


def _layer_norm(z, g, b):
    mu = jnp.mean(z, axis=-1, keepdims=True)
    zc = z - mu
    var = jnp.mean(zc * zc, axis=-1, keepdims=True)
    return zc * lax.rsqrt(var + NORM_EPS) * g + b


def _mod_kernel(c_ref, w_ref, b_ref, o_ref):
    c = c_ref[...]
    act = c * jax.nn.sigmoid(c)
    o_ref[0] = jnp.dot(act, w_ref[0], preferred_element_type=F32,
                       precision=lax.Precision.HIGHEST) + b_ref[0]


def _modulation(cvec, w_mod, b_mod):
    depth, d, n = w_mod.shape
    tn = 1024
    return pl.pallas_call(
        _mod_kernel,
        grid=(depth, n // tn),
        in_specs=[pl.BlockSpec((8, d), lambda l, j: (0, 0)),
                  pl.BlockSpec((1, d, tn), lambda l, j: (l, 0, j)),
                  pl.BlockSpec((1, 1, tn), lambda l, j: (l, 0, j))],
        out_specs=pl.BlockSpec((1, 8, tn), lambda l, j: (l, 0, j)),
        out_shape=jax.ShapeDtypeStruct((depth, 8, n), F32),
        compiler_params=_cparams(("arbitrary", "arbitrary")),
        name="modulation",
    )(cvec, w_mod, b_mod.reshape(depth, 1, n))


def _inproj_kernel(x_ref, sc_ref, sh_ref, w_ref, cos64_ref, sin64_ref, cos32_ref, sin32_ref,
                   cosm_ref, sinm_ref, qng_ref, wuq_ref, kvng_ref, wukv_ref,
                   qa_o, ka_o, va_o, mq_o, mk_o, mv_o, dq_o, dk_o, dv_o, nq_o, nk_o, nv_o):
    h = (x_ref[0] * (1.0 + sc_ref[0]) + sh_ref[0]).astype(BF16)

    def seg(name):
        a, b = _SEG[name]
        return _dot(h, w_ref[:, a:b])

    def split_heads(val, out_ref, n, width):
        for i in range(n):
            out_ref[0, i] = val[:, i * width:(i + 1) * width].astype(out_ref.dtype)

    cos64 = cos64_ref[...]
    sin64 = sin64_ref[...]
    cos32 = cos32_ref[...]
    sin32 = sin32_ref[...]
    cosm = cosm_ref[...]
    sinm = sinm_ref[...]

    qa = (seg("qa") * cos64 + seg("qa_r") * sin64) * SWA_SCALE
    split_heads(qa, qa_o, N_HEADS, HEAD_DIM)
    ka = seg("ka") * cos64[:, :128] + seg("ka_r") * sin64[:, :128]
    split_heads(ka, ka_o, SWA_KV_HEADS, HEAD_DIM)
    split_heads(seg("va"), va_o, SWA_KV_HEADS, HEAD_DIM)

    qn = (_rms(seg("mqr")) * qng_ref[...]).astype(BF16)
    uq = _dot(qn, wuq_ref[...])
    half = N_HEADS * MLA_QK_PAD
    for i in range(N_HEADS):
        a = i * MLA_QK_PAD
        mq = (uq[:, a:a + MLA_QK_PAD] * cosm + uq[:, half + a:half + a + MLA_QK_PAD] * sinm) * MLA_SCALE
        mq_o[0, i] = mq.astype(BF16)
    kvn = (_rms(seg("mkvr")) * kvng_ref[...]).astype(BF16)
    ukv = _dot(kvn, wukv_ref[...])
    k_rope = seg("mkr") * cosm + seg("mkr_r") * sinm
    for i in range(N_HEADS):
        a = i * MLA_QK_PAD
        mk_o[0, i] = (ukv[:, a:a + MLA_QK_PAD] + k_rope).astype(BF16)
        b = half + i * MLA_V
        mv_o[0, i] = ukv[:, b:b + MLA_V].astype(BF16)

    dq = (seg("dq") * cos32 + seg("dq_r") * sin32) * DIFF_SCALE
    split_heads(dq, dq_o, 2 * N_HEADS, DIFF_QK)
    dk = seg("dk") * cos32 + seg("dk_r") * sin32
    split_heads(dk, dk_o, 2 * N_HEADS, DIFF_QK)
    split_heads(seg("dv"), dv_o, N_HEADS, DIFF_V)

    split_heads(seg("nq") * NA_SCALE, nq_o, N_HEADS, HEAD_DIM)
    split_heads(seg("nk"), nk_o, N_HEADS, HEAD_DIM)
    split_heads(seg("nv"), nv_o, N_HEADS, HEAD_DIM)


_HEAD_OUTS = (("qa", N_HEADS, HEAD_DIM), ("ka", SWA_KV_HEADS, HEAD_DIM), ("va", SWA_KV_HEADS, HEAD_DIM),
              ("mq", N_HEADS, MLA_QK_PAD), ("mk", N_HEADS, MLA_QK_PAD), ("mv", N_HEADS, MLA_V),
              ("dq", 2 * N_HEADS, DIFF_QK), ("dk", 2 * N_HEADS, DIFF_QK), ("dv", N_HEADS, DIFF_V),
              ("nq", N_HEADS, HEAD_DIM), ("nk", N_HEADS, HEAD_DIM), ("nv", N_HEADS, HEAD_DIM))


def _input_projection(x, sc, sh, w_all, tables, qng, wuq, kvng, wukv):
    B, N, D = x.shape
    tm = min(512, N)
    tok = lambda i, b: (b, i, 0)
    vec = lambda i, b: (b, 0, 0)
    tab = lambda i, b: (i, 0)
    const = lambda i, b: (0, 0)
    in_specs = [pl.BlockSpec((1, tm, D), tok),
                pl.BlockSpec((1, 1, D), vec), pl.BlockSpec((1, 1, D), vec),
                pl.BlockSpec(w_all.shape, const)]
    in_specs += [pl.BlockSpec((tm, t.shape[1]), tab) for t in tables]
    in_specs += [pl.BlockSpec(a.shape, const) for a in (qng, wuq, kvng, wukv)]
    out_specs = [pl.BlockSpec((1, n, tm, w), lambda i, b: (b, 0, i, 0)) for _, n, w in _HEAD_OUTS]
    out_shape = [jax.ShapeDtypeStruct((B, n, N, w), BF16) for _, n, w in _HEAD_OUTS]
    outs = pl.pallas_call(
        _inproj_kernel,
        grid=(N // tm, B),
        in_specs=in_specs, out_specs=out_specs, out_shape=out_shape,
        compiler_params=_cparams(("arbitrary", "arbitrary")),
        name="input_projection",
    )(x, sc, sh, w_all, *tables, qng, wuq, kvng, wukv)
    return {name: o for (name, _, _), o in zip(_HEAD_OUTS, outs)}


def _softmax_parts(scores, extra_logit=None):
    m = functools.reduce(jnp.maximum, [jnp.max(s, axis=-1, keepdims=True) for s in scores])
    if extra_logit is not None:
        m = jnp.maximum(m, extra_logit)
    ps = [jnp.exp2(s - m) for s in scores]
    denom = functools.reduce(jnp.add, [jnp.sum(p, axis=-1, keepdims=True) for p in ps])
    if extra_logit is not None:
        denom = denom + jnp.exp2(extra_logit - m)
    return ps, denom


def _flash(qs, ctx_ks, ctx_vs, k_ats, v_ats, n_chunks, dv):
    tq = qs[0].shape[0]

    def update(carry, q, k, v):
        m, l, acc = carry
        s = _dot_nt(q, k)
        m_new = jnp.maximum(m, jnp.max(s, axis=-1, keepdims=True))
        alpha = jnp.exp2(m - m_new)
        p = jnp.exp2(s - m_new)
        l = alpha * l + jnp.sum(p, axis=-1, keepdims=True)
        acc = alpha * acc + _dot(p.astype(BF16), v)
        return m_new, l, acc

    init = (jnp.full((tq, 1), NEG_INF, F32), jnp.zeros((tq, 1), F32), jnp.zeros((tq, dv), F32))
    carries = tuple(update(init, q, k, v) for q, k, v in zip(qs, ctx_ks, ctx_vs))

    def body(i, carries):
        return tuple(update(cr, q, k_at(i), v_at(i))
                     for cr, q, k_at, v_at in zip(carries, qs, k_ats, v_ats))

    carries = lax.fori_loop(0, n_chunks, body, carries, unroll=min(FLASH_UNROLL, n_chunks))
    return [acc / l for _, l, acc in carries]


def _lambda_value(lam_ref, lam_init):
    lv = lam_ref[...]
    return (jnp.exp(jnp.sum(lv[0:1] * lv[1:2], axis=-1, keepdims=True))
            - jnp.exp(jnp.sum(lv[2:3] * lv[3:4], axis=-1, keepdims=True)) + lam_init)


def _sub_ln(o, g_ref, lam_init):
    return _rms(o) * g_ref[...] * (1.0 - lam_init)


SWA_TQ = 2 * SWA_BLOCK
SWA_SPAN = SWA_TQ + 2 * SWA_WINDOW


def _swa_kernel(sink_ref, q_ref, k_ref, v_ref, kc_ref, vc_ref, o_ref):
    L = k_ref.shape[2]
    qb = pl.program_id(1)
    start = pl.multiple_of(jnp.clip(qb * SWA_TQ - SWA_WINDOW, 0, L - SWA_SPAN), SWA_BLOCK)
    group = N_HEADS // SWA_KV_HEADS
    rows = group * SWA_TQ
    k_abs = start + lax.broadcasted_iota(jnp.int32, (rows, SWA_SPAN), 1)
    row = lax.broadcasted_iota(jnp.int32, (rows, SWA_SPAN), 0)
    q_abs = qb * SWA_TQ + jnp.where(row >= SWA_TQ, row - SWA_TQ, row)
    in_window = jnp.abs(k_abs - q_abs) <= SWA_WINDOW
    for hk in range(SWA_KV_HEADS):
        kb = k_ref[0, hk, pl.ds(start, SWA_SPAN), :]
        vb = v_ref[0, hk, pl.ds(start, SWA_SPAN), :]
        q = jnp.concatenate([q_ref[0, hk * group + g] for g in range(group)], axis=0)
        sink = jnp.concatenate([jnp.full((SWA_TQ, 1), sink_ref[hk * group + g], F32) for g in range(group)],
                               axis=0) * LOG2E
        s_band = jnp.where(in_window, _dot_nt(q, kb), NEG_INF)
        s_ctx = _dot_nt(q, kc_ref[0, hk])
        (p_band, p_ctx), denom = _softmax_parts([s_band, s_ctx], sink)
        o = (_dot(p_band.astype(BF16), vb) + _dot(p_ctx.astype(BF16), vc_ref[0, hk])) / denom
        for g in range(group):
            h = hk * group + g
            o_ref[0, :, h * HEAD_DIM:(h + 1) * HEAD_DIM] = o[g * SWA_TQ:(g + 1) * SWA_TQ].astype(o_ref.dtype)


def _swa_attention(sink, hd, hc):
    B, _, L, _ = hd["qa"].shape
    C = hc["ka"].shape[2]
    assert L % SWA_TQ == 0 and L >= SWA_SPAN
    whole = lambda b, i: (b, 0, 0, 0)
    return pl.pallas_call(
        _swa_kernel,
        grid=(B, L // SWA_TQ),
        in_specs=[pl.BlockSpec(memory_space=pltpu.SMEM),
                  pl.BlockSpec((1, N_HEADS, SWA_TQ, HEAD_DIM), lambda b, i: (b, 0, i, 0)),
                  pl.BlockSpec((1, SWA_KV_HEADS, L, HEAD_DIM), whole),
                  pl.BlockSpec((1, SWA_KV_HEADS, L, HEAD_DIM), whole),
                  pl.BlockSpec((1, SWA_KV_HEADS, C, HEAD_DIM), whole),
                  pl.BlockSpec((1, SWA_KV_HEADS, C, HEAD_DIM), whole)],
        out_specs=pl.BlockSpec((1, SWA_TQ, GROUP_WIDTH), lambda b, i: (b, i, 0)),
        out_shape=jax.ShapeDtypeStruct((B, L, GROUP_WIDTH), BF16),
        compiler_params=_cparams(("arbitrary", "arbitrary")),
        name="swa_attention",
    )(sink, hd["qa"], hd["ka"], hd["va"], hc["ka"], hc["va"])


GLOBAL_TQ = 512
GLOBAL_TK = 1024
HEADS_PER_STEP = 2
FLASH_UNROLL = 2


def _chunk_at(ref, lead, tk):
    return lambda i: ref[lead + (pl.ds(pl.multiple_of(i * tk, tk), tk), slice(None))]


def _mla_kernel(q_ref, kc_ref, vc_ref, k_ref, v_ref, o_ref, *, tk):
    n_chunks = k_ref.shape[2] // tk
    heads = range(HEADS_PER_STEP)
    outs = _flash([q_ref[0, j] for j in heads], [kc_ref[0, j] for j in heads], [vc_ref[0, j] for j in heads],
                  [_chunk_at(k_ref, (0, j), tk) for j in heads], [_chunk_at(v_ref, (0, j), tk) for j in heads],
                  n_chunks, MLA_V)
    o_ref[0] = jnp.concatenate(outs, axis=-1).astype(o_ref.dtype)


def _mla_attention(hd, hc):
    B, H, L, dk = hd["mq"].shape
    C = hc["mk"].shape[2]
    tq = min(GLOBAL_TQ, L)
    tk = min(GLOBAL_TK, L)
    hp = HEADS_PER_STEP
    whole = lambda b, h, i: (b, h, 0, 0)
    return pl.pallas_call(
        functools.partial(_mla_kernel, tk=tk),
        grid=(B, H // hp, L // tq),
        in_specs=[pl.BlockSpec((1, hp, tq, dk), lambda b, h, i: (b, h, i, 0)),
                  pl.BlockSpec((1, hp, C, dk), whole),
                  pl.BlockSpec((1, hp, C, MLA_V), whole),
                  pl.BlockSpec((1, hp, L, dk), whole),
                  pl.BlockSpec((1, hp, L, MLA_V), whole)],
        out_specs=pl.BlockSpec((1, tq, hp * MLA_V), lambda b, h, i: (b, i, h)),
        out_shape=jax.ShapeDtypeStruct((B, L, H * MLA_V), BF16),
        compiler_params=_cparams(("arbitrary", "arbitrary", "arbitrary")),
        name="mla_attention",
    )(hd["mq"], hc["mk"], hc["mv"], hd["mk"], hd["mv"])


def _diff_kernel(lam_ref, g_ref, q_ref, kc_ref, vc_ref, k_ref, v_ref, o_ref, *, tk, lam_init):
    n_chunks = k_ref.shape[2] // tk
    lam = _lambda_value(lam_ref, lam_init)
    heads = []
    for h in range(HEADS_PER_STEP):
        branches = (2 * h, 2 * h + 1)
        o1, o2 = _flash([q_ref[0, j] for j in branches], [kc_ref[0, j] for j in branches], [vc_ref[0, h]] * 2,
                        [_chunk_at(k_ref, (0, j), tk) for j in branches], [_chunk_at(v_ref, (0, h), tk)] * 2,
                        n_chunks, DIFF_V)
        heads.append(_sub_ln(o1 - lam * o2, g_ref, lam_init))
    o_ref[0] = jnp.concatenate(heads, axis=-1).astype(o_ref.dtype)


def _diff_attention(lam_vecs, subln_g, hd, hc, lam_init):
    B, _, L, dk = hd["dq"].shape
    C = hc["dk"].shape[2]
    tq = min(GLOBAL_TQ, L)
    tk = min(GLOBAL_TK, L)
    hp = HEADS_PER_STEP
    whole = lambda b, h, i: (b, h, 0, 0)
    const = lambda b, h, i: (0, 0)
    return pl.pallas_call(
        functools.partial(_diff_kernel, tk=tk, lam_init=lam_init),
        grid=(B, N_HEADS // hp, L // tq),
        in_specs=[pl.BlockSpec(lam_vecs.shape, const),
                  pl.BlockSpec(subln_g.shape, const),
                  pl.BlockSpec((1, 2 * hp, tq, dk), lambda b, h, i: (b, h, i, 0)),
                  pl.BlockSpec((1, 2 * hp, C, dk), whole),
                  pl.BlockSpec((1, hp, C, DIFF_V), whole),
                  pl.BlockSpec((1, 2 * hp, L, dk), whole),
                  pl.BlockSpec((1, hp, L, DIFF_V), whole)],
        out_specs=pl.BlockSpec((1, tq, hp * DIFF_V), lambda b, h, i: (b, i, h)),
        out_shape=jax.ShapeDtypeStruct((B, L, N_HEADS * DIFF_V), BF16),
        compiler_params=_cparams(("arbitrary", "arbitrary", "arbitrary")),
        name="diff_attention",
    )(lam_vecs, subln_g, hd["dq"], hc["dk"], hc["dv"], hd["dk"], hd["dv"])


def _na_plan(rows):
    kh = min(NA_KH, rows)
    n_row_off = 2 * NA_KH - 1
    col = np.arange(GRID_W)
    col_start = np.clip(col - NA_KW // 2, 0, GRID_W - NA_KW)
    col_ok = (col[None, :] >= col_start[:, None]) & (col[None, :] < col_start[:, None] + NA_KW)
    col_off = col[None, :] - col[:, None] + (NA_KW - 1)
    col_onehot = ((col_off[None] == np.arange(2 * NA_KW - 1)[:, None, None]) & col_ok[None]).astype(np.float32)
    patterns, starts, ids = {}, [], []
    for blk in range(rows // NA_Q_ROWS):
        r0 = blk * NA_Q_ROWS
        ks = int(np.clip(r0 - kh // 2, 0, rows - NA_K_ROWS))
        q_row = r0 + np.arange(NA_Q_ROWS)
        k_row = ks + np.arange(NA_K_ROWS)
        r_start = np.clip(q_row - kh // 2, 0, rows - kh)
        row_ok = (k_row[None, :] >= r_start[:, None]) & (k_row[None, :] < r_start[:, None] + kh)
        row_off = k_row[None, :] - q_row[:, None] + (NA_KH - 1)
        sel = np.where(row_ok, row_off, n_row_off)
        key = tuple(int(v) for v in sel.reshape(-1))
        if key not in patterns:
            patterns[key] = (len(patterns), sel)
        starts.append(ks)
        ids.append(patterns[key][0])
    row_sel = np.stack([p[1] for p in sorted(patterns.values(), key=lambda p: p[0])])
    return np.asarray(starts, np.int32), np.asarray(ids, np.int32), row_sel, col_onehot, col_ok


def _na_kernel(ks_ref, pid_ref, q_ref, k_ref, v_ref, kc_ref, vc_ref, bias_ref, o_ref):
    del pid_ref
    k_len = NA_K_ROWS * GRID_W
    start = pl.multiple_of(ks_ref[pl.program_id(1)] * GRID_W, GRID_W)
    for h in range(N_HEADS):
        q = q_ref[0, h]
        kw = k_ref[0, h, pl.ds(start, k_len), :]
        vw = v_ref[0, h, pl.ds(start, k_len), :]
        s_win = _dot_nt(q, kw) + bias_ref[0, h]
        s_ctx = _dot_nt(q, kc_ref[0, h])
        (p_win, p_ctx), denom = _softmax_parts([s_win, s_ctx])
        o = _dot(p_win.astype(BF16), vw) + _dot(p_ctx.astype(BF16), vc_ref[0, h])
        o_ref[0, :, h * HEAD_DIM:(h + 1) * HEAD_DIM] = (o / denom).astype(o_ref.dtype)


def _na_attention(rpb, hd, hc):
    B, H, L, _ = hd["nq"].shape
    C = hc["nk"].shape[2]
    rows = L // GRID_W
    starts, ids, row_sel, col_onehot, col_ok = _na_plan(rows)
    q_len = NA_Q_ROWS * GRID_W
    k_len = NA_K_ROWS * GRID_W
    slab = jnp.einsum("hrj,jqk->hrqk", rpb * LOG2E, col_onehot, precision=lax.Precision.HIGHEST)
    slab = jnp.where(col_ok[None, None], slab, NEG_INF)
    slab = jnp.concatenate([slab, jnp.full_like(slab[:, :1], NEG_INF)], axis=1)
    bias = jnp.take(slab, jnp.asarray(row_sel), axis=1)
    bias = jnp.transpose(bias, (1, 0, 2, 4, 3, 5)).reshape(row_sel.shape[0], H, q_len, k_len)
    whole = lambda b, i, ks, pid: (b, 0, 0, 0)
    grid_spec = pltpu.PrefetchScalarGridSpec(
        num_scalar_prefetch=2,
        grid=(B, rows // NA_Q_ROWS),
        in_specs=[pl.BlockSpec((1, H, q_len, HEAD_DIM), lambda b, i, ks, pid: (b, 0, i, 0)),
                  pl.BlockSpec((1, H, L, HEAD_DIM), whole),
                  pl.BlockSpec((1, H, L, HEAD_DIM), whole),
                  pl.BlockSpec((1, H, C, HEAD_DIM), whole),
                  pl.BlockSpec((1, H, C, HEAD_DIM), whole),
                  pl.BlockSpec((1, H, q_len, k_len), lambda b, i, ks, pid: (pid[i], 0, 0, 0))],
        out_specs=pl.BlockSpec((1, q_len, GROUP_WIDTH), lambda b, i, ks, pid: (b, i, 0)))
    return pl.pallas_call(
        _na_kernel,
        grid_spec=grid_spec,
        out_shape=jax.ShapeDtypeStruct((B, L, GROUP_WIDTH), BF16),
        compiler_params=_cparams(("arbitrary", "arbitrary")),
        name="neighborhood_attention",
    )(jnp.asarray(starts), jnp.asarray(ids), hd["nq"], hd["nk"], hd["nv"], hc["nk"], hc["nv"], bias)


def _ctx_attn_kernel(sink_ref, lam_ref, g_ref, qa_ref, ka_ref, va_ref, mq_ref, mk_ref, mv_ref,
                     dq_ref, dk_ref, dv_ref, nq_ref, nk_ref, nv_ref,
                     ya_ref, yb_ref, yc_ref, yd_ref, *, lam_init):
    def attend(q, k, v, extra=None):
        (p,), denom = _softmax_parts([_dot_nt(q, k)], extra)
        return _dot(p.astype(BF16), v) / denom

    group = N_HEADS // SWA_KV_HEADS
    lam = _lambda_value(lam_ref, lam_init)
    for h in range(N_HEADS):
        lanes = slice(h * HEAD_DIM, (h + 1) * HEAD_DIM)
        ya_ref[0, :, lanes] = attend(qa_ref[0, h], ka_ref[0, h // group], va_ref[0, h // group],
                                     sink_ref[h] * LOG2E).astype(ya_ref.dtype)
        yb_ref[0, :, lanes] = attend(mq_ref[0, h], mk_ref[0, h], mv_ref[0, h]).astype(yb_ref.dtype)
        o = (attend(dq_ref[0, 2 * h], dk_ref[0, 2 * h], dv_ref[0, h])
             - lam * attend(dq_ref[0, 2 * h + 1], dk_ref[0, 2 * h + 1], dv_ref[0, h]))
        yc_ref[0, :, lanes] = _sub_ln(o, g_ref, lam_init).astype(yc_ref.dtype)
        yd_ref[0, :, lanes] = attend(nq_ref[0, h], nk_ref[0, h], nv_ref[0, h]).astype(yd_ref.dtype)


def _ctx_attention(sink, lam_vecs, subln_g, hc, lam_init):
    names = ("qa", "ka", "va", "mq", "mk", "mv", "dq", "dk", "dv", "nq", "nk", "nv")
    B, _, C, _ = hc["qa"].shape
    whole4 = lambda b: (b, 0, 0, 0)
    const = lambda b: (0, 0)
    in_specs = [pl.BlockSpec(memory_space=pltpu.SMEM),
                pl.BlockSpec(lam_vecs.shape, const), pl.BlockSpec(subln_g.shape, const)]
    in_specs += [pl.BlockSpec((1,) + hc[n].shape[1:], whole4) for n in names]
    tok = pl.BlockSpec((1, C, GROUP_WIDTH), lambda b: (b, 0, 0))
    return pl.pallas_call(
        functools.partial(_ctx_attn_kernel, lam_init=lam_init),
        grid=(B,),
        in_specs=in_specs,
        out_specs=[tok] * N_MIXERS,
        out_shape=[jax.ShapeDtypeStruct((B, C, GROUP_WIDTH), BF16)] * N_MIXERS,
        compiler_params=_cparams(("arbitrary",)),
        name="context_attention",
    )(sink, lam_vecs, subln_g, *[hc[n] for n in names])


def _route(r):
    lane = lax.broadcasted_iota(jnp.int32, r.shape, 1)
    lane_f = lane.astype(F32)
    big = float(ROUTER_LANES)
    is_grp = (lane >= N_EXPERTS) & (lane < N_EXPERTS + N_GROUPS)
    g_log = jnp.where(is_grp, r, NEG_INF)
    g_max = jnp.max(g_log, axis=-1, keepdims=True)
    g_val = 1.0 / jnp.sum(jnp.exp(g_log - g_max), axis=-1, keepdims=True)
    g_idx = jnp.min(jnp.where(g_log == g_max, lane_f, big), axis=-1, keepdims=True) - float(N_EXPERTS)
    lane_grp = lax.shift_right_logical(lane, int(math.log2(EXPERTS_PER_GROUP))).astype(F32)
    in_grp = (lane < N_EXPERTS) & (lane_grp == g_idx)
    e_log = jnp.where(in_grp, r, NEG_INF)
    e_max = jnp.max(e_log, axis=-1, keepdims=True)
    i1 = jnp.min(jnp.where(e_log == e_max, lane_f, big), axis=-1, keepdims=True)
    e_rest = jnp.where(lane_f == i1, NEG_INF, e_log)
    e_max2 = jnp.max(e_rest, axis=-1, keepdims=True)
    i2 = jnp.min(jnp.where(e_rest == e_max2, lane_f, big), axis=-1, keepdims=True)
    p2 = jnp.exp(e_max2 - e_max)
    w1 = 1.0 / (1.0 + p2)
    w2 = p2 / (1.0 + p2)
    gates = g_val * jnp.where(lane_f == i1, w1, jnp.where(lane_f == i2, w2, 0.0))
    return jnp.where(lane == GROUP_LANE, g_idx, gates)


def _split_bf16(v):
    hi = v.astype(BF16)
    return hi, (v - hi.astype(F32)).astype(BF16)


def _outproj_kernel(x_ref, a_ref, b_ref, c_ref, d_ref, wo_ref, g1_ref, sc2_ref, sh2_ref,
                    lng_ref, lnb_ref, wr_ref, br_ref, x1_ref, h2_ref, gate_ref, *, alpha):
    y = _dot(a_ref[0], wo_ref[0:GROUP_WIDTH, :])
    for i, m_ref in enumerate((b_ref, c_ref, d_ref), start=1):
        y += _dot(m_ref[0], wo_ref[i * GROUP_WIDTH:(i + 1) * GROUP_WIDTH, :])
    x1 = _layer_norm(alpha * x_ref[0] + g1_ref[0] * y, lng_ref[...], lnb_ref[...])
    x1_ref[0] = x1
    h2 = x1 * (1.0 + sc2_ref[0]) + sh2_ref[0]
    h2_ref[0] = h2.astype(h2_ref.dtype)
    h_hi, h_lo = _split_bf16(h2)
    r = (_dot(h_hi, wr_ref[0]) + _dot(h_lo, wr_ref[0]) + _dot(h_hi, wr_ref[1])) + br_ref[...]
    gate_ref[0] = _route(r)


def _output_projection(x, att, wo, g1, sc2, sh2, ln_g, ln_b, wr, br, alpha):
    B, N, D = x.shape
    tm = min(512, N)
    tok = lambda w: pl.BlockSpec((1, tm, w), lambda b, i: (b, i, 0))
    vec = pl.BlockSpec((1, 1, D), lambda b, i: (b, 0, 0))
    const = lambda a: pl.BlockSpec(a.shape, lambda b, i: (0,) * a.ndim)
    return pl.pallas_call(
        functools.partial(_outproj_kernel, alpha=alpha),
        grid=(B, N // tm),
        in_specs=[tok(D)] + [tok(GROUP_WIDTH)] * N_MIXERS + [const(wo), vec, vec, vec,
                  const(ln_g), const(ln_b), const(wr), const(br)],
        out_specs=[tok(D), tok(D), tok(ROUTER_LANES)],
        out_shape=[jax.ShapeDtypeStruct((B, N, D), F32), jax.ShapeDtypeStruct((B, N, D), BF16),
                   jax.ShapeDtypeStruct((B, N, ROUTER_LANES), F32)],
        compiler_params=_cparams(("arbitrary", "arbitrary")),
        name="output_projection",
    )(x, att[0], att[1], att[2], att[3], wo, g1, sc2, sh2, ln_g, ln_b, wr, br)


MOE_TM = 1024
MOE_SLOTS = MOE_TM // N_GROUPS + MOE_TM // 16
MOE_HALVES = 2
MOE_EXPERTS_PER_STEP = EXPERTS_PER_GROUP // MOE_HALVES
MOE_DENSE_ROWS = 256
GROUP_LANE = N_EXPERTS


def _expert_ffn(h, gate_cols, wg_ref, wu_ref, wd_ref):
    pre = _dot(h, wg_ref[0, 0])
    hid = pre * jax.nn.sigmoid(pre) * _dot(h, wu_ref[0, 0])
    hid = jnp.concatenate([hid[:, e * EXPERT_HIDDEN:(e + 1) * EXPERT_HIDDEN] * gate_cols[e]
                           for e in range(MOE_EXPERTS_PER_STEP)], axis=-1)
    return _dot(hid.astype(BF16), wd_ref[0, 0])


def _moe_kernel(dense_ref, h_ref, gate_ref, x1_ref, wg_ref, wu_ref, wd_ref, g2_ref, lng_ref, lnb_ref,
                o_ref, acc_ref, hcat_ref, col_ref, row_ref, hc_ref, ge_ref, y_ref, *, alpha, slots):
    tm, D = acc_ref.shape
    g = pl.program_id(2)
    half = pl.program_id(3)
    tile = pl.program_id(0) * pl.num_programs(1) + pl.program_id(1)
    dense = dense_ref[tile] != 0
    g_f = g.astype(F32)

    @pl.when((g == 0) & (half == 0))
    def _():
        acc_ref[...] = jnp.zeros_like(acc_ref)

    @pl.when((g == 0) & (half == 0) & jnp.logical_not(dense))
    def _():
        gates = gate_ref[0]
        g_hi, g_lo = _split_bf16(gates)
        hcat_ref[:, :D] = h_ref[0]
        hcat_ref[:, D:D + ROUTER_LANES] = g_hi
        hcat_ref[:, D + ROUTER_LANES:] = g_lo
        lane = lax.broadcasted_iota(jnp.int32, gates.shape, 1).astype(F32)
        gidx_col = gates[:, GROUP_LANE:GROUP_LANE + 1]
        onehot_col = (gidx_col == lane).astype(BF16)
        pick = (lax.broadcasted_iota(jnp.int32, (8, ROUTER_LANES), 1) == GROUP_LANE).astype(BF16)
        gidx_row = _dot_nt(pick, g_hi)[0:1]
        sub = lax.broadcasted_iota(jnp.int32, (8, tm), 0).astype(F32)
        onehot_row = (gidx_row == sub).astype(BF16)
        rb = MOE_DENSE_ROWS
        cums_row = jnp.zeros((8, tm), F32)
        for r0 in range(0, tm, rb):
            earlier_c = (lax.broadcasted_iota(jnp.int32, (rb, tm), 1)
                         < r0 + lax.broadcasted_iota(jnp.int32, (rb, tm), 0)).astype(BF16)
            cums_col = _dot(earlier_c, onehot_col)
            rank_col = jnp.sum(cums_col * onehot_col[r0:r0 + rb].astype(F32), axis=-1, keepdims=True)
            first_lane = lax.broadcasted_iota(jnp.int32, (rb, ROUTER_LANES), 1) == 0
            col_ref[r0:r0 + rb, :] = jnp.where(first_lane, gidx_col[r0:r0 + rb], rank_col)
            earlier_r = (r0 + lax.broadcasted_iota(jnp.int32, (rb, tm), 0)
                         < lax.broadcasted_iota(jnp.int32, (rb, tm), 1)).astype(BF16)
            cums_row = cums_row + _dot(onehot_row[:, r0:r0 + rb], earlier_r)
        rank_row = jnp.sum(cums_row * onehot_row.astype(F32), axis=0, keepdims=True)
        row_ref[...] = jnp.where(sub == 0.0, gidx_row, rank_row)

    @pl.when(jnp.logical_not(dense))
    def _():
        @pl.when(half == 0)
        def _():
            slot = lax.broadcasted_iota(jnp.int32, (slots, tm), 0).astype(F32)
            pick = ((row_ref[1:2, :] == slot) & (row_ref[0:1, :] == g_f)).astype(BF16)
            comp = _dot(pick, hcat_ref[...])
            hc_ref[...] = comp[:, :D].astype(BF16)
            gc = comp[:, D:D + ROUTER_LANES] + comp[:, D + ROUTER_LANES:]
            w = EXPERTS_PER_GROUP
            ge_ref[...] = functools.reduce(jnp.add, [gc[:, j * w:(j + 1) * w] for j in range(N_GROUPS)])

        ge = ge_ref[...]
        n = MOE_EXPERTS_PER_STEP
        cols = [jnp.where(half == 0, ge[:, e:e + 1], ge[:, n + e:n + e + 1]) for e in range(n)]
        y_part = _expert_ffn(hc_ref[...], cols, wg_ref, wu_ref, wd_ref)

        @pl.when(half == 0)
        def _():
            y_ref[...] = y_part

        @pl.when(half == MOE_HALVES - 1)
        def _():
            y_hi, y_lo = _split_bf16(y_ref[...] + y_part)
            slot = lax.broadcasted_iota(jnp.int32, (tm, slots), 1).astype(F32)
            put = ((col_ref[:, 1:2] == slot) & (col_ref[:, 0:1] == g_f)).astype(BF16)
            acc_ref[...] += _dot(put, y_hi) + _dot(put, y_lo)

    @pl.when(dense)
    def _():
        lane = lax.broadcasted_iota(jnp.int32, (MOE_DENSE_ROWS, ROUTER_LANES), 1)
        first = g * EXPERTS_PER_GROUP + half * MOE_EXPERTS_PER_STEP
        for r0 in range(0, tm, MOE_DENSE_ROWS):
            rows = slice(r0, r0 + MOE_DENSE_ROWS)
            gates = gate_ref[0, rows, :]
            cols = [jnp.sum(jnp.where(lane == first + e, gates, 0.0), axis=-1, keepdims=True)
                    for e in range(MOE_EXPERTS_PER_STEP)]
            acc_ref[rows, :] += _expert_ffn(h_ref[0, rows, :], cols, wg_ref, wu_ref, wd_ref)

    @pl.when((g == pl.num_programs(2) - 1) & (half == MOE_HALVES - 1))
    def _():
        o_ref[0] = _layer_norm(alpha * x1_ref[0] + g2_ref[0] * acc_ref[...], lng_ref[...], lnb_ref[...])


def _moe(h2, gates, x1, wg, wu, wd, g2, ln_g, ln_b, alpha):
    B, N, D = x1.shape
    tm = min(MOE_TM, N)
    slots = tm // N_GROUPS + tm // 16
    nt = N // tm
    if tm < MOE_TM:
        dense = jnp.ones((B * nt,), jnp.int32)
    else:
        gidx = gates[:, :, GROUP_LANE].reshape(B * nt, tm)
        counts = jnp.sum(gidx[:, :, None] == jnp.arange(N_GROUPS, dtype=F32), axis=1)
        dense = (jnp.max(counts, axis=-1) > slots).astype(jnp.int32)
    tok = lambda w: pl.BlockSpec((1, tm, w), lambda b, i, g, s, d: (b, i, 0))
    const = lambda a: pl.BlockSpec(a.shape, lambda b, i, g, s, d: (0, 0))
    wspec = lambda a: pl.BlockSpec((1, 1) + a.shape[2:], lambda b, i, g, s, d: (g, s, 0, 0))
    grid_spec = pltpu.PrefetchScalarGridSpec(
        num_scalar_prefetch=1,
        grid=(B, nt, N_GROUPS, MOE_HALVES),
        in_specs=[tok(D), tok(ROUTER_LANES), tok(D), wspec(wg), wspec(wu), wspec(wd),
                  pl.BlockSpec((1, 1, D), lambda b, i, g, s, d: (b, 0, 0)), const(ln_g), const(ln_b)],
        out_specs=tok(D),
        scratch_shapes=[pltpu.VMEM((tm, D), F32),
                        pltpu.VMEM((tm, D + 2 * ROUTER_LANES), BF16),
                        pltpu.VMEM((tm, ROUTER_LANES), F32),
                        pltpu.VMEM((8, tm), F32),
                        pltpu.VMEM((slots, D), BF16),
                        pltpu.VMEM((slots, EXPERTS_PER_GROUP), F32),
                        pltpu.VMEM((slots, D), F32)])
    return pl.pallas_call(
        functools.partial(_moe_kernel, alpha=alpha, slots=slots),
        grid_spec=grid_spec,
        out_shape=jax.ShapeDtypeStruct((B, N, D), F32),
        compiler_params=_cparams(("arbitrary",) * 4),
        name="moe",
    )(dense, h2, gates, x1, wg, wu, wd, g2, ln_g, ln_b)


def _rot_cols(w, d):
    k, n = w.shape
    q = d // 4
    w4 = w.reshape(k, n // d, 4, q)
    return jnp.stack([-w4[:, :, 1], w4[:, :, 0], -w4[:, :, 3], w4[:, :, 2]], axis=2).reshape(k, n)


def _rope_tables(L):
    t = jnp.arange(L, dtype=jnp.int32)
    rows = (t // GRID_W).astype(F32)
    cols = (t % GRID_W).astype(F32)

    def cos_sin(d):
        q = d // 4
        inv = ROPE_BASE ** (-jnp.arange(q, dtype=F32) / q)
        ar = rows[:, None] * inv[None, :]
        ac = cols[:, None] * inv[None, :]
        return (jnp.concatenate([jnp.cos(ar), jnp.cos(ar), jnp.cos(ac), jnp.cos(ac)], -1),
                jnp.concatenate([jnp.sin(ar), jnp.sin(ar), jnp.sin(ac), jnp.sin(ac)], -1))

    c64, s64 = cos_sin(HEAD_DIM)
    c32, s32 = cos_sin(MLA_ROPE)
    ones = jnp.ones((L, MLA_NOPE), F32)
    zeros_n = jnp.zeros((L, MLA_NOPE), F32)
    zeros_p = jnp.zeros((L, MLA_QK_PAD - MLA_NOPE - MLA_ROPE), F32)
    return (jnp.tile(c64, (1, N_HEADS)), jnp.tile(s64, (1, N_HEADS)),
            jnp.tile(c32, (1, 2 * N_HEADS)), jnp.tile(s32, (1, 2 * N_HEADS)),
            jnp.concatenate([ones, c32, zeros_p], -1), jnp.concatenate([zeros_n, s32, zeros_p], -1))


def _identity_tables(C):
    one = jnp.ones((C, GROUP_WIDTH), F32)
    zero = jnp.zeros((C, GROUP_WIDTH), F32)
    pad = MLA_QK_PAD - MLA_NOPE - MLA_ROPE
    cosm = jnp.concatenate([jnp.ones((C, MLA_NOPE + MLA_ROPE), F32), jnp.zeros((C, pad), F32)], -1)
    return one, zero, one, zero, cosm, jnp.zeros((C, MLA_QK_PAD), F32)


def _fused_in_weight(w_in):
    d = w_in.shape[0]
    p = jnp.split(w_in, IN_CUTS, axis=1)
    z = lambda n: jnp.zeros((d, n), w_in.dtype)
    pad_r = MLA_QK_PAD - MLA_NOPE - MLA_ROPE
    parts = {"qa": p[0], "qa_r": _rot_cols(p[0], HEAD_DIM), "ka": p[1], "ka_r": _rot_cols(p[1], HEAD_DIM),
             "va": p[2], "mqr": p[3], "mkvr": p[4],
             "mkr": jnp.concatenate([z(MLA_NOPE), p[5], z(pad_r)], 1),
             "mkr_r": jnp.concatenate([z(MLA_NOPE), _rot_cols(p[5], MLA_ROPE), z(pad_r)], 1),
             "dq": p[6], "dq_r": _rot_cols(p[6], DIFF_QK), "dk": p[7], "dk_r": _rot_cols(p[7], DIFF_QK),
             "dv": p[8], "nq": p[9], "nk": p[10], "nv": p[11]}
    return jnp.concatenate([parts[n] for n, _ in _SEG_LAYOUT], axis=1).astype(BF16)


def _mla_up_weights(w_uq, w_ukv):
    rq = w_uq.shape[0]
    pad_r = MLA_QK_PAD - MLA_NOPE - MLA_ROPE
    wq = w_uq.reshape(rq, N_HEADS, MLA_NOPE + MLA_ROPE)
    zq = lambda n: jnp.zeros((rq, N_HEADS, n), w_uq.dtype)
    rope_rot = _rot_cols(wq[:, :, MLA_NOPE:].reshape(rq, -1), MLA_ROPE).reshape(rq, N_HEADS, MLA_ROPE)
    main = jnp.concatenate([wq, zq(pad_r)], -1).reshape(rq, -1)
    rot = jnp.concatenate([zq(MLA_NOPE), rope_rot, zq(pad_r)], -1).reshape(rq, -1)
    wuq_ext = jnp.concatenate([main, rot], 1).astype(BF16)
    rk = w_ukv.shape[0]
    wkv = w_ukv.reshape(rk, N_HEADS, MLA_NOPE + MLA_V)
    k_part = jnp.concatenate([wkv[:, :, :MLA_NOPE],
                              jnp.zeros((rk, N_HEADS, MLA_QK_PAD - MLA_NOPE), w_ukv.dtype)], -1)
    wukv_ext = jnp.concatenate([k_part.reshape(rk, -1), wkv[:, :, MLA_NOPE:].reshape(rk, -1)], 1).astype(BF16)
    return wuq_ext, wukv_ext


def kernel(x, c, ctx, c_ctx, w_mod, b_mod, w_in, attn_sink, mla_q_norm, w_uq, mla_kv_norm, w_ukv,
           lam_q1, lam_k1, lam_q2, lam_k2, diff_subln, na_rpb, w_out, ln1_g, ln1_b,
           w_group, b_group, w_router, b_router, w_gate, w_up, w_down, ln2_g, ln2_b):
    B, L, D = x.shape
    C = ctx.shape[1]
    depth = w_mod.shape[0]
    alpha = (2 * depth) ** 0.25
    assert D == D_MODEL and B + 1 <= 8 and L % (NA_Q_ROWS * GRID_W) == 0

    cvec = jnp.concatenate([c, c_ctx[None, :], jnp.zeros((8 - B - 1, D), F32)], axis=0)
    mod = _modulation(cvec, w_mod, b_mod)
    lat_tables = _rope_tables(L)
    ctx_tables = _identity_tables(C)
    row = lambda a: a.reshape(1, -1)

    xc = ctx
    for l in range(depth):
        need_ctx = l < depth - 1
        lam_init = 0.8 - 0.6 * math.exp(-0.3 * l)
        chunks = [mod[l, :, i * D:(i + 1) * D] for i in range(6)]
        sh1, sc1, g1, sh2, sc2, g2 = [m[:B, None, :] for m in chunks]
        sh1c, sc1c, g1c, sh2c, sc2c, g2c = [jnp.broadcast_to(m[B:B + 1, None, :], (B, 1, D)) for m in chunks]

        w_all = _fused_in_weight(w_in[l])
        wuq_ext, wukv_ext = _mla_up_weights(w_uq[l], w_ukv[l])
        proj = functools.partial(_input_projection, w_all=w_all, qng=row(mla_q_norm[l]), wuq=wuq_ext,
                                 kvng=row(mla_kv_norm[l]), wukv=wukv_ext)
        hd = proj(x, sc1, sh1, tables=lat_tables)
        hc = proj(xc, sc1c, sh1c, tables=ctx_tables)

        lam_vecs = jnp.stack([lam_q1[l], lam_k1[l], lam_q2[l], lam_k2[l]])
        subln_g = row(diff_subln[l])
        att = (_swa_attention(attn_sink[l], hd, hc), _mla_attention(hd, hc),
               _diff_attention(lam_vecs, subln_g, hd, hc, lam_init), _na_attention(na_rpb[l], hd, hc))

        wo = w_out[l].astype(BF16)
        pad = ROUTER_LANES - N_EXPERTS - N_GROUPS
        wr = jnp.stack(_split_bf16(jnp.concatenate([w_router[l], w_group[l], jnp.zeros((D, pad), F32)], axis=1)))
        br = row(jnp.concatenate([b_router[l], b_group[l], jnp.zeros((pad,), F32)]))
        per_step = (N_GROUPS, MOE_HALVES, MOE_EXPERTS_PER_STEP)
        n_hid = MOE_EXPERTS_PER_STEP * EXPERT_HIDDEN
        side_by_side = lambda w: jnp.transpose(w.astype(BF16).reshape(per_step + (D, EXPERT_HIDDEN)),
                                               (0, 1, 3, 2, 4)).reshape(N_GROUPS, MOE_HALVES, D, n_hid)
        wg = side_by_side(w_gate[l])
        wu = side_by_side(w_up[l])
        wd = w_down[l].astype(BF16).reshape(N_GROUPS, MOE_HALVES, n_hid, D)
        post = functools.partial(_output_projection, wo=wo, ln_g=row(ln1_g[l]), ln_b=row(ln1_b[l]),
                                 wr=wr, br=br, alpha=alpha)
        ffn = functools.partial(_moe, wg=wg, wu=wu, wd=wd, ln_g=row(ln2_g[l]), ln_b=row(ln2_b[l]), alpha=alpha)

        x1, h2, gates = post(x, att, g1=g1, sc2=sc2, sh2=sh2)
        x = ffn(h2, gates, x1, g2=g2)
        if need_ctx:
            att_c = _ctx_attention(attn_sink[l], lam_vecs, subln_g, hc, lam_init)
            xc1, hc2, gates_c = post(xc, att_c, g1=g1c, sc2=sc2c, sh2=sh2c)
            xc = ffn(hc2, gates_c, xc1, g2=g2c)
    return x
```

```python
import functools
import math

import jax
import jax.numpy as jnp
import numpy as np
from jax import lax
from jax.experimental import pallas as pl
from jax.experimental.pallas import tpu as pltpu

F32 = jnp.float32
BF16 = jnp.bfloat16

D_MODEL = 1024
GRID_W = 64
HEAD_DIM = 64
N_HEADS = 4
N_MIXERS = 4
GROUP_WIDTH = N_HEADS * HEAD_DIM
SWA_KV_HEADS = 2
SWA_WINDOW = 128
SWA_BLOCK = 128
MLA_Q_RANK = 256
MLA_KV_RANK = 128
MLA_NOPE = 64
MLA_ROPE = 32
MLA_V = 64
MLA_QK_PAD = 128
DIFF_QK = 32
DIFF_V = 64
NA_KH = 8
NA_KW = 16
NA_Q_ROWS = 4
NA_K_ROWS = NA_Q_ROWS + NA_KH
N_GROUPS = 4
EXPERTS_PER_GROUP = 8
N_EXPERTS = N_GROUPS * EXPERTS_PER_GROUP
EXPERT_HIDDEN = 256
ROUTER_LANES = 128
GROUP_LANE = N_EXPERTS
ROPE_BASE = 10000.0
NORM_EPS = 1e-5
NEG_INF = -1e30
LOG2E = math.log2(math.e)
SWA_SCALE = HEAD_DIM ** -0.5 * LOG2E
MLA_SCALE = (MLA_NOPE + MLA_ROPE) ** -0.5 * LOG2E
DIFF_SCALE = DIFF_QK ** -0.5 * LOG2E
NA_SCALE = HEAD_DIM ** -0.5 * LOG2E
IN_SPLITS = (GROUP_WIDTH, SWA_KV_HEADS * HEAD_DIM, SWA_KV_HEADS * HEAD_DIM,
             MLA_Q_RANK, MLA_KV_RANK, MLA_ROPE,
             N_HEADS * 2 * DIFF_QK, N_HEADS * 2 * DIFF_QK, N_HEADS * DIFF_V,
             GROUP_WIDTH, GROUP_WIDTH, GROUP_WIDTH)
IN_CUTS = tuple(int(v) for v in np.cumsum(IN_SPLITS)[:-1])

_SEG_LAYOUT = (("qa", 256), ("qa_r", 256), ("ka", 128), ("ka_r", 128), ("va", 128),
               ("mqr", 256), ("mkvr", 128), ("mkr", 128), ("mkr_r", 128),
               ("dq", 256), ("dq_r", 256), ("dk", 256), ("dk_r", 256), ("dv", 256),
               ("nq", 256), ("nk", 256), ("nv", 256))
_SEG = {}
_off = 0
for _name, _w in _SEG_LAYOUT:
    _SEG[_name] = (_off, _off + _w)
    _off += _w
W_ALL_COLS = _off

VMEM_LIMIT_BYTES = 56 * 1024 * 1024


def _cparams(sem):
    return pltpu.CompilerParams(dimension_semantics=sem, vmem_limit_bytes=VMEM_LIMIT_BYTES)


def _dot(a, b):
    return jnp.dot(a, b, preferred_element_type=F32)


def _dot_nt(a, b):
    return lax.dot_general(a, b, (((1,), (1,)), ((), ())), preferred_element_type=F32)


def _rms(x):
    return x * lax.rsqrt(jnp.mean(x * x, axis=-1, keepdims=True) + NORM_EPS)


def _layer_norm(z, g, b):
    mu = jnp.mean(z, axis=-1, keepdims=True)
    zc = z - mu
    var = jnp.mean(zc * zc, axis=-1, keepdims=True)
    return zc * lax.rsqrt(var + NORM_EPS) * g + b


def _mod_kernel(c_ref, w_ref, b_ref, o_ref):
    c = c_ref[...]
    act = c * jax.nn.sigmoid(c)
    o_ref[0] = jnp.dot(act, w_ref[0], preferred_element_type=F32,
                       precision=lax.Precision.HIGHEST) + b_ref[0]


def _modulation(cvec, w_mod, b_mod):
    depth, d, n = w_mod.shape
    tn = 1024
    return pl.pallas_call(
        _mod_kernel,
        grid=(depth, n // tn),
        in_specs=[pl.BlockSpec((8, d), lambda l, j: (0, 0)),
                  pl.BlockSpec((1, d, tn), lambda l, j: (l, 0, j)),
                  pl.BlockSpec((1, 1, tn), lambda l, j: (l, 0, j))],
        out_specs=pl.BlockSpec((1, 8, tn), lambda l, j: (l, 0, j)),
        out_shape=jax.ShapeDtypeStruct((depth, 8, n), F32),
        compiler_params=_cparams(("arbitrary", "arbitrary")),
        name="modulation",
    )(cvec, w_mod, b_mod.reshape(depth, 1, n))


def _inproj_kernel(x_ref, sc_ref, sh_ref, w_ref, cos64_ref, sin64_ref, cos32_ref, sin32_ref,
                   cosm_ref, sinm_ref, qng_ref, wuq_ref, kvng_ref, wukv_ref,
                   qa_o, ka_o, va_o, mq_o, mk_o, mv_o, dq_o, dk_o, dv_o, nq_o, nk_o, nv_o):
    h = (x_ref[0] * (1.0 + sc_ref[0]) + sh_ref[0]).astype(BF16)

    def seg(name):
        a, b = _SEG[name]
        return _dot(h, w_ref[:, a:b])

    def split_heads(val, out_ref, n, width):
        for i in range(n):
            out_ref[0, i] = val[:, i * width:(i + 1) * width].astype(out_ref.dtype)

    cos64 = cos64_ref[...]
    sin64 = sin64_ref[...]
    cos32 = cos32_ref[...]
    sin32 = sin32_ref[...]
    cosm = cosm_ref[...]
    sinm = sinm_ref[...]

    qa = (seg("qa") * cos64 + seg("qa_r") * sin64) * SWA_SCALE
    split_heads(qa, qa_o, N_HEADS, HEAD_DIM)
    ka = seg("ka") * cos64[:, :128] + seg("ka_r") * sin64[:, :128]
    split_heads(ka, ka_o, SWA_KV_HEADS, HEAD_DIM)
    split_heads(seg("va"), va_o, SWA_KV_HEADS, HEAD_DIM)

    qn = (_rms(seg("mqr")) * qng_ref[...]).astype(BF16)
    uq = _dot(qn, wuq_ref[...])
    half = N_HEADS * MLA_QK_PAD
    for i in range(N_HEADS):
        a = i * MLA_QK_PAD
        mq = (uq[:, a:a + MLA_QK_PAD] * cosm + uq[:, half + a:half + a + MLA_QK_PAD] * sinm) * MLA_SCALE
        mq_o[0, i] = mq.astype(BF16)
    kvn = (_rms(seg("mkvr")) * kvng_ref[...]).astype(BF16)
    ukv = _dot(kvn, wukv_ref[...])
    k_rope = seg("mkr") * cosm + seg("mkr_r") * sinm
    for i in range(N_HEADS):
        a = i * MLA_QK_PAD
        mk_o[0, i] = (ukv[:, a:a + MLA_QK_PAD] + k_rope).astype(BF16)
        b = half + i * MLA_V
        mv_o[0, i] = ukv[:, b:b + MLA_V].astype(BF16)

    dq = (seg("dq") * cos32 + seg("dq_r") * sin32) * DIFF_SCALE
    split_heads(dq, dq_o, 2 * N_HEADS, DIFF_QK)
    dk = seg("dk") * cos32 + seg("dk_r") * sin32
    split_heads(dk, dk_o, 2 * N_HEADS, DIFF_QK)
    split_heads(seg("dv"), dv_o, N_HEADS, DIFF_V)

    split_heads(seg("nq") * NA_SCALE, nq_o, N_HEADS, HEAD_DIM)
    split_heads(seg("nk"), nk_o, N_HEADS, HEAD_DIM)
    split_heads(seg("nv"), nv_o, N_HEADS, HEAD_DIM)


_HEAD_OUTS = (("qa", N_HEADS, HEAD_DIM), ("ka", SWA_KV_HEADS, HEAD_DIM), ("va", SWA_KV_HEADS, HEAD_DIM),
              ("mq", N_HEADS, MLA_QK_PAD), ("mk", N_HEADS, MLA_QK_PAD), ("mv", N_HEADS, MLA_V),
              ("dq", 2 * N_HEADS, DIFF_QK), ("dk", 2 * N_HEADS, DIFF_QK), ("dv", N_HEADS, DIFF_V),
              ("nq", N_HEADS, HEAD_DIM), ("nk", N_HEADS, HEAD_DIM), ("nv", N_HEADS, HEAD_DIM))


def _input_projection(x, sc, sh, w_all, tables, qng, wuq, kvng, wukv):
    B, N, D = x.shape
    tm = min(512, N)
    tok = lambda i, b: (b, i, 0)
    vec = lambda i, b: (b, 0, 0)
    tab = lambda i, b: (i, 0)
    const = lambda i, b: (0, 0)
    in_specs = [pl.BlockSpec((1, tm, D), tok),
                pl.BlockSpec((1, 1, D), vec), pl.BlockSpec((1, 1, D), vec),
                pl.BlockSpec(w_all.shape, const)]
    in_specs += [pl.BlockSpec((tm, t.shape[1]), tab) for t in tables]
    in_specs += [pl.BlockSpec(a.shape, const) for a in (qng, wuq, kvng, wukv)]
    out_specs = [pl.BlockSpec((1, n, tm, w), lambda i, b: (b, 0, i, 0)) for _, n, w in _HEAD_OUTS]
    out_shape = [jax.ShapeDtypeStruct((B, n, N, w), BF16) for _, n, w in _HEAD_OUTS]
    outs = pl.pallas_call(
        _inproj_kernel,
        grid=(N // tm, B),
        in_specs=in_specs, out_specs=out_specs, out_shape=out_shape,
        compiler_params=_cparams(("arbitrary", "arbitrary")),
        name="input_projection",
    )(x, sc, sh, w_all, *tables, qng, wuq, kvng, wukv)
    return {name: o for (name, _, _), o in zip(_HEAD_OUTS, outs)}


def _softmax_parts(scores, extra_logit=None):
    m = functools.reduce(jnp.maximum, [jnp.max(s, axis=-1, keepdims=True) for s in scores])
    if extra_logit is not None:
        m = jnp.maximum(m, extra_logit)
    ps = [jnp.exp2(s - m) for s in scores]
    denom = functools.reduce(jnp.add, [jnp.sum(p, axis=-1, keepdims=True) for p in ps])
    if extra_logit is not None:
        denom = denom + jnp.exp2(extra_logit - m)
    return ps, denom


def _flash(qs, ctx_ks, ctx_vs, k_ats, v_ats, n_chunks, dv):
    tq = qs[0].shape[0]

    def update(carry, q, k, v):
        m, l, acc = carry
        s = _dot_nt(q, k)
        m_new = jnp.maximum(m, jnp.max(s, axis=-1, keepdims=True))
        alpha = jnp.exp2(m - m_new)
        p = jnp.exp2(s - m_new)
        l = alpha * l + jnp.sum(p, axis=-1, keepdims=True)
        acc = alpha * acc + _dot(p.astype(BF16), v)
        return m_new, l, acc

    init = (jnp.full((tq, 1), NEG_INF, F32), jnp.zeros((tq, 1), F32), jnp.zeros((tq, dv), F32))
    carries = tuple(update(init, q, k, v) for q, k, v in zip(qs, ctx_ks, ctx_vs))

    def body(i, carries):
        return tuple(update(cr, q, k_at(i), v_at(i))
                     for cr, q, k_at, v_at in zip(carries, qs, k_ats, v_ats))

    carries = lax.fori_loop(0, n_chunks, body, carries, unroll=min(FLASH_UNROLL, n_chunks))
    return [acc / l for _, l, acc in carries]


def _lambda_value(lam_ref, lam_init):
    lv = lam_ref[...]
    return (jnp.exp(jnp.sum(lv[0:1] * lv[1:2], axis=-1, keepdims=True))
            - jnp.exp(jnp.sum(lv[2:3] * lv[3:4], axis=-1, keepdims=True)) + lam_init)


def _sub_ln(o, g_ref, lam_init):
    return _rms(o) * g_ref[...] * (1.0 - lam_init)


SWA_TQ = 2 * SWA_BLOCK
SWA_SPAN = SWA_TQ + 2 * SWA_WINDOW


def _swa_kernel(sink_ref, q_ref, k_ref, v_ref, kc_ref, vc_ref, o_ref):
    L = k_ref.shape[2]
    qb = pl.program_id(1)
    start = pl.multiple_of(jnp.clip(qb * SWA_TQ - SWA_WINDOW, 0, L - SWA_SPAN), SWA_BLOCK)
    group = N_HEADS // SWA_KV_HEADS
    rows = group * SWA_TQ
    k_abs = start + lax.broadcasted_iota(jnp.int32, (rows, SWA_SPAN), 1)
    row = lax.broadcasted_iota(jnp.int32, (rows, SWA_SPAN), 0)
    q_abs = qb * SWA_TQ + jnp.where(row >= SWA_TQ, row - SWA_TQ, row)
    in_window = jnp.abs(k_abs - q_abs) <= SWA_WINDOW
    for hk in range(SWA_KV_HEADS):
        kb = k_ref[0, hk, pl.ds(start, SWA_SPAN), :]
        vb = v_ref[0, hk, pl.ds(start, SWA_SPAN), :]
        q = jnp.concatenate([q_ref[0, hk * group + g] for g in range(group)], axis=0)
        sink = jnp.concatenate([jnp.full((SWA_TQ, 1), sink_ref[hk * group + g], F32) for g in range(group)],
                               axis=0) * LOG2E
        s_band = jnp.where(in_window, _dot_nt(q, kb), NEG_INF)
        s_ctx = _dot_nt(q, kc_ref[0, hk])
        (p_band, p_ctx), denom = _softmax_parts([s_band, s_ctx], sink)
        o = (_dot(p_band.astype(BF16), vb) + _dot(p_ctx.astype(BF16), vc_ref[0, hk])) / denom
        for g in range(group):
            h = hk * group + g
            o_ref[0, :, h * HEAD_DIM:(h + 1) * HEAD_DIM] = o[g * SWA_TQ:(g + 1) * SWA_TQ].astype(o_ref.dtype)


def _swa_attention(sink, hd, hc):
    B, _, L, _ = hd["qa"].shape
    C = hc["ka"].shape[2]
    assert L % SWA_TQ == 0 and L >= SWA_SPAN
    whole = lambda b, i: (b, 0, 0, 0)
    return pl.pallas_call(
        _swa_kernel,
        grid=(B, L // SWA_TQ),
        in_specs=[pl.BlockSpec(memory_space=pltpu.SMEM),
                  pl.BlockSpec((1, N_HEADS, SWA_TQ, HEAD_DIM), lambda b, i: (b, 0, i, 0)),
                  pl.BlockSpec((1, SWA_KV_HEADS, L, HEAD_DIM), whole),
                  pl.BlockSpec((1, SWA_KV_HEADS, L, HEAD_DIM), whole),
                  pl.BlockSpec((1, SWA_KV_HEADS, C, HEAD_DIM), whole),
                  pl.BlockSpec((1, SWA_KV_HEADS, C, HEAD_DIM), whole)],
        out_specs=pl.BlockSpec((1, SWA_TQ, GROUP_WIDTH), lambda b, i: (b, i, 0)),
        out_shape=jax.ShapeDtypeStruct((B, L, GROUP_WIDTH), BF16),
        compiler_params=_cparams(("arbitrary", "arbitrary")),
        name="swa_attention",
    )(sink, hd["qa"], hd["ka"], hd["va"], hc["ka"], hc["va"])


GLOBAL_TQ = 512
GLOBAL_TK = 1024
HEADS_PER_STEP = 2
FLASH_UNROLL = 2


def _chunk_at(ref, lead, tk):
    return lambda i: ref[lead + (pl.ds(pl.multiple_of(i * tk, tk), tk), slice(None))]


def _mla_kernel(q_ref, kc_ref, vc_ref, k_ref, v_ref, o_ref, *, tk):
    n_chunks = k_ref.shape[2] // tk
    heads = range(HEADS_PER_STEP)
    outs = _flash([q_ref[0, j] for j in heads], [kc_ref[0, j] for j in heads], [vc_ref[0, j] for j in heads],
                  [_chunk_at(k_ref, (0, j), tk) for j in heads], [_chunk_at(v_ref, (0, j), tk) for j in heads],
                  n_chunks, MLA_V)
    o_ref[0] = jnp.concatenate(outs, axis=-1).astype(o_ref.dtype)


def _mla_attention(hd, hc):
    B, H, L, dk = hd["mq"].shape
    C = hc["mk"].shape[2]
    tq = min(GLOBAL_TQ, L)
    tk = min(GLOBAL_TK, L)
    hp = HEADS_PER_STEP
    whole = lambda b, h, i: (b, h, 0, 0)
    return pl.pallas_call(
        functools.partial(_mla_kernel, tk=tk),
        grid=(B, H // hp, L // tq),
        in_specs=[pl.BlockSpec((1, hp, tq, dk), lambda b, h, i: (b, h, i, 0)),
                  pl.BlockSpec((1, hp, C, dk), whole),
                  pl.BlockSpec((1, hp, C, MLA_V), whole),
                  pl.BlockSpec((1, hp, L, dk), whole),
                  pl.BlockSpec((1, hp, L, MLA_V), whole)],
        out_specs=pl.BlockSpec((1, tq, hp * MLA_V), lambda b, h, i: (b, i, h)),
        out_shape=jax.ShapeDtypeStruct((B, L, H * MLA_V), BF16),
        compiler_params=_cparams(("arbitrary", "arbitrary", "arbitrary")),
        name="mla_attention",
    )(hd["mq"], hc["mk"], hc["mv"], hd["mk"], hd["mv"])


def _diff_kernel(lam_ref, g_ref, q_ref, kc_ref, vc_ref, k_ref, v_ref, o_ref, *, tk, lam_init):
    n_chunks = k_ref.shape[2] // tk
    lam = _lambda_value(lam_ref, lam_init)
    heads = []
    for h in range(HEADS_PER_STEP):
        branches = (2 * h, 2 * h + 1)
        o1, o2 = _flash([q_ref[0, j] for j in branches], [kc_ref[0, j] for j in branches], [vc_ref[0, h]] * 2,
                        [_chunk_at(k_ref, (0, j), tk) for j in branches], [_chunk_at(v_ref, (0, h), tk)] * 2,
                        n_chunks, DIFF_V)
        heads.append(_sub_ln(o1 - lam * o2, g_ref, lam_init))
    o_ref[0] = jnp.concatenate(heads, axis=-1).astype(o_ref.dtype)


def _diff_attention(lam_vecs, subln_g, hd, hc, lam_init):
    B, _, L, dk = hd["dq"].shape
    C = hc["dk"].shape[2]
    tq = min(GLOBAL_TQ, L)
    tk = min(GLOBAL_TK, L)
    hp = HEADS_PER_STEP
    whole = lambda b, h, i: (b, h, 0, 0)
    const = lambda b, h, i: (0, 0)
    return pl.pallas_call(
        functools.partial(_diff_kernel, tk=tk, lam_init=lam_init),
        grid=(B, N_HEADS // hp, L // tq),
        in_specs=[pl.BlockSpec(lam_vecs.shape, const),
                  pl.BlockSpec(subln_g.shape, const),
                  pl.BlockSpec((1, 2 * hp, tq, dk), lambda b, h, i: (b, h, i, 0)),
                  pl.BlockSpec((1, 2 * hp, C, dk), whole),
                  pl.BlockSpec((1, hp, C, DIFF_V), whole),
                  pl.BlockSpec((1, 2 * hp, L, dk), whole),
                  pl.BlockSpec((1, hp, L, DIFF_V), whole)],
        out_specs=pl.BlockSpec((1, tq, hp * DIFF_V), lambda b, h, i: (b, i, h)),
        out_shape=jax.ShapeDtypeStruct((B, L, N_HEADS * DIFF_V), BF16),
        compiler_params=_cparams(("arbitrary", "arbitrary", "arbitrary")),
        name="diff_attention",
    )(lam_vecs, subln_g, hd["dq"], hc["dk"], hc["dv"], hd["dk"], hd["dv"])


def _na_plan(rows):
    kh = min(NA_KH, rows)
    n_row_off = 2 * NA_KH - 1
    col = np.arange(GRID_W)
    col_start = np.clip(col - NA_KW // 2, 0, GRID_W - NA_KW)
    col_ok = (col[None, :] >= col_start[:, None]) & (col[None, :] < col_start[:, None] + NA_KW)
    col_off = col[None, :] - col[:, None] + (NA_KW - 1)
    col_onehot = ((col_off[None] == np.arange(2 * NA_KW - 1)[:, None, None]) & col_ok[None]).astype(np.float32)
    patterns, starts, ids = {}, [], []
    for blk in range(rows // NA_Q_ROWS):
        r0 = blk * NA_Q_ROWS
        ks = int(np.clip(r0 - kh // 2, 0, rows - NA_K_ROWS))
        q_row = r0 + np.arange(NA_Q_ROWS)
        k_row = ks + np.arange(NA_K_ROWS)
        r_start = np.clip(q_row - kh // 2, 0, rows - kh)
        row_ok = (k_row[None, :] >= r_start[:, None]) & (k_row[None, :] < r_start[:, None] + kh)
        row_off = k_row[None, :] - q_row[:, None] + (NA_KH - 1)
        sel = np.where(row_ok, row_off, n_row_off)
        key = tuple(int(v) for v in sel.reshape(-1))
        if key not in patterns:
            patterns[key] = (len(patterns), sel)
        starts.append(ks)
        ids.append(patterns[key][0])
    row_sel = np.stack([p[1] for p in sorted(patterns.values(), key=lambda p: p[0])])
    return np.asarray(starts, np.int32), np.asarray(ids, np.int32), row_sel, col_onehot, col_ok


def _na_kernel(ks_ref, pid_ref, q_ref, k_ref, v_ref, kc_ref, vc_ref, bias_ref, o_ref):
    del pid_ref
    k_len = NA_K_ROWS * GRID_W
    start = pl.multiple_of(ks_ref[pl.program_id(1)] * GRID_W, GRID_W)
    for h in range(N_HEADS):
        q = q_ref[0, h]
        kw = k_ref[0, h, pl.ds(start, k_len), :]
        vw = v_ref[0, h, pl.ds(start, k_len), :]
        s_win = _dot_nt(q, kw) + bias_ref[0, h]
        s_ctx = _dot_nt(q, kc_ref[0, h])
        (p_win, p_ctx), denom = _softmax_parts([s_win, s_ctx])
        o = _dot(p_win.astype(BF16), vw) + _dot(p_ctx.astype(BF16), vc_ref[0, h])
        o_ref[0, :, h * HEAD_DIM:(h + 1) * HEAD_DIM] = (o / denom).astype(o_ref.dtype)


def _na_attention(rpb, hd, hc):
    B, H, L, _ = hd["nq"].shape
    C = hc["nk"].shape[2]
    rows = L // GRID_W
    starts, ids, row_sel, col_onehot, col_ok = _na_plan(rows)
    q_len = NA_Q_ROWS * GRID_W
    k_len = NA_K_ROWS * GRID_W
    slab = jnp.einsum("hrj,jqk->hrqk", rpb * LOG2E, col_onehot, precision=lax.Precision.HIGHEST)
    slab = jnp.where(col_ok[None, None], slab, NEG_INF)
    slab = jnp.concatenate([slab, jnp.full_like(slab[:, :1], NEG_INF)], axis=1)
    bias = jnp.take(slab, jnp.asarray(row_sel), axis=1)
    bias = jnp.transpose(bias, (1, 0, 2, 4, 3, 5)).reshape(row_sel.shape[0], H, q_len, k_len)
    whole = lambda b, i, ks, pid: (b, 0, 0, 0)
    grid_spec = pltpu.PrefetchScalarGridSpec(
        num_scalar_prefetch=2,
        grid=(B, rows // NA_Q_ROWS),
        in_specs=[pl.BlockSpec((1, H, q_len, HEAD_DIM), lambda b, i, ks, pid: (b, 0, i, 0)),
                  pl.BlockSpec((1, H, L, HEAD_DIM), whole),
                  pl.BlockSpec((1, H, L, HEAD_DIM), whole),
                  pl.BlockSpec((1, H, C, HEAD_DIM), whole),
                  pl.BlockSpec((1, H, C, HEAD_DIM), whole),
                  pl.BlockSpec((1, H, q_len, k_len), lambda b, i, ks, pid: (pid[i], 0, 0, 0))],
        out_specs=pl.BlockSpec((1, q_len, GROUP_WIDTH), lambda b, i, ks, pid: (b, i, 0)))
    return pl.pallas_call(
        _na_kernel,
        grid_spec=grid_spec,
        out_shape=jax.ShapeDtypeStruct((B, L, GROUP_WIDTH), BF16),
        compiler_params=_cparams(("arbitrary", "arbitrary")),
        name="neighborhood_attention",
    )(jnp.asarray(starts), jnp.asarray(ids), hd["nq"], hd["nk"], hd["nv"], hc["nk"], hc["nv"], bias)


def _ctx_attn_kernel(sink_ref, lam_ref, g_ref, qa_ref, ka_ref, va_ref, mq_ref, mk_ref, mv_ref,
                     dq_ref, dk_ref, dv_ref, nq_ref, nk_ref, nv_ref,
                     ya_ref, yb_ref, yc_ref, yd_ref, *, lam_init):
    def attend(q, k, v, extra=None):
        (p,), denom = _softmax_parts([_dot_nt(q, k)], extra)
        return _dot(p.astype(BF16), v) / denom

    group = N_HEADS // SWA_KV_HEADS
    lam = _lambda_value(lam_ref, lam_init)
    for h in range(N_HEADS):
        lanes = slice(h * HEAD_DIM, (h + 1) * HEAD_DIM)
        ya_ref[0, :, lanes] = attend(qa_ref[0, h], ka_ref[0, h // group], va_ref[0, h // group],
                                     sink_ref[h] * LOG2E).astype(ya_ref.dtype)
        yb_ref[0, :, lanes] = attend(mq_ref[0, h], mk_ref[0, h], mv_ref[0, h]).astype(yb_ref.dtype)
        o = (attend(dq_ref[0, 2 * h], dk_ref[0, 2 * h], dv_ref[0, h])
             - lam * attend(dq_ref[0, 2 * h + 1], dk_ref[0, 2 * h + 1], dv_ref[0, h]))
        yc_ref[0, :, lanes] = _sub_ln(o, g_ref, lam_init).astype(yc_ref.dtype)
        yd_ref[0, :, lanes] = attend(nq_ref[0, h], nk_ref[0, h], nv_ref[0, h]).astype(yd_ref.dtype)


def _ctx_attention(sink, lam_vecs, subln_g, hc, lam_init):
    names = ("qa", "ka", "va", "mq", "mk", "mv", "dq", "dk", "dv", "nq", "nk", "nv")
    B, _, C, _ = hc["qa"].shape
    whole4 = lambda b: (b, 0, 0, 0)
    const = lambda b: (0, 0)
    in_specs = [pl.BlockSpec(memory_space=pltpu.SMEM),
                pl.BlockSpec(lam_vecs.shape, const), pl.BlockSpec(subln_g.shape, const)]
    in_specs += [pl.BlockSpec((1,) + hc[n].shape[1:], whole4) for n in names]
    tok = pl.BlockSpec((1, C, GROUP_WIDTH), lambda b: (b, 0, 0))
    return pl.pallas_call(
        functools.partial(_ctx_attn_kernel, lam_init=lam_init),
        grid=(B,),
        in_specs=in_specs,
        out_specs=[tok] * N_MIXERS,
        out_shape=[jax.ShapeDtypeStruct((B, C, GROUP_WIDTH), BF16)] * N_MIXERS,
        compiler_params=_cparams(("arbitrary",)),
        name="context_attention",
    )(sink, lam_vecs, subln_g, *[hc[n] for n in names])


def _route(r):
    lane = lax.broadcasted_iota(jnp.int32, r.shape, 1)
    lane_f = lane.astype(F32)
    big = float(ROUTER_LANES)
    is_grp = (lane >= N_EXPERTS) & (lane < N_EXPERTS + N_GROUPS)
    g_log = jnp.where(is_grp, r, NEG_INF)
    g_max = jnp.max(g_log, axis=-1, keepdims=True)
    g_val = 1.0 / jnp.sum(jnp.exp(g_log - g_max), axis=-1, keepdims=True)
    g_idx = jnp.min(jnp.where(g_log == g_max, lane_f, big), axis=-1, keepdims=True) - float(N_EXPERTS)
    lane_grp = lax.shift_right_logical(lane, int(math.log2(EXPERTS_PER_GROUP))).astype(F32)
    in_grp = (lane < N_EXPERTS) & (lane_grp == g_idx)
    e_log = jnp.where(in_grp, r, NEG_INF)
    e_max = jnp.max(e_log, axis=-1, keepdims=True)
    i1 = jnp.min(jnp.where(e_log == e_max, lane_f, big), axis=-1, keepdims=True)
    e_rest = jnp.where(lane_f == i1, NEG_INF, e_log)
    e_max2 = jnp.max(e_rest, axis=-1, keepdims=True)
    i2 = jnp.min(jnp.where(e_rest == e_max2, lane_f, big), axis=-1, keepdims=True)
    p2 = jnp.exp(e_max2 - e_max)
    w1 = 1.0 / (1.0 + p2)
    w2 = p2 / (1.0 + p2)
    gates = g_val * jnp.where(lane_f == i1, w1, jnp.where(lane_f == i2, w2, 0.0))
    return jnp.where(lane == GROUP_LANE, g_idx, gates)


def _split_bf16(v):
    hi = v.astype(BF16)
    return hi, (v - hi.astype(F32)).astype(BF16)


def _outproj_kernel(x_ref, a_ref, b_ref, c_ref, d_ref, wo_ref, g1_ref, sc2_ref, sh2_ref,
                    lng_ref, lnb_ref, wr_ref, br_ref, x1_ref, rec_ref, *, alpha):
    y = _dot(a_ref[0], wo_ref[0:GROUP_WIDTH, :])
    for i, m_ref in enumerate((b_ref, c_ref, d_ref), start=1):
        y += _dot(m_ref[0], wo_ref[i * GROUP_WIDTH:(i + 1) * GROUP_WIDTH, :])
    x1 = _layer_norm(alpha * x_ref[0] + g1_ref[0] * y, lng_ref[...], lnb_ref[...])
    x1_ref[0] = x1
    h2 = x1 * (1.0 + sc2_ref[0]) + sh2_ref[0]
    rec_ref[0, :, :D_MODEL] = h2
    h_hi, h_lo = _split_bf16(h2)
    r = (_dot(h_hi, wr_ref[0]) + _dot(h_lo, wr_ref[0]) + _dot(h_hi, wr_ref[1])) + br_ref[...]
    rec_ref[0, :, D_MODEL:] = _route(r)


def _output_projection(x, att, wo, g1, sc2, sh2, ln_g, ln_b, wr, br, alpha):
    B, N, D = x.shape
    tm = min(512, N)
    tok = lambda w: pl.BlockSpec((1, tm, w), lambda b, i: (b, i, 0))
    vec = pl.BlockSpec((1, 1, D), lambda b, i: (b, 0, 0))
    const = lambda a: pl.BlockSpec(a.shape, lambda b, i: (0,) * a.ndim)
    return pl.pallas_call(
        functools.partial(_outproj_kernel, alpha=alpha),
        grid=(B, N // tm),
        in_specs=[tok(D)] + [tok(GROUP_WIDTH)] * N_MIXERS + [const(wo), vec, vec, vec,
                  const(ln_g), const(ln_b), const(wr), const(br)],
        out_specs=[tok(D), tok(REC_W)],
        out_shape=[jax.ShapeDtypeStruct((B, N, D), F32), jax.ShapeDtypeStruct((B, N, REC_W), F32)],
        compiler_params=_cparams(("arbitrary", "arbitrary")),
        name="output_projection",
    )(x, att[0], att[1], att[2], att[3], wo, g1, sc2, sh2, ln_g, ln_b, wr, br)


MOE_TM = 1024
MOE_ROWS = 512
MOE_HALVES = 2
MOE_EXPERTS_PER_STEP = EXPERTS_PER_GROUP // MOE_HALVES
GATHER_ROWS = 512
REC_W = D_MODEL + ROUTER_LANES


def _issue_row_gather(idx_ref, src_ref, dst_ref, sem):
    n = dst_ref.shape[0]

    def row_copy(j):
        return pltpu.make_async_copy(src_ref.at[pl.ds(idx_ref[0, 0, j], 1), :], dst_ref.at[pl.ds(j, 1), :], sem)

    def issue(j, carry):
        row_copy(j).start()
        return carry

    lax.fori_loop(0, n, issue, 0)
    pltpu.make_async_copy(src_ref.at[pl.ds(0, n), :], dst_ref, sem).wait()


def _gather_rows_kernel(idx_ref, src_ref, o_ref, sem):
    _issue_row_gather(idx_ref, src_ref, o_ref, sem)


def _gather_rows(src, idx):
    T, W = src.shape
    tg = min(GATHER_ROWS, T)
    return pl.pallas_call(
        _gather_rows_kernel,
        grid=(T // tg,),
        in_specs=[pl.BlockSpec((1, 1, tg), lambda i: (i, 0, 0), memory_space=pltpu.SMEM),
                  pl.BlockSpec(memory_space=pl.ANY)],
        out_specs=pl.BlockSpec((tg, W), lambda i: (i, 0)),
        out_shape=jax.ShapeDtypeStruct((T, W), src.dtype),
        scratch_shapes=[pltpu.SemaphoreType.DMA(())],
        compiler_params=_cparams(("arbitrary",)),
        name="gather_rows",
    )(idx.reshape(T // tg, 1, tg), src)


def _moe_ffn_kernel(tile_ref, group_ref, first_ref, valid_ref, rec_ref, wg_ref, wu_ref, wd_ref, f_ref):
    del tile_ref
    w = pl.program_id(0)
    half = pl.program_id(1)
    tm = f_ref.shape[0]

    @pl.when(valid_ref[w] != 0)
    def _():
        first = group_ref[w] * EXPERTS_PER_GROUP + half * MOE_EXPERTS_PER_STEP
        lane = lax.broadcasted_iota(jnp.int32, (MOE_ROWS, ROUTER_LANES), 1)
        for r0 in range(0, tm, MOE_ROWS):
            rows = slice(r0, r0 + MOE_ROWS)
            h = rec_ref[rows, :D_MODEL].astype(BF16)
            gates = rec_ref[rows, D_MODEL:]
            pre = _dot(h, wg_ref[0, 0])
            hid = pre * jax.nn.sigmoid(pre) * _dot(h, wu_ref[0, 0])
            cols = [jnp.sum(jnp.where(lane == first + e, gates, 0.0), axis=-1, keepdims=True)
                    for e in range(MOE_EXPERTS_PER_STEP)]
            hid = jnp.concatenate([hid[:, e * EXPERT_HIDDEN:(e + 1) * EXPERT_HIDDEN] * cols[e]
                                   for e in range(MOE_EXPERTS_PER_STEP)], axis=-1)
            y = _dot(hid.astype(BF16), wd_ref[0, 0])
            starts_tile = (first_ref[w] != 0) & (half == 0)

            @pl.when(starts_tile)
            def _():
                f_ref[rows, :] = y

            @pl.when(jnp.logical_not(starts_tile))
            def _():
                f_ref[rows, :] += y


def _moe_ffn(rec, items, wg, wu, wd):
    T = rec.shape[0]
    tm = min(MOE_TM, T)
    tile, group, first, valid = items
    n_items = tile.shape[0]
    wspec = lambda a: pl.BlockSpec((1, 1) + a.shape[2:], lambda w, s, t, g, f, v: (g[w], s, 0, 0))
    grid_spec = pltpu.PrefetchScalarGridSpec(
        num_scalar_prefetch=4,
        grid=(n_items, MOE_HALVES),
        in_specs=[pl.BlockSpec((tm, REC_W), lambda w, s, t, g, f, v: (t[w], 0)), wspec(wg), wspec(wu), wspec(wd)],
        out_specs=pl.BlockSpec((tm, D_MODEL), lambda w, s, t, g, f, v: (t[w], 0)))
    return pl.pallas_call(
        _moe_ffn_kernel,
        grid_spec=grid_spec,
        out_shape=jax.ShapeDtypeStruct((T, D_MODEL), F32),
        compiler_params=_cparams(("arbitrary", "arbitrary")),
        name="moe_ffn",
    )(tile, group, first, valid, rec, wg, wu, wd)


def _moe_items_sorted(sorted_group, tm):
    T = sorted_group.shape[0]
    nt = T // tm
    g_lo = sorted_group[0::tm]
    g_hi = sorted_group[tm - 1::tm]
    per_tile = g_hi - g_lo + 1
    start = jnp.cumsum(per_tile) - per_tile
    n_items = nt + N_GROUPS - 1
    w = jnp.arange(n_items, dtype=jnp.int32)
    tile = jnp.clip(jnp.searchsorted(start, w, side="right") - 1, 0, nt - 1).astype(jnp.int32)
    valid = w < jnp.sum(per_tile)
    offset = w - start[tile]
    group = jnp.where(valid, g_lo[tile] + offset, g_hi[nt - 1]).astype(jnp.int32)
    return tile, group, (valid & (offset == 0)).astype(jnp.int32), valid.astype(jnp.int32)


def _moe_items_dense(T, tm):
    nt = T // tm
    w = np.arange(nt * N_GROUPS, dtype=np.int32)
    return (jnp.asarray(w // N_GROUPS), jnp.asarray(w % N_GROUPS), jnp.asarray((w % N_GROUPS == 0).astype(np.int32)),
            jnp.ones((nt * N_GROUPS,), jnp.int32))


def _ln2_kernel(idx_ref, f_ref, x1_ref, g2_ref, lng_ref, lnb_ref, o_ref, buf_ref, sem, *, alpha):
    _issue_row_gather(idx_ref, f_ref, buf_ref, sem)
    o_ref[0] = _layer_norm(alpha * x1_ref[0] + g2_ref[0] * buf_ref[...], lng_ref[...], lnb_ref[...])


def _ln2_unsort(f_rows, idx, x1, g2, ln_g, ln_b, alpha):
    B, N, D = x1.shape
    tg = min(GATHER_ROWS, N)
    nb = N // tg
    const = lambda a: pl.BlockSpec(a.shape, lambda b, i: (0, 0))
    return pl.pallas_call(
        functools.partial(_ln2_kernel, alpha=alpha),
        grid=(B, nb),
        in_specs=[pl.BlockSpec((1, 1, tg), lambda b, i: (b * nb + i, 0, 0), memory_space=pltpu.SMEM),
                  pl.BlockSpec(memory_space=pl.ANY),
                  pl.BlockSpec((1, tg, D), lambda b, i: (b, i, 0)),
                  pl.BlockSpec((1, 1, D), lambda b, i: (b, 0, 0)), const(ln_g), const(ln_b)],
        out_specs=pl.BlockSpec((1, tg, D), lambda b, i: (b, i, 0)),
        out_shape=jax.ShapeDtypeStruct((B, N, D), F32),
        scratch_shapes=[pltpu.VMEM((tg, D), F32), pltpu.SemaphoreType.DMA(())],
        compiler_params=_cparams(("arbitrary", "arbitrary")),
        name="ln2_unsort",
    )(idx.reshape(B * nb, 1, tg), f_rows, x1, g2, ln_g, ln_b)


def _moe(rec, x1, wg, wu, wd, g2, ln_g, ln_b, alpha, sort):
    B, N, _ = rec.shape
    T = B * N
    tm = min(MOE_TM, T)
    flat = rec.reshape(T, REC_W)
    if sort:
        group = flat[:, D_MODEL + GROUP_LANE].astype(jnp.int32)
        order = jnp.argsort(group, stable=True).astype(jnp.int32)
        place = jnp.zeros((T,), jnp.int32).at[order].set(jnp.arange(T, dtype=jnp.int32))
        flat = _gather_rows(flat, order)
        items = _moe_items_sorted(group[order], tm)
    else:
        place = jnp.arange(T, dtype=jnp.int32)
        items = _moe_items_dense(T, tm)
    f_rows = _moe_ffn(flat, items, wg, wu, wd)
    return _ln2_unsort(f_rows, place, x1, g2, ln_g, ln_b, alpha)


def _rot_cols(w, d):
    k, n = w.shape
    q = d // 4
    w4 = w.reshape(k, n // d, 4, q)
    return jnp.stack([-w4[:, :, 1], w4[:, :, 0], -w4[:, :, 3], w4[:, :, 2]], axis=2).reshape(k, n)


def _rope_tables(L):
    t = jnp.arange(L, dtype=jnp.int32)
    rows = (t // GRID_W).astype(F32)
    cols = (t % GRID_W).astype(F32)

    def cos_sin(d):
        q = d // 4
        inv = ROPE_BASE ** (-jnp.arange(q, dtype=F32) / q)
        ar = rows[:, None] * inv[None, :]
        ac = cols[:, None] * inv[None, :]
        return (jnp.concatenate([jnp.cos(ar), jnp.cos(ar), jnp.cos(ac), jnp.cos(ac)], -1),
                jnp.concatenate([jnp.sin(ar), jnp.sin(ar), jnp.sin(ac), jnp.sin(ac)], -1))

    c64, s64 = cos_sin(HEAD_DIM)
    c32, s32 = cos_sin(MLA_ROPE)
    ones = jnp.ones((L, MLA_NOPE), F32)
    zeros_n = jnp.zeros((L, MLA_NOPE), F32)
    zeros_p = jnp.zeros((L, MLA_QK_PAD - MLA_NOPE - MLA_ROPE), F32)
    return (jnp.tile(c64, (1, N_HEADS)), jnp.tile(s64, (1, N_HEADS)),
            jnp.tile(c32, (1, 2 * N_HEADS)), jnp.tile(s32, (1, 2 * N_HEADS)),
            jnp.concatenate([ones, c32, zeros_p], -1), jnp.concatenate([zeros_n, s32, zeros_p], -1))


def _identity_tables(C):
    one = jnp.ones((C, GROUP_WIDTH), F32)
    zero = jnp.zeros((C, GROUP_WIDTH), F32)
    pad = MLA_QK_PAD - MLA_NOPE - MLA_ROPE
    cosm = jnp.concatenate([jnp.ones((C, MLA_NOPE + MLA_ROPE), F32), jnp.zeros((C, pad), F32)], -1)
    return one, zero, one, zero, cosm, jnp.zeros((C, MLA_QK_PAD), F32)


def _fused_in_weight(w_in):
    d = w_in.shape[0]
    p = jnp.split(w_in, IN_CUTS, axis=1)
    z = lambda n: jnp.zeros((d, n), w_in.dtype)
    pad_r = MLA_QK_PAD - MLA_NOPE - MLA_ROPE
    parts = {"qa": p[0], "qa_r": _rot_cols(p[0], HEAD_DIM), "ka": p[1], "ka_r": _rot_cols(p[1], HEAD_DIM),
             "va": p[2], "mqr": p[3], "mkvr": p[4],
             "mkr": jnp.concatenate([z(MLA_NOPE), p[5], z(pad_r)], 1),
             "mkr_r": jnp.concatenate([z(MLA_NOPE), _rot_cols(p[5], MLA_ROPE), z(pad_r)], 1),
             "dq": p[6], "dq_r": _rot_cols(p[6], DIFF_QK), "dk": p[7], "dk_r": _rot_cols(p[7], DIFF_QK),
             "dv": p[8], "nq": p[9], "nk": p[10], "nv": p[11]}
    return jnp.concatenate([parts[n] for n, _ in _SEG_LAYOUT], axis=1).astype(BF16)


def _mla_up_weights(w_uq, w_ukv):
    rq = w_uq.shape[0]
    pad_r = MLA_QK_PAD - MLA_NOPE - MLA_ROPE
    wq = w_uq.reshape(rq, N_HEADS, MLA_NOPE + MLA_ROPE)
    zq = lambda n: jnp.zeros((rq, N_HEADS, n), w_uq.dtype)
    rope_rot = _rot_cols(wq[:, :, MLA_NOPE:].reshape(rq, -1), MLA_ROPE).reshape(rq, N_HEADS, MLA_ROPE)
    main = jnp.concatenate([wq, zq(pad_r)], -1).reshape(rq, -1)
    rot = jnp.concatenate([zq(MLA_NOPE), rope_rot, zq(pad_r)], -1).reshape(rq, -1)
    wuq_ext = jnp.concatenate([main, rot], 1).astype(BF16)
    rk = w_ukv.shape[0]
    wkv = w_ukv.reshape(rk, N_HEADS, MLA_NOPE + MLA_V)
    k_part = jnp.concatenate([wkv[:, :, :MLA_NOPE],
                              jnp.zeros((rk, N_HEADS, MLA_QK_PAD - MLA_NOPE), w_ukv.dtype)], -1)
    wukv_ext = jnp.concatenate([k_part.reshape(rk, -1), wkv[:, :, MLA_NOPE:].reshape(rk, -1)], 1).astype(BF16)
    return wuq_ext, wukv_ext


def kernel(x, c, ctx, c_ctx, w_mod, b_mod, w_in, attn_sink, mla_q_norm, w_uq, mla_kv_norm, w_ukv,
           lam_q1, lam_k1, lam_q2, lam_k2, diff_subln, na_rpb, w_out, ln1_g, ln1_b,
           w_group, b_group, w_router, b_router, w_gate, w_up, w_down, ln2_g, ln2_b):
    B, L, D = x.shape
    C = ctx.shape[1]
    depth = w_mod.shape[0]
    alpha = (2 * depth) ** 0.25
    assert D == D_MODEL and B + 1 <= 8 and L % (NA_Q_ROWS * GRID_W) == 0

    cvec = jnp.concatenate([c, c_ctx[None, :], jnp.zeros((8 - B - 1, D), F32)], axis=0)
    mod = _modulation(cvec, w_mod, b_mod)
    lat_tables = _rope_tables(L)
    ctx_tables = _identity_tables(C)
    row = lambda a: a.reshape(1, -1)

    xc = ctx
    for l in range(depth):
        need_ctx = l < depth - 1
        lam_init = 0.8 - 0.6 * math.exp(-0.3 * l)
        chunks = [mod[l, :, i * D:(i + 1) * D] for i in range(6)]
        sh1, sc1, g1, sh2, sc2, g2 = [m[:B, None, :] for m in chunks]
        sh1c, sc1c, g1c, sh2c, sc2c, g2c = [jnp.broadcast_to(m[B:B + 1, None, :], (B, 1, D)) for m in chunks]

        w_all = _fused_in_weight(w_in[l])
        wuq_ext, wukv_ext = _mla_up_weights(w_uq[l], w_ukv[l])
        proj = functools.partial(_input_projection, w_all=w_all, qng=row(mla_q_norm[l]), wuq=wuq_ext,
                                 kvng=row(mla_kv_norm[l]), wukv=wukv_ext)
        hd = proj(x, sc1, sh1, tables=lat_tables)
        hc = proj(xc, sc1c, sh1c, tables=ctx_tables)

        lam_vecs = jnp.stack([lam_q1[l], lam_k1[l], lam_q2[l], lam_k2[l]])
        subln_g = row(diff_subln[l])
        att = (_swa_attention(attn_sink[l], hd, hc), _mla_attention(hd, hc),
               _diff_attention(lam_vecs, subln_g, hd, hc, lam_init), _na_attention(na_rpb[l], hd, hc))

        wo = w_out[l].astype(BF16)
        pad = ROUTER_LANES - N_EXPERTS - N_GROUPS
        wr = jnp.stack(_split_bf16(jnp.concatenate([w_router[l], w_group[l], jnp.zeros((D, pad), F32)], axis=1)))
        br = row(jnp.concatenate([b_router[l], b_group[l], jnp.zeros((pad,), F32)]))
        per_step = (N_GROUPS, MOE_HALVES, MOE_EXPERTS_PER_STEP)
        n_hid = MOE_EXPERTS_PER_STEP * EXPERT_HIDDEN
        side_by_side = lambda w: jnp.transpose(w.astype(BF16).reshape(per_step + (D, EXPERT_HIDDEN)),
                                               (0, 1, 3, 2, 4)).reshape(N_GROUPS, MOE_HALVES, D, n_hid)
        wg = side_by_side(w_gate[l])
        wu = side_by_side(w_up[l])
        wd = w_down[l].astype(BF16).reshape(N_GROUPS, MOE_HALVES, n_hid, D)
        post = functools.partial(_output_projection, wo=wo, ln_g=row(ln1_g[l]), ln_b=row(ln1_b[l]),
                                 wr=wr, br=br, alpha=alpha)
        ffn = functools.partial(_moe, wg=wg, wu=wu, wd=wd, ln_g=row(ln2_g[l]), ln_b=row(ln2_b[l]), alpha=alpha)

        x1, rec = post(x, att, g1=g1, sc2=sc2, sh2=sh2)
        x = ffn(rec, x1, g2=g2, sort=True)
        if need_ctx:
            att_c = _ctx_attention(attn_sink[l], lam_vecs, subln_g, hc, lam_init)
            xc1, rec_c = post(xc, att_c, g1=g1c, sc2=sc2c, sh2=sh2c)
            xc = ffn(rec_c, xc1, g2=g2c, sort=False)
    return x
```

```python
import functools
import math

import jax
import jax.numpy as jnp
import numpy as np
from jax import lax
from jax.experimental import pallas as pl
from jax.experimental.pallas import tpu as pltpu

F32 = jnp.float32
BF16 = jnp.bfloat16

D_MODEL = 1024
GRID_W = 64
HEAD_DIM = 64
N_HEADS = 4
N_MIXERS = 4
GROUP_WIDTH = N_HEADS * HEAD_DIM
SWA_KV_HEADS = 2
SWA_WINDOW = 128
SWA_BLOCK = 128
MLA_Q_RANK = 256
MLA_KV_RANK = 128
MLA_NOPE = 64
MLA_ROPE = 32
MLA_V = 64
MLA_QK_PAD = 128
DIFF_QK = 32
DIFF_V = 64
NA_KH = 8
NA_KW = 16
NA_Q_ROWS = 4
NA_K_ROWS = NA_Q_ROWS + NA_KH
N_GROUPS = 4
EXPERTS_PER_GROUP = 8
N_EXPERTS = N_GROUPS * EXPERTS_PER_GROUP
EXPERT_HIDDEN = 256
ROUTER_LANES = 128
GROUP_LANE = N_EXPERTS
ROPE_BASE = 10000.0
NORM_EPS = 1e-5
NEG_INF = -1e30
LOG2E = math.log2(math.e)
SWA_SCALE = HEAD_DIM ** -0.5 * LOG2E
MLA_SCALE = (MLA_NOPE + MLA_ROPE) ** -0.5 * LOG2E
DIFF_SCALE = DIFF_QK ** -0.5 * LOG2E
NA_SCALE = HEAD_DIM ** -0.5 * LOG2E
IN_SPLITS = (GROUP_WIDTH, SWA_KV_HEADS * HEAD_DIM, SWA_KV_HEADS * HEAD_DIM,
             MLA_Q_RANK, MLA_KV_RANK, MLA_ROPE,
             N_HEADS * 2 * DIFF_QK, N_HEADS * 2 * DIFF_QK, N_HEADS * DIFF_V,
             GROUP_WIDTH, GROUP_WIDTH, GROUP_WIDTH)
IN_CUTS = tuple(int(v) for v in np.cumsum(IN_SPLITS)[:-1])

_SEG_LAYOUT = (("qa", 256), ("qa_r", 256), ("ka", 128), ("ka_r", 128), ("va", 128),
               ("mqr", 256), ("mkvr", 128), ("mkr", 128), ("mkr_r", 128),
               ("dq", 256), ("dq_r", 256), ("dk", 256), ("dk_r", 256), ("dv", 256),
               ("nq", 256), ("nk", 256), ("nv", 256))
_SEG = {}
_off = 0
for _name, _w in _SEG_LAYOUT:
    _SEG[_name] = (_off, _off + _w)
    _off += _w
W_ALL_COLS = _off

VMEM_LIMIT_BYTES = 56 * 1024 * 1024


def _cparams(sem):
    return pltpu.CompilerParams(dimension_semantics=sem, vmem_limit_bytes=VMEM_LIMIT_BYTES)


def _dot(a, b):
    return jnp.dot(a, b, preferred_element_type=F32)


def _dot_nt(a, b):
    return lax.dot_general(a, b, (((1,), (1,)), ((), ())), preferred_element_type=F32)


def _rms(x):
    return x * lax.rsqrt(jnp.mean(x * x, axis=-1, keepdims=True) + NORM_EPS)


def _layer_norm(z, g, b):
    mu = jnp.mean(z, axis=-1, keepdims=True)
    zc = z - mu
    var = jnp.mean(zc * zc, axis=-1, keepdims=True)
    return zc * lax.rsqrt(var + NORM_EPS) * g + b


def _mod_kernel(c_ref, w_ref, b_ref, o_ref):
    c = c_ref[...]
    act = c * jax.nn.sigmoid(c)
    o_ref[0] = jnp.dot(act, w_ref[0], preferred_element_type=F32,
                       precision=lax.Precision.HIGHEST) + b_ref[0]


def _modulation(cvec, w_mod, b_mod):
    depth, d, n = w_mod.shape
    tn = 1024
    return pl.pallas_call(
        _mod_kernel,
        grid=(depth, n // tn),
        in_specs=[pl.BlockSpec((8, d), lambda l, j: (0, 0)),
                  pl.BlockSpec((1, d, tn), lambda l, j: (l, 0, j)),
                  pl.BlockSpec((1, 1, tn), lambda l, j: (l, 0, j))],
        out_specs=pl.BlockSpec((1, 8, tn), lambda l, j: (l, 0, j)),
        out_shape=jax.ShapeDtypeStruct((depth, 8, n), F32),
        compiler_params=_cparams(("arbitrary", "arbitrary")),
        name="modulation",
    )(cvec, w_mod, b_mod.reshape(depth, 1, n))


def _inproj_kernel(x_ref, sc_ref, sh_ref, w_ref, cos64_ref, sin64_ref, cos32_ref, sin32_ref,
                   cosm_ref, sinm_ref, qng_ref, wuq_ref, kvng_ref, wukv_ref,
                   qa_o, ka_o, va_o, mq_o, mk_o, mv_o, dq_o, dk_o, dv_o, nq_o, nk_o, nv_o):
    h = (x_ref[0] * (1.0 + sc_ref[0]) + sh_ref[0]).astype(BF16)

    def seg(name):
        a, b = _SEG[name]
        return _dot(h, w_ref[:, a:b])

    def split_heads(val, out_ref, n, width):
        for i in range(n):
            out_ref[0, i] = val[:, i * width:(i + 1) * width].astype(out_ref.dtype)

    cos64 = cos64_ref[...]
    sin64 = sin64_ref[...]
    cos32 = cos32_ref[...]
    sin32 = sin32_ref[...]
    cosm = cosm_ref[...]
    sinm = sinm_ref[...]

    qa = (seg("qa") * cos64 + seg("qa_r") * sin64) * SWA_SCALE
    split_heads(qa, qa_o, N_HEADS, HEAD_DIM)
    ka = seg("ka") * cos64[:, :128] + seg("ka_r") * sin64[:, :128]
    split_heads(ka, ka_o, SWA_KV_HEADS, HEAD_DIM)
    split_heads(seg("va"), va_o, SWA_KV_HEADS, HEAD_DIM)

    qn = (_rms(seg("mqr")) * qng_ref[...]).astype(BF16)
    uq = _dot(qn, wuq_ref[...])
    half = N_HEADS * MLA_QK_PAD
    for i in range(N_HEADS):
        a = i * MLA_QK_PAD
        mq = (uq[:, a:a + MLA_QK_PAD] * cosm + uq[:, half + a:half + a + MLA_QK_PAD] * sinm) * MLA_SCALE
        mq_o[0, i] = mq.astype(BF16)
    kvn = (_rms(seg("mkvr")) * kvng_ref[...]).astype(BF16)
    ukv = _dot(kvn, wukv_ref[...])
    k_rope = seg("mkr") * cosm + seg("mkr_r") * sinm
    for i in range(N_HEADS):
        a = i * MLA_QK_PAD
        mk_o[0, i] = (ukv[:, a:a + MLA_QK_PAD] + k_rope).astype(BF16)
        b = half + i * MLA_V
        mv_o[0, i] = ukv[:, b:b + MLA_V].astype(BF16)

    dq = (seg("dq") * cos32 + seg("dq_r") * sin32) * DIFF_SCALE
    split_heads(dq, dq_o, 2 * N_HEADS, DIFF_QK)
    dk = seg("dk") * cos32 + seg("dk_r") * sin32
    split_heads(dk, dk_o, 2 * N_HEADS, DIFF_QK)
    split_heads(seg("dv"), dv_o, N_HEADS, DIFF_V)

    split_heads(seg("nq") * NA_SCALE, nq_o, N_HEADS, HEAD_DIM)
    split_heads(seg("nk"), nk_o, N_HEADS, HEAD_DIM)
    split_heads(seg("nv"), nv_o, N_HEADS, HEAD_DIM)


_HEAD_OUTS = (("qa", N_HEADS, HEAD_DIM), ("ka", SWA_KV_HEADS, HEAD_DIM), ("va", SWA_KV_HEADS, HEAD_DIM),
              ("mq", N_HEADS, MLA_QK_PAD), ("mk", N_HEADS, MLA_QK_PAD), ("mv", N_HEADS, MLA_V),
              ("dq", 2 * N_HEADS, DIFF_QK), ("dk", 2 * N_HEADS, DIFF_QK), ("dv", N_HEADS, DIFF_V),
              ("nq", N_HEADS, HEAD_DIM), ("nk", N_HEADS, HEAD_DIM), ("nv", N_HEADS, HEAD_DIM))


def _input_projection(x, sc, sh, w_all, tables, qng, wuq, kvng, wukv):
    B, N, D = x.shape
    tm = min(512, N)
    tok = lambda i, b: (b, i, 0)
    vec = lambda i, b: (b, 0, 0)
    tab = lambda i, b: (i, 0)
    const = lambda i, b: (0, 0)
    in_specs = [pl.BlockSpec((1, tm, D), tok),
                pl.BlockSpec((1, 1, D), vec), pl.BlockSpec((1, 1, D), vec),
                pl.BlockSpec(w_all.shape, const)]
    in_specs += [pl.BlockSpec((tm, t.shape[1]), tab) for t in tables]
    in_specs += [pl.BlockSpec(a.shape, const) for a in (qng, wuq, kvng, wukv)]
    out_specs = [pl.BlockSpec((1, n, tm, w), lambda i, b: (b, 0, i, 0)) for _, n, w in _HEAD_OUTS]
    out_shape = [jax.ShapeDtypeStruct((B, n, N, w), BF16) for _, n, w in _HEAD_OUTS]
    outs = pl.pallas_call(
        _inproj_kernel,
        grid=(N // tm, B),
        in_specs=in_specs, out_specs=out_specs, out_shape=out_shape,
        compiler_params=_cparams(("arbitrary", "arbitrary")),
        name="input_projection",
    )(x, sc, sh, w_all, *tables, qng, wuq, kvng, wukv)
    return {name: o for (name, _, _), o in zip(_HEAD_OUTS, outs)}


def _softmax_parts(scores, extra_logit=None):
    m = functools.reduce(jnp.maximum, [jnp.max(s, axis=-1, keepdims=True) for s in scores])
    if extra_logit is not None:
        m = jnp.maximum(m, extra_logit)
    ps = [jnp.exp2(s - m) for s in scores]
    denom = functools.reduce(jnp.add, [jnp.sum(p, axis=-1, keepdims=True) for p in ps])
    if extra_logit is not None:
        denom = denom + jnp.exp2(extra_logit - m)
    return ps, denom


def _flash(qs, ctx_ks, ctx_vs, k_ats, v_ats, n_chunks, dv):
    tq = qs[0].shape[0]

    def update(carry, q, k, v):
        m, l, acc = carry
        s = _dot_nt(q, k)
        m_new = jnp.maximum(m, jnp.max(s, axis=-1, keepdims=True))
        alpha = jnp.exp2(m - m_new)
        p = jnp.exp2(s - m_new)
        l = alpha * l + jnp.sum(p, axis=-1, keepdims=True)
        acc = alpha * acc + _dot(p.astype(BF16), v)
        return m_new, l, acc

    init = (jnp.full((tq, 1), NEG_INF, F32), jnp.zeros((tq, 1), F32), jnp.zeros((tq, dv), F32))
    carries = tuple(update(init, q, k, v) for q, k, v in zip(qs, ctx_ks, ctx_vs))

    def body(i, carries):
        return tuple(update(cr, q, k_at(i), v_at(i))
                     for cr, q, k_at, v_at in zip(carries, qs, k_ats, v_ats))

    carries = lax.fori_loop(0, n_chunks, body, carries, unroll=min(FLASH_UNROLL, n_chunks))
    return [acc / l for _, l, acc in carries]


def _lambda_value(lam_ref, lam_init):
    lv = lam_ref[...]
    return (jnp.exp(jnp.sum(lv[0:1] * lv[1:2], axis=-1, keepdims=True))
            - jnp.exp(jnp.sum(lv[2:3] * lv[3:4], axis=-1, keepdims=True)) + lam_init)


def _sub_ln(o, g_ref, lam_init):
    return _rms(o) * g_ref[...] * (1.0 - lam_init)


SWA_TQ = 2 * SWA_BLOCK
SWA_SPAN = SWA_TQ + 2 * SWA_WINDOW


def _swa_kernel(sink_ref, q_ref, k_ref, v_ref, kc_ref, vc_ref, o_ref):
    L = k_ref.shape[2]
    qb = pl.program_id(1)
    start = pl.multiple_of(jnp.clip(qb * SWA_TQ - SWA_WINDOW, 0, L - SWA_SPAN), SWA_BLOCK)
    group = N_HEADS // SWA_KV_HEADS
    rows = group * SWA_TQ
    k_abs = start + lax.broadcasted_iota(jnp.int32, (rows, SWA_SPAN), 1)
    row = lax.broadcasted_iota(jnp.int32, (rows, SWA_SPAN), 0)
    q_abs = qb * SWA_TQ + jnp.where(row >= SWA_TQ, row - SWA_TQ, row)
    in_window = jnp.abs(k_abs - q_abs) <= SWA_WINDOW
    for hk in range(SWA_KV_HEADS):
        kb = k_ref[0, hk, pl.ds(start, SWA_SPAN), :]
        vb = v_ref[0, hk, pl.ds(start, SWA_SPAN), :]
        q = jnp.concatenate([q_ref[0, hk * group + g] for g in range(group)], axis=0)
        sink = jnp.concatenate([jnp.full((SWA_TQ, 1), sink_ref[hk * group + g], F32) for g in range(group)],
                               axis=0) * LOG2E
        s_band = jnp.where(in_window, _dot_nt(q, kb), NEG_INF)
        s_ctx = _dot_nt(q, kc_ref[0, hk])
        (p_band, p_ctx), denom = _softmax_parts([s_band, s_ctx], sink)
        o = (_dot(p_band.astype(BF16), vb) + _dot(p_ctx.astype(BF16), vc_ref[0, hk])) / denom
        for g in range(group):
            h = hk * group + g
            o_ref[0, :, h * HEAD_DIM:(h + 1) * HEAD_DIM] = o[g * SWA_TQ:(g + 1) * SWA_TQ].astype(o_ref.dtype)


def _swa_attention(sink, hd, hc):
    B, _, L, _ = hd["qa"].shape
    C = hc["ka"].shape[2]
    assert L % SWA_TQ == 0 and L >= SWA_SPAN
    whole = lambda b, i: (b, 0, 0, 0)
    return pl.pallas_call(
        _swa_kernel,
        grid=(B, L // SWA_TQ),
        in_specs=[pl.BlockSpec(memory_space=pltpu.SMEM),
                  pl.BlockSpec((1, N_HEADS, SWA_TQ, HEAD_DIM), lambda b, i: (b, 0, i, 0)),
                  pl.BlockSpec((1, SWA_KV_HEADS, L, HEAD_DIM), whole),
                  pl.BlockSpec((1, SWA_KV_HEADS, L, HEAD_DIM), whole),
                  pl.BlockSpec((1, SWA_KV_HEADS, C, HEAD_DIM), whole),
                  pl.BlockSpec((1, SWA_KV_HEADS, C, HEAD_DIM), whole)],
        out_specs=pl.BlockSpec((1, SWA_TQ, GROUP_WIDTH), lambda b, i: (b, i, 0)),
        out_shape=jax.ShapeDtypeStruct((B, L, GROUP_WIDTH), BF16),
        compiler_params=_cparams(("arbitrary", "arbitrary")),
        name="swa_attention",
    )(sink, hd["qa"], hd["ka"], hd["va"], hc["ka"], hc["va"])


GLOBAL_TQ = 512
GLOBAL_TK = 2048
HEADS_PER_STEP = 2
FLASH_UNROLL = 2


def _chunk_at(ref, lead, tk):
    return lambda i: ref[lead + (pl.ds(pl.multiple_of(i * tk, tk), tk), slice(None))]


def _mla_kernel(q_ref, kc_ref, vc_ref, k_ref, v_ref, o_ref, *, tk):
    n_chunks = k_ref.shape[2] // tk
    heads = range(HEADS_PER_STEP)
    outs = _flash([q_ref[0, j] for j in heads], [kc_ref[0, j] for j in heads], [vc_ref[0, j] for j in heads],
                  [_chunk_at(k_ref, (0, j), tk) for j in heads], [_chunk_at(v_ref, (0, j), tk) for j in heads],
                  n_chunks, MLA_V)
    o_ref[0] = jnp.concatenate(outs, axis=-1).astype(o_ref.dtype)


def _mla_attention(hd, hc):
    B, H, L, dk = hd["mq"].shape
    C = hc["mk"].shape[2]
    tq = min(GLOBAL_TQ, L)
    tk = min(GLOBAL_TK, L)
    hp = HEADS_PER_STEP
    whole = lambda b, h, i: (b, h, 0, 0)
    return pl.pallas_call(
        functools.partial(_mla_kernel, tk=tk),
        grid=(B, H // hp, L // tq),
        in_specs=[pl.BlockSpec((1, hp, tq, dk), lambda b, h, i: (b, h, i, 0)),
                  pl.BlockSpec((1, hp, C, dk), whole),
                  pl.BlockSpec((1, hp, C, MLA_V), whole),
                  pl.BlockSpec((1, hp, L, dk), whole),
                  pl.BlockSpec((1, hp, L, MLA_V), whole)],
        out_specs=pl.BlockSpec((1, tq, hp * MLA_V), lambda b, h, i: (b, i, h)),
        out_shape=jax.ShapeDtypeStruct((B, L, H * MLA_V), BF16),
        compiler_params=_cparams(("arbitrary", "arbitrary", "arbitrary")),
        name="mla_attention",
    )(hd["mq"], hc["mk"], hc["mv"], hd["mk"], hd["mv"])


def _diff_kernel(lam_ref, g_ref, q_ref, kc_ref, vc_ref, k_ref, v_ref, o_ref, *, tk, lam_init):
    n_chunks = k_ref.shape[2] // tk
    lam = _lambda_value(lam_ref, lam_init)
    heads = []
    for h in range(HEADS_PER_STEP):
        branches = (2 * h, 2 * h + 1)
        o1, o2 = _flash([q_ref[0, j] for j in branches], [kc_ref[0, j] for j in branches], [vc_ref[0, h]] * 2,
                        [_chunk_at(k_ref, (0, j), tk) for j in branches], [_chunk_at(v_ref, (0, h), tk)] * 2,
                        n_chunks, DIFF_V)
        heads.append(_sub_ln(o1 - lam * o2, g_ref, lam_init))
    o_ref[0] = jnp.concatenate(heads, axis=-1).astype(o_ref.dtype)


def _diff_attention(lam_vecs, subln_g, hd, hc, lam_init):
    B, _, L, dk = hd["dq"].shape
    C = hc["dk"].shape[2]
    tq = min(GLOBAL_TQ, L)
    tk = min(GLOBAL_TK, L)
    hp = HEADS_PER_STEP
    whole = lambda b, h, i: (b, h, 0, 0)
    const = lambda b, h, i: (0, 0)
    return pl.pallas_call(
        functools.partial(_diff_kernel, tk=tk, lam_init=lam_init),
        grid=(B, N_HEADS // hp, L // tq),
        in_specs=[pl.BlockSpec(lam_vecs.shape, const),
                  pl.BlockSpec(subln_g.shape, const),
                  pl.BlockSpec((1, 2 * hp, tq, dk), lambda b, h, i: (b, h, i, 0)),
                  pl.BlockSpec((1, 2 * hp, C, dk), whole),
                  pl.BlockSpec((1, hp, C, DIFF_V), whole),
                  pl.BlockSpec((1, 2 * hp, L, dk), whole),
                  pl.BlockSpec((1, hp, L, DIFF_V), whole)],
        out_specs=pl.BlockSpec((1, tq, hp * DIFF_V), lambda b, h, i: (b, i, h)),
        out_shape=jax.ShapeDtypeStruct((B, L, N_HEADS * DIFF_V), BF16),
        compiler_params=_cparams(("arbitrary", "arbitrary", "arbitrary")),
        name="diff_attention",
    )(lam_vecs, subln_g, hd["dq"], hc["dk"], hc["dv"], hd["dk"], hd["dv"])


def _na_plan(rows):
    kh = min(NA_KH, rows)
    n_row_off = 2 * NA_KH - 1
    col = np.arange(GRID_W)
    col_start = np.clip(col - NA_KW // 2, 0, GRID_W - NA_KW)
    col_ok = (col[None, :] >= col_start[:, None]) & (col[None, :] < col_start[:, None] + NA_KW)
    col_off = col[None, :] - col[:, None] + (NA_KW - 1)
    col_onehot = ((col_off[None] == np.arange(2 * NA_KW - 1)[:, None, None]) & col_ok[None]).astype(np.float32)
    patterns, starts, ids = {}, [], []
    for blk in range(rows // NA_Q_ROWS):
        r0 = blk * NA_Q_ROWS
        ks = int(np.clip(r0 - kh // 2, 0, rows - NA_K_ROWS))
        q_row = r0 + np.arange(NA_Q_ROWS)
        k_row = ks + np.arange(NA_K_ROWS)
        r_start = np.clip(q_row - kh // 2, 0, rows - kh)
        row_ok = (k_row[None, :] >= r_start[:, None]) & (k_row[None, :] < r_start[:, None] + kh)
        row_off = k_row[None, :] - q_row[:, None] + (NA_KH - 1)
        sel = np.where(row_ok, row_off, n_row_off)
        key = tuple(int(v) for v in sel.reshape(-1))
        if key not in patterns:
            patterns[key] = (len(patterns), sel)
        starts.append(ks)
        ids.append(patterns[key][0])
    row_sel = np.stack([p[1] for p in sorted(patterns.values(), key=lambda p: p[0])])
    return np.asarray(starts, np.int32), np.asarray(ids, np.int32), row_sel, col_onehot, col_ok


def _na_kernel(ks_ref, pid_ref, q_ref, k_ref, v_ref, kc_ref, vc_ref, bias_ref, o_ref):
    del pid_ref
    k_len = NA_K_ROWS * GRID_W
    start = pl.multiple_of(ks_ref[pl.program_id(1)] * GRID_W, GRID_W)
    for h in range(N_HEADS):
        q = q_ref[0, h]
        kw = k_ref[0, h, pl.ds(start, k_len), :]
        vw = v_ref[0, h, pl.ds(start, k_len), :]
        s_win = _dot_nt(q, kw) + bias_ref[0, h]
        s_ctx = _dot_nt(q, kc_ref[0, h])
        (p_win, p_ctx), denom = _softmax_parts([s_win, s_ctx])
        o = _dot(p_win.astype(BF16), vw) + _dot(p_ctx.astype(BF16), vc_ref[0, h])
        o_ref[0, :, h * HEAD_DIM:(h + 1) * HEAD_DIM] = (o / denom).astype(o_ref.dtype)


def _na_attention(rpb, hd, hc):
    B, H, L, _ = hd["nq"].shape
    C = hc["nk"].shape[2]
    rows = L // GRID_W
    starts, ids, row_sel, col_onehot, col_ok = _na_plan(rows)
    q_len = NA_Q_ROWS * GRID_W
    k_len = NA_K_ROWS * GRID_W
    slab = jnp.einsum("hrj,jqk->hrqk", rpb * LOG2E, col_onehot, precision=lax.Precision.HIGHEST)
    slab = jnp.where(col_ok[None, None], slab, NEG_INF)
    slab = jnp.concatenate([slab, jnp.full_like(slab[:, :1], NEG_INF)], axis=1)
    bias = jnp.take(slab, jnp.asarray(row_sel), axis=1)
    bias = jnp.transpose(bias, (1, 0, 2, 4, 3, 5)).reshape(row_sel.shape[0], H, q_len, k_len)
    whole = lambda b, i, ks, pid: (b, 0, 0, 0)
    grid_spec = pltpu.PrefetchScalarGridSpec(
        num_scalar_prefetch=2,
        grid=(B, rows // NA_Q_ROWS),
        in_specs=[pl.BlockSpec((1, H, q_len, HEAD_DIM), lambda b, i, ks, pid: (b, 0, i, 0)),
                  pl.BlockSpec((1, H, L, HEAD_DIM), whole),
                  pl.BlockSpec((1, H, L, HEAD_DIM), whole),
                  pl.BlockSpec((1, H, C, HEAD_DIM), whole),
                  pl.BlockSpec((1, H, C, HEAD_DIM), whole),
                  pl.BlockSpec((1, H, q_len, k_len), lambda b, i, ks, pid: (pid[i], 0, 0, 0))],
        out_specs=pl.BlockSpec((1, q_len, GROUP_WIDTH), lambda b, i, ks, pid: (b, i, 0)))
    return pl.pallas_call(
        _na_kernel,
        grid_spec=grid_spec,
        out_shape=jax.ShapeDtypeStruct((B, L, GROUP_WIDTH), BF16),
        compiler_params=_cparams(("arbitrary", "arbitrary")),
        name="neighborhood_attention",
    )(jnp.asarray(starts), jnp.asarray(ids), hd["nq"], hd["nk"], hd["nv"], hc["nk"], hc["nv"], bias)


def _ctx_attn_kernel(sink_ref, lam_ref, g_ref, qa_ref, ka_ref, va_ref, mq_ref, mk_ref, mv_ref,
                     dq_ref, dk_ref, dv_ref, nq_ref, nk_ref, nv_ref,
                     ya_ref, yb_ref, yc_ref, yd_ref, *, lam_init):
    def attend(q, k, v, extra=None):
        (p,), denom = _softmax_parts([_dot_nt(q, k)], extra)
        return _dot(p.astype(BF16), v) / denom

    group = N_HEADS // SWA_KV_HEADS
    lam = _lambda_value(lam_ref, lam_init)
    for h in range(N_HEADS):
        lanes = slice(h * HEAD_DIM, (h + 1) * HEAD_DIM)
        ya_ref[0, :, lanes] = attend(qa_ref[0, h], ka_ref[0, h // group], va_ref[0, h // group],
                                     sink_ref[h] * LOG2E).astype(ya_ref.dtype)
        yb_ref[0, :, lanes] = attend(mq_ref[0, h], mk_ref[0, h], mv_ref[0, h]).astype(yb_ref.dtype)
        o = (attend(dq_ref[0, 2 * h], dk_ref[0, 2 * h], dv_ref[0, h])
             - lam * attend(dq_ref[0, 2 * h + 1], dk_ref[0, 2 * h + 1], dv_ref[0, h]))
        yc_ref[0, :, lanes] = _sub_ln(o, g_ref, lam_init).astype(yc_ref.dtype)
        yd_ref[0, :, lanes] = attend(nq_ref[0, h], nk_ref[0, h], nv_ref[0, h]).astype(yd_ref.dtype)


def _ctx_attention(sink, lam_vecs, subln_g, hc, lam_init):
    names = ("qa", "ka", "va", "mq", "mk", "mv", "dq", "dk", "dv", "nq", "nk", "nv")
    B, _, C, _ = hc["qa"].shape
    whole4 = lambda b: (b, 0, 0, 0)
    const = lambda b: (0, 0)
    in_specs = [pl.BlockSpec(memory_space=pltpu.SMEM),
                pl.BlockSpec(lam_vecs.shape, const), pl.BlockSpec(subln_g.shape, const)]
    in_specs += [pl.BlockSpec((1,) + hc[n].shape[1:], whole4) for n in names]
    tok = pl.BlockSpec((1, C, GROUP_WIDTH), lambda b: (b, 0, 0))
    return pl.pallas_call(
        functools.partial(_ctx_attn_kernel, lam_init=lam_init),
        grid=(B,),
        in_specs=in_specs,
        out_specs=[tok] * N_MIXERS,
        out_shape=[jax.ShapeDtypeStruct((B, C, GROUP_WIDTH), BF16)] * N_MIXERS,
        compiler_params=_cparams(("arbitrary",)),
        name="context_attention",
    )(sink, lam_vecs, subln_g, *[hc[n] for n in names])


def _route(r):
    lane = lax.broadcasted_iota(jnp.int32, r.shape, 1)
    lane_f = lane.astype(F32)
    big = float(ROUTER_LANES)
    is_grp = (lane >= N_EXPERTS) & (lane < N_EXPERTS + N_GROUPS)
    g_log = jnp.where(is_grp, r, NEG_INF)
    g_max = jnp.max(g_log, axis=-1, keepdims=True)
    g_val = 1.0 / jnp.sum(jnp.exp(g_log - g_max), axis=-1, keepdims=True)
    g_idx = jnp.min(jnp.where(g_log == g_max, lane_f, big), axis=-1, keepdims=True) - float(N_EXPERTS)
    lane_grp = lax.shift_right_logical(lane, int(math.log2(EXPERTS_PER_GROUP))).astype(F32)
    in_grp = (lane < N_EXPERTS) & (lane_grp == g_idx)
    e_log = jnp.where(in_grp, r, NEG_INF)
    e_max = jnp.max(e_log, axis=-1, keepdims=True)
    i1 = jnp.min(jnp.where(e_log == e_max, lane_f, big), axis=-1, keepdims=True)
    e_rest = jnp.where(lane_f == i1, NEG_INF, e_log)
    e_max2 = jnp.max(e_rest, axis=-1, keepdims=True)
    i2 = jnp.min(jnp.where(e_rest == e_max2, lane_f, big), axis=-1, keepdims=True)
    p2 = jnp.exp(e_max2 - e_max)
    w1 = 1.0 / (1.0 + p2)
    w2 = p2 / (1.0 + p2)
    gates = g_val * jnp.where(lane_f == i1, w1, jnp.where(lane_f == i2, w2, 0.0))
    return jnp.where(lane == GROUP_LANE, g_idx, gates)


def _split_bf16(v):
    hi = v.astype(BF16)
    return hi, (v - hi.astype(F32)).astype(BF16)


def _outproj_kernel(x_ref, a_ref, b_ref, c_ref, d_ref, wo_ref, g1_ref, sc2_ref, sh2_ref,
                    lng_ref, lnb_ref, wr_ref, br_ref, x1_ref, rec_ref, *, alpha):
    y = _dot(a_ref[0], wo_ref[0:GROUP_WIDTH, :])
    for i, m_ref in enumerate((b_ref, c_ref, d_ref), start=1):
        y += _dot(m_ref[0], wo_ref[i * GROUP_WIDTH:(i + 1) * GROUP_WIDTH, :])
    x1 = _layer_norm(alpha * x_ref[0] + g1_ref[0] * y, lng_ref[...], lnb_ref[...])
    x1_ref[0] = x1
    h2 = x1 * (1.0 + sc2_ref[0]) + sh2_ref[0]
    rec_ref[0, :, :D_MODEL] = h2
    h_hi, h_lo = _split_bf16(h2)
    r = (_dot(h_hi, wr_ref[0]) + _dot(h_lo, wr_ref[0]) + _dot(h_hi, wr_ref[1])) + br_ref[...]
    rec_ref[0, :, D_MODEL:] = _route(r)


def _output_projection(x, att, wo, g1, sc2, sh2, ln_g, ln_b, wr, br, alpha):
    B, N, D = x.shape
    tm = min(512, N)
    tok = lambda w: pl.BlockSpec((1, tm, w), lambda b, i: (b, i, 0))
    vec = pl.BlockSpec((1, 1, D), lambda b, i: (b, 0, 0))
    const = lambda a: pl.BlockSpec(a.shape, lambda b, i: (0,) * a.ndim)
    return pl.pallas_call(
        functools.partial(_outproj_kernel, alpha=alpha),
        grid=(B, N // tm),
        in_specs=[tok(D)] + [tok(GROUP_WIDTH)] * N_MIXERS + [const(wo), vec, vec, vec,
                  const(ln_g), const(ln_b), const(wr), const(br)],
        out_specs=[tok(D), tok(REC_W)],
        out_shape=[jax.ShapeDtypeStruct((B, N, D), F32), jax.ShapeDtypeStruct((B, N, REC_W), F32)],
        compiler_params=_cparams(("arbitrary", "arbitrary")),
        name="output_projection",
    )(x, att[0], att[1], att[2], att[3], wo, g1, sc2, sh2, ln_g, ln_b, wr, br)


MOE_TM = 1024
MOE_ROWS = 512
MOE_STEPS = 4
MOE_EXPERTS_PER_STEP = EXPERTS_PER_GROUP // MOE_STEPS
GATHER_ROWS = 1024
GATHER_UNROLL = 8
REC_W = D_MODEL + ROUTER_LANES


def _issue_row_gather(idx_ref, src_ref, dst_ref, sem):
    n = dst_ref.shape[0]

    def row_copy(j):
        return pltpu.make_async_copy(src_ref.at[pl.ds(idx_ref[0, 0, j], 1), :], dst_ref.at[pl.ds(j, 1), :], sem)

    def issue(j, carry):
        row_copy(j).start()
        return carry

    lax.fori_loop(0, n, issue, 0, unroll=GATHER_UNROLL)
    pltpu.make_async_copy(src_ref.at[pl.ds(0, n), :], dst_ref, sem).wait()


def _gather_rows_kernel(idx_ref, src_ref, o_ref, sem):
    _issue_row_gather(idx_ref, src_ref, o_ref, sem)


def _gather_rows(src, idx):
    T, W = src.shape
    tg = min(GATHER_ROWS, T)
    return pl.pallas_call(
        _gather_rows_kernel,
        grid=(T // tg,),
        in_specs=[pl.BlockSpec((1, 1, tg), lambda i: (i, 0, 0), memory_space=pltpu.SMEM),
                  pl.BlockSpec(memory_space=pl.ANY)],
        out_specs=pl.BlockSpec((tg, W), lambda i: (i, 0)),
        out_shape=jax.ShapeDtypeStruct((T, W), src.dtype),
        scratch_shapes=[pltpu.SemaphoreType.DMA(())],
        compiler_params=_cparams(("arbitrary",)),
        name="gather_rows",
    )(idx.reshape(T // tg, 1, tg), src)


def _moe_ffn_kernel(tile_ref, group_ref, first_ref, valid_ref, rec_ref, wg_ref, wu_ref, wd_ref, f_ref):
    del tile_ref
    w = pl.program_id(0)
    step = pl.program_id(1)
    tm = f_ref.shape[0]

    @pl.when(valid_ref[w] != 0)
    def _():
        first = group_ref[w] * EXPERTS_PER_GROUP + step * MOE_EXPERTS_PER_STEP
        lane = lax.broadcasted_iota(jnp.int32, (MOE_ROWS, ROUTER_LANES), 1)
        experts = range(MOE_EXPERTS_PER_STEP)
        wgs = [wg_ref[0, 0, e].astype(BF16) for e in experts]
        wus = [wu_ref[0, 0, e].astype(BF16) for e in experts]
        wds = [wd_ref[0, 0, e].astype(BF16) for e in experts]
        for r0 in range(0, tm, MOE_ROWS):
            rows = slice(r0, r0 + MOE_ROWS)
            h = rec_ref[rows, :D_MODEL].astype(BF16)
            gates = rec_ref[rows, D_MODEL:]
            y = None
            for e in experts:
                pre = _dot(h, wgs[e])
                hid = pre * jax.nn.sigmoid(pre) * _dot(h, wus[e])
                col = jnp.sum(jnp.where(lane == first + e, gates, 0.0), axis=-1, keepdims=True)
                part = _dot((hid * col).astype(BF16), wds[e])
                y = part if y is None else y + part
            starts_tile = (first_ref[w] != 0) & (step == 0)

            @pl.when(starts_tile)
            def _():
                f_ref[rows, :] = y

            @pl.when(jnp.logical_not(starts_tile))
            def _():
                f_ref[rows, :] += y


def _moe_ffn(rec, items, wg, wu, wd, layer):
    T = rec.shape[0]
    tm = min(MOE_TM, T)
    tile, group, first, valid = items
    n_items = tile.shape[0]
    wspec = lambda a: pl.BlockSpec((1, 1, MOE_EXPERTS_PER_STEP) + a.shape[3:],
                                   lambda w, s, t, g, f, v: (layer, g[w], s, 0, 0))
    grid_spec = pltpu.PrefetchScalarGridSpec(
        num_scalar_prefetch=4,
        grid=(n_items, MOE_STEPS),
        in_specs=[pl.BlockSpec((tm, REC_W), lambda w, s, t, g, f, v: (t[w], 0)), wspec(wg), wspec(wu), wspec(wd)],
        out_specs=pl.BlockSpec((tm, D_MODEL), lambda w, s, t, g, f, v: (t[w], 0)))
    return pl.pallas_call(
        _moe_ffn_kernel,
        grid_spec=grid_spec,
        out_shape=jax.ShapeDtypeStruct((T, D_MODEL), F32),
        compiler_params=_cparams(("arbitrary", "arbitrary")),
        name="moe_ffn",
    )(tile, group, first, valid, rec, wg, wu, wd)


def _moe_items_sorted(sorted_group, tm):
    T = sorted_group.shape[0]
    nt = T // tm
    g_lo = sorted_group[0::tm]
    g_hi = sorted_group[tm - 1::tm]
    per_tile = g_hi - g_lo + 1
    start = jnp.cumsum(per_tile) - per_tile
    n_items = nt + N_GROUPS - 1
    w = jnp.arange(n_items, dtype=jnp.int32)
    tile = jnp.clip(jnp.searchsorted(start, w, side="right") - 1, 0, nt - 1).astype(jnp.int32)
    valid = w < jnp.sum(per_tile)
    offset = w - start[tile]
    group = jnp.where(valid, g_lo[tile] + offset, g_hi[nt - 1]).astype(jnp.int32)
    return tile, group, (valid & (offset == 0)).astype(jnp.int32), valid.astype(jnp.int32)


def _moe_items_dense(T, tm):
    nt = T // tm
    w = np.arange(nt * N_GROUPS, dtype=np.int32)
    return (jnp.asarray(w // N_GROUPS), jnp.asarray(w % N_GROUPS), jnp.asarray((w % N_GROUPS == 0).astype(np.int32)),
            jnp.ones((nt * N_GROUPS,), jnp.int32))


def _ln2_kernel(idx_ref, f_ref, x1_ref, g2_ref, lng_ref, lnb_ref, o_ref, buf_ref, sem, *, alpha):
    _issue_row_gather(idx_ref, f_ref, buf_ref, sem)
    o_ref[0] = _layer_norm(alpha * x1_ref[0] + g2_ref[0] * buf_ref[...], lng_ref[...], lnb_ref[...])


def _ln2_unsort(f_rows, idx, x1, g2, ln_g, ln_b, alpha):
    B, N, D = x1.shape
    tg = min(GATHER_ROWS, N)
    nb = N // tg
    const = lambda a: pl.BlockSpec(a.shape, lambda b, i: (0, 0))
    return pl.pallas_call(
        functools.partial(_ln2_kernel, alpha=alpha),
        grid=(B, nb),
        in_specs=[pl.BlockSpec((1, 1, tg), lambda b, i: (b * nb + i, 0, 0), memory_space=pltpu.SMEM),
                  pl.BlockSpec(memory_space=pl.ANY),
                  pl.BlockSpec((1, tg, D), lambda b, i: (b, i, 0)),
                  pl.BlockSpec((1, 1, D), lambda b, i: (b, 0, 0)), const(ln_g), const(ln_b)],
        out_specs=pl.BlockSpec((1, tg, D), lambda b, i: (b, i, 0)),
        out_shape=jax.ShapeDtypeStruct((B, N, D), F32),
        scratch_shapes=[pltpu.VMEM((tg, D), F32), pltpu.SemaphoreType.DMA(())],
        compiler_params=_cparams(("arbitrary", "arbitrary")),
        name="ln2_unsort",
    )(idx.reshape(B * nb, 1, tg), f_rows, x1, g2, ln_g, ln_b)


def _moe(rec, x1, wg, wu, wd, layer, g2, ln_g, ln_b, alpha, sort):
    B, N, _ = rec.shape
    T = B * N
    tm = min(MOE_TM, T)
    flat = rec.reshape(T, REC_W)
    if sort:
        group = flat[:, D_MODEL + GROUP_LANE].astype(jnp.int32)
        order = jnp.argsort(group, stable=True).astype(jnp.int32)
        place = jnp.argsort(order).astype(jnp.int32)
        flat = _gather_rows(flat, order)
        items = _moe_items_sorted(group[order], tm)
    else:
        place = jnp.arange(T, dtype=jnp.int32)
        items = _moe_items_dense(T, tm)
    f_rows = _moe_ffn(flat, items, wg, wu, wd, layer)
    return _ln2_unsort(f_rows, place, x1, g2, ln_g, ln_b, alpha)


def _rot_cols(w, d):
    k, n = w.shape
    q = d // 4
    w4 = w.reshape(k, n // d, 4, q)
    return jnp.stack([-w4[:, :, 1], w4[:, :, 0], -w4[:, :, 3], w4[:, :, 2]], axis=2).reshape(k, n)


def _rope_tables(L):
    t = jnp.arange(L, dtype=jnp.int32)
    rows = (t // GRID_W).astype(F32)
    cols = (t % GRID_W).astype(F32)

    def cos_sin(d):
        q = d // 4
        inv = ROPE_BASE ** (-jnp.arange(q, dtype=F32) / q)
        ar = rows[:, None] * inv[None, :]
        ac = cols[:, None] * inv[None, :]
        return (jnp.concatenate([jnp.cos(ar), jnp.cos(ar), jnp.cos(ac), jnp.cos(ac)], -1),
                jnp.concatenate([jnp.sin(ar), jnp.sin(ar), jnp.sin(ac), jnp.sin(ac)], -1))

    c64, s64 = cos_sin(HEAD_DIM)
    c32, s32 = cos_sin(MLA_ROPE)
    ones = jnp.ones((L, MLA_NOPE), F32)
    zeros_n = jnp.zeros((L, MLA_NOPE), F32)
    zeros_p = jnp.zeros((L, MLA_QK_PAD - MLA_NOPE - MLA_ROPE), F32)
    return (jnp.tile(c64, (1, N_HEADS)), jnp.tile(s64, (1, N_HEADS)),
            jnp.tile(c32, (1, 2 * N_HEADS)), jnp.tile(s32, (1, 2 * N_HEADS)),
            jnp.concatenate([ones, c32, zeros_p], -1), jnp.concatenate([zeros_n, s32, zeros_p], -1))


def _identity_tables(C):
    one = jnp.ones((C, GROUP_WIDTH), F32)
    zero = jnp.zeros((C, GROUP_WIDTH), F32)
    pad = MLA_QK_PAD - MLA_NOPE - MLA_ROPE
    cosm = jnp.concatenate([jnp.ones((C, MLA_NOPE + MLA_ROPE), F32), jnp.zeros((C, pad), F32)], -1)
    return one, zero, one, zero, cosm, jnp.zeros((C, MLA_QK_PAD), F32)


def _fused_in_weight(w_in):
    d = w_in.shape[0]
    p = jnp.split(w_in, IN_CUTS, axis=1)
    z = lambda n: jnp.zeros((d, n), w_in.dtype)
    pad_r = MLA_QK_PAD - MLA_NOPE - MLA_ROPE
    parts = {"qa": p[0], "qa_r": _rot_cols(p[0], HEAD_DIM), "ka": p[1], "ka_r": _rot_cols(p[1], HEAD_DIM),
             "va": p[2], "mqr": p[3], "mkvr": p[4],
             "mkr": jnp.concatenate([z(MLA_NOPE), p[5], z(pad_r)], 1),
             "mkr_r": jnp.concatenate([z(MLA_NOPE), _rot_cols(p[5], MLA_ROPE), z(pad_r)], 1),
             "dq": p[6], "dq_r": _rot_cols(p[6], DIFF_QK), "dk": p[7], "dk_r": _rot_cols(p[7], DIFF_QK),
             "dv": p[8], "nq": p[9], "nk": p[10], "nv": p[11]}
    return jnp.concatenate([parts[n] for n, _ in _SEG_LAYOUT], axis=1).astype(BF16)


def _mla_up_weights(w_uq, w_ukv):
    rq = w_uq.shape[0]
    pad_r = MLA_QK_PAD - MLA_NOPE - MLA_ROPE
    wq = w_uq.reshape(rq, N_HEADS, MLA_NOPE + MLA_ROPE)
    zq = lambda n: jnp.zeros((rq, N_HEADS, n), w_uq.dtype)
    rope_rot = _rot_cols(wq[:, :, MLA_NOPE:].reshape(rq, -1), MLA_ROPE).reshape(rq, N_HEADS, MLA_ROPE)
    main = jnp.concatenate([wq, zq(pad_r)], -1).reshape(rq, -1)
    rot = jnp.concatenate([zq(MLA_NOPE), rope_rot, zq(pad_r)], -1).reshape(rq, -1)
    wuq_ext = jnp.concatenate([main, rot], 1).astype(BF16)
    rk = w_ukv.shape[0]
    wkv = w_ukv.reshape(rk, N_HEADS, MLA_NOPE + MLA_V)
    k_part = jnp.concatenate([wkv[:, :, :MLA_NOPE],
                              jnp.zeros((rk, N_HEADS, MLA_QK_PAD - MLA_NOPE), w_ukv.dtype)], -1)
    wukv_ext = jnp.concatenate([k_part.reshape(rk, -1), wkv[:, :, MLA_NOPE:].reshape(rk, -1)], 1).astype(BF16)
    return wuq_ext, wukv_ext


def kernel(x, c, ctx, c_ctx, w_mod, b_mod, w_in, attn_sink, mla_q_norm, w_uq, mla_kv_norm, w_ukv,
           lam_q1, lam_k1, lam_q2, lam_k2, diff_subln, na_rpb, w_out, ln1_g, ln1_b,
           w_group, b_group, w_router, b_router, w_gate, w_up, w_down, ln2_g, ln2_b):
    B, L, D = x.shape
    C = ctx.shape[1]
    depth = w_mod.shape[0]
    alpha = (2 * depth) ** 0.25
    assert D == D_MODEL and B + 1 <= 8 and L % (NA_Q_ROWS * GRID_W) == 0

    cvec = jnp.concatenate([c, c_ctx[None, :], jnp.zeros((8 - B - 1, D), F32)], axis=0)
    mod = _modulation(cvec, w_mod, b_mod)
    lat_tables = _rope_tables(L)
    ctx_tables = _identity_tables(C)
    row = lambda a: a.reshape(1, -1)

    xc = ctx
    for l in range(depth):
        need_ctx = l < depth - 1
        lam_init = 0.8 - 0.6 * math.exp(-0.3 * l)
        chunks = [mod[l, :, i * D:(i + 1) * D] for i in range(6)]
        sh1, sc1, g1, sh2, sc2, g2 = [m[:B, None, :] for m in chunks]
        sh1c, sc1c, g1c, sh2c, sc2c, g2c = [jnp.broadcast_to(m[B:B + 1, None, :], (B, 1, D)) for m in chunks]

        w_all = _fused_in_weight(w_in[l])
        wuq_ext, wukv_ext = _mla_up_weights(w_uq[l], w_ukv[l])
        proj = functools.partial(_input_projection, w_all=w_all, qng=row(mla_q_norm[l]), wuq=wuq_ext,
                                 kvng=row(mla_kv_norm[l]), wukv=wukv_ext)
        hd = proj(x, sc1, sh1, tables=lat_tables)
        hc = proj(xc, sc1c, sh1c, tables=ctx_tables)

        lam_vecs = jnp.stack([lam_q1[l], lam_k1[l], lam_q2[l], lam_k2[l]])
        subln_g = row(diff_subln[l])
        att = (_swa_attention(attn_sink[l], hd, hc), _mla_attention(hd, hc),
               _diff_attention(lam_vecs, subln_g, hd, hc, lam_init), _na_attention(na_rpb[l], hd, hc))

        wo = w_out[l].astype(BF16)
        pad = ROUTER_LANES - N_EXPERTS - N_GROUPS
        wr = jnp.stack(_split_bf16(jnp.concatenate([w_router[l], w_group[l], jnp.zeros((D, pad), F32)], axis=1)))
        br = row(jnp.concatenate([b_router[l], b_group[l], jnp.zeros((pad,), F32)]))
        post = functools.partial(_output_projection, wo=wo, ln_g=row(ln1_g[l]), ln_b=row(ln1_b[l]),
                                 wr=wr, br=br, alpha=alpha)
        ffn = functools.partial(_moe, wg=w_gate, wu=w_up, wd=w_down, layer=l,
                                ln_g=row(ln2_g[l]), ln_b=row(ln2_b[l]), alpha=alpha)

        x1, rec = post(x, att, g1=g1, sc2=sc2, sh2=sh2)
        x = ffn(rec, x1, g2=g2, sort=True)
        if need_ctx:
            att_c = _ctx_attention(attn_sink[l], lam_vecs, subln_g, hc, lam_init)
            xc1, rec_c = post(xc, att_c, g1=g1c, sc2=sc2c, sh2=sh2c)
            xc = ffn(rec_c, xc1, g2=g2c, sort=False)
    return x
```

```python
import functools
import math

import jax
import jax.numpy as jnp
import numpy as np
from jax import lax
from jax.experimental import pallas as pl
from jax.experimental.pallas import tpu as pltpu

F32 = jnp.float32
BF16 = jnp.bfloat16

D_MODEL = 1024
GRID_W = 64
HEAD_DIM = 64
N_HEADS = 4
N_MIXERS = 4
GROUP_WIDTH = N_HEADS * HEAD_DIM
SWA_KV_HEADS = 2
SWA_WINDOW = 128
SWA_BLOCK = 128
MLA_Q_RANK = 256
MLA_KV_RANK = 128
MLA_NOPE = 64
MLA_ROPE = 32
MLA_V = 64
MLA_QK_PAD = 128
DIFF_QK = 32
DIFF_V = 64
NA_KH = 8
NA_KW = 16
NA_Q_ROWS = 4
NA_K_ROWS = NA_Q_ROWS + NA_KH
N_GROUPS = 4
EXPERTS_PER_GROUP = 8
N_EXPERTS = N_GROUPS * EXPERTS_PER_GROUP
EXPERT_HIDDEN = 256
ROUTER_LANES = 128
GROUP_LANE = N_EXPERTS
ROPE_BASE = 10000.0
NORM_EPS = 1e-5
NEG_INF = -1e30
LOG2E = math.log2(math.e)
SWA_SCALE = HEAD_DIM ** -0.5 * LOG2E
MLA_SCALE = (MLA_NOPE + MLA_ROPE) ** -0.5 * LOG2E
DIFF_SCALE = DIFF_QK ** -0.5 * LOG2E
NA_SCALE = HEAD_DIM ** -0.5 * LOG2E
IN_SPLITS = (GROUP_WIDTH, SWA_KV_HEADS * HEAD_DIM, SWA_KV_HEADS * HEAD_DIM,
             MLA_Q_RANK, MLA_KV_RANK, MLA_ROPE,
             N_HEADS * 2 * DIFF_QK, N_HEADS * 2 * DIFF_QK, N_HEADS * DIFF_V,
             GROUP_WIDTH, GROUP_WIDTH, GROUP_WIDTH)
IN_CUTS = tuple(int(v) for v in np.cumsum(IN_SPLITS)[:-1])

_SEG_LAYOUT = (("qa", 256), ("qa_r", 256), ("ka", 128), ("ka_r", 128), ("va", 128),
               ("mqr", 256), ("mkvr", 128), ("mkr", 128), ("mkr_r", 128),
               ("dq", 256), ("dq_r", 256), ("dk", 256), ("dk_r", 256), ("dv", 256),
               ("nq", 256), ("nk", 256), ("nv", 256))
_SEG = {}
_off = 0
for _name, _w in _SEG_LAYOUT:
    _SEG[_name] = (_off, _off + _w)
    _off += _w
W_ALL_COLS = _off

VMEM_LIMIT_BYTES = 56 * 1024 * 1024


def _cparams(sem):
    return pltpu.CompilerParams(dimension_semantics=sem, vmem_limit_bytes=VMEM_LIMIT_BYTES)


def _dot(a, b):
    return jnp.dot(a, b, preferred_element_type=F32)


def _dot_nt(a, b):
    return lax.dot_general(a, b, (((1,), (1,)), ((), ())), preferred_element_type=F32)


def _rms(x):
    return x * lax.rsqrt(jnp.mean(x * x, axis=-1, keepdims=True) + NORM_EPS)


def _layer_norm(z, g, b):
    mu = jnp.mean(z, axis=-1, keepdims=True)
    zc = z - mu
    var = jnp.mean(zc * zc, axis=-1, keepdims=True)
    return zc * lax.rsqrt(var + NORM_EPS) * g + b


def _mod_kernel(c_ref, w_ref, b_ref, o_ref):
    c = c_ref[...]
    act = c * jax.nn.sigmoid(c)
    o_ref[0] = jnp.dot(act, w_ref[0], preferred_element_type=F32,
                       precision=lax.Precision.HIGHEST) + b_ref[0]


def _modulation(cvec, w_mod, b_mod):
    depth, d, n = w_mod.shape
    tn = 1024
    return pl.pallas_call(
        _mod_kernel,
        grid=(depth, n // tn),
        in_specs=[pl.BlockSpec((8, d), lambda l, j: (0, 0)),
                  pl.BlockSpec((1, d, tn), lambda l, j: (l, 0, j)),
                  pl.BlockSpec((1, 1, tn), lambda l, j: (l, 0, j))],
        out_specs=pl.BlockSpec((1, 8, tn), lambda l, j: (l, 0, j)),
        out_shape=jax.ShapeDtypeStruct((depth, 8, n), F32),
        compiler_params=_cparams(("arbitrary", "arbitrary")),
        name="modulation",
    )(cvec, w_mod, b_mod.reshape(depth, 1, n))


def _inproj_kernel(x_ref, sc_ref, sh_ref, w_ref, cos64_ref, sin64_ref, cos32_ref, sin32_ref,
                   cosm_ref, sinm_ref, qng_ref, wuq_ref, kvng_ref, wukv_ref,
                   qa_o, ka_o, va_o, mq_o, mk_o, mv_o, dq_o, dk_o, dv_o, nq_o, nk_o, nv_o):
    h = (x_ref[0] * (1.0 + sc_ref[0]) + sh_ref[0]).astype(BF16)

    def seg(name):
        a, b = _SEG[name]
        return _dot(h, w_ref[:, a:b])

    def split_heads(val, out_ref, n, width):
        for i in range(n):
            out_ref[0, i] = val[:, i * width:(i + 1) * width].astype(out_ref.dtype)

    cos64 = cos64_ref[...]
    sin64 = sin64_ref[...]
    cos32 = cos32_ref[...]
    sin32 = sin32_ref[...]
    cosm = cosm_ref[...]
    sinm = sinm_ref[...]

    qa = (seg("qa") * cos64 + seg("qa_r") * sin64) * SWA_SCALE
    split_heads(qa, qa_o, N_HEADS, HEAD_DIM)
    ka = seg("ka") * cos64[:, :128] + seg("ka_r") * sin64[:, :128]
    split_heads(ka, ka_o, SWA_KV_HEADS, HEAD_DIM)
    split_heads(seg("va"), va_o, SWA_KV_HEADS, HEAD_DIM)

    qn = (_rms(seg("mqr")) * qng_ref[...]).astype(BF16)
    uq = _dot(qn, wuq_ref[...])
    half = N_HEADS * MLA_QK_PAD
    for i in range(N_HEADS):
        a = i * MLA_QK_PAD
        mq = (uq[:, a:a + MLA_QK_PAD] * cosm + uq[:, half + a:half + a + MLA_QK_PAD] * sinm) * MLA_SCALE
        mq_o[0, i] = mq.astype(BF16)
    kvn = (_rms(seg("mkvr")) * kvng_ref[...]).astype(BF16)
    ukv = _dot(kvn, wukv_ref[...])
    k_rope = seg("mkr") * cosm + seg("mkr_r") * sinm
    for i in range(N_HEADS):
        a = i * MLA_QK_PAD
        mk_o[0, i] = (ukv[:, a:a + MLA_QK_PAD] + k_rope).astype(BF16)
        b = half + i * MLA_V
        mv_o[0, i] = ukv[:, b:b + MLA_V].astype(BF16)

    dq = (seg("dq") * cos32 + seg("dq_r") * sin32) * DIFF_SCALE
    split_heads(dq, dq_o, 2 * N_HEADS, DIFF_QK)
    dk = seg("dk") * cos32 + seg("dk_r") * sin32
    split_heads(dk, dk_o, 2 * N_HEADS, DIFF_QK)
    split_heads(seg("dv"), dv_o, N_HEADS, DIFF_V)

    split_heads(seg("nq") * NA_SCALE, nq_o, N_HEADS, HEAD_DIM)
    split_heads(seg("nk"), nk_o, N_HEADS, HEAD_DIM)
    split_heads(seg("nv"), nv_o, N_HEADS, HEAD_DIM)


_HEAD_OUTS = (("qa", N_HEADS, HEAD_DIM), ("ka", SWA_KV_HEADS, HEAD_DIM), ("va", SWA_KV_HEADS, HEAD_DIM),
              ("mq", N_HEADS, MLA_QK_PAD), ("mk", N_HEADS, MLA_QK_PAD), ("mv", N_HEADS, MLA_V),
              ("dq", 2 * N_HEADS, DIFF_QK), ("dk", 2 * N_HEADS, DIFF_QK), ("dv", N_HEADS, DIFF_V),
              ("nq", N_HEADS, HEAD_DIM), ("nk", N_HEADS, HEAD_DIM), ("nv", N_HEADS, HEAD_DIM))


def _input_projection(x, sc, sh, w_all, tables, qng, wuq, kvng, wukv):
    B, N, D = x.shape
    tm = min(512, N)
    tok = lambda i, b: (b, i, 0)
    vec = lambda i, b: (b, 0, 0)
    tab = lambda i, b: (i, 0)
    const = lambda i, b: (0, 0)
    in_specs = [pl.BlockSpec((1, tm, D), tok),
                pl.BlockSpec((1, 1, D), vec), pl.BlockSpec((1, 1, D), vec),
                pl.BlockSpec(w_all.shape, const)]
    in_specs += [pl.BlockSpec((tm, t.shape[1]), tab) for t in tables]
    in_specs += [pl.BlockSpec(a.shape, const) for a in (qng, wuq, kvng, wukv)]
    out_specs = [pl.BlockSpec((1, n, tm, w), lambda i, b: (b, 0, i, 0)) for _, n, w in _HEAD_OUTS]
    out_shape = [jax.ShapeDtypeStruct((B, n, N, w), BF16) for _, n, w in _HEAD_OUTS]
    outs = pl.pallas_call(
        _inproj_kernel,
        grid=(N // tm, B),
        in_specs=in_specs, out_specs=out_specs, out_shape=out_shape,
        compiler_params=_cparams(("arbitrary", "arbitrary")),
        name="input_projection",
    )(x, sc, sh, w_all, *tables, qng, wuq, kvng, wukv)
    return {name: o for (name, _, _), o in zip(_HEAD_OUTS, outs)}


def _softmax_parts(scores, extra_logit=None):
    m = functools.reduce(jnp.maximum, [jnp.max(s, axis=-1, keepdims=True) for s in scores])
    if extra_logit is not None:
        m = jnp.maximum(m, extra_logit)
    ps = [jnp.exp2(s - m) for s in scores]
    denom = functools.reduce(jnp.add, [jnp.sum(p, axis=-1, keepdims=True) for p in ps])
    if extra_logit is not None:
        denom = denom + jnp.exp2(extra_logit - m)
    return ps, denom


def _flash(qs, ctx_ks, ctx_vs, k_ats, v_ats, n_chunks, dv):
    tq = qs[0].shape[0]

    def update(carry, q, k, v):
        m, l, acc = carry
        s = _dot_nt(q, k)
        m_new = jnp.maximum(m, jnp.max(s, axis=-1, keepdims=True))
        alpha = jnp.exp2(m - m_new)
        p = jnp.exp2(s - m_new)
        l = alpha * l + jnp.sum(p, axis=-1, keepdims=True)
        acc = alpha * acc + _dot(p.astype(BF16), v)
        return m_new, l, acc

    init = (jnp.full((tq, 1), NEG_INF, F32), jnp.zeros((tq, 1), F32), jnp.zeros((tq, dv), F32))
    carries = tuple(update(init, q, k, v) for q, k, v in zip(qs, ctx_ks, ctx_vs))

    def body(i, carries):
        return tuple(update(cr, q, k_at(i), v_at(i))
                     for cr, q, k_at, v_at in zip(carries, qs, k_ats, v_ats))

    carries = lax.fori_loop(0, n_chunks, body, carries, unroll=min(FLASH_UNROLL, n_chunks))
    return [acc / l for _, l, acc in carries]


def _lambda_value(lam_ref, lam_init):
    lv = lam_ref[...]
    return (jnp.exp(jnp.sum(lv[0:1] * lv[1:2], axis=-1, keepdims=True))
            - jnp.exp(jnp.sum(lv[2:3] * lv[3:4], axis=-1, keepdims=True)) + lam_init)


def _sub_ln(o, g_ref, lam_init):
    return _rms(o) * g_ref[...] * (1.0 - lam_init)


SWA_TQ = 2 * SWA_BLOCK
SWA_SPAN = SWA_TQ + 2 * SWA_WINDOW


def _swa_kernel(sink_ref, q_ref, k_ref, v_ref, kc_ref, vc_ref, o_ref):
    L = k_ref.shape[2]
    qb = pl.program_id(1)
    start = pl.multiple_of(jnp.clip(qb * SWA_TQ - SWA_WINDOW, 0, L - SWA_SPAN), SWA_BLOCK)
    group = N_HEADS // SWA_KV_HEADS
    rows = group * SWA_TQ
    n_keys = SWA_SPAN + kc_ref.shape[2]
    col = lax.broadcasted_iota(jnp.int32, (rows, n_keys), 1)
    row = lax.broadcasted_iota(jnp.int32, (rows, n_keys), 0)
    q_abs = qb * SWA_TQ + jnp.where(row >= SWA_TQ, row - SWA_TQ, row)
    allowed = (col >= SWA_SPAN) | (jnp.abs(start + col - q_abs) <= SWA_WINDOW)
    for hk in range(SWA_KV_HEADS):
        k_all = jnp.concatenate([k_ref[0, hk, pl.ds(start, SWA_SPAN), :], kc_ref[0, hk]], axis=0)
        v_all = jnp.concatenate([v_ref[0, hk, pl.ds(start, SWA_SPAN), :], vc_ref[0, hk]], axis=0)
        q = jnp.concatenate([q_ref[0, hk * group + g] for g in range(group)], axis=0)
        sink = jnp.concatenate([jnp.full((SWA_TQ, 1), sink_ref[hk * group + g], F32) for g in range(group)],
                               axis=0) * LOG2E
        (p,), denom = _softmax_parts([jnp.where(allowed, _dot_nt(q, k_all), NEG_INF)], sink)
        o = _dot(p.astype(BF16), v_all) / denom
        for g in range(group):
            h = hk * group + g
            o_ref[0, :, h * HEAD_DIM:(h + 1) * HEAD_DIM] = o[g * SWA_TQ:(g + 1) * SWA_TQ].astype(o_ref.dtype)


def _swa_attention(sink, hd, hc):
    B, _, L, _ = hd["qa"].shape
    C = hc["ka"].shape[2]
    assert L % SWA_TQ == 0 and L >= SWA_SPAN
    whole = lambda b, i: (b, 0, 0, 0)
    return pl.pallas_call(
        _swa_kernel,
        grid=(B, L // SWA_TQ),
        in_specs=[pl.BlockSpec(memory_space=pltpu.SMEM),
                  pl.BlockSpec((1, N_HEADS, SWA_TQ, HEAD_DIM), lambda b, i: (b, 0, i, 0)),
                  pl.BlockSpec((1, SWA_KV_HEADS, L, HEAD_DIM), whole),
                  pl.BlockSpec((1, SWA_KV_HEADS, L, HEAD_DIM), whole),
                  pl.BlockSpec((1, SWA_KV_HEADS, C, HEAD_DIM), whole),
                  pl.BlockSpec((1, SWA_KV_HEADS, C, HEAD_DIM), whole)],
        out_specs=pl.BlockSpec((1, SWA_TQ, GROUP_WIDTH), lambda b, i: (b, i, 0)),
        out_shape=jax.ShapeDtypeStruct((B, L, GROUP_WIDTH), BF16),
        compiler_params=_cparams(("arbitrary", "arbitrary")),
        name="swa_attention",
    )(sink, hd["qa"], hd["ka"], hd["va"], hc["ka"], hc["va"])


GLOBAL_TQ = 512
GLOBAL_TK = 2048
HEADS_PER_STEP = 2
FLASH_UNROLL = 2


def _chunk_at(ref, lead, tk):
    return lambda i: ref[lead + (pl.ds(pl.multiple_of(i * tk, tk), tk), slice(None))]


def _mla_kernel(q_ref, kc_ref, vc_ref, k_ref, v_ref, o_ref, *, tk):
    n_chunks = k_ref.shape[2] // tk
    heads = range(HEADS_PER_STEP)
    outs = _flash([q_ref[0, j] for j in heads], [kc_ref[0, j] for j in heads], [vc_ref[0, j] for j in heads],
                  [_chunk_at(k_ref, (0, j), tk) for j in heads], [_chunk_at(v_ref, (0, j), tk) for j in heads],
                  n_chunks, MLA_V)
    o_ref[0] = jnp.concatenate(outs, axis=-1).astype(o_ref.dtype)


def _mla_attention(hd, hc):
    B, H, L, dk = hd["mq"].shape
    C = hc["mk"].shape[2]
    tq = min(GLOBAL_TQ, L)
    tk = min(GLOBAL_TK, L)
    hp = HEADS_PER_STEP
    whole = lambda b, h, i: (b, h, 0, 0)
    return pl.pallas_call(
        functools.partial(_mla_kernel, tk=tk),
        grid=(B, H // hp, L // tq),
        in_specs=[pl.BlockSpec((1, hp, tq, dk), lambda b, h, i: (b, h, i, 0)),
                  pl.BlockSpec((1, hp, C, dk), whole),
                  pl.BlockSpec((1, hp, C, MLA_V), whole),
                  pl.BlockSpec((1, hp, L, dk), whole),
                  pl.BlockSpec((1, hp, L, MLA_V), whole)],
        out_specs=pl.BlockSpec((1, tq, hp * MLA_V), lambda b, h, i: (b, i, h)),
        out_shape=jax.ShapeDtypeStruct((B, L, H * MLA_V), BF16),
        compiler_params=_cparams(("arbitrary", "arbitrary", "arbitrary")),
        name="mla_attention",
    )(hd["mq"], hc["mk"], hc["mv"], hd["mk"], hd["mv"])


def _diff_kernel(lam_ref, g_ref, q_ref, kc_ref, vc_ref, k_ref, v_ref, o_ref, *, tk, lam_init):
    n_chunks = k_ref.shape[2] // tk
    lam = _lambda_value(lam_ref, lam_init)
    heads = []
    for h in range(HEADS_PER_STEP):
        branches = (2 * h, 2 * h + 1)
        o1, o2 = _flash([q_ref[0, j] for j in branches], [kc_ref[0, j] for j in branches], [vc_ref[0, h]] * 2,
                        [_chunk_at(k_ref, (0, j), tk) for j in branches], [_chunk_at(v_ref, (0, h), tk)] * 2,
                        n_chunks, DIFF_V)
        heads.append(_sub_ln(o1 - lam * o2, g_ref, lam_init))
    o_ref[0] = jnp.concatenate(heads, axis=-1).astype(o_ref.dtype)


def _diff_attention(lam_vecs, subln_g, hd, hc, lam_init):
    B, _, L, dk = hd["dq"].shape
    C = hc["dk"].shape[2]
    tq = min(GLOBAL_TQ, L)
    tk = min(GLOBAL_TK, L)
    hp = HEADS_PER_STEP
    whole = lambda b, h, i: (b, h, 0, 0)
    const = lambda b, h, i: (0, 0)
    return pl.pallas_call(
        functools.partial(_diff_kernel, tk=tk, lam_init=lam_init),
        grid=(B, N_HEADS // hp, L // tq),
        in_specs=[pl.BlockSpec(lam_vecs.shape, const),
                  pl.BlockSpec(subln_g.shape, const),
                  pl.BlockSpec((1, 2 * hp, tq, dk), lambda b, h, i: (b, h, i, 0)),
                  pl.BlockSpec((1, 2 * hp, C, dk), whole),
                  pl.BlockSpec((1, hp, C, DIFF_V), whole),
                  pl.BlockSpec((1, 2 * hp, L, dk), whole),
                  pl.BlockSpec((1, hp, L, DIFF_V), whole)],
        out_specs=pl.BlockSpec((1, tq, hp * DIFF_V), lambda b, h, i: (b, i, h)),
        out_shape=jax.ShapeDtypeStruct((B, L, N_HEADS * DIFF_V), BF16),
        compiler_params=_cparams(("arbitrary", "arbitrary", "arbitrary")),
        name="diff_attention",
    )(lam_vecs, subln_g, hd["dq"], hc["dk"], hc["dv"], hd["dk"], hd["dv"])


def _na_plan(rows):
    kh = min(NA_KH, rows)
    n_row_off = 2 * NA_KH - 1
    col = np.arange(GRID_W)
    col_start = np.clip(col - NA_KW // 2, 0, GRID_W - NA_KW)
    col_ok = (col[None, :] >= col_start[:, None]) & (col[None, :] < col_start[:, None] + NA_KW)
    col_off = col[None, :] - col[:, None] + (NA_KW - 1)
    col_onehot = ((col_off[None] == np.arange(2 * NA_KW - 1)[:, None, None]) & col_ok[None]).astype(np.float32)
    patterns, starts, ids = {}, [], []
    for blk in range(rows // NA_Q_ROWS):
        r0 = blk * NA_Q_ROWS
        ks = int(np.clip(r0 - kh // 2, 0, rows - NA_K_ROWS))
        q_row = r0 + np.arange(NA_Q_ROWS)
        k_row = ks + np.arange(NA_K_ROWS)
        r_start = np.clip(q_row - kh // 2, 0, rows - kh)
        row_ok = (k_row[None, :] >= r_start[:, None]) & (k_row[None, :] < r_start[:, None] + kh)
        row_off = k_row[None, :] - q_row[:, None] + (NA_KH - 1)
        sel = np.where(row_ok, row_off, n_row_off)
        key = tuple(int(v) for v in sel.reshape(-1))
        if key not in patterns:
            patterns[key] = (len(patterns), sel)
        starts.append(ks)
        ids.append(patterns[key][0])
    row_sel = np.stack([p[1] for p in sorted(patterns.values(), key=lambda p: p[0])])
    return np.asarray(starts, np.int32), np.asarray(ids, np.int32), row_sel, col_onehot, col_ok


def _na_kernel(ks_ref, pid_ref, q_ref, k_ref, v_ref, kc_ref, vc_ref, bias_ref, o_ref):
    del pid_ref
    k_len = NA_K_ROWS * GRID_W
    start = pl.multiple_of(ks_ref[pl.program_id(1)] * GRID_W, GRID_W)
    for h in range(N_HEADS):
        q = q_ref[0, h]
        kw = k_ref[0, h, pl.ds(start, k_len), :]
        vw = v_ref[0, h, pl.ds(start, k_len), :]
        s_win = _dot_nt(q, kw) + bias_ref[0, h]
        s_ctx = _dot_nt(q, kc_ref[0, h])
        (p_win, p_ctx), denom = _softmax_parts([s_win, s_ctx])
        o = _dot(p_win.astype(BF16), vw) + _dot(p_ctx.astype(BF16), vc_ref[0, h])
        o_ref[0, :, h * HEAD_DIM:(h + 1) * HEAD_DIM] = (o / denom).astype(o_ref.dtype)


def _na_attention(rpb, hd, hc):
    B, H, L, _ = hd["nq"].shape
    C = hc["nk"].shape[2]
    rows = L // GRID_W
    starts, ids, row_sel, col_onehot, col_ok = _na_plan(rows)
    q_len = NA_Q_ROWS * GRID_W
    k_len = NA_K_ROWS * GRID_W
    slab = jnp.einsum("hrj,jqk->hrqk", rpb * LOG2E, col_onehot, precision=lax.Precision.HIGHEST)
    slab = jnp.where(col_ok[None, None], slab, NEG_INF)
    slab = jnp.concatenate([slab, jnp.full_like(slab[:, :1], NEG_INF)], axis=1)
    bias = jnp.take(slab, jnp.asarray(row_sel), axis=1)
    bias = jnp.transpose(bias, (1, 0, 2, 4, 3, 5)).reshape(row_sel.shape[0], H, q_len, k_len)
    whole = lambda b, i, ks, pid: (b, 0, 0, 0)
    grid_spec = pltpu.PrefetchScalarGridSpec(
        num_scalar_prefetch=2,
        grid=(B, rows // NA_Q_ROWS),
        in_specs=[pl.BlockSpec((1, H, q_len, HEAD_DIM), lambda b, i, ks, pid: (b, 0, i, 0)),
                  pl.BlockSpec((1, H, L, HEAD_DIM), whole),
                  pl.BlockSpec((1, H, L, HEAD_DIM), whole),
                  pl.BlockSpec((1, H, C, HEAD_DIM), whole),
                  pl.BlockSpec((1, H, C, HEAD_DIM), whole),
                  pl.BlockSpec((1, H, q_len, k_len), lambda b, i, ks, pid: (pid[i], 0, 0, 0))],
        out_specs=pl.BlockSpec((1, q_len, GROUP_WIDTH), lambda b, i, ks, pid: (b, i, 0)))
    return pl.pallas_call(
        _na_kernel,
        grid_spec=grid_spec,
        out_shape=jax.ShapeDtypeStruct((B, L, GROUP_WIDTH), BF16),
        compiler_params=_cparams(("arbitrary", "arbitrary")),
        name="neighborhood_attention",
    )(jnp.asarray(starts), jnp.asarray(ids), hd["nq"], hd["nk"], hd["nv"], hc["nk"], hc["nv"], bias)


def _ctx_attn_kernel(sink_ref, lam_ref, g_ref, qa_ref, ka_ref, va_ref, mq_ref, mk_ref, mv_ref,
                     dq_ref, dk_ref, dv_ref, nq_ref, nk_ref, nv_ref,
                     ya_ref, yb_ref, yc_ref, yd_ref, *, lam_init):
    def attend(q, k, v, extra=None):
        (p,), denom = _softmax_parts([_dot_nt(q, k)], extra)
        return _dot(p.astype(BF16), v) / denom

    group = N_HEADS // SWA_KV_HEADS
    lam = _lambda_value(lam_ref, lam_init)
    for h in range(N_HEADS):
        lanes = slice(h * HEAD_DIM, (h + 1) * HEAD_DIM)
        ya_ref[0, :, lanes] = attend(qa_ref[0, h], ka_ref[0, h // group], va_ref[0, h // group],
                                     sink_ref[h] * LOG2E).astype(ya_ref.dtype)
        yb_ref[0, :, lanes] = attend(mq_ref[0, h], mk_ref[0, h], mv_ref[0, h]).astype(yb_ref.dtype)
        o = (attend(dq_ref[0, 2 * h], dk_ref[0, 2 * h], dv_ref[0, h])
             - lam * attend(dq_ref[0, 2 * h + 1], dk_ref[0, 2 * h + 1], dv_ref[0, h]))
        yc_ref[0, :, lanes] = _sub_ln(o, g_ref, lam_init).astype(yc_ref.dtype)
        yd_ref[0, :, lanes] = attend(nq_ref[0, h], nk_ref[0, h], nv_ref[0, h]).astype(yd_ref.dtype)


def _ctx_attention(sink, lam_vecs, subln_g, hc, lam_init):
    names = ("qa", "ka", "va", "mq", "mk", "mv", "dq", "dk", "dv", "nq", "nk", "nv")
    B, _, C, _ = hc["qa"].shape
    whole4 = lambda b: (b, 0, 0, 0)
    const = lambda b: (0, 0)
    in_specs = [pl.BlockSpec(memory_space=pltpu.SMEM),
                pl.BlockSpec(lam_vecs.shape, const), pl.BlockSpec(subln_g.shape, const)]
    in_specs += [pl.BlockSpec((1,) + hc[n].shape[1:], whole4) for n in names]
    tok = pl.BlockSpec((1, C, GROUP_WIDTH), lambda b: (b, 0, 0))
    return pl.pallas_call(
        functools.partial(_ctx_attn_kernel, lam_init=lam_init),
        grid=(B,),
        in_specs=in_specs,
        out_specs=[tok] * N_MIXERS,
        out_shape=[jax.ShapeDtypeStruct((B, C, GROUP_WIDTH), BF16)] * N_MIXERS,
        compiler_params=_cparams(("arbitrary",)),
        name="context_attention",
    )(sink, lam_vecs, subln_g, *[hc[n] for n in names])


def _route(r):
    lane = lax.broadcasted_iota(jnp.int32, r.shape, 1)
    lane_f = lane.astype(F32)
    big = float(ROUTER_LANES)
    is_grp = (lane >= N_EXPERTS) & (lane < N_EXPERTS + N_GROUPS)
    g_log = jnp.where(is_grp, r, NEG_INF)
    g_max = jnp.max(g_log, axis=-1, keepdims=True)
    g_val = 1.0 / jnp.sum(jnp.exp(g_log - g_max), axis=-1, keepdims=True)
    g_idx = jnp.min(jnp.where(g_log == g_max, lane_f, big), axis=-1, keepdims=True) - float(N_EXPERTS)
    lane_grp = lax.shift_right_logical(lane, int(math.log2(EXPERTS_PER_GROUP))).astype(F32)
    in_grp = (lane < N_EXPERTS) & (lane_grp == g_idx)
    e_log = jnp.where(in_grp, r, NEG_INF)
    e_max = jnp.max(e_log, axis=-1, keepdims=True)
    i1 = jnp.min(jnp.where(e_log == e_max, lane_f, big), axis=-1, keepdims=True)
    e_rest = jnp.where(lane_f == i1, NEG_INF, e_log)
    e_max2 = jnp.max(e_rest, axis=-1, keepdims=True)
    i2 = jnp.min(jnp.where(e_rest == e_max2, lane_f, big), axis=-1, keepdims=True)
    p2 = jnp.exp(e_max2 - e_max)
    w1 = 1.0 / (1.0 + p2)
    w2 = p2 / (1.0 + p2)
    gates = g_val * jnp.where(lane_f == i1, w1, jnp.where(lane_f == i2, w2, 0.0))
    return jnp.where(lane == GROUP_LANE, g_idx, gates)


def _split_bf16(v):
    hi = v.astype(BF16)
    return hi, (v - hi.astype(F32)).astype(BF16)


def _outproj_kernel(x_ref, a_ref, b_ref, c_ref, d_ref, wo_ref, g1_ref, sc2_ref, sh2_ref,
                    lng_ref, lnb_ref, wr_ref, br_ref, x1_ref, rec_ref, *, alpha):
    y = _dot(a_ref[0], wo_ref[0:GROUP_WIDTH, :])
    for i, m_ref in enumerate((b_ref, c_ref, d_ref), start=1):
        y += _dot(m_ref[0], wo_ref[i * GROUP_WIDTH:(i + 1) * GROUP_WIDTH, :])
    x1 = _layer_norm(alpha * x_ref[0] + g1_ref[0] * y, lng_ref[...], lnb_ref[...])
    x1_ref[0] = x1
    h2 = x1 * (1.0 + sc2_ref[0]) + sh2_ref[0]
    rec_ref[0, :, :D_MODEL] = h2
    h_hi, h_lo = _split_bf16(h2)
    r = (_dot(h_hi, wr_ref[0]) + _dot(h_lo, wr_ref[0]) + _dot(h_hi, wr_ref[1])) + br_ref[...]
    rec_ref[0, :, D_MODEL:] = _route(r)


def _output_projection(x, att, wo, g1, sc2, sh2, ln_g, ln_b, wr, br, alpha):
    B, N, D = x.shape
    tm = min(512, N)
    tok = lambda w: pl.BlockSpec((1, tm, w), lambda b, i: (b, i, 0))
    vec = pl.BlockSpec((1, 1, D), lambda b, i: (b, 0, 0))
    const = lambda a: pl.BlockSpec(a.shape, lambda b, i: (0,) * a.ndim)
    return pl.pallas_call(
        functools.partial(_outproj_kernel, alpha=alpha),
        grid=(B, N // tm),
        in_specs=[tok(D)] + [tok(GROUP_WIDTH)] * N_MIXERS + [const(wo), vec, vec, vec,
                  const(ln_g), const(ln_b), const(wr), const(br)],
        out_specs=[tok(D), tok(REC_W)],
        out_shape=[jax.ShapeDtypeStruct((B, N, D), F32), jax.ShapeDtypeStruct((B, N, REC_W), F32)],
        compiler_params=_cparams(("arbitrary", "arbitrary")),
        name="output_projection",
    )(x, att[0], att[1], att[2], att[3], wo, g1, sc2, sh2, ln_g, ln_b, wr, br)


MOE_TM = 1024
MOE_ROWS = 512
MOE_STEPS = 4
MOE_EXPERTS_PER_STEP = EXPERTS_PER_GROUP // MOE_STEPS
GATHER_ROWS = 1024
GATHER_UNROLL = 8
REC_W = D_MODEL + ROUTER_LANES


def _issue_row_gather(idx_ref, src_ref, dst_ref, sem):
    n = dst_ref.shape[0]

    def row_copy(j):
        return pltpu.make_async_copy(src_ref.at[pl.ds(idx_ref[0, 0, j], 1), :], dst_ref.at[pl.ds(j, 1), :], sem)

    def issue(j, carry):
        row_copy(j).start()
        return carry

    lax.fori_loop(0, n, issue, 0, unroll=GATHER_UNROLL)
    pltpu.make_async_copy(src_ref.at[pl.ds(0, n), :], dst_ref, sem).wait()


def _gather_rows_kernel(idx_ref, src_ref, o_ref, sem):
    _issue_row_gather(idx_ref, src_ref, o_ref, sem)


def _gather_rows(src, idx):
    T, W = src.shape
    tg = min(GATHER_ROWS, T)
    return pl.pallas_call(
        _gather_rows_kernel,
        grid=(T // tg,),
        in_specs=[pl.BlockSpec((1, 1, tg), lambda i: (i, 0, 0), memory_space=pltpu.SMEM),
                  pl.BlockSpec(memory_space=pl.ANY)],
        out_specs=pl.BlockSpec((tg, W), lambda i: (i, 0)),
        out_shape=jax.ShapeDtypeStruct((T, W), src.dtype),
        scratch_shapes=[pltpu.SemaphoreType.DMA(())],
        compiler_params=_cparams(("arbitrary",)),
        name="gather_rows",
    )(idx.reshape(T // tg, 1, tg), src)


def _moe_ffn_kernel(tile_ref, group_ref, first_ref, valid_ref, rec_ref, wg_ref, wu_ref, wd_ref, f_ref):
    del tile_ref
    w = pl.program_id(0)
    step = pl.program_id(1)
    tm = f_ref.shape[0]

    @pl.when(valid_ref[w] != 0)
    def _():
        first = group_ref[w] * EXPERTS_PER_GROUP + step * MOE_EXPERTS_PER_STEP
        lane = lax.broadcasted_iota(jnp.int32, (MOE_ROWS, ROUTER_LANES), 1)
        experts = range(MOE_EXPERTS_PER_STEP)
        wgs = [wg_ref[0, 0, e].astype(BF16) for e in experts]
        wus = [wu_ref[0, 0, e].astype(BF16) for e in experts]
        wds = [wd_ref[0, 0, e].astype(BF16) for e in experts]
        for r0 in range(0, tm, MOE_ROWS):
            rows = slice(r0, r0 + MOE_ROWS)
            h = rec_ref[rows, :D_MODEL].astype(BF16)
            gates = rec_ref[rows, D_MODEL:]
            y = None
            for e in experts:
                pre = _dot(h, wgs[e])
                hid = pre * jax.nn.sigmoid(pre) * _dot(h, wus[e])
                col = jnp.sum(jnp.where(lane == first + e, gates, 0.0), axis=-1, keepdims=True)
                part = _dot((hid * col).astype(BF16), wds[e])
                y = part if y is None else y + part
            starts_tile = (first_ref[w] != 0) & (step == 0)

            @pl.when(starts_tile)
            def _():
                f_ref[rows, :] = y

            @pl.when(jnp.logical_not(starts_tile))
            def _():
                f_ref[rows, :] += y


def _moe_ffn(rec, items, wg, wu, wd, layer):
    T = rec.shape[0]
    tm = min(MOE_TM, T)
    tile, group, first, valid = items
    n_items = tile.shape[0]
    wspec = lambda a: pl.BlockSpec((1, 1, MOE_EXPERTS_PER_STEP) + a.shape[3:],
                                   lambda w, s, t, g, f, v: (layer, g[w], s, 0, 0))
    grid_spec = pltpu.PrefetchScalarGridSpec(
        num_scalar_prefetch=4,
        grid=(n_items, MOE_STEPS),
        in_specs=[pl.BlockSpec((tm, REC_W), lambda w, s, t, g, f, v: (t[w], 0)), wspec(wg), wspec(wu), wspec(wd)],
        out_specs=pl.BlockSpec((tm, D_MODEL), lambda w, s, t, g, f, v: (t[w], 0)))
    return pl.pallas_call(
        _moe_ffn_kernel,
        grid_spec=grid_spec,
        out_shape=jax.ShapeDtypeStruct((T, D_MODEL), F32),
        compiler_params=_cparams(("arbitrary", "arbitrary")),
        name="moe_ffn",
    )(tile, group, first, valid, rec, wg, wu, wd)


def _moe_items_sorted(sorted_group, tm):
    T = sorted_group.shape[0]
    nt = T // tm
    g_lo = sorted_group[0::tm]
    g_hi = sorted_group[tm - 1::tm]
    per_tile = g_hi - g_lo + 1
    start = jnp.cumsum(per_tile) - per_tile
    n_items = nt + N_GROUPS - 1
    w = jnp.arange(n_items, dtype=jnp.int32)
    tile = jnp.clip(jnp.searchsorted(start, w, side="right") - 1, 0, nt - 1).astype(jnp.int32)
    valid = w < jnp.sum(per_tile)
    offset = w - start[tile]
    group = jnp.where(valid, g_lo[tile] + offset, g_hi[nt - 1]).astype(jnp.int32)
    return tile, group, (valid & (offset == 0)).astype(jnp.int32), valid.astype(jnp.int32)


def _moe_items_dense(T, tm):
    nt = T // tm
    w = np.arange(nt * N_GROUPS, dtype=np.int32)
    return (jnp.asarray(w // N_GROUPS), jnp.asarray(w % N_GROUPS), jnp.asarray((w % N_GROUPS == 0).astype(np.int32)),
            jnp.ones((nt * N_GROUPS,), jnp.int32))


def _ln2_kernel(idx_ref, f_ref, x1_ref, g2_ref, lng_ref, lnb_ref, o_ref, buf_ref, sem, *, alpha):
    _issue_row_gather(idx_ref, f_ref, buf_ref, sem)
    o_ref[0] = _layer_norm(alpha * x1_ref[0] + g2_ref[0] * buf_ref[...], lng_ref[...], lnb_ref[...])


def _ln2_unsort(f_rows, idx, x1, g2, ln_g, ln_b, alpha):
    B, N, D = x1.shape
    tg = min(GATHER_ROWS, N)
    nb = N // tg
    const = lambda a: pl.BlockSpec(a.shape, lambda b, i: (0, 0))
    return pl.pallas_call(
        functools.partial(_ln2_kernel, alpha=alpha),
        grid=(B, nb),
        in_specs=[pl.BlockSpec((1, 1, tg), lambda b, i: (b * nb + i, 0, 0), memory_space=pltpu.SMEM),
                  pl.BlockSpec(memory_space=pl.ANY),
                  pl.BlockSpec((1, tg, D), lambda b, i: (b, i, 0)),
                  pl.BlockSpec((1, 1, D), lambda b, i: (b, 0, 0)), const(ln_g), const(ln_b)],
        out_specs=pl.BlockSpec((1, tg, D), lambda b, i: (b, i, 0)),
        out_shape=jax.ShapeDtypeStruct((B, N, D), F32),
        scratch_shapes=[pltpu.VMEM((tg, D), F32), pltpu.SemaphoreType.DMA(())],
        compiler_params=_cparams(("arbitrary", "arbitrary")),
        name="ln2_unsort",
    )(idx.reshape(B * nb, 1, tg), f_rows, x1, g2, ln_g, ln_b)


def _moe(rec, x1, wg, wu, wd, layer, g2, ln_g, ln_b, alpha, sort):
    B, N, _ = rec.shape
    T = B * N
    tm = min(MOE_TM, T)
    flat = rec.reshape(T, REC_W)
    if sort:
        group = flat[:, D_MODEL + GROUP_LANE].astype(jnp.int32)
        order = jnp.argsort(group, stable=True).astype(jnp.int32)
        place = jnp.argsort(order).astype(jnp.int32)
        flat = _gather_rows(flat, order)
        items = _moe_items_sorted(group[order], tm)
    else:
        place = jnp.arange(T, dtype=jnp.int32)
        items = _moe_items_dense(T, tm)
    f_rows = _moe_ffn(flat, items, wg, wu, wd, layer)
    return _ln2_unsort(f_rows, place, x1, g2, ln_g, ln_b, alpha)


def _rot_cols(w, d):
    k, n = w.shape
    q = d // 4
    w4 = w.reshape(k, n // d, 4, q)
    return jnp.stack([-w4[:, :, 1], w4[:, :, 0], -w4[:, :, 3], w4[:, :, 2]], axis=2).reshape(k, n)


def _rope_tables(L):
    t = jnp.arange(L, dtype=jnp.int32)
    rows = (t // GRID_W).astype(F32)
    cols = (t % GRID_W).astype(F32)

    def cos_sin(d):
        q = d // 4
        inv = ROPE_BASE ** (-jnp.arange(q, dtype=F32) / q)
        ar = rows[:, None] * inv[None, :]
        ac = cols[:, None] * inv[None, :]
        return (jnp.concatenate([jnp.cos(ar), jnp.cos(ar), jnp.cos(ac), jnp.cos(ac)], -1),
                jnp.concatenate([jnp.sin(ar), jnp.sin(ar), jnp.sin(ac), jnp.sin(ac)], -1))

    c64, s64 = cos_sin(HEAD_DIM)
    c32, s32 = cos_sin(MLA_ROPE)
    ones = jnp.ones((L, MLA_NOPE), F32)
    zeros_n = jnp.zeros((L, MLA_NOPE), F32)
    zeros_p = jnp.zeros((L, MLA_QK_PAD - MLA_NOPE - MLA_ROPE), F32)
    return (jnp.tile(c64, (1, N_HEADS)), jnp.tile(s64, (1, N_HEADS)),
            jnp.tile(c32, (1, 2 * N_HEADS)), jnp.tile(s32, (1, 2 * N_HEADS)),
            jnp.concatenate([ones, c32, zeros_p], -1), jnp.concatenate([zeros_n, s32, zeros_p], -1))


def _identity_tables(C):
    one = jnp.ones((C, GROUP_WIDTH), F32)
    zero = jnp.zeros((C, GROUP_WIDTH), F32)
    pad = MLA_QK_PAD - MLA_NOPE - MLA_ROPE
    cosm = jnp.concatenate([jnp.ones((C, MLA_NOPE + MLA_ROPE), F32), jnp.zeros((C, pad), F32)], -1)
    return one, zero, one, zero, cosm, jnp.zeros((C, MLA_QK_PAD), F32)


def _fused_in_weight(w_in):
    d = w_in.shape[0]
    p = jnp.split(w_in, IN_CUTS, axis=1)
    z = lambda n: jnp.zeros((d, n), w_in.dtype)
    pad_r = MLA_QK_PAD - MLA_NOPE - MLA_ROPE
    parts = {"qa": p[0], "qa_r": _rot_cols(p[0], HEAD_DIM), "ka": p[1], "ka_r": _rot_cols(p[1], HEAD_DIM),
             "va": p[2], "mqr": p[3], "mkvr": p[4],
             "mkr": jnp.concatenate([z(MLA_NOPE), p[5], z(pad_r)], 1),
             "mkr_r": jnp.concatenate([z(MLA_NOPE), _rot_cols(p[5], MLA_ROPE), z(pad_r)], 1),
             "dq": p[6], "dq_r": _rot_cols(p[6], DIFF_QK), "dk": p[7], "dk_r": _rot_cols(p[7], DIFF_QK),
             "dv": p[8], "nq": p[9], "nk": p[10], "nv": p[11]}
    return jnp.concatenate([parts[n] for n, _ in _SEG_LAYOUT], axis=1).astype(BF16)


def _mla_up_weights(w_uq, w_ukv):
    rq = w_uq.shape[0]
    pad_r = MLA_QK_PAD - MLA_NOPE - MLA_ROPE
    wq = w_uq.reshape(rq, N_HEADS, MLA_NOPE + MLA_ROPE)
    zq = lambda n: jnp.zeros((rq, N_HEADS, n), w_uq.dtype)
    rope_rot = _rot_cols(wq[:, :, MLA_NOPE:].reshape(rq, -1), MLA_ROPE).reshape(rq, N_HEADS, MLA_ROPE)
    main = jnp.concatenate([wq, zq(pad_r)], -1).reshape(rq, -1)
    rot = jnp.concatenate([zq(MLA_NOPE), rope_rot, zq(pad_r)], -1).reshape(rq, -1)
    wuq_ext = jnp.concatenate([main, rot], 1).astype(BF16)
    rk = w_ukv.shape[0]
    wkv = w_ukv.reshape(rk, N_HEADS, MLA_NOPE + MLA_V)
    k_part = jnp.concatenate([wkv[:, :, :MLA_NOPE],
                              jnp.zeros((rk, N_HEADS, MLA_QK_PAD - MLA_NOPE), w_ukv.dtype)], -1)
    wukv_ext = jnp.concatenate([k_part.reshape(rk, -1), wkv[:, :, MLA_NOPE:].reshape(rk, -1)], 1).astype(BF16)
    return wuq_ext, wukv_ext


def kernel(x, c, ctx, c_ctx, w_mod, b_mod, w_in, attn_sink, mla_q_norm, w_uq, mla_kv_norm, w_ukv,
           lam_q1, lam_k1, lam_q2, lam_k2, diff_subln, na_rpb, w_out, ln1_g, ln1_b,
           w_group, b_group, w_router, b_router, w_gate, w_up, w_down, ln2_g, ln2_b):
    B, L, D = x.shape
    C = ctx.shape[1]
    depth = w_mod.shape[0]
    alpha = (2 * depth) ** 0.25
    assert D == D_MODEL and B + 1 <= 8 and L % (NA_Q_ROWS * GRID_W) == 0

    cvec = jnp.concatenate([c, c_ctx[None, :], jnp.zeros((8 - B - 1, D), F32)], axis=0)
    mod = _modulation(cvec, w_mod, b_mod)
    lat_tables = _rope_tables(L)
    ctx_tables = _identity_tables(C)
    row = lambda a: a.reshape(1, -1)

    xc = ctx
    for l in range(depth):
        need_ctx = l < depth - 1
        lam_init = 0.8 - 0.6 * math.exp(-0.3 * l)
        chunks = [mod[l, :, i * D:(i + 1) * D] for i in range(6)]
        sh1, sc1, g1, sh2, sc2, g2 = [m[:B, None, :] for m in chunks]
        sh1c, sc1c, g1c, sh2c, sc2c, g2c = [jnp.broadcast_to(m[B:B + 1, None, :], (B, 1, D)) for m in chunks]

        w_all = _fused_in_weight(w_in[l])
        wuq_ext, wukv_ext = _mla_up_weights(w_uq[l], w_ukv[l])
        proj = functools.partial(_input_projection, w_all=w_all, qng=row(mla_q_norm[l]), wuq=wuq_ext,
                                 kvng=row(mla_kv_norm[l]), wukv=wukv_ext)
        hd = proj(x, sc1, sh1, tables=lat_tables)
        hc = proj(xc, sc1c, sh1c, tables=ctx_tables)

        lam_vecs = jnp.stack([lam_q1[l], lam_k1[l], lam_q2[l], lam_k2[l]])
        subln_g = row(diff_subln[l])
        att = (_swa_attention(attn_sink[l], hd, hc), _mla_attention(hd, hc),
               _diff_attention(lam_vecs, subln_g, hd, hc, lam_init), _na_attention(na_rpb[l], hd, hc))

        wo = w_out[l].astype(BF16)
        pad = ROUTER_LANES - N_EXPERTS - N_GROUPS
        wr = jnp.stack(_split_bf16(jnp.concatenate([w_router[l], w_group[l], jnp.zeros((D, pad), F32)], axis=1)))
        br = row(jnp.concatenate([b_router[l], b_group[l], jnp.zeros((pad,), F32)]))
        post = functools.partial(_output_projection, wo=wo, ln_g=row(ln1_g[l]), ln_b=row(ln1_b[l]),
                                 wr=wr, br=br, alpha=alpha)
        ffn = functools.partial(_moe, wg=w_gate, wu=w_up, wd=w_down, layer=l,
                                ln_g=row(ln2_g[l]), ln_b=row(ln2_b[l]), alpha=alpha)

        x1, rec = post(x, att, g1=g1, sc2=sc2, sh2=sh2)
        x = ffn(rec, x1, g2=g2, sort=True)
        if need_ctx:
            att_c = _ctx_attention(attn_sink[l], lam_vecs, subln_g, hc, lam_init)
            xc1, rec_c = post(xc, att_c, g1=g1c, sc2=sc2c, sh2=sh2c)
            xc = ffn(rec_c, xc1, g2=g2c, sort=False)
    return x
```

```python
import functools
import math

import jax
import jax.numpy as jnp
import numpy as np
from jax import lax
from jax.experimental import pallas as pl
from jax.experimental.pallas import tpu as pltpu

F32 = jnp.float32
BF16 = jnp.bfloat16

D_MODEL = 1024
GRID_W = 64
HEAD_DIM = 64
N_HEADS = 4
N_MIXERS = 4
GROUP_WIDTH = N_HEADS * HEAD_DIM
SWA_KV_HEADS = 2
SWA_WINDOW = 128
SWA_BLOCK = 128
MLA_Q_RANK = 256
MLA_KV_RANK = 128
MLA_NOPE = 64
MLA_ROPE = 32
MLA_V = 64
MLA_QK_PAD = 128
V_EXT = 128
DIFF_QK = 32
DIFF_V = 64
NA_KH = 8
NA_KW = 16
NA_Q_ROWS = 4
NA_K_ROWS = NA_Q_ROWS + NA_KH
N_GROUPS = 4
EXPERTS_PER_GROUP = 8
N_EXPERTS = N_GROUPS * EXPERTS_PER_GROUP
EXPERT_HIDDEN = 256
ROUTER_LANES = 128
GROUP_LANE = N_EXPERTS
ROPE_BASE = 10000.0
NORM_EPS = 1e-5
NEG_INF = -1e30
LOG2E = math.log2(math.e)
SWA_SCALE = HEAD_DIM ** -0.5 * LOG2E
MLA_SCALE = (MLA_NOPE + MLA_ROPE) ** -0.5 * LOG2E
DIFF_SCALE = DIFF_QK ** -0.5 * LOG2E
NA_SCALE = HEAD_DIM ** -0.5 * LOG2E
IN_SPLITS = (GROUP_WIDTH, SWA_KV_HEADS * HEAD_DIM, SWA_KV_HEADS * HEAD_DIM,
             MLA_Q_RANK, MLA_KV_RANK, MLA_ROPE,
             N_HEADS * 2 * DIFF_QK, N_HEADS * 2 * DIFF_QK, N_HEADS * DIFF_V,
             GROUP_WIDTH, GROUP_WIDTH, GROUP_WIDTH)
IN_CUTS = tuple(int(v) for v in np.cumsum(IN_SPLITS)[:-1])

_SEG_LAYOUT = (("qa", 256), ("qa_r", 256), ("ka", 128), ("ka_r", 128), ("va", 128),
               ("mqr", 256), ("mkvr", 128), ("mkr", 128), ("mkr_r", 128),
               ("dq", 256), ("dq_r", 256), ("dk", 256), ("dk_r", 256), ("dv", 256),
               ("nq", 256), ("nk", 256), ("nv", 256))
_SEG = {}
_off = 0
for _name, _w in _SEG_LAYOUT:
    _SEG[_name] = (_off, _off + _w)
    _off += _w
W_ALL_COLS = _off

VMEM_LIMIT_BYTES = 56 * 1024 * 1024


def _cparams(sem):
    return pltpu.CompilerParams(dimension_semantics=sem, vmem_limit_bytes=VMEM_LIMIT_BYTES)


def _dot(a, b):
    return jnp.dot(a, b, preferred_element_type=F32)


def _dot_nt(a, b):
    return lax.dot_general(a, b, (((1,), (1,)), ((), ())), preferred_element_type=F32)


def _with_ones_lane(v):
    lane = lax.broadcasted_iota(jnp.int32, (v.shape[0], V_EXT - v.shape[1]), 1)
    return jnp.concatenate([v, (lane == 0).astype(v.dtype)], axis=-1)


def _rms(x):
    return x * lax.rsqrt(jnp.mean(x * x, axis=-1, keepdims=True) + NORM_EPS)


def _layer_norm(z, g, b):
    mu = jnp.mean(z, axis=-1, keepdims=True)
    zc = z - mu
    var = jnp.mean(zc * zc, axis=-1, keepdims=True)
    return zc * lax.rsqrt(var + NORM_EPS) * g + b


def _mod_kernel(c_ref, w_ref, b_ref, o_ref):
    c = c_ref[...]
    act = c * jax.nn.sigmoid(c)
    o_ref[0] = jnp.dot(act, w_ref[0], preferred_element_type=F32,
                       precision=lax.Precision.HIGHEST) + b_ref[0]


def _modulation(cvec, w_mod, b_mod):
    depth, d, n = w_mod.shape
    tn = 1024
    return pl.pallas_call(
        _mod_kernel,
        grid=(depth, n // tn),
        in_specs=[pl.BlockSpec((8, d), lambda l, j: (0, 0)),
                  pl.BlockSpec((1, d, tn), lambda l, j: (l, 0, j)),
                  pl.BlockSpec((1, 1, tn), lambda l, j: (l, 0, j))],
        out_specs=pl.BlockSpec((1, 8, tn), lambda l, j: (l, 0, j)),
        out_shape=jax.ShapeDtypeStruct((depth, 8, n), F32),
        compiler_params=_cparams(("arbitrary", "arbitrary")),
        name="modulation",
    )(cvec, w_mod, b_mod.reshape(depth, 1, n))


def _inproj_kernel(x_ref, sc_ref, sh_ref, w_ref, cos64_ref, sin64_ref, cos32_ref, sin32_ref,
                   cosm_ref, sinm_ref, qng_ref, wuq_ref, kvng_ref, wukv_ref,
                   qa_o, ka_o, va_o, mq_o, mk_o, mv_o, dq_o, dk_o, dv_o, nq_o, nk_o, nv_o):
    h = (x_ref[0] * (1.0 + sc_ref[0]) + sh_ref[0]).astype(BF16)

    def seg(name):
        a, b = _SEG[name]
        return _dot(h, w_ref[:, a:b])

    def split_heads(val, out_ref, n, width):
        for i in range(n):
            out_ref[0, i] = val[:, i * width:(i + 1) * width].astype(out_ref.dtype)

    cos64 = cos64_ref[...]
    sin64 = sin64_ref[...]
    cos32 = cos32_ref[...]
    sin32 = sin32_ref[...]
    cosm = cosm_ref[...]
    sinm = sinm_ref[...]

    qa = (seg("qa") * cos64 + seg("qa_r") * sin64) * SWA_SCALE
    split_heads(qa, qa_o, N_HEADS, HEAD_DIM)
    ka = seg("ka") * cos64[:, :128] + seg("ka_r") * sin64[:, :128]
    split_heads(ka, ka_o, SWA_KV_HEADS, HEAD_DIM)
    split_heads(seg("va"), va_o, SWA_KV_HEADS, HEAD_DIM)

    qn = (_rms(seg("mqr")) * qng_ref[...]).astype(BF16)
    uq = _dot(qn, wuq_ref[...])
    half = N_HEADS * MLA_QK_PAD
    for i in range(N_HEADS):
        a = i * MLA_QK_PAD
        mq = (uq[:, a:a + MLA_QK_PAD] * cosm + uq[:, half + a:half + a + MLA_QK_PAD] * sinm) * MLA_SCALE
        mq_o[0, i] = mq.astype(BF16)
    kvn = (_rms(seg("mkvr")) * kvng_ref[...]).astype(BF16)
    ukv = _dot(kvn, wukv_ref[...])
    k_rope = seg("mkr") * cosm + seg("mkr_r") * sinm
    for i in range(N_HEADS):
        a = i * MLA_QK_PAD
        mk_o[0, i] = (ukv[:, a:a + MLA_QK_PAD] + k_rope).astype(BF16)
        b = half + i * MLA_V
        mv_o[0, i] = _with_ones_lane(ukv[:, b:b + MLA_V]).astype(BF16)

    dq = (seg("dq") * cos32 + seg("dq_r") * sin32) * DIFF_SCALE
    split_heads(dq, dq_o, 2 * N_HEADS, DIFF_QK)
    dk = seg("dk") * cos32 + seg("dk_r") * sin32
    split_heads(dk, dk_o, 2 * N_HEADS, DIFF_QK)
    dv = seg("dv")
    for i in range(N_HEADS):
        dv_o[0, i] = _with_ones_lane(dv[:, i * DIFF_V:(i + 1) * DIFF_V]).astype(BF16)

    split_heads(seg("nq") * NA_SCALE, nq_o, N_HEADS, HEAD_DIM)
    split_heads(seg("nk"), nk_o, N_HEADS, HEAD_DIM)
    split_heads(seg("nv"), nv_o, N_HEADS, HEAD_DIM)


_HEAD_OUTS = (("qa", N_HEADS, HEAD_DIM), ("ka", SWA_KV_HEADS, HEAD_DIM), ("va", SWA_KV_HEADS, HEAD_DIM),
              ("mq", N_HEADS, MLA_QK_PAD), ("mk", N_HEADS, MLA_QK_PAD), ("mv", N_HEADS, V_EXT),
              ("dq", 2 * N_HEADS, DIFF_QK), ("dk", 2 * N_HEADS, DIFF_QK), ("dv", N_HEADS, V_EXT),
              ("nq", N_HEADS, HEAD_DIM), ("nk", N_HEADS, HEAD_DIM), ("nv", N_HEADS, HEAD_DIM))


def _input_projection(x, sc, sh, w_all, tables, qng, wuq, kvng, wukv):
    B, N, D = x.shape
    tm = min(512, N)
    tok = lambda i, b: (b, i, 0)
    vec = lambda i, b: (b, 0, 0)
    tab = lambda i, b: (i, 0)
    const = lambda i, b: (0, 0)
    in_specs = [pl.BlockSpec((1, tm, D), tok),
                pl.BlockSpec((1, 1, D), vec), pl.BlockSpec((1, 1, D), vec),
                pl.BlockSpec(w_all.shape, const)]
    in_specs += [pl.BlockSpec((tm, t.shape[1]), tab) for t in tables]
    in_specs += [pl.BlockSpec(a.shape, const) for a in (qng, wuq, kvng, wukv)]
    out_specs = [pl.BlockSpec((1, n, tm, w), lambda i, b: (b, 0, i, 0)) for _, n, w in _HEAD_OUTS]
    out_shape = [jax.ShapeDtypeStruct((B, n, N, w), BF16) for _, n, w in _HEAD_OUTS]
    outs = pl.pallas_call(
        _inproj_kernel,
        grid=(N // tm, B),
        in_specs=in_specs, out_specs=out_specs, out_shape=out_shape,
        compiler_params=_cparams(("arbitrary", "arbitrary")),
        name="input_projection",
    )(x, sc, sh, w_all, *tables, qng, wuq, kvng, wukv)
    return {name: o for (name, _, _), o in zip(_HEAD_OUTS, outs)}


def _softmax_parts(scores, extra_logit=None):
    m = functools.reduce(jnp.maximum, [jnp.max(s, axis=-1, keepdims=True) for s in scores])
    if extra_logit is not None:
        m = jnp.maximum(m, extra_logit)
    ps = [jnp.exp2(s - m) for s in scores]
    denom = functools.reduce(jnp.add, [jnp.sum(p, axis=-1, keepdims=True) for p in ps])
    if extra_logit is not None:
        denom = denom + jnp.exp2(extra_logit - m)
    return ps, denom


def _flash(qs, ctx_ks, ctx_vs, k_ats, v_ats, n_chunks, dv):
    tq = qs[0].shape[0]

    def update(carry, q, k, v):
        m, acc = carry
        s = _dot_nt(q, k)
        m_new = jnp.maximum(m, jnp.max(s, axis=-1, keepdims=True))
        p = jnp.exp2(s - m_new)
        acc = jnp.exp2(m - m_new) * acc + _dot(p.astype(BF16), v)
        return m_new, acc

    init = (jnp.full((tq, 1), NEG_INF, F32), jnp.zeros((tq, V_EXT), F32))
    carries = tuple(update(init, q, k, v) for q, k, v in zip(qs, ctx_ks, ctx_vs))

    def body(i, carries):
        return tuple(update(cr, q, k_at(i), v_at(i))
                     for cr, q, k_at, v_at in zip(carries, qs, k_ats, v_ats))

    carries = lax.fori_loop(0, n_chunks, body, carries, unroll=min(FLASH_UNROLL, n_chunks))
    return [acc[:, :dv] / acc[:, dv:dv + 1] for _, acc in carries]


def _lambda_value(lam_ref, lam_init):
    lv = lam_ref[...]
    return (jnp.exp(jnp.sum(lv[0:1] * lv[1:2], axis=-1, keepdims=True))
            - jnp.exp(jnp.sum(lv[2:3] * lv[3:4], axis=-1, keepdims=True)) + lam_init)


def _sub_ln(o, g_ref, lam_init):
    return _rms(o) * g_ref[...] * (1.0 - lam_init)


SWA_TQ = 2 * SWA_BLOCK
SWA_SPAN = SWA_TQ + 2 * SWA_WINDOW


def _swa_kernel(sink_ref, q_ref, k_ref, v_ref, kc_ref, vc_ref, o_ref):
    L = k_ref.shape[2]
    qb = pl.program_id(1)
    start = pl.multiple_of(jnp.clip(qb * SWA_TQ - SWA_WINDOW, 0, L - SWA_SPAN), SWA_BLOCK)
    group = N_HEADS // SWA_KV_HEADS
    rows = group * SWA_TQ
    n_keys = SWA_SPAN + kc_ref.shape[2]
    col = lax.broadcasted_iota(jnp.int32, (rows, n_keys), 1)
    row = lax.broadcasted_iota(jnp.int32, (rows, n_keys), 0)
    q_abs = qb * SWA_TQ + jnp.where(row >= SWA_TQ, row - SWA_TQ, row)
    allowed = (col >= SWA_SPAN) | (jnp.abs(start + col - q_abs) <= SWA_WINDOW)
    for hk in range(SWA_KV_HEADS):
        k_all = jnp.concatenate([k_ref[0, hk, pl.ds(start, SWA_SPAN), :], kc_ref[0, hk]], axis=0)
        v_all = jnp.concatenate([v_ref[0, hk, pl.ds(start, SWA_SPAN), :], vc_ref[0, hk]], axis=0)
        q = jnp.concatenate([q_ref[0, hk * group + g] for g in range(group)], axis=0)
        sink = jnp.concatenate([jnp.full((SWA_TQ, 1), sink_ref[hk * group + g], F32) for g in range(group)],
                               axis=0) * LOG2E
        (p,), denom = _softmax_parts([jnp.where(allowed, _dot_nt(q, k_all), NEG_INF)], sink)
        o = _dot(p.astype(BF16), v_all) / denom
        for g in range(group):
            h = hk * group + g
            o_ref[0, :, h * HEAD_DIM:(h + 1) * HEAD_DIM] = o[g * SWA_TQ:(g + 1) * SWA_TQ].astype(o_ref.dtype)


def _swa_attention(sink, hd, hc):
    B, _, L, _ = hd["qa"].shape
    C = hc["ka"].shape[2]
    assert L % SWA_TQ == 0 and L >= SWA_SPAN
    whole = lambda b, i: (b, 0, 0, 0)
    return pl.pallas_call(
        _swa_kernel,
        grid=(B, L // SWA_TQ),
        in_specs=[pl.BlockSpec(memory_space=pltpu.SMEM),
                  pl.BlockSpec((1, N_HEADS, SWA_TQ, HEAD_DIM), lambda b, i: (b, 0, i, 0)),
                  pl.BlockSpec((1, SWA_KV_HEADS, L, HEAD_DIM), whole),
                  pl.BlockSpec((1, SWA_KV_HEADS, L, HEAD_DIM), whole),
                  pl.BlockSpec((1, SWA_KV_HEADS, C, HEAD_DIM), whole),
                  pl.BlockSpec((1, SWA_KV_HEADS, C, HEAD_DIM), whole)],
        out_specs=pl.BlockSpec((1, SWA_TQ, GROUP_WIDTH), lambda b, i: (b, i, 0)),
        out_shape=jax.ShapeDtypeStruct((B, L, GROUP_WIDTH), BF16),
        compiler_params=_cparams(("arbitrary", "arbitrary")),
        name="swa_attention",
    )(sink, hd["qa"], hd["ka"], hd["va"], hc["ka"], hc["va"])


GLOBAL_TQ = 512
GLOBAL_TK = 2048
HEADS_PER_STEP = 2
FLASH_UNROLL = 2


def _chunk_at(ref, lead, tk):
    return lambda i: ref[lead + (pl.ds(pl.multiple_of(i * tk, tk), tk), slice(None))]


def _mla_kernel(q_ref, kc_ref, vc_ref, k_ref, v_ref, o_ref, *, tk):
    n_chunks = k_ref.shape[2] // tk
    heads = range(HEADS_PER_STEP)
    outs = _flash([q_ref[0, j] for j in heads], [kc_ref[0, j] for j in heads], [vc_ref[0, j] for j in heads],
                  [_chunk_at(k_ref, (0, j), tk) for j in heads], [_chunk_at(v_ref, (0, j), tk) for j in heads],
                  n_chunks, MLA_V)
    o_ref[0] = jnp.concatenate(outs, axis=-1).astype(o_ref.dtype)


def _mla_attention(hd, hc):
    B, H, L, dk = hd["mq"].shape
    C = hc["mk"].shape[2]
    tq = min(GLOBAL_TQ, L)
    tk = min(GLOBAL_TK, L)
    hp = HEADS_PER_STEP
    whole = lambda b, h, i: (b, h, 0, 0)
    return pl.pallas_call(
        functools.partial(_mla_kernel, tk=tk),
        grid=(B, H // hp, L // tq),
        in_specs=[pl.BlockSpec((1, hp, tq, dk), lambda b, h, i: (b, h, i, 0)),
                  pl.BlockSpec((1, hp, C, dk), whole),
                  pl.BlockSpec((1, hp, C, V_EXT), whole),
                  pl.BlockSpec((1, hp, L, dk), whole),
                  pl.BlockSpec((1, hp, L, V_EXT), whole)],
        out_specs=pl.BlockSpec((1, tq, hp * MLA_V), lambda b, h, i: (b, i, h)),
        out_shape=jax.ShapeDtypeStruct((B, L, H * MLA_V), BF16),
        compiler_params=_cparams(("arbitrary", "arbitrary", "arbitrary")),
        name="mla_attention",
    )(hd["mq"], hc["mk"], hc["mv"], hd["mk"], hd["mv"])


def _diff_kernel(lam_ref, g_ref, q_ref, kc_ref, vc_ref, k_ref, v_ref, o_ref, *, tk, lam_init):
    n_chunks = k_ref.shape[2] // tk
    lam = _lambda_value(lam_ref, lam_init)
    heads = []
    for h in range(HEADS_PER_STEP):
        branches = (2 * h, 2 * h + 1)
        o1, o2 = _flash([q_ref[0, j] for j in branches], [kc_ref[0, j] for j in branches], [vc_ref[0, h]] * 2,
                        [_chunk_at(k_ref, (0, j), tk) for j in branches], [_chunk_at(v_ref, (0, h), tk)] * 2,
                        n_chunks, DIFF_V)
        heads.append(_sub_ln(o1 - lam * o2, g_ref, lam_init))
    o_ref[0] = jnp.concatenate(heads, axis=-1).astype(o_ref.dtype)


def _diff_attention(lam_vecs, subln_g, hd, hc, lam_init):
    B, _, L, dk = hd["dq"].shape
    C = hc["dk"].shape[2]
    tq = min(GLOBAL_TQ, L)
    tk = min(GLOBAL_TK, L)
    hp = HEADS_PER_STEP
    whole = lambda b, h, i: (b, h, 0, 0)
    const = lambda b, h, i: (0, 0)
    return pl.pallas_call(
        functools.partial(_diff_kernel, tk=tk, lam_init=lam_init),
        grid=(B, N_HEADS // hp, L // tq),
        in_specs=[pl.BlockSpec(lam_vecs.shape, const),
                  pl.BlockSpec(subln_g.shape, const),
                  pl.BlockSpec((1, 2 * hp, tq, dk), lambda b, h, i: (b, h, i, 0)),
                  pl.BlockSpec((1, 2 * hp, C, dk), whole),
                  pl.BlockSpec((1, hp, C, V_EXT), whole),
                  pl.BlockSpec((1, 2 * hp, L, dk), whole),
                  pl.BlockSpec((1, hp, L, V_EXT), whole)],
        out_specs=pl.BlockSpec((1, tq, hp * DIFF_V), lambda b, h, i: (b, i, h)),
        out_shape=jax.ShapeDtypeStruct((B, L, N_HEADS * DIFF_V), BF16),
        compiler_params=_cparams(("arbitrary", "arbitrary", "arbitrary")),
        name="diff_attention",
    )(lam_vecs, subln_g, hd["dq"], hc["dk"], hc["dv"], hd["dk"], hd["dv"])


def _na_plan(rows):
    kh = min(NA_KH, rows)
    n_row_off = 2 * NA_KH - 1
    col = np.arange(GRID_W)
    col_start = np.clip(col - NA_KW // 2, 0, GRID_W - NA_KW)
    col_ok = (col[None, :] >= col_start[:, None]) & (col[None, :] < col_start[:, None] + NA_KW)
    col_off = col[None, :] - col[:, None] + (NA_KW - 1)
    col_onehot = ((col_off[None] == np.arange(2 * NA_KW - 1)[:, None, None]) & col_ok[None]).astype(np.float32)
    patterns, starts, ids = {}, [], []
    for blk in range(rows // NA_Q_ROWS):
        r0 = blk * NA_Q_ROWS
        ks = int(np.clip(r0 - kh // 2, 0, rows - NA_K_ROWS))
        q_row = r0 + np.arange(NA_Q_ROWS)
        k_row = ks + np.arange(NA_K_ROWS)
        r_start = np.clip(q_row - kh // 2, 0, rows - kh)
        row_ok = (k_row[None, :] >= r_start[:, None]) & (k_row[None, :] < r_start[:, None] + kh)
        row_off = k_row[None, :] - q_row[:, None] + (NA_KH - 1)
        sel = np.where(row_ok, row_off, n_row_off)
        key = tuple(int(v) for v in sel.reshape(-1))
        if key not in patterns:
            patterns[key] = (len(patterns), sel)
        starts.append(ks)
        ids.append(patterns[key][0])
    row_sel = np.stack([p[1] for p in sorted(patterns.values(), key=lambda p: p[0])])
    return np.asarray(starts, np.int32), np.asarray(ids, np.int32), row_sel, col_onehot, col_ok


def _na_kernel(ks_ref, pid_ref, q_ref, k_ref, v_ref, kc_ref, vc_ref, bias_ref, o_ref):
    del pid_ref
    k_len = NA_K_ROWS * GRID_W
    start = pl.multiple_of(ks_ref[pl.program_id(1)] * GRID_W, GRID_W)
    for h in range(N_HEADS):
        q = q_ref[0, h]
        kw = k_ref[0, h, pl.ds(start, k_len), :]
        vw = v_ref[0, h, pl.ds(start, k_len), :]
        s_win = _dot_nt(q, kw) + bias_ref[0, h]
        s_ctx = _dot_nt(q, kc_ref[0, h])
        (p_win, p_ctx), denom = _softmax_parts([s_win, s_ctx])
        o = _dot(p_win.astype(BF16), vw) + _dot(p_ctx.astype(BF16), vc_ref[0, h])
        o_ref[0, :, h * HEAD_DIM:(h + 1) * HEAD_DIM] = (o / denom).astype(o_ref.dtype)


def _na_attention(rpb, hd, hc):
    B, H, L, _ = hd["nq"].shape
    C = hc["nk"].shape[2]
    rows = L // GRID_W
    starts, ids, row_sel, col_onehot, col_ok = _na_plan(rows)
    q_len = NA_Q_ROWS * GRID_W
    k_len = NA_K_ROWS * GRID_W
    slab = jnp.einsum("hrj,jqk->hrqk", rpb * LOG2E, col_onehot, precision=lax.Precision.HIGHEST)
    slab = jnp.where(col_ok[None, None], slab, NEG_INF)
    slab = jnp.concatenate([slab, jnp.full_like(slab[:, :1], NEG_INF)], axis=1)
    bias = jnp.take(slab, jnp.asarray(row_sel), axis=1)
    bias = jnp.transpose(bias, (1, 0, 2, 4, 3, 5)).reshape(row_sel.shape[0], H, q_len, k_len)
    whole = lambda b, i, ks, pid: (b, 0, 0, 0)
    grid_spec = pltpu.PrefetchScalarGridSpec(
        num_scalar_prefetch=2,
        grid=(B, rows // NA_Q_ROWS),
        in_specs=[pl.BlockSpec((1, H, q_len, HEAD_DIM), lambda b, i, ks, pid: (b, 0, i, 0)),
                  pl.BlockSpec((1, H, L, HEAD_DIM), whole),
                  pl.BlockSpec((1, H, L, HEAD_DIM), whole),
                  pl.BlockSpec((1, H, C, HEAD_DIM), whole),
                  pl.BlockSpec((1, H, C, HEAD_DIM), whole),
                  pl.BlockSpec((1, H, q_len, k_len), lambda b, i, ks, pid: (pid[i], 0, 0, 0))],
        out_specs=pl.BlockSpec((1, q_len, GROUP_WIDTH), lambda b, i, ks, pid: (b, i, 0)))
    return pl.pallas_call(
        _na_kernel,
        grid_spec=grid_spec,
        out_shape=jax.ShapeDtypeStruct((B, L, GROUP_WIDTH), BF16),
        compiler_params=_cparams(("arbitrary", "arbitrary")),
        name="neighborhood_attention",
    )(jnp.asarray(starts), jnp.asarray(ids), hd["nq"], hd["nk"], hd["nv"], hc["nk"], hc["nv"], bias)


def _ctx_attn_kernel(sink_ref, lam_ref, g_ref, qa_ref, ka_ref, va_ref, mq_ref, mk_ref, mv_ref,
                     dq_ref, dk_ref, dv_ref, nq_ref, nk_ref, nv_ref,
                     ya_ref, yb_ref, yc_ref, yd_ref, *, lam_init):
    def attend(q, k, v, extra=None):
        (p,), denom = _softmax_parts([_dot_nt(q, k)], extra)
        return _dot(p.astype(BF16), v) / denom

    group = N_HEADS // SWA_KV_HEADS
    lam = _lambda_value(lam_ref, lam_init)
    for h in range(N_HEADS):
        lanes = slice(h * HEAD_DIM, (h + 1) * HEAD_DIM)
        ya_ref[0, :, lanes] = attend(qa_ref[0, h], ka_ref[0, h // group], va_ref[0, h // group],
                                     sink_ref[h] * LOG2E).astype(ya_ref.dtype)
        mv = mv_ref[0, h][:, :MLA_V]
        dv = dv_ref[0, h][:, :DIFF_V]
        yb_ref[0, :, lanes] = attend(mq_ref[0, h], mk_ref[0, h], mv).astype(yb_ref.dtype)
        o = (attend(dq_ref[0, 2 * h], dk_ref[0, 2 * h], dv)
             - lam * attend(dq_ref[0, 2 * h + 1], dk_ref[0, 2 * h + 1], dv))
        yc_ref[0, :, lanes] = _sub_ln(o, g_ref, lam_init).astype(yc_ref.dtype)
        yd_ref[0, :, lanes] = attend(nq_ref[0, h], nk_ref[0, h], nv_ref[0, h]).astype(yd_ref.dtype)


def _ctx_attention(sink, lam_vecs, subln_g, hc, lam_init):
    names = ("qa", "ka", "va", "mq", "mk", "mv", "dq", "dk", "dv", "nq", "nk", "nv")
    B, _, C, _ = hc["qa"].shape
    whole4 = lambda b: (b, 0, 0, 0)
    const = lambda b: (0, 0)
    in_specs = [pl.BlockSpec(memory_space=pltpu.SMEM),
                pl.BlockSpec(lam_vecs.shape, const), pl.BlockSpec(subln_g.shape, const)]
    in_specs += [pl.BlockSpec((1,) + hc[n].shape[1:], whole4) for n in names]
    tok = pl.BlockSpec((1, C, GROUP_WIDTH), lambda b: (b, 0, 0))
    return pl.pallas_call(
        functools.partial(_ctx_attn_kernel, lam_init=lam_init),
        grid=(B,),
        in_specs=in_specs,
        out_specs=[tok] * N_MIXERS,
        out_shape=[jax.ShapeDtypeStruct((B, C, GROUP_WIDTH), BF16)] * N_MIXERS,
        compiler_params=_cparams(("arbitrary",)),
        name="context_attention",
    )(sink, lam_vecs, subln_g, *[hc[n] for n in names])


def _route(r):
    lane = lax.broadcasted_iota(jnp.int32, r.shape, 1)
    lane_f = lane.astype(F32)
    big = float(ROUTER_LANES)
    is_grp = (lane >= N_EXPERTS) & (lane < N_EXPERTS + N_GROUPS)
    g_log = jnp.where(is_grp, r, NEG_INF)
    g_max = jnp.max(g_log, axis=-1, keepdims=True)
    g_val = 1.0 / jnp.sum(jnp.exp(g_log - g_max), axis=-1, keepdims=True)
    g_idx = jnp.min(jnp.where(g_log == g_max, lane_f, big), axis=-1, keepdims=True) - float(N_EXPERTS)
    lane_grp = lax.shift_right_logical(lane, int(math.log2(EXPERTS_PER_GROUP))).astype(F32)
    in_grp = (lane < N_EXPERTS) & (lane_grp == g_idx)
    e_log = jnp.where(in_grp, r, NEG_INF)
    e_max = jnp.max(e_log, axis=-1, keepdims=True)
    i1 = jnp.min(jnp.where(e_log == e_max, lane_f, big), axis=-1, keepdims=True)
    e_rest = jnp.where(lane_f == i1, NEG_INF, e_log)
    e_max2 = jnp.max(e_rest, axis=-1, keepdims=True)
    i2 = jnp.min(jnp.where(e_rest == e_max2, lane_f, big), axis=-1, keepdims=True)
    p2 = jnp.exp(e_max2 - e_max)
    w1 = 1.0 / (1.0 + p2)
    w2 = p2 / (1.0 + p2)
    gates = g_val * jnp.where(lane_f == i1, w1, jnp.where(lane_f == i2, w2, 0.0))
    return jnp.where(lane == GROUP_LANE, g_idx, gates)


def _split_bf16(v):
    hi = v.astype(BF16)
    return hi, (v - hi.astype(F32)).astype(BF16)


def _outproj_kernel(x_ref, a_ref, b_ref, c_ref, d_ref, wo_ref, g1_ref, sc2_ref, sh2_ref,
                    lng_ref, lnb_ref, wr_ref, br_ref, x1_ref, rec_ref, *, alpha):
    y = _dot(a_ref[0], wo_ref[0:GROUP_WIDTH, :])
    for i, m_ref in enumerate((b_ref, c_ref, d_ref), start=1):
        y += _dot(m_ref[0], wo_ref[i * GROUP_WIDTH:(i + 1) * GROUP_WIDTH, :])
    x1 = _layer_norm(alpha * x_ref[0] + g1_ref[0] * y, lng_ref[...], lnb_ref[...])
    x1_ref[0] = x1
    h2 = x1 * (1.0 + sc2_ref[0]) + sh2_ref[0]
    rec_ref[0, :, :D_MODEL] = h2
    h_hi, h_lo = _split_bf16(h2)
    r = (_dot(h_hi, wr_ref[0]) + _dot(h_lo, wr_ref[0]) + _dot(h_hi, wr_ref[1])) + br_ref[...]
    rec_ref[0, :, D_MODEL:] = _route(r)


def _output_projection(x, att, wo, g1, sc2, sh2, ln_g, ln_b, wr, br, alpha):
    B, N, D = x.shape
    tm = min(512, N)
    tok = lambda w: pl.BlockSpec((1, tm, w), lambda b, i: (b, i, 0))
    vec = pl.BlockSpec((1, 1, D), lambda b, i: (b, 0, 0))
    const = lambda a: pl.BlockSpec(a.shape, lambda b, i: (0,) * a.ndim)
    return pl.pallas_call(
        functools.partial(_outproj_kernel, alpha=alpha),
        grid=(B, N // tm),
        in_specs=[tok(D)] + [tok(GROUP_WIDTH)] * N_MIXERS + [const(wo), vec, vec, vec,
                  const(ln_g), const(ln_b), const(wr), const(br)],
        out_specs=[tok(D), tok(REC_W)],
        out_shape=[jax.ShapeDtypeStruct((B, N, D), F32), jax.ShapeDtypeStruct((B, N, REC_W), F32)],
        compiler_params=_cparams(("arbitrary", "arbitrary")),
        name="output_projection",
    )(x, att[0], att[1], att[2], att[3], wo, g1, sc2, sh2, ln_g, ln_b, wr, br)


MOE_TM = 1024
MOE_ROWS = 512
MOE_STEPS = 4
MOE_EXPERTS_PER_STEP = EXPERTS_PER_GROUP // MOE_STEPS
GATHER_ROWS = 1024
GATHER_UNROLL = 8
REC_W = D_MODEL + ROUTER_LANES


def _issue_row_gather(idx_ref, src_ref, dst_ref, sem):
    n = dst_ref.shape[0]

    def row_copy(j):
        return pltpu.make_async_copy(src_ref.at[pl.ds(idx_ref[0, 0, j], 1), :], dst_ref.at[pl.ds(j, 1), :], sem)

    def issue(j, carry):
        row_copy(j).start()
        return carry

    lax.fori_loop(0, n, issue, 0, unroll=GATHER_UNROLL)
    pltpu.make_async_copy(src_ref.at[pl.ds(0, n), :], dst_ref, sem).wait()


def _gather_rows_kernel(idx_ref, src_ref, o_ref, sem):
    _issue_row_gather(idx_ref, src_ref, o_ref, sem)


def _gather_rows(src, idx):
    T, W = src.shape
    tg = min(GATHER_ROWS, T)
    return pl.pallas_call(
        _gather_rows_kernel,
        grid=(T // tg,),
        in_specs=[pl.BlockSpec((1, 1, tg), lambda i: (i, 0, 0), memory_space=pltpu.SMEM),
                  pl.BlockSpec(memory_space=pl.ANY)],
        out_specs=pl.BlockSpec((tg, W), lambda i: (i, 0)),
        out_shape=jax.ShapeDtypeStruct((T, W), src.dtype),
        scratch_shapes=[pltpu.SemaphoreType.DMA(())],
        compiler_params=_cparams(("arbitrary",)),
        name="gather_rows",
    )(idx.reshape(T // tg, 1, tg), src)


def _moe_ffn_kernel(tile_ref, group_ref, first_ref, valid_ref, rec_ref, wg_ref, wu_ref, wd_ref, f_ref):
    del tile_ref
    w = pl.program_id(0)
    step = pl.program_id(1)
    tm = f_ref.shape[0]

    @pl.when(valid_ref[w] != 0)
    def _():
        first = group_ref[w] * EXPERTS_PER_GROUP + step * MOE_EXPERTS_PER_STEP
        lane = lax.broadcasted_iota(jnp.int32, (MOE_ROWS, ROUTER_LANES), 1)
        experts = range(MOE_EXPERTS_PER_STEP)
        wgs = [wg_ref[0, 0, e].astype(BF16) for e in experts]
        wus = [wu_ref[0, 0, e].astype(BF16) for e in experts]
        wds = [wd_ref[0, 0, e].astype(BF16) for e in experts]
        for r0 in range(0, tm, MOE_ROWS):
            rows = slice(r0, r0 + MOE_ROWS)
            h = rec_ref[rows, :D_MODEL].astype(BF16)
            gates = rec_ref[rows, D_MODEL:]
            y = None
            for e in experts:
                pre = _dot(h, wgs[e])
                hid = pre * jax.nn.sigmoid(pre) * _dot(h, wus[e])
                col = jnp.sum(jnp.where(lane == first + e, gates, 0.0), axis=-1, keepdims=True)
                part = _dot((hid * col).astype(BF16), wds[e])
                y = part if y is None else y + part
            starts_tile = (first_ref[w] != 0) & (step == 0)

            @pl.when(starts_tile)
            def _():
                f_ref[rows, :] = y

            @pl.when(jnp.logical_not(starts_tile))
            def _():
                f_ref[rows, :] += y


def _moe_ffn(rec, items, wg, wu, wd, layer):
    T = rec.shape[0]
    tm = min(MOE_TM, T)
    tile, group, first, valid = items
    n_items = tile.shape[0]
    wspec = lambda a: pl.BlockSpec((1, 1, MOE_EXPERTS_PER_STEP) + a.shape[3:],
                                   lambda w, s, t, g, f, v: (layer, g[w], s, 0, 0))
    grid_spec = pltpu.PrefetchScalarGridSpec(
        num_scalar_prefetch=4,
        grid=(n_items, MOE_STEPS),
        in_specs=[pl.BlockSpec((tm, REC_W), lambda w, s, t, g, f, v: (t[w], 0)), wspec(wg), wspec(wu), wspec(wd)],
        out_specs=pl.BlockSpec((tm, D_MODEL), lambda w, s, t, g, f, v: (t[w], 0)))
    return pl.pallas_call(
        _moe_ffn_kernel,
        grid_spec=grid_spec,
        out_shape=jax.ShapeDtypeStruct((T, D_MODEL), F32),
        compiler_params=_cparams(("arbitrary", "arbitrary")),
        name="moe_ffn",
    )(tile, group, first, valid, rec, wg, wu, wd)


def _moe_items_sorted(sorted_group, tm):
    T = sorted_group.shape[0]
    nt = T // tm
    g_lo = sorted_group[0::tm]
    g_hi = sorted_group[tm - 1::tm]
    per_tile = g_hi - g_lo + 1
    start = jnp.cumsum(per_tile) - per_tile
    n_items = nt + N_GROUPS - 1
    w = jnp.arange(n_items, dtype=jnp.int32)
    tile = jnp.clip(jnp.searchsorted(start, w, side="right") - 1, 0, nt - 1).astype(jnp.int32)
    valid = w < jnp.sum(per_tile)
    offset = w - start[tile]
    group = jnp.where(valid, g_lo[tile] + offset, g_hi[nt - 1]).astype(jnp.int32)
    return tile, group, (valid & (offset == 0)).astype(jnp.int32), valid.astype(jnp.int32)


def _moe_items_dense(T, tm):
    nt = T // tm
    w = np.arange(nt * N_GROUPS, dtype=np.int32)
    return (jnp.asarray(w // N_GROUPS), jnp.asarray(w % N_GROUPS), jnp.asarray((w % N_GROUPS == 0).astype(np.int32)),
            jnp.ones((nt * N_GROUPS,), jnp.int32))


def _ln2_kernel(idx_ref, f_ref, x1_ref, g2_ref, lng_ref, lnb_ref, o_ref, buf_ref, sem, *, alpha):
    _issue_row_gather(idx_ref, f_ref, buf_ref, sem)
    o_ref[0] = _layer_norm(alpha * x1_ref[0] + g2_ref[0] * buf_ref[...], lng_ref[...], lnb_ref[...])


def _ln2_unsort(f_rows, idx, x1, g2, ln_g, ln_b, alpha):
    B, N, D = x1.shape
    tg = min(GATHER_ROWS, N)
    nb = N // tg
    const = lambda a: pl.BlockSpec(a.shape, lambda b, i: (0, 0))
    return pl.pallas_call(
        functools.partial(_ln2_kernel, alpha=alpha),
        grid=(B, nb),
        in_specs=[pl.BlockSpec((1, 1, tg), lambda b, i: (b * nb + i, 0, 0), memory_space=pltpu.SMEM),
                  pl.BlockSpec(memory_space=pl.ANY),
                  pl.BlockSpec((1, tg, D), lambda b, i: (b, i, 0)),
                  pl.BlockSpec((1, 1, D), lambda b, i: (b, 0, 0)), const(ln_g), const(ln_b)],
        out_specs=pl.BlockSpec((1, tg, D), lambda b, i: (b, i, 0)),
        out_shape=jax.ShapeDtypeStruct((B, N, D), F32),
        scratch_shapes=[pltpu.VMEM((tg, D), F32), pltpu.SemaphoreType.DMA(())],
        compiler_params=_cparams(("arbitrary", "arbitrary")),
        name="ln2_unsort",
    )(idx.reshape(B * nb, 1, tg), f_rows, x1, g2, ln_g, ln_b)


def _moe(rec, x1, wg, wu, wd, layer, g2, ln_g, ln_b, alpha, sort):
    B, N, _ = rec.shape
    T = B * N
    tm = min(MOE_TM, T)
    flat = rec.reshape(T, REC_W)
    if sort:
        group = flat[:, D_MODEL + GROUP_LANE].astype(jnp.int32)
        order = jnp.argsort(group, stable=True).astype(jnp.int32)
        place = jnp.argsort(order).astype(jnp.int32)
        flat = _gather_rows(flat, order)
        items = _moe_items_sorted(group[order], tm)
    else:
        place = jnp.arange(T, dtype=jnp.int32)
        items = _moe_items_dense(T, tm)
    f_rows = _moe_ffn(flat, items, wg, wu, wd, layer)
    return _ln2_unsort(f_rows, place, x1, g2, ln_g, ln_b, alpha)


def _rot_cols(w, d):
    k, n = w.shape
    q = d // 4
    w4 = w.reshape(k, n // d, 4, q)
    return jnp.stack([-w4[:, :, 1], w4[:, :, 0], -w4[:, :, 3], w4[:, :, 2]], axis=2).reshape(k, n)


def _rope_tables(L):
    t = jnp.arange(L, dtype=jnp.int32)
    rows = (t // GRID_W).astype(F32)
    cols = (t % GRID_W).astype(F32)

    def cos_sin(d):
        q = d // 4
        inv = ROPE_BASE ** (-jnp.arange(q, dtype=F32) / q)
        ar = rows[:, None] * inv[None, :]
        ac = cols[:, None] * inv[None, :]
        return (jnp.concatenate([jnp.cos(ar), jnp.cos(ar), jnp.cos(ac), jnp.cos(ac)], -1),
                jnp.concatenate([jnp.sin(ar), jnp.sin(ar), jnp.sin(ac), jnp.sin(ac)], -1))

    c64, s64 = cos_sin(HEAD_DIM)
    c32, s32 = cos_sin(MLA_ROPE)
    ones = jnp.ones((L, MLA_NOPE), F32)
    zeros_n = jnp.zeros((L, MLA_NOPE), F32)
    zeros_p = jnp.zeros((L, MLA_QK_PAD - MLA_NOPE - MLA_ROPE), F32)
    return (jnp.tile(c64, (1, N_HEADS)), jnp.tile(s64, (1, N_HEADS)),
            jnp.tile(c32, (1, 2 * N_HEADS)), jnp.tile(s32, (1, 2 * N_HEADS)),
            jnp.concatenate([ones, c32, zeros_p], -1), jnp.concatenate([zeros_n, s32, zeros_p], -1))


def _identity_tables(C):
    one = jnp.ones((C, GROUP_WIDTH), F32)
    zero = jnp.zeros((C, GROUP_WIDTH), F32)
    pad = MLA_QK_PAD - MLA_NOPE - MLA_ROPE
    cosm = jnp.concatenate([jnp.ones((C, MLA_NOPE + MLA_ROPE), F32), jnp.zeros((C, pad), F32)], -1)
    return one, zero, one, zero, cosm, jnp.zeros((C, MLA_QK_PAD), F32)


def _fused_in_weight(w_in):
    d = w_in.shape[0]
    p = jnp.split(w_in, IN_CUTS, axis=1)
    z = lambda n: jnp.zeros((d, n), w_in.dtype)
    pad_r = MLA_QK_PAD - MLA_NOPE - MLA_ROPE
    parts = {"qa": p[0], "qa_r": _rot_cols(p[0], HEAD_DIM), "ka": p[1], "ka_r": _rot_cols(p[1], HEAD_DIM),
             "va": p[2], "mqr": p[3], "mkvr": p[4],
             "mkr": jnp.concatenate([z(MLA_NOPE), p[5], z(pad_r)], 1),
             "mkr_r": jnp.concatenate([z(MLA_NOPE), _rot_cols(p[5], MLA_ROPE), z(pad_r)], 1),
             "dq": p[6], "dq_r": _rot_cols(p[6], DIFF_QK), "dk": p[7], "dk_r": _rot_cols(p[7], DIFF_QK),
             "dv": p[8], "nq": p[9], "nk": p[10], "nv": p[11]}
    return jnp.concatenate([parts[n] for n, _ in _SEG_LAYOUT], axis=1).astype(BF16)


def _mla_up_weights(w_uq, w_ukv):
    rq = w_uq.shape[0]
    pad_r = MLA_QK_PAD - MLA_NOPE - MLA_ROPE
    wq = w_uq.reshape(rq, N_HEADS, MLA_NOPE + MLA_ROPE)
    zq = lambda n: jnp.zeros((rq, N_HEADS, n), w_uq.dtype)
    rope_rot = _rot_cols(wq[:, :, MLA_NOPE:].reshape(rq, -1), MLA_ROPE).reshape(rq, N_HEADS, MLA_ROPE)
    main = jnp.concatenate([wq, zq(pad_r)], -1).reshape(rq, -1)
    rot = jnp.concatenate([zq(MLA_NOPE), rope_rot, zq(pad_r)], -1).reshape(rq, -1)
    wuq_ext = jnp.concatenate([main, rot], 1).astype(BF16)
    rk = w_ukv.shape[0]
    wkv = w_ukv.reshape(rk, N_HEADS, MLA_NOPE + MLA_V)
    k_part = jnp.concatenate([wkv[:, :, :MLA_NOPE],
                              jnp.zeros((rk, N_HEADS, MLA_QK_PAD - MLA_NOPE), w_ukv.dtype)], -1)
    wukv_ext = jnp.concatenate([k_part.reshape(rk, -1), wkv[:, :, MLA_NOPE:].reshape(rk, -1)], 1).astype(BF16)
    return wuq_ext, wukv_ext


def kernel(x, c, ctx, c_ctx, w_mod, b_mod, w_in, attn_sink, mla_q_norm, w_uq, mla_kv_norm, w_ukv,
           lam_q1, lam_k1, lam_q2, lam_k2, diff_subln, na_rpb, w_out, ln1_g, ln1_b,
           w_group, b_group, w_router, b_router, w_gate, w_up, w_down, ln2_g, ln2_b):
    B, L, D = x.shape
    C = ctx.shape[1]
    depth = w_mod.shape[0]
    alpha = (2 * depth) ** 0.25
    assert D == D_MODEL and B + 1 <= 8 and L % (NA_Q_ROWS * GRID_W) == 0

    cvec = jnp.concatenate([c, c_ctx[None, :], jnp.zeros((8 - B - 1, D), F32)], axis=0)
    mod = _modulation(cvec, w_mod, b_mod)
    lat_tables = _rope_tables(L)
    ctx_tables = _identity_tables(C)
    row = lambda a: a.reshape(1, -1)

    xc = ctx
    for l in range(depth):
        need_ctx = l < depth - 1
        lam_init = 0.8 - 0.6 * math.exp(-0.3 * l)
        chunks = [mod[l, :, i * D:(i + 1) * D] for i in range(6)]
        sh1, sc1, g1, sh2, sc2, g2 = [m[:B, None, :] for m in chunks]
        sh1c, sc1c, g1c, sh2c, sc2c, g2c = [jnp.broadcast_to(m[B:B + 1, None, :], (B, 1, D)) for m in chunks]

        w_all = _fused_in_weight(w_in[l])
        wuq_ext, wukv_ext = _mla_up_weights(w_uq[l], w_ukv[l])
        proj = functools.partial(_input_projection, w_all=w_all, qng=row(mla_q_norm[l]), wuq=wuq_ext,
                                 kvng=row(mla_kv_norm[l]), wukv=wukv_ext)
        hd = proj(x, sc1, sh1, tables=lat_tables)
        hc = proj(xc, sc1c, sh1c, tables=ctx_tables)

        lam_vecs = jnp.stack([lam_q1[l], lam_k1[l], lam_q2[l], lam_k2[l]])
        subln_g = row(diff_subln[l])
        att = (_swa_attention(attn_sink[l], hd, hc), _mla_attention(hd, hc),
               _diff_attention(lam_vecs, subln_g, hd, hc, lam_init), _na_attention(na_rpb[l], hd, hc))

        wo = w_out[l].astype(BF16)
        pad = ROUTER_LANES - N_EXPERTS - N_GROUPS
        wr = jnp.stack(_split_bf16(jnp.concatenate([w_router[l], w_group[l], jnp.zeros((D, pad), F32)], axis=1)))
        br = row(jnp.concatenate([b_router[l], b_group[l], jnp.zeros((pad,), F32)]))
        post = functools.partial(_output_projection, wo=wo, ln_g=row(ln1_g[l]), ln_b=row(ln1_b[l]),
                                 wr=wr, br=br, alpha=alpha)
        ffn = functools.partial(_moe, wg=w_gate, wu=w_up, wd=w_down, layer=l,
                                ln_g=row(ln2_g[l]), ln_b=row(ln2_b[l]), alpha=alpha)

        x1, rec = post(x, att, g1=g1, sc2=sc2, sh2=sh2)
        x = ffn(rec, x1, g2=g2, sort=True)
        if need_ctx:
            att_c = _ctx_attention(attn_sink[l], lam_vecs, subln_g, hc, lam_init)
            xc1, rec_c = post(xc, att_c, g1=g1c, sc2=sc2c, sh2=sh2c)
            xc = ffn(rec_c, xc1, g2=g2c, sort=False)
    return x
```

```python
import functools
import math

import jax
import jax.numpy as jnp
import numpy as np
from jax import lax
from jax.experimental import pallas as pl
from jax.experimental.pallas import tpu as pltpu

F32 = jnp.float32
BF16 = jnp.bfloat16

D_MODEL = 1024
GRID_W = 64
HEAD_DIM = 64
N_HEADS = 4
N_MIXERS = 4
GROUP_WIDTH = N_HEADS * HEAD_DIM
SWA_KV_HEADS = 2
SWA_WINDOW = 128
SWA_BLOCK = 128
MLA_Q_RANK = 256
MLA_KV_RANK = 128
MLA_NOPE = 64
MLA_ROPE = 32
MLA_V = 64
MLA_QK_PAD = 128
V_EXT = 128
DIFF_QK = 32
DIFF_V = 64
NA_KH = 8
NA_KW = 16
NA_Q_ROWS = 4
NA_K_ROWS = NA_Q_ROWS + NA_KH
N_GROUPS = 4
EXPERTS_PER_GROUP = 8
N_EXPERTS = N_GROUPS * EXPERTS_PER_GROUP
EXPERT_HIDDEN = 256
ROUTER_LANES = 128
GROUP_LANE = N_EXPERTS
ROPE_BASE = 10000.0
NORM_EPS = 1e-5
NEG_INF = -1e30
LOG2E = math.log2(math.e)
SWA_SCALE = HEAD_DIM ** -0.5 * LOG2E
MLA_SCALE = (MLA_NOPE + MLA_ROPE) ** -0.5 * LOG2E
DIFF_SCALE = DIFF_QK ** -0.5 * LOG2E
NA_SCALE = HEAD_DIM ** -0.5 * LOG2E
IN_SPLITS = (GROUP_WIDTH, SWA_KV_HEADS * HEAD_DIM, SWA_KV_HEADS * HEAD_DIM,
             MLA_Q_RANK, MLA_KV_RANK, MLA_ROPE,
             N_HEADS * 2 * DIFF_QK, N_HEADS * 2 * DIFF_QK, N_HEADS * DIFF_V,
             GROUP_WIDTH, GROUP_WIDTH, GROUP_WIDTH)
IN_CUTS = tuple(int(v) for v in np.cumsum(IN_SPLITS)[:-1])

_SEG_LAYOUT = (("qa", 256), ("qa_r", 256), ("ka", 128), ("ka_r", 128), ("va", 128),
               ("mqr", 256), ("mkvr", 128), ("mkr", 128), ("mkr_r", 128),
               ("dq", 256), ("dq_r", 256), ("dk", 256), ("dk_r", 256), ("dv", 256),
               ("nq", 256), ("nk", 256), ("nv", 256))
_SEG = {}
_off = 0
for _name, _w in _SEG_LAYOUT:
    _SEG[_name] = (_off, _off + _w)
    _off += _w
W_ALL_COLS = _off

VMEM_LIMIT_BYTES = 56 * 1024 * 1024


def _cparams(sem):
    return pltpu.CompilerParams(dimension_semantics=sem, vmem_limit_bytes=VMEM_LIMIT_BYTES)


def _dot(a, b):
    return jnp.dot(a, b, preferred_element_type=F32)


def _dot_nt(a, b):
    return lax.dot_general(a, b, (((1,), (1,)), ((), ())), preferred_element_type=F32)


def _with_ones_lane(v):
    lane = lax.broadcasted_iota(jnp.int32, (v.shape[0], V_EXT - v.shape[1]), 1)
    return jnp.concatenate([v, (lane == 0).astype(v.dtype)], axis=-1)


def _rms(x):
    return x * lax.rsqrt(jnp.mean(x * x, axis=-1, keepdims=True) + NORM_EPS)


def _layer_norm(z, g, b):
    mu = jnp.mean(z, axis=-1, keepdims=True)
    zc = z - mu
    var = jnp.mean(zc * zc, axis=-1, keepdims=True)
    return zc * lax.rsqrt(var + NORM_EPS) * g + b


def _mod_kernel(c_ref, w_ref, b_ref, o_ref):
    c = c_ref[...]
    act = c * jax.nn.sigmoid(c)
    o_ref[0] = jnp.dot(act, w_ref[0], preferred_element_type=F32,
                       precision=lax.Precision.HIGHEST) + b_ref[0]


def _modulation(cvec, w_mod, b_mod):
    depth, d, n = w_mod.shape
    tn = 1024
    return pl.pallas_call(
        _mod_kernel,
        grid=(depth, n // tn),
        in_specs=[pl.BlockSpec((8, d), lambda l, j: (0, 0)),
                  pl.BlockSpec((1, d, tn), lambda l, j: (l, 0, j)),
                  pl.BlockSpec((1, 1, tn), lambda l, j: (l, 0, j))],
        out_specs=pl.BlockSpec((1, 8, tn), lambda l, j: (l, 0, j)),
        out_shape=jax.ShapeDtypeStruct((depth, 8, n), F32),
        compiler_params=_cparams(("arbitrary", "arbitrary")),
        name="modulation",
    )(cvec, w_mod, b_mod.reshape(depth, 1, n))


def _inproj_kernel(x_ref, sc_ref, sh_ref, w_ref, cos64_ref, sin64_ref, cos32_ref, sin32_ref,
                   cosm_ref, sinm_ref, qng_ref, wuq_ref, kvng_ref, wukv_ref,
                   qa_o, ka_o, va_o, mq_o, mk_o, mv_o, dq_o, dk_o, dv_o, nq_o, nk_o, nv_o):
    h = (x_ref[0] * (1.0 + sc_ref[0]) + sh_ref[0]).astype(BF16)

    def seg(name):
        a, b = _SEG[name]
        return _dot(h, w_ref[:, a:b])

    def split_heads(val, out_ref, n, width):
        for i in range(n):
            out_ref[0, i] = val[:, i * width:(i + 1) * width].astype(out_ref.dtype)

    cos64 = cos64_ref[...]
    sin64 = sin64_ref[...]
    cos32 = cos32_ref[...]
    sin32 = sin32_ref[...]
    cosm = cosm_ref[...]
    sinm = sinm_ref[...]

    qa = (seg("qa") * cos64 + seg("qa_r") * sin64) * SWA_SCALE
    split_heads(qa, qa_o, N_HEADS, HEAD_DIM)
    ka = seg("ka") * cos64[:, :128] + seg("ka_r") * sin64[:, :128]
    split_heads(ka, ka_o, SWA_KV_HEADS, HEAD_DIM)
    split_heads(seg("va"), va_o, SWA_KV_HEADS, HEAD_DIM)

    qn = (_rms(seg("mqr")) * qng_ref[...]).astype(BF16)
    uq = _dot(qn, wuq_ref[...])
    half = N_HEADS * MLA_QK_PAD
    for i in range(N_HEADS):
        a = i * MLA_QK_PAD
        mq = (uq[:, a:a + MLA_QK_PAD] * cosm + uq[:, half + a:half + a + MLA_QK_PAD] * sinm) * MLA_SCALE
        mq_o[0, i] = mq.astype(BF16)
    kvn = (_rms(seg("mkvr")) * kvng_ref[...]).astype(BF16)
    ukv = _dot(kvn, wukv_ref[...])
    k_rope = seg("mkr") * cosm + seg("mkr_r") * sinm
    for i in range(N_HEADS):
        a = i * MLA_QK_PAD
        mk_o[0, i] = (ukv[:, a:a + MLA_QK_PAD] + k_rope).astype(BF16)
        b = half + i * MLA_V
        mv_o[0, i] = _with_ones_lane(ukv[:, b:b + MLA_V]).astype(BF16)

    dq = (seg("dq") * cos32 + seg("dq_r") * sin32) * DIFF_SCALE
    split_heads(dq, dq_o, 2 * N_HEADS, DIFF_QK)
    dk = seg("dk") * cos32 + seg("dk_r") * sin32
    split_heads(dk, dk_o, 2 * N_HEADS, DIFF_QK)
    dv = seg("dv")
    for i in range(N_HEADS):
        dv_o[0, i] = _with_ones_lane(dv[:, i * DIFF_V:(i + 1) * DIFF_V]).astype(BF16)

    split_heads(seg("nq") * NA_SCALE, nq_o, N_HEADS, HEAD_DIM)
    split_heads(seg("nk"), nk_o, N_HEADS, HEAD_DIM)
    split_heads(seg("nv"), nv_o, N_HEADS, HEAD_DIM)


INPROJ_TM = 512

_HEAD_OUTS = (("qa", N_HEADS, HEAD_DIM), ("ka", SWA_KV_HEADS, HEAD_DIM), ("va", SWA_KV_HEADS, HEAD_DIM),
              ("mq", N_HEADS, MLA_QK_PAD), ("mk", N_HEADS, MLA_QK_PAD), ("mv", N_HEADS, V_EXT),
              ("dq", 2 * N_HEADS, DIFF_QK), ("dk", 2 * N_HEADS, DIFF_QK), ("dv", N_HEADS, V_EXT),
              ("nq", N_HEADS, HEAD_DIM), ("nk", N_HEADS, HEAD_DIM), ("nv", N_HEADS, HEAD_DIM))


def _input_projection(x, sc, sh, w_all, tables, qng, wuq, kvng, wukv):
    B, N, D = x.shape
    tm = min(INPROJ_TM, N)
    tok = lambda i, b: (b, i, 0)
    vec = lambda i, b: (b, 0, 0)
    tab = lambda i, b: (i, 0)
    const = lambda i, b: (0, 0)
    in_specs = [pl.BlockSpec((1, tm, D), tok),
                pl.BlockSpec((1, 1, D), vec), pl.BlockSpec((1, 1, D), vec),
                pl.BlockSpec(w_all.shape, const)]
    in_specs += [pl.BlockSpec((tm, t.shape[1]), tab) for t in tables]
    in_specs += [pl.BlockSpec(a.shape, const) for a in (qng, wuq, kvng, wukv)]
    out_specs = [pl.BlockSpec((1, n, tm, w), lambda i, b: (b, 0, i, 0)) for _, n, w in _HEAD_OUTS]
    out_shape = [jax.ShapeDtypeStruct((B, n, N, w), BF16) for _, n, w in _HEAD_OUTS]
    outs = pl.pallas_call(
        _inproj_kernel,
        grid=(N // tm, B),
        in_specs=in_specs, out_specs=out_specs, out_shape=out_shape,
        compiler_params=_cparams(("arbitrary", "arbitrary")),
        name="input_projection",
    )(x, sc, sh, w_all, *tables, qng, wuq, kvng, wukv)
    return {name: o for (name, _, _), o in zip(_HEAD_OUTS, outs)}


def _softmax_parts(scores, extra_logit=None):
    m = functools.reduce(jnp.maximum, [jnp.max(s, axis=-1, keepdims=True) for s in scores])
    if extra_logit is not None:
        m = jnp.maximum(m, extra_logit)
    ps = [jnp.exp2(s - m) for s in scores]
    denom = functools.reduce(jnp.add, [jnp.sum(p, axis=-1, keepdims=True) for p in ps])
    if extra_logit is not None:
        denom = denom + jnp.exp2(extra_logit - m)
    return ps, denom


def _flash_update(carry, q, k, v):
    m, acc = carry
    s = _dot_nt(q, k)
    m_new = jnp.maximum(m, jnp.max(s, axis=-1, keepdims=True))
    p = jnp.exp2(s - m_new)
    acc = jnp.exp2(m - m_new) * acc + _dot(p.astype(BF16), v)
    return m_new, acc


def _flash_start(qs, ctx_ks, ctx_vs):
    tq = qs[0].shape[0]
    init = (jnp.full((tq, 1), NEG_INF, F32), jnp.zeros((tq, V_EXT), F32))
    return tuple(_flash_update(init, q, k, v) for q, k, v in zip(qs, ctx_ks, ctx_vs))


def _flash_finish(carries, qs, k_ats, v_ats, n_chunks, dv):
    def body(i, carries):
        return tuple(_flash_update(cr, q, k_at(i), v_at(i))
                     for cr, q, k_at, v_at in zip(carries, qs, k_ats, v_ats))

    carries = lax.fori_loop(0, n_chunks, body, carries, unroll=min(FLASH_UNROLL, n_chunks))
    return [acc[:, :dv] / acc[:, dv:dv + 1] for _, acc in carries]


def _lambda_value(lam_ref, lam_init):
    lv = lam_ref[...]
    return (jnp.exp(jnp.sum(lv[0:1] * lv[1:2], axis=-1, keepdims=True))
            - jnp.exp(jnp.sum(lv[2:3] * lv[3:4], axis=-1, keepdims=True)) + lam_init)


def _sub_ln(o, g_ref, lam_init):
    return _rms(o) * g_ref[...] * (1.0 - lam_init)


SWA_TQ = 2 * SWA_BLOCK
SWA_SPAN = SWA_TQ + 2 * SWA_WINDOW


def _swa_kernel(sink_ref, q_ref, k_ref, v_ref, kc_ref, vc_ref, o_ref):
    L = k_ref.shape[2]
    qb = pl.program_id(1)
    start = pl.multiple_of(jnp.clip(qb * SWA_TQ - SWA_WINDOW, 0, L - SWA_SPAN), SWA_BLOCK)
    group = N_HEADS // SWA_KV_HEADS
    rows = group * SWA_TQ
    n_keys = SWA_SPAN + kc_ref.shape[2]
    col = lax.broadcasted_iota(jnp.int32, (rows, n_keys), 1)
    row = lax.broadcasted_iota(jnp.int32, (rows, n_keys), 0)
    q_abs = qb * SWA_TQ + jnp.where(row >= SWA_TQ, row - SWA_TQ, row)
    allowed = (col >= SWA_SPAN) | (jnp.abs(start + col - q_abs) <= SWA_WINDOW)
    for hk in range(SWA_KV_HEADS):
        k_all = jnp.concatenate([k_ref[0, hk, pl.ds(start, SWA_SPAN), :], kc_ref[0, hk]], axis=0)
        v_all = jnp.concatenate([v_ref[0, hk, pl.ds(start, SWA_SPAN), :], vc_ref[0, hk]], axis=0)
        q = jnp.concatenate([q_ref[0, hk * group + g] for g in range(group)], axis=0)
        sink = jnp.concatenate([jnp.full((SWA_TQ, 1), sink_ref[hk * group + g], F32) for g in range(group)],
                               axis=0) * LOG2E
        (p,), denom = _softmax_parts([jnp.where(allowed, _dot_nt(q, k_all), NEG_INF)], sink)
        o = _dot(p.astype(BF16), v_all) / denom
        for g in range(group):
            h = hk * group + g
            o_ref[0, :, h * HEAD_DIM:(h + 1) * HEAD_DIM] = o[g * SWA_TQ:(g + 1) * SWA_TQ].astype(o_ref.dtype)


def _swa_attention(sink, hd, hc):
    B, _, L, _ = hd["qa"].shape
    C = hc["ka"].shape[2]
    assert L % SWA_TQ == 0 and L >= SWA_SPAN
    whole = lambda b, i: (b, 0, 0, 0)
    return pl.pallas_call(
        _swa_kernel,
        grid=(B, L // SWA_TQ),
        in_specs=[pl.BlockSpec(memory_space=pltpu.SMEM),
                  pl.BlockSpec((1, N_HEADS, SWA_TQ, HEAD_DIM), lambda b, i: (b, 0, i, 0)),
                  pl.BlockSpec((1, SWA_KV_HEADS, L, HEAD_DIM), whole),
                  pl.BlockSpec((1, SWA_KV_HEADS, L, HEAD_DIM), whole),
                  pl.BlockSpec((1, SWA_KV_HEADS, C, HEAD_DIM), whole),
                  pl.BlockSpec((1, SWA_KV_HEADS, C, HEAD_DIM), whole)],
        out_specs=pl.BlockSpec((1, SWA_TQ, GROUP_WIDTH), lambda b, i: (b, i, 0)),
        out_shape=jax.ShapeDtypeStruct((B, L, GROUP_WIDTH), BF16),
        compiler_params=_cparams(("arbitrary", "arbitrary")),
        name="swa_attention",
    )(sink, hd["qa"], hd["ka"], hd["va"], hc["ka"], hc["va"])


GLOBAL_TQ = 512
GLOBAL_TK = 2048
HEADS_PER_STEP = 2
FLASH_UNROLL = 2


def _chunk_at(ref, lead, tk):
    return lambda i: ref[lead + (pl.ds(pl.multiple_of(i * tk, tk), tk), slice(None))]


def _mla_kernel(q_ref, kc_ref, vc_ref, k_ref, v_ref, o_ref, *, tk):
    n_chunks = k_ref.shape[2] // tk
    heads = range(HEADS_PER_STEP)
    qs = [q_ref[0, j] for j in heads]
    carries = _flash_start(qs, [kc_ref[0, j] for j in heads], [vc_ref[0, j] for j in heads])
    outs = _flash_finish(carries, qs, [_chunk_at(k_ref, (0, j), tk) for j in heads],
                         [_chunk_at(v_ref, (0, j), tk) for j in heads], n_chunks, MLA_V)
    o_ref[0] = jnp.concatenate(outs, axis=-1).astype(o_ref.dtype)


def _mla_attention(hd, hc):
    B, H, L, dk = hd["mq"].shape
    C = hc["mk"].shape[2]
    tq = min(GLOBAL_TQ, L)
    tk = min(GLOBAL_TK, L)
    hp = HEADS_PER_STEP
    whole = lambda b, h, i: (b, h, 0, 0)
    return pl.pallas_call(
        functools.partial(_mla_kernel, tk=tk),
        grid=(B, H // hp, L // tq),
        in_specs=[pl.BlockSpec((1, hp, tq, dk), lambda b, h, i: (b, h, i, 0)),
                  pl.BlockSpec((1, hp, C, dk), whole),
                  pl.BlockSpec((1, hp, C, V_EXT), whole),
                  pl.BlockSpec((1, hp, L, dk), whole),
                  pl.BlockSpec((1, hp, L, V_EXT), whole)],
        out_specs=pl.BlockSpec((1, tq, hp * MLA_V), lambda b, h, i: (b, i, h)),
        out_shape=jax.ShapeDtypeStruct((B, L, H * MLA_V), BF16),
        compiler_params=_cparams(("arbitrary", "arbitrary", "arbitrary")),
        name="mla_attention",
    )(hd["mq"], hc["mk"], hc["mv"], hd["mk"], hd["mv"])


def _diff_kernel(lam_ref, g_ref, q_ref, kc_ref, vc_ref, k_ref, v_ref, o_ref, *, tk, lam_init):
    n_chunks = k_ref.shape[2] // tk
    lam = _lambda_value(lam_ref, lam_init)
    branches = [(2 * h, 2 * h + 1) for h in range(HEADS_PER_STEP)]
    qs = [[q_ref[0, j] for j in br] for br in branches]
    starts = [_flash_start(qs[h], [kc_ref[0, j] for j in br], [vc_ref[0, h]] * 2) for h, br in enumerate(branches)]
    heads = []
    for h, br in enumerate(branches):
        o1, o2 = _flash_finish(starts[h], qs[h], [_chunk_at(k_ref, (0, j), tk) for j in br],
                               [_chunk_at(v_ref, (0, h), tk)] * 2, n_chunks, DIFF_V)
        heads.append(_sub_ln(o1 - lam * o2, g_ref, lam_init))
    o_ref[0] = jnp.concatenate(heads, axis=-1).astype(o_ref.dtype)


def _diff_attention(lam_vecs, subln_g, hd, hc, lam_init):
    B, _, L, dk = hd["dq"].shape
    C = hc["dk"].shape[2]
    tq = min(GLOBAL_TQ, L)
    tk = min(GLOBAL_TK, L)
    hp = HEADS_PER_STEP
    whole = lambda b, h, i: (b, h, 0, 0)
    const = lambda b, h, i: (0, 0)
    return pl.pallas_call(
        functools.partial(_diff_kernel, tk=tk, lam_init=lam_init),
        grid=(B, N_HEADS // hp, L // tq),
        in_specs=[pl.BlockSpec(lam_vecs.shape, const),
                  pl.BlockSpec(subln_g.shape, const),
                  pl.BlockSpec((1, 2 * hp, tq, dk), lambda b, h, i: (b, h, i, 0)),
                  pl.BlockSpec((1, 2 * hp, C, dk), whole),
                  pl.BlockSpec((1, hp, C, V_EXT), whole),
                  pl.BlockSpec((1, 2 * hp, L, dk), whole),
                  pl.BlockSpec((1, hp, L, V_EXT), whole)],
        out_specs=pl.BlockSpec((1, tq, hp * DIFF_V), lambda b, h, i: (b, i, h)),
        out_shape=jax.ShapeDtypeStruct((B, L, N_HEADS * DIFF_V), BF16),
        compiler_params=_cparams(("arbitrary", "arbitrary", "arbitrary")),
        name="diff_attention",
    )(lam_vecs, subln_g, hd["dq"], hc["dk"], hc["dv"], hd["dk"], hd["dv"])


def _na_plan(rows):
    kh = min(NA_KH, rows)
    n_row_off = 2 * NA_KH - 1
    col = np.arange(GRID_W)
    col_start = np.clip(col - NA_KW // 2, 0, GRID_W - NA_KW)
    col_ok = (col[None, :] >= col_start[:, None]) & (col[None, :] < col_start[:, None] + NA_KW)
    col_off = col[None, :] - col[:, None] + (NA_KW - 1)
    col_onehot = ((col_off[None] == np.arange(2 * NA_KW - 1)[:, None, None]) & col_ok[None]).astype(np.float32)
    patterns, starts, ids = {}, [], []
    for blk in range(rows // NA_Q_ROWS):
        r0 = blk * NA_Q_ROWS
        ks = int(np.clip(r0 - kh // 2, 0, rows - NA_K_ROWS))
        q_row = r0 + np.arange(NA_Q_ROWS)
        k_row = ks + np.arange(NA_K_ROWS)
        r_start = np.clip(q_row - kh // 2, 0, rows - kh)
        row_ok = (k_row[None, :] >= r_start[:, None]) & (k_row[None, :] < r_start[:, None] + kh)
        row_off = k_row[None, :] - q_row[:, None] + (NA_KH - 1)
        sel = np.where(row_ok, row_off, n_row_off)
        key = tuple(int(v) for v in sel.reshape(-1))
        if key not in patterns:
            patterns[key] = (len(patterns), sel)
        starts.append(ks)
        ids.append(patterns[key][0])
    row_sel = np.stack([p[1] for p in sorted(patterns.values(), key=lambda p: p[0])])
    return np.asarray(starts, np.int32), np.asarray(ids, np.int32), row_sel, col_onehot, col_ok


def _na_kernel(ks_ref, pid_ref, q_ref, k_ref, v_ref, kc_ref, vc_ref, bias_ref, o_ref):
    del pid_ref
    k_len = NA_K_ROWS * GRID_W
    start = pl.multiple_of(ks_ref[pl.program_id(1)] * GRID_W, GRID_W)
    for h in range(N_HEADS):
        q = q_ref[0, h]
        kw = k_ref[0, h, pl.ds(start, k_len), :]
        vw = v_ref[0, h, pl.ds(start, k_len), :]
        s_win = _dot_nt(q, kw) + bias_ref[0, h]
        s_ctx = _dot_nt(q, kc_ref[0, h])
        (p_win, p_ctx), denom = _softmax_parts([s_win, s_ctx])
        o = _dot(p_win.astype(BF16), vw) + _dot(p_ctx.astype(BF16), vc_ref[0, h])
        o_ref[0, :, h * HEAD_DIM:(h + 1) * HEAD_DIM] = (o / denom).astype(o_ref.dtype)


def _na_attention(rpb, hd, hc):
    B, H, L, _ = hd["nq"].shape
    C = hc["nk"].shape[2]
    rows = L // GRID_W
    starts, ids, row_sel, col_onehot, col_ok = _na_plan(rows)
    q_len = NA_Q_ROWS * GRID_W
    k_len = NA_K_ROWS * GRID_W
    slab = jnp.einsum("hrj,jqk->hrqk", rpb * LOG2E, col_onehot, precision=lax.Precision.HIGHEST)
    slab = jnp.where(col_ok[None, None], slab, NEG_INF)
    slab = jnp.concatenate([slab, jnp.full_like(slab[:, :1], NEG_INF)], axis=1)
    bias = jnp.take(slab, jnp.asarray(row_sel), axis=1)
    bias = jnp.transpose(bias, (1, 0, 2, 4, 3, 5)).reshape(row_sel.shape[0], H, q_len, k_len)
    whole = lambda b, i, ks, pid: (b, 0, 0, 0)
    grid_spec = pltpu.PrefetchScalarGridSpec(
        num_scalar_prefetch=2,
        grid=(B, rows // NA_Q_ROWS),
        in_specs=[pl.BlockSpec((1, H, q_len, HEAD_DIM), lambda b, i, ks, pid: (b, 0, i, 0)),
                  pl.BlockSpec((1, H, L, HEAD_DIM), whole),
                  pl.BlockSpec((1, H, L, HEAD_DIM), whole),
                  pl.BlockSpec((1, H, C, HEAD_DIM), whole),
                  pl.BlockSpec((1, H, C, HEAD_DIM), whole),
                  pl.BlockSpec((1, H, q_len, k_len), lambda b, i, ks, pid: (pid[i], 0, 0, 0))],
        out_specs=pl.BlockSpec((1, q_len, GROUP_WIDTH), lambda b, i, ks, pid: (b, i, 0)))
    return pl.pallas_call(
        _na_kernel,
        grid_spec=grid_spec,
        out_shape=jax.ShapeDtypeStruct((B, L, GROUP_WIDTH), BF16),
        compiler_params=_cparams(("arbitrary", "arbitrary")),
        name="neighborhood_attention",
    )(jnp.asarray(starts), jnp.asarray(ids), hd["nq"], hd["nk"], hd["nv"], hc["nk"], hc["nv"], bias)


def _ctx_attn_kernel(sink_ref, lam_ref, g_ref, qa_ref, ka_ref, va_ref, mq_ref, mk_ref, mv_ref,
                     dq_ref, dk_ref, dv_ref, nq_ref, nk_ref, nv_ref,
                     ya_ref, yb_ref, yc_ref, yd_ref, *, lam_init):
    def attend(q, k, v, extra=None):
        (p,), denom = _softmax_parts([_dot_nt(q, k)], extra)
        return _dot(p.astype(BF16), v) / denom

    group = N_HEADS // SWA_KV_HEADS
    lam = _lambda_value(lam_ref, lam_init)
    for h in range(N_HEADS):
        lanes = slice(h * HEAD_DIM, (h + 1) * HEAD_DIM)
        ya_ref[0, :, lanes] = attend(qa_ref[0, h], ka_ref[0, h // group], va_ref[0, h // group],
                                     sink_ref[h] * LOG2E).astype(ya_ref.dtype)
        mv = mv_ref[0, h][:, :MLA_V]
        dv = dv_ref[0, h][:, :DIFF_V]
        yb_ref[0, :, lanes] = attend(mq_ref[0, h], mk_ref[0, h], mv).astype(yb_ref.dtype)
        o = (attend(dq_ref[0, 2 * h], dk_ref[0, 2 * h], dv)
             - lam * attend(dq_ref[0, 2 * h + 1], dk_ref[0, 2 * h + 1], dv))
        yc_ref[0, :, lanes] = _sub_ln(o, g_ref, lam_init).astype(yc_ref.dtype)
        yd_ref[0, :, lanes] = attend(nq_ref[0, h], nk_ref[0, h], nv_ref[0, h]).astype(yd_ref.dtype)


def _ctx_attention(sink, lam_vecs, subln_g, hc, lam_init):
    names = ("qa", "ka", "va", "mq", "mk", "mv", "dq", "dk", "dv", "nq", "nk", "nv")
    B, _, C, _ = hc["qa"].shape
    whole4 = lambda b: (b, 0, 0, 0)
    const = lambda b: (0, 0)
    in_specs = [pl.BlockSpec(memory_space=pltpu.SMEM),
                pl.BlockSpec(lam_vecs.shape, const), pl.BlockSpec(subln_g.shape, const)]
    in_specs += [pl.BlockSpec((1,) + hc[n].shape[1:], whole4) for n in names]
    tok = pl.BlockSpec((1, C, GROUP_WIDTH), lambda b: (b, 0, 0))
    return pl.pallas_call(
        functools.partial(_ctx_attn_kernel, lam_init=lam_init),
        grid=(B,),
        in_specs=in_specs,
        out_specs=[tok] * N_MIXERS,
        out_shape=[jax.ShapeDtypeStruct((B, C, GROUP_WIDTH), BF16)] * N_MIXERS,
        compiler_params=_cparams(("arbitrary",)),
        name="context_attention",
    )(sink, lam_vecs, subln_g, *[hc[n] for n in names])


def _route(r):
    lane = lax.broadcasted_iota(jnp.int32, r.shape, 1)
    lane_f = lane.astype(F32)
    big = float(ROUTER_LANES)
    is_grp = (lane >= N_EXPERTS) & (lane < N_EXPERTS + N_GROUPS)
    g_log = jnp.where(is_grp, r, NEG_INF)
    g_max = jnp.max(g_log, axis=-1, keepdims=True)
    g_val = 1.0 / jnp.sum(jnp.exp(g_log - g_max), axis=-1, keepdims=True)
    g_idx = jnp.min(jnp.where(g_log == g_max, lane_f, big), axis=-1, keepdims=True) - float(N_EXPERTS)
    lane_grp = lax.shift_right_logical(lane, int(math.log2(EXPERTS_PER_GROUP))).astype(F32)
    in_grp = (lane < N_EXPERTS) & (lane_grp == g_idx)
    e_log = jnp.where(in_grp, r, NEG_INF)
    e_max = jnp.max(e_log, axis=-1, keepdims=True)
    i1 = jnp.min(jnp.where(e_log == e_max, lane_f, big), axis=-1, keepdims=True)
    e_rest = jnp.where(lane_f == i1, NEG_INF, e_log)
    e_max2 = jnp.max(e_rest, axis=-1, keepdims=True)
    i2 = jnp.min(jnp.where(e_rest == e_max2, lane_f, big), axis=-1, keepdims=True)
    p2 = jnp.exp(e_max2 - e_max)
    w1 = 1.0 / (1.0 + p2)
    w2 = p2 / (1.0 + p2)
    gates = g_val * jnp.where(lane_f == i1, w1, jnp.where(lane_f == i2, w2, 0.0))
    return jnp.where(lane == GROUP_LANE, g_idx, gates)


def _split_bf16(v):
    hi = v.astype(BF16)
    return hi, (v - hi.astype(F32)).astype(BF16)


def _outproj_kernel(x_ref, a_ref, b_ref, c_ref, d_ref, wo_ref, g1_ref, sc2_ref, sh2_ref,
                    lng_ref, lnb_ref, wr_ref, br_ref, x1_ref, rec_ref, *, alpha):
    y = _dot(a_ref[0], wo_ref[0:GROUP_WIDTH, :])
    for i, m_ref in enumerate((b_ref, c_ref, d_ref), start=1):
        y += _dot(m_ref[0], wo_ref[i * GROUP_WIDTH:(i + 1) * GROUP_WIDTH, :])
    x1 = _layer_norm(alpha * x_ref[0] + g1_ref[0] * y, lng_ref[...], lnb_ref[...])
    x1_ref[0] = x1
    h2 = x1 * (1.0 + sc2_ref[0]) + sh2_ref[0]
    rec_ref[0, :, :D_MODEL] = h2
    h_hi, h_lo = _split_bf16(h2)
    r = (_dot(h_hi, wr_ref[0]) + _dot(h_lo, wr_ref[0]) + _dot(h_hi, wr_ref[1])) + br_ref[...]
    rec_ref[0, :, D_MODEL:] = _route(r)


def _output_projection(x, att, wo, g1, sc2, sh2, ln_g, ln_b, wr, br, alpha):
    B, N, D = x.shape
    tm = min(512, N)
    tok = lambda w: pl.BlockSpec((1, tm, w), lambda b, i: (b, i, 0))
    vec = pl.BlockSpec((1, 1, D), lambda b, i: (b, 0, 0))
    const = lambda a: pl.BlockSpec(a.shape, lambda b, i: (0,) * a.ndim)
    return pl.pallas_call(
        functools.partial(_outproj_kernel, alpha=alpha),
        grid=(B, N // tm),
        in_specs=[tok(D)] + [tok(GROUP_WIDTH)] * N_MIXERS + [const(wo), vec, vec, vec,
                  const(ln_g), const(ln_b), const(wr), const(br)],
        out_specs=[tok(D), tok(REC_W)],
        out_shape=[jax.ShapeDtypeStruct((B, N, D), F32), jax.ShapeDtypeStruct((B, N, REC_W), F32)],
        compiler_params=_cparams(("arbitrary", "arbitrary")),
        name="output_projection",
    )(x, att[0], att[1], att[2], att[3], wo, g1, sc2, sh2, ln_g, ln_b, wr, br)


MOE_TM = 1024
MOE_ROWS = 1024
MOE_STEPS = 4
MOE_EXPERTS_PER_STEP = EXPERTS_PER_GROUP // MOE_STEPS
GATHER_ROWS = 1024
GATHER_UNROLL = 8
REC_W = D_MODEL + ROUTER_LANES


def _issue_row_gather(idx_ref, src_ref, dst_ref, sem):
    n = dst_ref.shape[0]

    def row_copy(j):
        return pltpu.make_async_copy(src_ref.at[pl.ds(idx_ref[0, 0, j], 1), :], dst_ref.at[pl.ds(j, 1), :], sem)

    def issue(j, carry):
        row_copy(j).start()
        return carry

    lax.fori_loop(0, n, issue, 0, unroll=GATHER_UNROLL)
    pltpu.make_async_copy(src_ref.at[pl.ds(0, n), :], dst_ref, sem).wait()


def _gather_rows_kernel(idx_ref, src_ref, o_ref, sem):
    _issue_row_gather(idx_ref, src_ref, o_ref, sem)


def _gather_rows(src, idx):
    T, W = src.shape
    tg = min(GATHER_ROWS, T)
    return pl.pallas_call(
        _gather_rows_kernel,
        grid=(T // tg,),
        in_specs=[pl.BlockSpec((1, 1, tg), lambda i: (i, 0, 0), memory_space=pltpu.SMEM),
                  pl.BlockSpec(memory_space=pl.ANY)],
        out_specs=pl.BlockSpec((tg, W), lambda i: (i, 0)),
        out_shape=jax.ShapeDtypeStruct((T, W), src.dtype),
        scratch_shapes=[pltpu.SemaphoreType.DMA(())],
        compiler_params=_cparams(("arbitrary",)),
        name="gather_rows",
    )(idx.reshape(T // tg, 1, tg), src)


def _moe_ffn_kernel(tile_ref, group_ref, first_ref, valid_ref, rec_ref, wg_ref, wu_ref, wd_ref, f_ref):
    del tile_ref
    w = pl.program_id(0)
    step = pl.program_id(1)
    tm = f_ref.shape[0]

    @pl.when(valid_ref[w] != 0)
    def _():
        first = group_ref[w] * EXPERTS_PER_GROUP + step * MOE_EXPERTS_PER_STEP
        rb = min(MOE_ROWS, tm)
        lane = lax.broadcasted_iota(jnp.int32, (rb, ROUTER_LANES), 1)
        experts = range(MOE_EXPERTS_PER_STEP)
        wgs = [wg_ref[0, 0, e].astype(BF16) for e in experts]
        wus = [wu_ref[0, 0, e].astype(BF16) for e in experts]
        wds = [wd_ref[0, 0, e].astype(BF16) for e in experts]
        for r0 in range(0, tm, rb):
            rows = slice(r0, r0 + rb)
            h = rec_ref[rows, :D_MODEL].astype(BF16)
            gates = rec_ref[rows, D_MODEL:]
            y = None
            for e in experts:
                pre = _dot(h, wgs[e])
                hid = pre * jax.nn.sigmoid(pre) * _dot(h, wus[e])
                col = jnp.sum(jnp.where(lane == first + e, gates, 0.0), axis=-1, keepdims=True)
                part = _dot((hid * col).astype(BF16), wds[e])
                y = part if y is None else y + part
            starts_tile = (first_ref[w] != 0) & (step == 0)

            @pl.when(starts_tile)
            def _():
                f_ref[rows, :] = y

            @pl.when(jnp.logical_not(starts_tile))
            def _():
                f_ref[rows, :] += y


def _moe_ffn(rec, items, wg, wu, wd, layer):
    T = rec.shape[0]
    tm = min(MOE_TM, T)
    tile, group, first, valid = items
    n_items = tile.shape[0]
    wspec = lambda a: pl.BlockSpec((1, 1, MOE_EXPERTS_PER_STEP) + a.shape[3:],
                                   lambda w, s, t, g, f, v: (layer, g[w], s, 0, 0))
    grid_spec = pltpu.PrefetchScalarGridSpec(
        num_scalar_prefetch=4,
        grid=(n_items, MOE_STEPS),
        in_specs=[pl.BlockSpec((tm, REC_W), lambda w, s, t, g, f, v: (t[w], 0)), wspec(wg), wspec(wu), wspec(wd)],
        out_specs=pl.BlockSpec((tm, D_MODEL), lambda w, s, t, g, f, v: (t[w], 0)))
    return pl.pallas_call(
        _moe_ffn_kernel,
        grid_spec=grid_spec,
        out_shape=jax.ShapeDtypeStruct((T, D_MODEL), F32),
        compiler_params=_cparams(("arbitrary", "arbitrary")),
        name="moe_ffn",
    )(tile, group, first, valid, rec, wg, wu, wd)


def _moe_items_sorted(sorted_group, tm):
    T = sorted_group.shape[0]
    nt = T // tm
    g_lo = sorted_group[0::tm]
    g_hi = sorted_group[tm - 1::tm]
    per_tile = g_hi - g_lo + 1
    start = jnp.cumsum(per_tile) - per_tile
    n_items = nt + N_GROUPS - 1
    w = jnp.arange(n_items, dtype=jnp.int32)
    tile = jnp.clip(jnp.searchsorted(start, w, side="right") - 1, 0, nt - 1).astype(jnp.int32)
    valid = w < jnp.sum(per_tile)
    offset = w - start[tile]
    group = jnp.where(valid, g_lo[tile] + offset, g_hi[nt - 1]).astype(jnp.int32)
    return tile, group, (valid & (offset == 0)).astype(jnp.int32), valid.astype(jnp.int32)


def _moe_items_dense(T, tm):
    nt = T // tm
    w = np.arange(nt * N_GROUPS, dtype=np.int32)
    return (jnp.asarray(w // N_GROUPS), jnp.asarray(w % N_GROUPS), jnp.asarray((w % N_GROUPS == 0).astype(np.int32)),
            jnp.ones((nt * N_GROUPS,), jnp.int32))


def _ln2_kernel(idx_ref, f_ref, x1_ref, g2_ref, lng_ref, lnb_ref, o_ref, buf_ref, sem, *, alpha):
    _issue_row_gather(idx_ref, f_ref, buf_ref, sem)
    o_ref[0] = _layer_norm(alpha * x1_ref[0] + g2_ref[0] * buf_ref[...], lng_ref[...], lnb_ref[...])


def _ln2_unsort(f_rows, idx, x1, g2, ln_g, ln_b, alpha):
    B, N, D = x1.shape
    tg = min(GATHER_ROWS, N)
    nb = N // tg
    const = lambda a: pl.BlockSpec(a.shape, lambda b, i: (0, 0))
    return pl.pallas_call(
        functools.partial(_ln2_kernel, alpha=alpha),
        grid=(B, nb),
        in_specs=[pl.BlockSpec((1, 1, tg), lambda b, i: (b * nb + i, 0, 0), memory_space=pltpu.SMEM),
                  pl.BlockSpec(memory_space=pl.ANY),
                  pl.BlockSpec((1, tg, D), lambda b, i: (b, i, 0)),
                  pl.BlockSpec((1, 1, D), lambda b, i: (b, 0, 0)), const(ln_g), const(ln_b)],
        out_specs=pl.BlockSpec((1, tg, D), lambda b, i: (b, i, 0)),
        out_shape=jax.ShapeDtypeStruct((B, N, D), F32),
        scratch_shapes=[pltpu.VMEM((tg, D), F32), pltpu.SemaphoreType.DMA(())],
        compiler_params=_cparams(("arbitrary", "arbitrary")),
        name="ln2_unsort",
    )(idx.reshape(B * nb, 1, tg), f_rows, x1, g2, ln_g, ln_b)


def _moe(rec, x1, wg, wu, wd, layer, g2, ln_g, ln_b, alpha, sort):
    B, N, _ = rec.shape
    T = B * N
    tm = min(MOE_TM, T)
    flat = rec.reshape(T, REC_W)
    if sort:
        group = flat[:, D_MODEL + GROUP_LANE].astype(jnp.int32)
        order = jnp.argsort(group, stable=True).astype(jnp.int32)
        place = jnp.argsort(order).astype(jnp.int32)
        flat = _gather_rows(flat, order)
        items = _moe_items_sorted(group[order], tm)
    else:
        place = jnp.arange(T, dtype=jnp.int32)
        items = _moe_items_dense(T, tm)
    f_rows = _moe_ffn(flat, items, wg, wu, wd, layer)
    return _ln2_unsort(f_rows, place, x1, g2, ln_g, ln_b, alpha)


def _rot_cols(w, d):
    k, n = w.shape
    q = d // 4
    w4 = w.reshape(k, n // d, 4, q)
    return jnp.stack([-w4[:, :, 1], w4[:, :, 0], -w4[:, :, 3], w4[:, :, 2]], axis=2).reshape(k, n)


def _rope_tables(L):
    t = jnp.arange(L, dtype=jnp.int32)
    rows = (t // GRID_W).astype(F32)
    cols = (t % GRID_W).astype(F32)

    def cos_sin(d):
        q = d // 4
        inv = ROPE_BASE ** (-jnp.arange(q, dtype=F32) / q)
        ar = rows[:, None] * inv[None, :]
        ac = cols[:, None] * inv[None, :]
        return (jnp.concatenate([jnp.cos(ar), jnp.cos(ar), jnp.cos(ac), jnp.cos(ac)], -1),
                jnp.concatenate([jnp.sin(ar), jnp.sin(ar), jnp.sin(ac), jnp.sin(ac)], -1))

    c64, s64 = cos_sin(HEAD_DIM)
    c32, s32 = cos_sin(MLA_ROPE)
    ones = jnp.ones((L, MLA_NOPE), F32)
    zeros_n = jnp.zeros((L, MLA_NOPE), F32)
    zeros_p = jnp.zeros((L, MLA_QK_PAD - MLA_NOPE - MLA_ROPE), F32)
    return (jnp.tile(c64, (1, N_HEADS)), jnp.tile(s64, (1, N_HEADS)),
            jnp.tile(c32, (1, 2 * N_HEADS)), jnp.tile(s32, (1, 2 * N_HEADS)),
            jnp.concatenate([ones, c32, zeros_p], -1), jnp.concatenate([zeros_n, s32, zeros_p], -1))


def _identity_tables(C):
    one = jnp.ones((C, GROUP_WIDTH), F32)
    zero = jnp.zeros((C, GROUP_WIDTH), F32)
    pad = MLA_QK_PAD - MLA_NOPE - MLA_ROPE
    cosm = jnp.concatenate([jnp.ones((C, MLA_NOPE + MLA_ROPE), F32), jnp.zeros((C, pad), F32)], -1)
    return one, zero, one, zero, cosm, jnp.zeros((C, MLA_QK_PAD), F32)


def _fused_in_weight(w_in):
    d = w_in.shape[0]
    p = jnp.split(w_in, IN_CUTS, axis=1)
    z = lambda n: jnp.zeros((d, n), w_in.dtype)
    pad_r = MLA_QK_PAD - MLA_NOPE - MLA_ROPE
    parts = {"qa": p[0], "qa_r": _rot_cols(p[0], HEAD_DIM), "ka": p[1], "ka_r": _rot_cols(p[1], HEAD_DIM),
             "va": p[2], "mqr": p[3], "mkvr": p[4],
             "mkr": jnp.concatenate([z(MLA_NOPE), p[5], z(pad_r)], 1),
             "mkr_r": jnp.concatenate([z(MLA_NOPE), _rot_cols(p[5], MLA_ROPE), z(pad_r)], 1),
             "dq": p[6], "dq_r": _rot_cols(p[6], DIFF_QK), "dk": p[7], "dk_r": _rot_cols(p[7], DIFF_QK),
             "dv": p[8], "nq": p[9], "nk": p[10], "nv": p[11]}
    return jnp.concatenate([parts[n] for n, _ in _SEG_LAYOUT], axis=1).astype(BF16)


def _mla_up_weights(w_uq, w_ukv):
    rq = w_uq.shape[0]
    pad_r = MLA_QK_PAD - MLA_NOPE - MLA_ROPE
    wq = w_uq.reshape(rq, N_HEADS, MLA_NOPE + MLA_ROPE)
    zq = lambda n: jnp.zeros((rq, N_HEADS, n), w_uq.dtype)
    rope_rot = _rot_cols(wq[:, :, MLA_NOPE:].reshape(rq, -1), MLA_ROPE).reshape(rq, N_HEADS, MLA_ROPE)
    main = jnp.concatenate([wq, zq(pad_r)], -1).reshape(rq, -1)
    rot = jnp.concatenate([zq(MLA_NOPE), rope_rot, zq(pad_r)], -1).reshape(rq, -1)
    wuq_ext = jnp.concatenate([main, rot], 1).astype(BF16)
    rk = w_ukv.shape[0]
    wkv = w_ukv.reshape(rk, N_HEADS, MLA_NOPE + MLA_V)
    k_part = jnp.concatenate([wkv[:, :, :MLA_NOPE],
                              jnp.zeros((rk, N_HEADS, MLA_QK_PAD - MLA_NOPE), w_ukv.dtype)], -1)
    wukv_ext = jnp.concatenate([k_part.reshape(rk, -1), wkv[:, :, MLA_NOPE:].reshape(rk, -1)], 1).astype(BF16)
    return wuq_ext, wukv_ext


def kernel(x, c, ctx, c_ctx, w_mod, b_mod, w_in, attn_sink, mla_q_norm, w_uq, mla_kv_norm, w_ukv,
           lam_q1, lam_k1, lam_q2, lam_k2, diff_subln, na_rpb, w_out, ln1_g, ln1_b,
           w_group, b_group, w_router, b_router, w_gate, w_up, w_down, ln2_g, ln2_b):
    B, L, D = x.shape
    C = ctx.shape[1]
    depth = w_mod.shape[0]
    alpha = (2 * depth) ** 0.25
    assert D == D_MODEL and B + 1 <= 8 and L % (NA_Q_ROWS * GRID_W) == 0

    cvec = jnp.concatenate([c, c_ctx[None, :], jnp.zeros((8 - B - 1, D), F32)], axis=0)
    mod = _modulation(cvec, w_mod, b_mod)
    lat_tables = _rope_tables(L)
    ctx_tables = _identity_tables(C)
    row = lambda a: a.reshape(1, -1)

    xc = ctx
    for l in range(depth):
        need_ctx = l < depth - 1
        lam_init = 0.8 - 0.6 * math.exp(-0.3 * l)
        chunks = [mod[l, :, i * D:(i + 1) * D] for i in range(6)]
        sh1, sc1, g1, sh2, sc2, g2 = [m[:B, None, :] for m in chunks]
        sh1c, sc1c, g1c, sh2c, sc2c, g2c = [jnp.broadcast_to(m[B:B + 1, None, :], (B, 1, D)) for m in chunks]

        w_all = _fused_in_weight(w_in[l])
        wuq_ext, wukv_ext = _mla_up_weights(w_uq[l], w_ukv[l])
        proj = functools.partial(_input_projection, w_all=w_all, qng=row(mla_q_norm[l]), wuq=wuq_ext,
                                 kvng=row(mla_kv_norm[l]), wukv=wukv_ext)
        hd = proj(x, sc1, sh1, tables=lat_tables)
        hc = proj(xc, sc1c, sh1c, tables=ctx_tables)

        lam_vecs = jnp.stack([lam_q1[l], lam_k1[l], lam_q2[l], lam_k2[l]])
        subln_g = row(diff_subln[l])
        att = (_swa_attention(attn_sink[l], hd, hc), _mla_attention(hd, hc),
               _diff_attention(lam_vecs, subln_g, hd, hc, lam_init), _na_attention(na_rpb[l], hd, hc))

        wo = w_out[l].astype(BF16)
        pad = ROUTER_LANES - N_EXPERTS - N_GROUPS
        wr = jnp.stack(_split_bf16(jnp.concatenate([w_router[l], w_group[l], jnp.zeros((D, pad), F32)], axis=1)))
        br = row(jnp.concatenate([b_router[l], b_group[l], jnp.zeros((pad,), F32)]))
        post = functools.partial(_output_projection, wo=wo, ln_g=row(ln1_g[l]), ln_b=row(ln1_b[l]),
                                 wr=wr, br=br, alpha=alpha)
        ffn = functools.partial(_moe, wg=w_gate, wu=w_up, wd=w_down, layer=l,
                                ln_g=row(ln2_g[l]), ln_b=row(ln2_b[l]), alpha=alpha)

        x1, rec = post(x, att, g1=g1, sc2=sc2, sh2=sh2)
        x = ffn(rec, x1, g2=g2, sort=True)
        if need_ctx:
            att_c = _ctx_attention(attn_sink[l], lam_vecs, subln_g, hc, lam_init)
            xc1, rec_c = post(xc, att_c, g1=g1c, sc2=sc2c, sh2=sh2c)
            xc = ffn(rec_c, xc1, g2=g2c, sort=False)
    return x
```

```python
import functools
import math

import jax
import jax.numpy as jnp
import numpy as np
from jax import lax
from jax.experimental import pallas as pl
from jax.experimental.pallas import tpu as pltpu

F32 = jnp.float32
BF16 = jnp.bfloat16

D_MODEL = 1024
GRID_W = 64
HEAD_DIM = 64
N_HEADS = 4
N_MIXERS = 4
GROUP_WIDTH = N_HEADS * HEAD_DIM
SWA_KV_HEADS = 2
SWA_WINDOW = 128
SWA_BLOCK = 128
MLA_Q_RANK = 256
MLA_KV_RANK = 128
MLA_NOPE = 64
MLA_ROPE = 32
MLA_V = 64
MLA_QK_PAD = 128
V_EXT = 128
DIFF_QK = 32
DIFF_V = 64
NA_KH = 8
NA_KW = 16
NA_Q_ROWS = 4
NA_K_ROWS = NA_Q_ROWS + NA_KH
N_GROUPS = 4
EXPERTS_PER_GROUP = 8
N_EXPERTS = N_GROUPS * EXPERTS_PER_GROUP
EXPERT_HIDDEN = 256
ROUTER_LANES = 128
GROUP_LANE = N_EXPERTS
ROPE_BASE = 10000.0
NORM_EPS = 1e-5
NEG_INF = -1e30
LOG2E = math.log2(math.e)
SWA_SCALE = HEAD_DIM ** -0.5 * LOG2E
MLA_SCALE = (MLA_NOPE + MLA_ROPE) ** -0.5 * LOG2E
DIFF_SCALE = DIFF_QK ** -0.5 * LOG2E
NA_SCALE = HEAD_DIM ** -0.5 * LOG2E
IN_SPLITS = (GROUP_WIDTH, SWA_KV_HEADS * HEAD_DIM, SWA_KV_HEADS * HEAD_DIM,
             MLA_Q_RANK, MLA_KV_RANK, MLA_ROPE,
             N_HEADS * 2 * DIFF_QK, N_HEADS * 2 * DIFF_QK, N_HEADS * DIFF_V,
             GROUP_WIDTH, GROUP_WIDTH, GROUP_WIDTH)
IN_CUTS = tuple(int(v) for v in np.cumsum(IN_SPLITS)[:-1])

_SEG_LAYOUT = (("qa", 256), ("qa_r", 256), ("ka", 128), ("ka_r", 128), ("va", 128),
               ("mqr", 256), ("mkvr", 128), ("mkr", 128), ("mkr_r", 128),
               ("dq", 256), ("dq_r", 256), ("dk", 256), ("dk_r", 256), ("dv", 256),
               ("nq", 256), ("nk", 256), ("nv", 256))
_SEG = {}
_off = 0
for _name, _w in _SEG_LAYOUT:
    _SEG[_name] = (_off, _off + _w)
    _off += _w
W_ALL_COLS = _off

VMEM_LIMIT_BYTES = 56 * 1024 * 1024


def _cparams(sem):
    return pltpu.CompilerParams(dimension_semantics=sem, vmem_limit_bytes=VMEM_LIMIT_BYTES)


def _dot(a, b):
    return jnp.dot(a, b, preferred_element_type=F32)


def _dot_nt(a, b):
    return lax.dot_general(a, b, (((1,), (1,)), ((), ())), preferred_element_type=F32)


def _with_ones_lane(v):
    lane = lax.broadcasted_iota(jnp.int32, (v.shape[0], V_EXT - v.shape[1]), 1)
    return jnp.concatenate([v, (lane == 0).astype(v.dtype)], axis=-1)


def _rms(x):
    return x * lax.rsqrt(jnp.mean(x * x, axis=-1, keepdims=True) + NORM_EPS)


def _layer_norm(z, g, b):
    mu = jnp.mean(z, axis=-1, keepdims=True)
    zc = z - mu
    var = jnp.mean(zc * zc, axis=-1, keepdims=True)
    return zc * lax.rsqrt(var + NORM_EPS) * g + b


def _mod_kernel(c_ref, w_ref, b_ref, o_ref):
    c = c_ref[...]
    act = c * jax.nn.sigmoid(c)
    o_ref[0] = jnp.dot(act, w_ref[0], preferred_element_type=F32,
                       precision=lax.Precision.HIGHEST) + b_ref[0]


def _modulation(cvec, w_mod, b_mod):
    depth, d, n = w_mod.shape
    tn = 1024
    return pl.pallas_call(
        _mod_kernel,
        grid=(depth, n // tn),
        in_specs=[pl.BlockSpec((8, d), lambda l, j: (0, 0)),
                  pl.BlockSpec((1, d, tn), lambda l, j: (l, 0, j)),
                  pl.BlockSpec((1, 1, tn), lambda l, j: (l, 0, j))],
        out_specs=pl.BlockSpec((1, 8, tn), lambda l, j: (l, 0, j)),
        out_shape=jax.ShapeDtypeStruct((depth, 8, n), F32),
        compiler_params=_cparams(("arbitrary", "arbitrary")),
        name="modulation",
    )(cvec, w_mod, b_mod.reshape(depth, 1, n))


def _inproj_kernel(x_ref, sc_ref, sh_ref, w_ref, cos64_ref, sin64_ref, cos32_ref, sin32_ref,
                   cosm_ref, sinm_ref, qng_ref, wuq_ref, kvng_ref, wukv_ref,
                   qa_o, ka_o, va_o, mq_o, mk_o, mv_o, dq_o, dk_o, dv_o, nq_o, nk_o, nv_o):
    h = (x_ref[0] * (1.0 + sc_ref[0]) + sh_ref[0]).astype(BF16)

    def seg(name):
        a, b = _SEG[name]
        return _dot(h, w_ref[:, a:b])

    def split_heads(val, out_ref, n, width):
        for i in range(n):
            out_ref[0, i] = val[:, i * width:(i + 1) * width].astype(out_ref.dtype)

    cos64 = cos64_ref[...]
    sin64 = sin64_ref[...]
    cos32 = cos32_ref[...]
    sin32 = sin32_ref[...]
    cosm = cosm_ref[...]
    sinm = sinm_ref[...]

    qa = (seg("qa") * cos64 + seg("qa_r") * sin64) * SWA_SCALE
    split_heads(qa, qa_o, N_HEADS, HEAD_DIM)
    ka = seg("ka") * cos64[:, :128] + seg("ka_r") * sin64[:, :128]
    split_heads(ka, ka_o, SWA_KV_HEADS, HEAD_DIM)
    split_heads(seg("va"), va_o, SWA_KV_HEADS, HEAD_DIM)

    qn = (_rms(seg("mqr")) * qng_ref[...]).astype(BF16)
    uq = _dot(qn, wuq_ref[...])
    half = N_HEADS * MLA_QK_PAD
    for i in range(N_HEADS):
        a = i * MLA_QK_PAD
        mq = (uq[:, a:a + MLA_QK_PAD] * cosm + uq[:, half + a:half + a + MLA_QK_PAD] * sinm) * MLA_SCALE
        mq_o[0, i] = mq.astype(BF16)
    kvn = (_rms(seg("mkvr")) * kvng_ref[...]).astype(BF16)
    ukv = _dot(kvn, wukv_ref[...])
    k_rope = seg("mkr") * cosm + seg("mkr_r") * sinm
    for i in range(N_HEADS):
        a = i * MLA_QK_PAD
        mk_o[0, i] = (ukv[:, a:a + MLA_QK_PAD] + k_rope).astype(BF16)
        b = half + i * MLA_V
        mv_o[0, i] = _with_ones_lane(ukv[:, b:b + MLA_V]).astype(BF16)

    dq = (seg("dq") * cos32 + seg("dq_r") * sin32) * DIFF_SCALE
    split_heads(dq, dq_o, 2 * N_HEADS, DIFF_QK)
    dk = seg("dk") * cos32 + seg("dk_r") * sin32
    split_heads(dk, dk_o, 2 * N_HEADS, DIFF_QK)
    dv = seg("dv")
    for i in range(N_HEADS):
        dv_o[0, i] = _with_ones_lane(dv[:, i * DIFF_V:(i + 1) * DIFF_V]).astype(BF16)

    split_heads(seg("nq") * NA_SCALE, nq_o, N_HEADS, HEAD_DIM)
    split_heads(seg("nk"), nk_o, N_HEADS, HEAD_DIM)
    split_heads(seg("nv"), nv_o, N_HEADS, HEAD_DIM)


INPROJ_TM = 512

_HEAD_OUTS = (("qa", N_HEADS, HEAD_DIM), ("ka", SWA_KV_HEADS, HEAD_DIM), ("va", SWA_KV_HEADS, HEAD_DIM),
              ("mq", N_HEADS, MLA_QK_PAD), ("mk", N_HEADS, MLA_QK_PAD), ("mv", N_HEADS, V_EXT),
              ("dq", 2 * N_HEADS, DIFF_QK), ("dk", 2 * N_HEADS, DIFF_QK), ("dv", N_HEADS, V_EXT),
              ("nq", N_HEADS, HEAD_DIM), ("nk", N_HEADS, HEAD_DIM), ("nv", N_HEADS, HEAD_DIM))


def _input_projection(x, sc, sh, w_all, tables, qng, wuq, kvng, wukv):
    B, N, D = x.shape
    tm = min(INPROJ_TM, N)
    tok = lambda i, b: (b, i, 0)
    vec = lambda i, b: (b, 0, 0)
    tab = lambda i, b: (i, 0)
    const = lambda i, b: (0, 0)
    in_specs = [pl.BlockSpec((1, tm, D), tok),
                pl.BlockSpec((1, 1, D), vec), pl.BlockSpec((1, 1, D), vec),
                pl.BlockSpec(w_all.shape, const)]
    in_specs += [pl.BlockSpec((tm, t.shape[1]), tab) for t in tables]
    in_specs += [pl.BlockSpec(a.shape, const) for a in (qng, wuq, kvng, wukv)]
    out_specs = [pl.BlockSpec((1, n, tm, w), lambda i, b: (b, 0, i, 0)) for _, n, w in _HEAD_OUTS]
    out_shape = [jax.ShapeDtypeStruct((B, n, N, w), BF16) for _, n, w in _HEAD_OUTS]
    outs = pl.pallas_call(
        _inproj_kernel,
        grid=(N // tm, B),
        in_specs=in_specs, out_specs=out_specs, out_shape=out_shape,
        compiler_params=_cparams(("arbitrary", "arbitrary")),
        name="input_projection",
    )(x, sc, sh, w_all, *tables, qng, wuq, kvng, wukv)
    return {name: o for (name, _, _), o in zip(_HEAD_OUTS, outs)}


def _softmax_parts(scores, extra_logit=None):
    m = functools.reduce(jnp.maximum, [jnp.max(s, axis=-1, keepdims=True) for s in scores])
    if extra_logit is not None:
        m = jnp.maximum(m, extra_logit)
    ps = [jnp.exp2(s - m) for s in scores]
    denom = functools.reduce(jnp.add, [jnp.sum(p, axis=-1, keepdims=True) for p in ps])
    if extra_logit is not None:
        denom = denom + jnp.exp2(extra_logit - m)
    return ps, denom


def _flash_update(carry, q, k, v):
    m, acc = carry
    s = _dot_nt(q, k)
    m_new = jnp.maximum(m, jnp.max(s, axis=-1, keepdims=True))
    p = jnp.exp2(s - m_new)
    acc = jnp.exp2(m - m_new) * acc + _dot(p.astype(BF16), v)
    return m_new, acc


def _flash_start(qs, ctx_ks, ctx_vs):
    tq = qs[0].shape[0]
    init = (jnp.full((tq, 1), NEG_INF, F32), jnp.zeros((tq, V_EXT), F32))
    return tuple(_flash_update(init, q, k, v) for q, k, v in zip(qs, ctx_ks, ctx_vs))


def _flash_finish(carries, qs, k_ats, v_ats, n_chunks, dv, unroll):
    def body(i, carries):
        return tuple(_flash_update(cr, q, k_at(i), v_at(i))
                     for cr, q, k_at, v_at in zip(carries, qs, k_ats, v_ats))

    carries = lax.fori_loop(0, n_chunks, body, carries, unroll=min(unroll, n_chunks))
    return [acc[:, :dv] / acc[:, dv:dv + 1] for _, acc in carries]


def _lambda_value(lam_ref, lam_init):
    lv = lam_ref[...]
    return (jnp.exp(jnp.sum(lv[0:1] * lv[1:2], axis=-1, keepdims=True))
            - jnp.exp(jnp.sum(lv[2:3] * lv[3:4], axis=-1, keepdims=True)) + lam_init)


def _sub_ln(o, g_ref, lam_init):
    return _rms(o) * g_ref[...] * (1.0 - lam_init)


SWA_TQ = 2 * SWA_BLOCK
SWA_SPAN = SWA_TQ + 2 * SWA_WINDOW


def _swa_kernel(sink_ref, q_ref, k_ref, v_ref, kc_ref, vc_ref, o_ref):
    L = k_ref.shape[2]
    qb = pl.program_id(1)
    start = pl.multiple_of(jnp.clip(qb * SWA_TQ - SWA_WINDOW, 0, L - SWA_SPAN), SWA_BLOCK)
    group = N_HEADS // SWA_KV_HEADS
    rows = group * SWA_TQ
    n_keys = SWA_SPAN + kc_ref.shape[2]
    col = lax.broadcasted_iota(jnp.int32, (rows, n_keys), 1)
    row = lax.broadcasted_iota(jnp.int32, (rows, n_keys), 0)
    q_abs = qb * SWA_TQ + jnp.where(row >= SWA_TQ, row - SWA_TQ, row)
    allowed = (col >= SWA_SPAN) | (jnp.abs(start + col - q_abs) <= SWA_WINDOW)
    for hk in range(SWA_KV_HEADS):
        k_all = jnp.concatenate([k_ref[0, hk, pl.ds(start, SWA_SPAN), :], kc_ref[0, hk]], axis=0)
        v_all = jnp.concatenate([v_ref[0, hk, pl.ds(start, SWA_SPAN), :], vc_ref[0, hk]], axis=0)
        q = jnp.concatenate([q_ref[0, hk * group + g] for g in range(group)], axis=0)
        sink = jnp.concatenate([jnp.full((SWA_TQ, 1), sink_ref[hk * group + g], F32) for g in range(group)],
                               axis=0) * LOG2E
        (p,), denom = _softmax_parts([jnp.where(allowed, _dot_nt(q, k_all), NEG_INF)], sink)
        o = _dot(p.astype(BF16), v_all) / denom
        for g in range(group):
            h = hk * group + g
            o_ref[0, :, h * HEAD_DIM:(h + 1) * HEAD_DIM] = o[g * SWA_TQ:(g + 1) * SWA_TQ].astype(o_ref.dtype)


def _swa_attention(sink, hd, hc):
    B, _, L, _ = hd["qa"].shape
    C = hc["ka"].shape[2]
    assert L % SWA_TQ == 0 and L >= SWA_SPAN
    whole = lambda b, i: (b, 0, 0, 0)
    return pl.pallas_call(
        _swa_kernel,
        grid=(B, L // SWA_TQ),
        in_specs=[pl.BlockSpec(memory_space=pltpu.SMEM),
                  pl.BlockSpec((1, N_HEADS, SWA_TQ, HEAD_DIM), lambda b, i: (b, 0, i, 0)),
                  pl.BlockSpec((1, SWA_KV_HEADS, L, HEAD_DIM), whole),
                  pl.BlockSpec((1, SWA_KV_HEADS, L, HEAD_DIM), whole),
                  pl.BlockSpec((1, SWA_KV_HEADS, C, HEAD_DIM), whole),
                  pl.BlockSpec((1, SWA_KV_HEADS, C, HEAD_DIM), whole)],
        out_specs=pl.BlockSpec((1, SWA_TQ, GROUP_WIDTH), lambda b, i: (b, i, 0)),
        out_shape=jax.ShapeDtypeStruct((B, L, GROUP_WIDTH), BF16),
        compiler_params=_cparams(("arbitrary", "arbitrary")),
        name="swa_attention",
    )(sink, hd["qa"], hd["ka"], hd["va"], hc["ka"], hc["va"])


GLOBAL_TQ = 512
GLOBAL_TK = 2048
HEADS_PER_STEP = 2
FLASH_UNROLL = 4
MLA_UNROLL = FLASH_UNROLL


def _chunk_at(ref, lead, tk):
    return lambda i: ref[lead + (pl.ds(pl.multiple_of(i * tk, tk), tk), slice(None))]


def _mla_kernel(q_ref, kc_ref, vc_ref, k_ref, v_ref, o_ref, *, tk):
    n_chunks = k_ref.shape[2] // tk
    heads = range(HEADS_PER_STEP)
    qs = [q_ref[0, j] for j in heads]
    carries = _flash_start(qs, [kc_ref[0, j] for j in heads], [vc_ref[0, j] for j in heads])
    outs = _flash_finish(carries, qs, [_chunk_at(k_ref, (0, j), tk) for j in heads],
                         [_chunk_at(v_ref, (0, j), tk) for j in heads], n_chunks, MLA_V, MLA_UNROLL)
    o_ref[0] = jnp.concatenate(outs, axis=-1).astype(o_ref.dtype)


def _mla_attention(hd, hc):
    B, H, L, dk = hd["mq"].shape
    C = hc["mk"].shape[2]
    tq = min(GLOBAL_TQ, L)
    tk = min(GLOBAL_TK, L)
    hp = HEADS_PER_STEP
    whole = lambda b, h, i: (b, h, 0, 0)
    return pl.pallas_call(
        functools.partial(_mla_kernel, tk=tk),
        grid=(B, H // hp, L // tq),
        in_specs=[pl.BlockSpec((1, hp, tq, dk), lambda b, h, i: (b, h, i, 0)),
                  pl.BlockSpec((1, hp, C, dk), whole),
                  pl.BlockSpec((1, hp, C, V_EXT), whole),
                  pl.BlockSpec((1, hp, L, dk), whole),
                  pl.BlockSpec((1, hp, L, V_EXT), whole)],
        out_specs=pl.BlockSpec((1, tq, hp * MLA_V), lambda b, h, i: (b, i, h)),
        out_shape=jax.ShapeDtypeStruct((B, L, H * MLA_V), BF16),
        compiler_params=_cparams(("arbitrary", "arbitrary", "arbitrary")),
        name="mla_attention",
    )(hd["mq"], hc["mk"], hc["mv"], hd["mk"], hd["mv"])


def _diff_kernel(lam_ref, g_ref, q_ref, kc_ref, vc_ref, k_ref, v_ref, o_ref, *, tk, lam_init):
    n_chunks = k_ref.shape[2] // tk
    lam = _lambda_value(lam_ref, lam_init)
    branches = [(2 * h, 2 * h + 1) for h in range(HEADS_PER_STEP)]
    qs = [[q_ref[0, j] for j in br] for br in branches]
    starts = [_flash_start(qs[h], [kc_ref[0, j] for j in br], [vc_ref[0, h]] * 2) for h, br in enumerate(branches)]
    heads = []
    for h, br in enumerate(branches):
        o1, o2 = _flash_finish(starts[h], qs[h], [_chunk_at(k_ref, (0, j), tk) for j in br],
                               [_chunk_at(v_ref, (0, h), tk)] * 2, n_chunks, DIFF_V, FLASH_UNROLL)
        heads.append(_sub_ln(o1 - lam * o2, g_ref, lam_init))
    o_ref[0] = jnp.concatenate(heads, axis=-1).astype(o_ref.dtype)


def _diff_attention(lam_vecs, subln_g, hd, hc, lam_init):
    B, _, L, dk = hd["dq"].shape
    C = hc["dk"].shape[2]
    tq = min(GLOBAL_TQ, L)
    tk = min(GLOBAL_TK, L)
    hp = HEADS_PER_STEP
    whole = lambda b, h, i: (b, h, 0, 0)
    const = lambda b, h, i: (0, 0)
    return pl.pallas_call(
        functools.partial(_diff_kernel, tk=tk, lam_init=lam_init),
        grid=(B, N_HEADS // hp, L // tq),
        in_specs=[pl.BlockSpec(lam_vecs.shape, const),
                  pl.BlockSpec(subln_g.shape, const),
                  pl.BlockSpec((1, 2 * hp, tq, dk), lambda b, h, i: (b, h, i, 0)),
                  pl.BlockSpec((1, 2 * hp, C, dk), whole),
                  pl.BlockSpec((1, hp, C, V_EXT), whole),
                  pl.BlockSpec((1, 2 * hp, L, dk), whole),
                  pl.BlockSpec((1, hp, L, V_EXT), whole)],
        out_specs=pl.BlockSpec((1, tq, hp * DIFF_V), lambda b, h, i: (b, i, h)),
        out_shape=jax.ShapeDtypeStruct((B, L, N_HEADS * DIFF_V), BF16),
        compiler_params=_cparams(("arbitrary", "arbitrary", "arbitrary")),
        name="diff_attention",
    )(lam_vecs, subln_g, hd["dq"], hc["dk"], hc["dv"], hd["dk"], hd["dv"])


def _na_plan(rows):
    kh = min(NA_KH, rows)
    n_row_off = 2 * NA_KH - 1
    col = np.arange(GRID_W)
    col_start = np.clip(col - NA_KW // 2, 0, GRID_W - NA_KW)
    col_ok = (col[None, :] >= col_start[:, None]) & (col[None, :] < col_start[:, None] + NA_KW)
    col_off = col[None, :] - col[:, None] + (NA_KW - 1)
    col_onehot = ((col_off[None] == np.arange(2 * NA_KW - 1)[:, None, None]) & col_ok[None]).astype(np.float32)
    patterns, starts, ids = {}, [], []
    for blk in range(rows // NA_Q_ROWS):
        r0 = blk * NA_Q_ROWS
        ks = int(np.clip(r0 - kh // 2, 0, rows - NA_K_ROWS))
        q_row = r0 + np.arange(NA_Q_ROWS)
        k_row = ks + np.arange(NA_K_ROWS)
        r_start = np.clip(q_row - kh // 2, 0, rows - kh)
        row_ok = (k_row[None, :] >= r_start[:, None]) & (k_row[None, :] < r_start[:, None] + kh)
        row_off = k_row[None, :] - q_row[:, None] + (NA_KH - 1)
        sel = np.where(row_ok, row_off, n_row_off)
        key = tuple(int(v) for v in sel.reshape(-1))
        if key not in patterns:
            patterns[key] = (len(patterns), sel)
        starts.append(ks)
        ids.append(patterns[key][0])
    row_sel = np.stack([p[1] for p in sorted(patterns.values(), key=lambda p: p[0])])
    return np.asarray(starts, np.int32), np.asarray(ids, np.int32), row_sel, col_onehot, col_ok


def _na_kernel(ks_ref, pid_ref, q_ref, k_ref, v_ref, kc_ref, vc_ref, bias_ref, o_ref):
    del pid_ref
    k_len = NA_K_ROWS * GRID_W
    start = pl.multiple_of(ks_ref[pl.program_id(1)] * GRID_W, GRID_W)
    for h in range(N_HEADS):
        q = q_ref[0, h]
        kw = k_ref[0, h, pl.ds(start, k_len), :]
        vw = v_ref[0, h, pl.ds(start, k_len), :]
        s_win = _dot_nt(q, kw) + bias_ref[0, h]
        s_ctx = _dot_nt(q, kc_ref[0, h])
        (p_win, p_ctx), denom = _softmax_parts([s_win, s_ctx])
        o = _dot(p_win.astype(BF16), vw) + _dot(p_ctx.astype(BF16), vc_ref[0, h])
        o_ref[0, :, h * HEAD_DIM:(h + 1) * HEAD_DIM] = (o / denom).astype(o_ref.dtype)


def _na_attention(rpb, hd, hc):
    B, H, L, _ = hd["nq"].shape
    C = hc["nk"].shape[2]
    rows = L // GRID_W
    starts, ids, row_sel, col_onehot, col_ok = _na_plan(rows)
    q_len = NA_Q_ROWS * GRID_W
    k_len = NA_K_ROWS * GRID_W
    slab = jnp.einsum("hrj,jqk->hrqk", rpb * LOG2E, col_onehot, precision=lax.Precision.HIGHEST)
    slab = jnp.where(col_ok[None, None], slab, NEG_INF)
    slab = jnp.concatenate([slab, jnp.full_like(slab[:, :1], NEG_INF)], axis=1)
    bias = jnp.take(slab, jnp.asarray(row_sel), axis=1)
    bias = jnp.transpose(bias, (1, 0, 2, 4, 3, 5)).reshape(row_sel.shape[0], H, q_len, k_len)
    whole = lambda b, i, ks, pid: (b, 0, 0, 0)
    grid_spec = pltpu.PrefetchScalarGridSpec(
        num_scalar_prefetch=2,
        grid=(B, rows // NA_Q_ROWS),
        in_specs=[pl.BlockSpec((1, H, q_len, HEAD_DIM), lambda b, i, ks, pid: (b, 0, i, 0)),
                  pl.BlockSpec((1, H, L, HEAD_DIM), whole),
                  pl.BlockSpec((1, H, L, HEAD_DIM), whole),
                  pl.BlockSpec((1, H, C, HEAD_DIM), whole),
                  pl.BlockSpec((1, H, C, HEAD_DIM), whole),
                  pl.BlockSpec((1, H, q_len, k_len), lambda b, i, ks, pid: (pid[i], 0, 0, 0))],
        out_specs=pl.BlockSpec((1, q_len, GROUP_WIDTH), lambda b, i, ks, pid: (b, i, 0)))
    return pl.pallas_call(
        _na_kernel,
        grid_spec=grid_spec,
        out_shape=jax.ShapeDtypeStruct((B, L, GROUP_WIDTH), BF16),
        compiler_params=_cparams(("arbitrary", "arbitrary")),
        name="neighborhood_attention",
    )(jnp.asarray(starts), jnp.asarray(ids), hd["nq"], hd["nk"], hd["nv"], hc["nk"], hc["nv"], bias)


def _ctx_attn_kernel(sink_ref, lam_ref, g_ref, qa_ref, ka_ref, va_ref, mq_ref, mk_ref, mv_ref,
                     dq_ref, dk_ref, dv_ref, nq_ref, nk_ref, nv_ref,
                     ya_ref, yb_ref, yc_ref, yd_ref, *, lam_init):
    def attend(q, k, v, extra=None):
        (p,), denom = _softmax_parts([_dot_nt(q, k)], extra)
        return _dot(p.astype(BF16), v) / denom

    group = N_HEADS // SWA_KV_HEADS
    lam = _lambda_value(lam_ref, lam_init)
    for h in range(N_HEADS):
        lanes = slice(h * HEAD_DIM, (h + 1) * HEAD_DIM)
        ya_ref[0, :, lanes] = attend(qa_ref[0, h], ka_ref[0, h // group], va_ref[0, h // group],
                                     sink_ref[h] * LOG2E).astype(ya_ref.dtype)
        mv = mv_ref[0, h][:, :MLA_V]
        dv = dv_ref[0, h][:, :DIFF_V]
        yb_ref[0, :, lanes] = attend(mq_ref[0, h], mk_ref[0, h], mv).astype(yb_ref.dtype)
        o = (attend(dq_ref[0, 2 * h], dk_ref[0, 2 * h], dv)
             - lam * attend(dq_ref[0, 2 * h + 1], dk_ref[0, 2 * h + 1], dv))
        yc_ref[0, :, lanes] = _sub_ln(o, g_ref, lam_init).astype(yc_ref.dtype)
        yd_ref[0, :, lanes] = attend(nq_ref[0, h], nk_ref[0, h], nv_ref[0, h]).astype(yd_ref.dtype)


def _ctx_attention(sink, lam_vecs, subln_g, hc, lam_init):
    names = ("qa", "ka", "va", "mq", "mk", "mv", "dq", "dk", "dv", "nq", "nk", "nv")
    B, _, C, _ = hc["qa"].shape
    whole4 = lambda b: (b, 0, 0, 0)
    const = lambda b: (0, 0)
    in_specs = [pl.BlockSpec(memory_space=pltpu.SMEM),
                pl.BlockSpec(lam_vecs.shape, const), pl.BlockSpec(subln_g.shape, const)]
    in_specs += [pl.BlockSpec((1,) + hc[n].shape[1:], whole4) for n in names]
    tok = pl.BlockSpec((1, C, GROUP_WIDTH), lambda b: (b, 0, 0))
    return pl.pallas_call(
        functools.partial(_ctx_attn_kernel, lam_init=lam_init),
        grid=(B,),
        in_specs=in_specs,
        out_specs=[tok] * N_MIXERS,
        out_shape=[jax.ShapeDtypeStruct((B, C, GROUP_WIDTH), BF16)] * N_MIXERS,
        compiler_params=_cparams(("arbitrary",)),
        name="context_attention",
    )(sink, lam_vecs, subln_g, *[hc[n] for n in names])


def _route(r):
    lane = lax.broadcasted_iota(jnp.int32, r.shape, 1)
    lane_f = lane.astype(F32)
    big = float(ROUTER_LANES)
    is_grp = (lane >= N_EXPERTS) & (lane < N_EXPERTS + N_GROUPS)
    g_log = jnp.where(is_grp, r, NEG_INF)
    g_max = jnp.max(g_log, axis=-1, keepdims=True)
    g_val = 1.0 / jnp.sum(jnp.exp(g_log - g_max), axis=-1, keepdims=True)
    g_idx = jnp.min(jnp.where(g_log == g_max, lane_f, big), axis=-1, keepdims=True) - float(N_EXPERTS)
    lane_grp = lax.shift_right_logical(lane, int(math.log2(EXPERTS_PER_GROUP))).astype(F32)
    in_grp = (lane < N_EXPERTS) & (lane_grp == g_idx)
    e_log = jnp.where(in_grp, r, NEG_INF)
    e_max = jnp.max(e_log, axis=-1, keepdims=True)
    i1 = jnp.min(jnp.where(e_log == e_max, lane_f, big), axis=-1, keepdims=True)
    e_rest = jnp.where(lane_f == i1, NEG_INF, e_log)
    e_max2 = jnp.max(e_rest, axis=-1, keepdims=True)
    i2 = jnp.min(jnp.where(e_rest == e_max2, lane_f, big), axis=-1, keepdims=True)
    p2 = jnp.exp(e_max2 - e_max)
    w1 = 1.0 / (1.0 + p2)
    w2 = p2 / (1.0 + p2)
    gates = g_val * jnp.where(lane_f == i1, w1, jnp.where(lane_f == i2, w2, 0.0))
    return jnp.where(lane == GROUP_LANE, g_idx, gates)


def _split_bf16(v):
    hi = v.astype(BF16)
    return hi, (v - hi.astype(F32)).astype(BF16)


def _outproj_kernel(x_ref, a_ref, b_ref, c_ref, d_ref, wo_ref, g1_ref, sc2_ref, sh2_ref,
                    lng_ref, lnb_ref, wr_ref, br_ref, x1_ref, rec_ref, *, alpha):
    y = _dot(a_ref[0], wo_ref[0:GROUP_WIDTH, :])
    for i, m_ref in enumerate((b_ref, c_ref, d_ref), start=1):
        y += _dot(m_ref[0], wo_ref[i * GROUP_WIDTH:(i + 1) * GROUP_WIDTH, :])
    x1 = _layer_norm(alpha * x_ref[0] + g1_ref[0] * y, lng_ref[...], lnb_ref[...])
    x1_ref[0] = x1
    h2 = x1 * (1.0 + sc2_ref[0]) + sh2_ref[0]
    rec_ref[0, :, :D_MODEL] = h2
    h_hi, h_lo = _split_bf16(h2)
    r = (_dot(h_hi, wr_ref[0]) + _dot(h_lo, wr_ref[0]) + _dot(h_hi, wr_ref[1])) + br_ref[...]
    rec_ref[0, :, D_MODEL:] = _route(r)


def _output_projection(x, att, wo, g1, sc2, sh2, ln_g, ln_b, wr, br, alpha):
    B, N, D = x.shape
    tm = min(512, N)
    tok = lambda w: pl.BlockSpec((1, tm, w), lambda b, i: (b, i, 0))
    vec = pl.BlockSpec((1, 1, D), lambda b, i: (b, 0, 0))
    const = lambda a: pl.BlockSpec(a.shape, lambda b, i: (0,) * a.ndim)
    return pl.pallas_call(
        functools.partial(_outproj_kernel, alpha=alpha),
        grid=(B, N // tm),
        in_specs=[tok(D)] + [tok(GROUP_WIDTH)] * N_MIXERS + [const(wo), vec, vec, vec,
                  const(ln_g), const(ln_b), const(wr), const(br)],
        out_specs=[tok(D), tok(REC_W)],
        out_shape=[jax.ShapeDtypeStruct((B, N, D), F32), jax.ShapeDtypeStruct((B, N, REC_W), F32)],
        compiler_params=_cparams(("arbitrary", "arbitrary")),
        name="output_projection",
    )(x, att[0], att[1], att[2], att[3], wo, g1, sc2, sh2, ln_g, ln_b, wr, br)


MOE_TM = 1024
MOE_ROWS = 1024
MOE_STEPS = 4
MOE_EXPERTS_PER_STEP = EXPERTS_PER_GROUP // MOE_STEPS
GATHER_ROWS = 1024
GATHER_UNROLL = 8
REC_W = D_MODEL + ROUTER_LANES


def _issue_row_gather(idx_ref, src_ref, dst_ref, sem):
    n = dst_ref.shape[0]

    def row_copy(j):
        return pltpu.make_async_copy(src_ref.at[pl.ds(idx_ref[0, 0, j], 1), :], dst_ref.at[pl.ds(j, 1), :], sem)

    def issue(j, carry):
        row_copy(j).start()
        return carry

    lax.fori_loop(0, n, issue, 0, unroll=GATHER_UNROLL)
    pltpu.make_async_copy(src_ref.at[pl.ds(0, n), :], dst_ref, sem).wait()


def _gather_rows_kernel(idx_ref, src_ref, o_ref, sem):
    _issue_row_gather(idx_ref, src_ref, o_ref, sem)


def _gather_rows(src, idx):
    T, W = src.shape
    tg = min(GATHER_ROWS, T)
    return pl.pallas_call(
        _gather_rows_kernel,
        grid=(T // tg,),
        in_specs=[pl.BlockSpec((1, 1, tg), lambda i: (i, 0, 0), memory_space=pltpu.SMEM),
                  pl.BlockSpec(memory_space=pl.ANY)],
        out_specs=pl.BlockSpec((tg, W), lambda i: (i, 0)),
        out_shape=jax.ShapeDtypeStruct((T, W), src.dtype),
        scratch_shapes=[pltpu.SemaphoreType.DMA(())],
        compiler_params=_cparams(("arbitrary",)),
        name="gather_rows",
    )(idx.reshape(T // tg, 1, tg), src)


def _moe_ffn_kernel(tile_ref, group_ref, first_ref, valid_ref, rec_ref, wg_ref, wu_ref, wd_ref, f_ref):
    del tile_ref
    w = pl.program_id(0)
    step = pl.program_id(1)
    tm = f_ref.shape[0]

    @pl.when(valid_ref[w] != 0)
    def _():
        first = group_ref[w] * EXPERTS_PER_GROUP + step * MOE_EXPERTS_PER_STEP
        rb = min(MOE_ROWS, tm)
        lane = lax.broadcasted_iota(jnp.int32, (rb, ROUTER_LANES), 1)
        experts = range(MOE_EXPERTS_PER_STEP)
        wgs = [wg_ref[0, 0, e].astype(BF16) for e in experts]
        wus = [wu_ref[0, 0, e].astype(BF16) for e in experts]
        wds = [wd_ref[0, 0, e].astype(BF16) for e in experts]
        for r0 in range(0, tm, rb):
            rows = slice(r0, r0 + rb)
            h = rec_ref[rows, :D_MODEL].astype(BF16)
            gates = rec_ref[rows, D_MODEL:]
            y = None
            for e in experts:
                pre = _dot(h, wgs[e])
                hid = pre * jax.nn.sigmoid(pre) * _dot(h, wus[e])
                col = jnp.sum(jnp.where(lane == first + e, gates, 0.0), axis=-1, keepdims=True)
                part = _dot((hid * col).astype(BF16), wds[e])
                y = part if y is None else y + part
            starts_tile = (first_ref[w] != 0) & (step == 0)

            @pl.when(starts_tile)
            def _():
                f_ref[rows, :] = y

            @pl.when(jnp.logical_not(starts_tile))
            def _():
                f_ref[rows, :] += y


def _moe_ffn(rec, items, wg, wu, wd, layer):
    T = rec.shape[0]
    tm = min(MOE_TM, T)
    tile, group, first, valid = items
    n_items = tile.shape[0]
    wspec = lambda a: pl.BlockSpec((1, 1, MOE_EXPERTS_PER_STEP) + a.shape[3:],
                                   lambda w, s, t, g, f, v: (layer, g[w], s, 0, 0))
    grid_spec = pltpu.PrefetchScalarGridSpec(
        num_scalar_prefetch=4,
        grid=(n_items, MOE_STEPS),
        in_specs=[pl.BlockSpec((tm, REC_W), lambda w, s, t, g, f, v: (t[w], 0)), wspec(wg), wspec(wu), wspec(wd)],
        out_specs=pl.BlockSpec((tm, D_MODEL), lambda w, s, t, g, f, v: (t[w], 0)))
    return pl.pallas_call(
        _moe_ffn_kernel,
        grid_spec=grid_spec,
        out_shape=jax.ShapeDtypeStruct((T, D_MODEL), F32),
        compiler_params=_cparams(("arbitrary", "arbitrary")),
        name="moe_ffn",
    )(tile, group, first, valid, rec, wg, wu, wd)


def _moe_items_sorted(sorted_group, tm):
    T = sorted_group.shape[0]
    nt = T // tm
    g_lo = sorted_group[0::tm]
    g_hi = sorted_group[tm - 1::tm]
    per_tile = g_hi - g_lo + 1
    start = jnp.cumsum(per_tile) - per_tile
    n_items = nt + N_GROUPS - 1
    w = jnp.arange(n_items, dtype=jnp.int32)
    tile = jnp.clip(jnp.searchsorted(start, w, side="right") - 1, 0, nt - 1).astype(jnp.int32)
    valid = w < jnp.sum(per_tile)
    offset = w - start[tile]
    group = jnp.where(valid, g_lo[tile] + offset, g_hi[nt - 1]).astype(jnp.int32)
    return tile, group, (valid & (offset == 0)).astype(jnp.int32), valid.astype(jnp.int32)


def _moe_items_dense(T, tm):
    nt = T // tm
    w = np.arange(nt * N_GROUPS, dtype=np.int32)
    return (jnp.asarray(w // N_GROUPS), jnp.asarray(w % N_GROUPS), jnp.asarray((w % N_GROUPS == 0).astype(np.int32)),
            jnp.ones((nt * N_GROUPS,), jnp.int32))


def _ln2_kernel(idx_ref, f_ref, x1_ref, g2_ref, lng_ref, lnb_ref, o_ref, buf_ref, sem, *, alpha):
    _issue_row_gather(idx_ref, f_ref, buf_ref, sem)
    o_ref[0] = _layer_norm(alpha * x1_ref[0] + g2_ref[0] * buf_ref[...], lng_ref[...], lnb_ref[...])


def _ln2_unsort(f_rows, idx, x1, g2, ln_g, ln_b, alpha):
    B, N, D = x1.shape
    tg = min(GATHER_ROWS, N)
    nb = N // tg
    const = lambda a: pl.BlockSpec(a.shape, lambda b, i: (0, 0))
    return pl.pallas_call(
        functools.partial(_ln2_kernel, alpha=alpha),
        grid=(B, nb),
        in_specs=[pl.BlockSpec((1, 1, tg), lambda b, i: (b * nb + i, 0, 0), memory_space=pltpu.SMEM),
                  pl.BlockSpec(memory_space=pl.ANY),
                  pl.BlockSpec((1, tg, D), lambda b, i: (b, i, 0)),
                  pl.BlockSpec((1, 1, D), lambda b, i: (b, 0, 0)), const(ln_g), const(ln_b)],
        out_specs=pl.BlockSpec((1, tg, D), lambda b, i: (b, i, 0)),
        out_shape=jax.ShapeDtypeStruct((B, N, D), F32),
        scratch_shapes=[pltpu.VMEM((tg, D), F32), pltpu.SemaphoreType.DMA(())],
        compiler_params=_cparams(("arbitrary", "arbitrary")),
        name="ln2_unsort",
    )(idx.reshape(B * nb, 1, tg), f_rows, x1, g2, ln_g, ln_b)


def _moe(rec, x1, wg, wu, wd, layer, g2, ln_g, ln_b, alpha, sort):
    B, N, _ = rec.shape
    T = B * N
    tm = min(MOE_TM, T)
    flat = rec.reshape(T, REC_W)
    if sort:
        group = flat[:, D_MODEL + GROUP_LANE].astype(jnp.int32)
        order = jnp.argsort(group, stable=True).astype(jnp.int32)
        place = jnp.argsort(order).astype(jnp.int32)
        flat = _gather_rows(flat, order)
        items = _moe_items_sorted(group[order], tm)
    else:
        place = jnp.arange(T, dtype=jnp.int32)
        items = _moe_items_dense(T, tm)
    f_rows = _moe_ffn(flat, items, wg, wu, wd, layer)
    return _ln2_unsort(f_rows, place, x1, g2, ln_g, ln_b, alpha)


def _rot_cols(w, d):
    k, n = w.shape
    q = d // 4
    w4 = w.reshape(k, n // d, 4, q)
    return jnp.stack([-w4[:, :, 1], w4[:, :, 0], -w4[:, :, 3], w4[:, :, 2]], axis=2).reshape(k, n)


def _rope_tables(L):
    t = jnp.arange(L, dtype=jnp.int32)
    rows = (t // GRID_W).astype(F32)
    cols = (t % GRID_W).astype(F32)

    def cos_sin(d):
        q = d // 4
        inv = ROPE_BASE ** (-jnp.arange(q, dtype=F32) / q)
        ar = rows[:, None] * inv[None, :]
        ac = cols[:, None] * inv[None, :]
        return (jnp.concatenate([jnp.cos(ar), jnp.cos(ar), jnp.cos(ac), jnp.cos(ac)], -1),
                jnp.concatenate([jnp.sin(ar), jnp.sin(ar), jnp.sin(ac), jnp.sin(ac)], -1))

    c64, s64 = cos_sin(HEAD_DIM)
    c32, s32 = cos_sin(MLA_ROPE)
    ones = jnp.ones((L, MLA_NOPE), F32)
    zeros_n = jnp.zeros((L, MLA_NOPE), F32)
    zeros_p = jnp.zeros((L, MLA_QK_PAD - MLA_NOPE - MLA_ROPE), F32)
    return (jnp.tile(c64, (1, N_HEADS)), jnp.tile(s64, (1, N_HEADS)),
            jnp.tile(c32, (1, 2 * N_HEADS)), jnp.tile(s32, (1, 2 * N_HEADS)),
            jnp.concatenate([ones, c32, zeros_p], -1), jnp.concatenate([zeros_n, s32, zeros_p], -1))


def _identity_tables(C):
    one = jnp.ones((C, GROUP_WIDTH), F32)
    zero = jnp.zeros((C, GROUP_WIDTH), F32)
    pad = MLA_QK_PAD - MLA_NOPE - MLA_ROPE
    cosm = jnp.concatenate([jnp.ones((C, MLA_NOPE + MLA_ROPE), F32), jnp.zeros((C, pad), F32)], -1)
    return one, zero, one, zero, cosm, jnp.zeros((C, MLA_QK_PAD), F32)


def _fused_in_weight(w_in):
    d = w_in.shape[0]
    p = jnp.split(w_in, IN_CUTS, axis=1)
    z = lambda n: jnp.zeros((d, n), w_in.dtype)
    pad_r = MLA_QK_PAD - MLA_NOPE - MLA_ROPE
    parts = {"qa": p[0], "qa_r": _rot_cols(p[0], HEAD_DIM), "ka": p[1], "ka_r": _rot_cols(p[1], HEAD_DIM),
             "va": p[2], "mqr": p[3], "mkvr": p[4],
             "mkr": jnp.concatenate([z(MLA_NOPE), p[5], z(pad_r)], 1),
             "mkr_r": jnp.concatenate([z(MLA_NOPE), _rot_cols(p[5], MLA_ROPE), z(pad_r)], 1),
             "dq": p[6], "dq_r": _rot_cols(p[6], DIFF_QK), "dk": p[7], "dk_r": _rot_cols(p[7], DIFF_QK),
             "dv": p[8], "nq": p[9], "nk": p[10], "nv": p[11]}
    return jnp.concatenate([parts[n] for n, _ in _SEG_LAYOUT], axis=1).astype(BF16)


def _mla_up_weights(w_uq, w_ukv):
    rq = w_uq.shape[0]
    pad_r = MLA_QK_PAD - MLA_NOPE - MLA_ROPE
    wq = w_uq.reshape(rq, N_HEADS, MLA_NOPE + MLA_ROPE)
    zq = lambda n: jnp.zeros((rq, N_HEADS, n), w_uq.dtype)
    rope_rot = _rot_cols(wq[:, :, MLA_NOPE:].reshape(rq, -1), MLA_ROPE).reshape(rq, N_HEADS, MLA_ROPE)
    main = jnp.concatenate([wq, zq(pad_r)], -1).reshape(rq, -1)
    rot = jnp.concatenate([zq(MLA_NOPE), rope_rot, zq(pad_r)], -1).reshape(rq, -1)
    wuq_ext = jnp.concatenate([main, rot], 1).astype(BF16)
    rk = w_ukv.shape[0]
    wkv = w_ukv.reshape(rk, N_HEADS, MLA_NOPE + MLA_V)
    k_part = jnp.concatenate([wkv[:, :, :MLA_NOPE],
                              jnp.zeros((rk, N_HEADS, MLA_QK_PAD - MLA_NOPE), w_ukv.dtype)], -1)
    wukv_ext = jnp.concatenate([k_part.reshape(rk, -1), wkv[:, :, MLA_NOPE:].reshape(rk, -1)], 1).astype(BF16)
    return wuq_ext, wukv_ext


def kernel(x, c, ctx, c_ctx, w_mod, b_mod, w_in, attn_sink, mla_q_norm, w_uq, mla_kv_norm, w_ukv,
           lam_q1, lam_k1, lam_q2, lam_k2, diff_subln, na_rpb, w_out, ln1_g, ln1_b,
           w_group, b_group, w_router, b_router, w_gate, w_up, w_down, ln2_g, ln2_b):
    B, L, D = x.shape
    C = ctx.shape[1]
    depth = w_mod.shape[0]
    alpha = (2 * depth) ** 0.25
    assert D == D_MODEL and B + 1 <= 8 and L % (NA_Q_ROWS * GRID_W) == 0

    cvec = jnp.concatenate([c, c_ctx[None, :], jnp.zeros((8 - B - 1, D), F32)], axis=0)
    mod = _modulation(cvec, w_mod, b_mod)
    lat_tables = _rope_tables(L)
    ctx_tables = _identity_tables(C)
    row = lambda a: a.reshape(1, -1)

    xc = ctx
    for l in range(depth):
        need_ctx = l < depth - 1
        lam_init = 0.8 - 0.6 * math.exp(-0.3 * l)
        chunks = [mod[l, :, i * D:(i + 1) * D] for i in range(6)]
        sh1, sc1, g1, sh2, sc2, g2 = [m[:B, None, :] for m in chunks]
        sh1c, sc1c, g1c, sh2c, sc2c, g2c = [jnp.broadcast_to(m[B:B + 1, None, :], (B, 1, D)) for m in chunks]

        w_all = _fused_in_weight(w_in[l])
        wuq_ext, wukv_ext = _mla_up_weights(w_uq[l], w_ukv[l])
        proj = functools.partial(_input_projection, w_all=w_all, qng=row(mla_q_norm[l]), wuq=wuq_ext,
                                 kvng=row(mla_kv_norm[l]), wukv=wukv_ext)
        hd = proj(x, sc1, sh1, tables=lat_tables)
        hc = proj(xc, sc1c, sh1c, tables=ctx_tables)

        lam_vecs = jnp.stack([lam_q1[l], lam_k1[l], lam_q2[l], lam_k2[l]])
        subln_g = row(diff_subln[l])
        att = (_swa_attention(attn_sink[l], hd, hc), _mla_attention(hd, hc),
               _diff_attention(lam_vecs, subln_g, hd, hc, lam_init), _na_attention(na_rpb[l], hd, hc))

        wo = w_out[l].astype(BF16)
        pad = ROUTER_LANES - N_EXPERTS - N_GROUPS
        wr = jnp.stack(_split_bf16(jnp.concatenate([w_router[l], w_group[l], jnp.zeros((D, pad), F32)], axis=1)))
        br = row(jnp.concatenate([b_router[l], b_group[l], jnp.zeros((pad,), F32)]))
        post = functools.partial(_output_projection, wo=wo, ln_g=row(ln1_g[l]), ln_b=row(ln1_b[l]),
                                 wr=wr, br=br, alpha=alpha)
        ffn = functools.partial(_moe, wg=w_gate, wu=w_up, wd=w_down, layer=l,
                                ln_g=row(ln2_g[l]), ln_b=row(ln2_b[l]), alpha=alpha)

        x1, rec = post(x, att, g1=g1, sc2=sc2, sh2=sh2)
        x = ffn(rec, x1, g2=g2, sort=True)
        if need_ctx:
            att_c = _ctx_attention(attn_sink[l], lam_vecs, subln_g, hc, lam_init)
            xc1, rec_c = post(xc, att_c, g1=g1c, sc2=sc2c, sh2=sh2c)
            xc = ffn(rec_c, xc1, g2=g2c, sort=False)
    return x
```

```python
import functools
import math

import jax
import jax.numpy as jnp
import numpy as np
from jax import lax
from jax.experimental import pallas as pl
from jax.experimental.pallas import tpu as pltpu

F32 = jnp.float32
BF16 = jnp.bfloat16

D_MODEL = 1024
GRID_W = 64
HEAD_DIM = 64
N_HEADS = 4
N_MIXERS = 4
GROUP_WIDTH = N_HEADS * HEAD_DIM
SWA_KV_HEADS = 2
SWA_WINDOW = 128
SWA_BLOCK = 128
MLA_Q_RANK = 256
MLA_KV_RANK = 128
MLA_NOPE = 64
MLA_ROPE = 32
MLA_V = 64
MLA_QK_PAD = 128
V_EXT = 128
DIFF_QK = 32
DIFF_V = 64
NA_KH = 8
NA_KW = 16
NA_Q_ROWS = 4
NA_K_ROWS = NA_Q_ROWS + NA_KH
N_GROUPS = 4
EXPERTS_PER_GROUP = 8
N_EXPERTS = N_GROUPS * EXPERTS_PER_GROUP
EXPERT_HIDDEN = 256
ROUTER_LANES = 128
GROUP_LANE = N_EXPERTS
ROPE_BASE = 10000.0
NORM_EPS = 1e-5
NEG_INF = -1e30
LOG2E = math.log2(math.e)
SWA_SCALE = HEAD_DIM ** -0.5 * LOG2E
MLA_SCALE = (MLA_NOPE + MLA_ROPE) ** -0.5 * LOG2E
DIFF_SCALE = DIFF_QK ** -0.5 * LOG2E
NA_SCALE = HEAD_DIM ** -0.5 * LOG2E
IN_SPLITS = (GROUP_WIDTH, SWA_KV_HEADS * HEAD_DIM, SWA_KV_HEADS * HEAD_DIM,
             MLA_Q_RANK, MLA_KV_RANK, MLA_ROPE,
             N_HEADS * 2 * DIFF_QK, N_HEADS * 2 * DIFF_QK, N_HEADS * DIFF_V,
             GROUP_WIDTH, GROUP_WIDTH, GROUP_WIDTH)
IN_CUTS = tuple(int(v) for v in np.cumsum(IN_SPLITS)[:-1])

_SEG_LAYOUT = (("qa", 256), ("qa_r", 256), ("ka", 128), ("ka_r", 128), ("va", 128),
               ("mqr", 256), ("mkvr", 128), ("mkr", 128), ("mkr_r", 128),
               ("dq", 256), ("dq_r", 256), ("dk", 256), ("dk_r", 256), ("dv", 256),
               ("nq", 256), ("nk", 256), ("nv", 256))
_SEG = {}
_off = 0
for _name, _w in _SEG_LAYOUT:
    _SEG[_name] = (_off, _off + _w)
    _off += _w
W_ALL_COLS = _off

VMEM_LIMIT_BYTES = 56 * 1024 * 1024


def _cparams(sem):
    return pltpu.CompilerParams(dimension_semantics=sem, vmem_limit_bytes=VMEM_LIMIT_BYTES)


def _dot(a, b):
    return jnp.dot(a, b, preferred_element_type=F32)


def _dot_nt(a, b):
    return lax.dot_general(a, b, (((1,), (1,)), ((), ())), preferred_element_type=F32)


def _with_ones_lane(v):
    lane = lax.broadcasted_iota(jnp.int32, (v.shape[0], V_EXT - v.shape[1]), 1)
    return jnp.concatenate([v, (lane == 0).astype(v.dtype)], axis=-1)


def _rms(x):
    return x * lax.rsqrt(jnp.mean(x * x, axis=-1, keepdims=True) + NORM_EPS)


def _layer_norm(z, g, b):
    mu = jnp.mean(z, axis=-1, keepdims=True)
    zc = z - mu
    var = jnp.mean(zc * zc, axis=-1, keepdims=True)
    return zc * lax.rsqrt(var + NORM_EPS) * g + b


def _mod_kernel(c_ref, w_ref, b_ref, o_ref):
    c = c_ref[...]
    act = c * jax.nn.sigmoid(c)
    o_ref[0] = jnp.dot(act, w_ref[0], preferred_element_type=F32,
                       precision=lax.Precision.HIGHEST) + b_ref[0]


def _modulation(cvec, w_mod, b_mod):
    depth, d, n = w_mod.shape
    tn = 1024
    return pl.pallas_call(
        _mod_kernel,
        grid=(depth, n // tn),
        in_specs=[pl.BlockSpec((8, d), lambda l, j: (0, 0)),
                  pl.BlockSpec((1, d, tn), lambda l, j: (l, 0, j)),
                  pl.BlockSpec((1, 1, tn), lambda l, j: (l, 0, j))],
        out_specs=pl.BlockSpec((1, 8, tn), lambda l, j: (l, 0, j)),
        out_shape=jax.ShapeDtypeStruct((depth, 8, n), F32),
        compiler_params=_cparams(("arbitrary", "arbitrary")),
        name="modulation",
    )(cvec, w_mod, b_mod.reshape(depth, 1, n))


def _inproj_kernel(x_ref, sc_ref, sh_ref, w_ref, cos64_ref, sin64_ref, cos32_ref, sin32_ref,
                   cosm_ref, sinm_ref, qng_ref, wuq_ref, kvng_ref, wukv_ref,
                   qa_o, ka_o, va_o, mq_o, mk_o, mv_o, dq_o, dk_o, dv_o, nq_o, nk_o, nv_o):
    h = (x_ref[0] * (1.0 + sc_ref[0]) + sh_ref[0]).astype(BF16)

    def seg(name):
        a, b = _SEG[name]
        return _dot(h, w_ref[:, a:b])

    def split_heads(val, out_ref, n, width):
        for i in range(n):
            out_ref[0, i] = val[:, i * width:(i + 1) * width].astype(out_ref.dtype)

    cos64 = cos64_ref[...]
    sin64 = sin64_ref[...]
    cos32 = cos32_ref[...]
    sin32 = sin32_ref[...]
    cosm = cosm_ref[...]
    sinm = sinm_ref[...]

    qa = (seg("qa") * cos64 + seg("qa_r") * sin64) * SWA_SCALE
    split_heads(qa, qa_o, N_HEADS, HEAD_DIM)
    ka = seg("ka") * cos64[:, :128] + seg("ka_r") * sin64[:, :128]
    split_heads(ka, ka_o, SWA_KV_HEADS, HEAD_DIM)
    split_heads(seg("va"), va_o, SWA_KV_HEADS, HEAD_DIM)

    qn = (_rms(seg("mqr")) * qng_ref[...]).astype(BF16)
    uq = _dot(qn, wuq_ref[...])
    half = N_HEADS * MLA_QK_PAD
    for i in range(N_HEADS):
        a = i * MLA_QK_PAD
        mq = (uq[:, a:a + MLA_QK_PAD] * cosm + uq[:, half + a:half + a + MLA_QK_PAD] * sinm) * MLA_SCALE
        mq_o[0, i] = mq.astype(BF16)
    kvn = (_rms(seg("mkvr")) * kvng_ref[...]).astype(BF16)
    ukv = _dot(kvn, wukv_ref[...])
    k_rope = seg("mkr") * cosm + seg("mkr_r") * sinm
    for i in range(N_HEADS):
        a = i * MLA_QK_PAD
        mk_o[0, i] = (ukv[:, a:a + MLA_QK_PAD] + k_rope).astype(BF16)
        b = half + i * MLA_V
        mv_o[0, i] = _with_ones_lane(ukv[:, b:b + MLA_V]).astype(BF16)

    dq = (seg("dq") * cos32 + seg("dq_r") * sin32) * DIFF_SCALE
    split_heads(dq, dq_o, 2 * N_HEADS, DIFF_QK)
    dk = seg("dk") * cos32 + seg("dk_r") * sin32
    split_heads(dk, dk_o, 2 * N_HEADS, DIFF_QK)
    dv = seg("dv")
    for i in range(N_HEADS):
        dv_o[0, i] = _with_ones_lane(dv[:, i * DIFF_V:(i + 1) * DIFF_V]).astype(BF16)

    split_heads(seg("nq") * NA_SCALE, nq_o, N_HEADS, HEAD_DIM)
    split_heads(seg("nk"), nk_o, N_HEADS, HEAD_DIM)
    split_heads(seg("nv"), nv_o, N_HEADS, HEAD_DIM)


INPROJ_TM = 512

_HEAD_OUTS = (("qa", N_HEADS, HEAD_DIM), ("ka", SWA_KV_HEADS, HEAD_DIM), ("va", SWA_KV_HEADS, HEAD_DIM),
              ("mq", N_HEADS, MLA_QK_PAD), ("mk", N_HEADS, MLA_QK_PAD), ("mv", N_HEADS, V_EXT),
              ("dq", 2 * N_HEADS, DIFF_QK), ("dk", 2 * N_HEADS, DIFF_QK), ("dv", N_HEADS, V_EXT),
              ("nq", N_HEADS, HEAD_DIM), ("nk", N_HEADS, HEAD_DIM), ("nv", N_HEADS, HEAD_DIM))


def _input_projection(x, sc, sh, w_all, tables, qng, wuq, kvng, wukv):
    B, N, D = x.shape
    tm = min(INPROJ_TM, N)
    tok = lambda i, b: (b, i, 0)
    vec = lambda i, b: (b, 0, 0)
    tab = lambda i, b: (i, 0)
    const = lambda i, b: (0, 0)
    in_specs = [pl.BlockSpec((1, tm, D), tok),
                pl.BlockSpec((1, 1, D), vec), pl.BlockSpec((1, 1, D), vec),
                pl.BlockSpec(w_all.shape, const)]
    in_specs += [pl.BlockSpec((tm, t.shape[1]), tab) for t in tables]
    in_specs += [pl.BlockSpec(a.shape, const) for a in (qng, wuq, kvng, wukv)]
    out_specs = [pl.BlockSpec((1, n, tm, w), lambda i, b: (b, 0, i, 0)) for _, n, w in _HEAD_OUTS]
    out_shape = [jax.ShapeDtypeStruct((B, n, N, w), BF16) for _, n, w in _HEAD_OUTS]
    outs = pl.pallas_call(
        _inproj_kernel,
        grid=(N // tm, B),
        in_specs=in_specs, out_specs=out_specs, out_shape=out_shape,
        compiler_params=_cparams(("arbitrary", "arbitrary")),
        name="input_projection",
    )(x, sc, sh, w_all, *tables, qng, wuq, kvng, wukv)
    return {name: o for (name, _, _), o in zip(_HEAD_OUTS, outs)}


def _softmax_parts(scores, extra_logit=None):
    m = functools.reduce(jnp.maximum, [jnp.max(s, axis=-1, keepdims=True) for s in scores])
    if extra_logit is not None:
        m = jnp.maximum(m, extra_logit)
    ps = [jnp.exp2(s - m) for s in scores]
    denom = functools.reduce(jnp.add, [jnp.sum(p, axis=-1, keepdims=True) for p in ps])
    if extra_logit is not None:
        denom = denom + jnp.exp2(extra_logit - m)
    return ps, denom


def _flash_update(carry, q, k, v):
    m, acc = carry
    s = _dot_nt(q, k)
    m_new = jnp.maximum(m, jnp.max(s, axis=-1, keepdims=True))
    p = jnp.exp2(s - m_new)
    acc = jnp.exp2(m - m_new) * acc + _dot(p.astype(BF16), v)
    return m_new, acc


def _flash_start(qs, ctx_ks, ctx_vs):
    tq = qs[0].shape[0]
    init = (jnp.full((tq, 1), NEG_INF, F32), jnp.zeros((tq, V_EXT), F32))
    return tuple(_flash_update(init, q, k, v) for q, k, v in zip(qs, ctx_ks, ctx_vs))


def _flash_finish(carries, qs, k_ats, v_ats, n_chunks, dv, unroll):
    def body(i, carries):
        return tuple(_flash_update(cr, q, k_at(i), v_at(i))
                     for cr, q, k_at, v_at in zip(carries, qs, k_ats, v_ats))

    carries = lax.fori_loop(0, n_chunks, body, carries, unroll=min(unroll, n_chunks))
    return [acc[:, :dv] / acc[:, dv:dv + 1] for _, acc in carries]


def _lambda_value(lam_ref, lam_init):
    lv = lam_ref[...]
    return (jnp.exp(jnp.sum(lv[0:1] * lv[1:2], axis=-1, keepdims=True))
            - jnp.exp(jnp.sum(lv[2:3] * lv[3:4], axis=-1, keepdims=True)) + lam_init)


def _sub_ln(o, g_ref, lam_init):
    return _rms(o) * g_ref[...] * (1.0 - lam_init)


SWA_TQ = 2 * SWA_BLOCK
SWA_SPAN = SWA_TQ + 2 * SWA_WINDOW


def _swa_kernel(sink_ref, q_ref, k_ref, v_ref, kc_ref, vc_ref, o_ref):
    L = k_ref.shape[2]
    qb = pl.program_id(1)
    start = pl.multiple_of(jnp.clip(qb * SWA_TQ - SWA_WINDOW, 0, L - SWA_SPAN), SWA_BLOCK)
    group = N_HEADS // SWA_KV_HEADS
    rows = group * SWA_TQ
    n_keys = SWA_SPAN + kc_ref.shape[2]
    col = lax.broadcasted_iota(jnp.int32, (rows, n_keys), 1)
    row = lax.broadcasted_iota(jnp.int32, (rows, n_keys), 0)
    q_abs = qb * SWA_TQ + jnp.where(row >= SWA_TQ, row - SWA_TQ, row)
    allowed = (col >= SWA_SPAN) | (jnp.abs(start + col - q_abs) <= SWA_WINDOW)
    for hk in range(SWA_KV_HEADS):
        k_all = jnp.concatenate([k_ref[0, hk, pl.ds(start, SWA_SPAN), :], kc_ref[0, hk]], axis=0)
        v_all = jnp.concatenate([v_ref[0, hk, pl.ds(start, SWA_SPAN), :], vc_ref[0, hk]], axis=0)
        q = jnp.concatenate([q_ref[0, hk * group + g] for g in range(group)], axis=0)
        sink = jnp.concatenate([jnp.full((SWA_TQ, 1), sink_ref[hk * group + g], F32) for g in range(group)],
                               axis=0) * LOG2E
        (p,), denom = _softmax_parts([jnp.where(allowed, _dot_nt(q, k_all), NEG_INF)], sink)
        o = _dot(p.astype(BF16), v_all) / denom
        for g in range(group):
            h = hk * group + g
            o_ref[0, :, h * HEAD_DIM:(h + 1) * HEAD_DIM] = o[g * SWA_TQ:(g + 1) * SWA_TQ].astype(o_ref.dtype)


def _swa_attention(sink, hd, hc):
    B, _, L, _ = hd["qa"].shape
    C = hc["ka"].shape[2]
    assert L % SWA_TQ == 0 and L >= SWA_SPAN
    whole = lambda b, i: (b, 0, 0, 0)
    return pl.pallas_call(
        _swa_kernel,
        grid=(B, L // SWA_TQ),
        in_specs=[pl.BlockSpec(memory_space=pltpu.SMEM),
                  pl.BlockSpec((1, N_HEADS, SWA_TQ, HEAD_DIM), lambda b, i: (b, 0, i, 0)),
                  pl.BlockSpec((1, SWA_KV_HEADS, L, HEAD_DIM), whole),
                  pl.BlockSpec((1, SWA_KV_HEADS, L, HEAD_DIM), whole),
                  pl.BlockSpec((1, SWA_KV_HEADS, C, HEAD_DIM), whole),
                  pl.BlockSpec((1, SWA_KV_HEADS, C, HEAD_DIM), whole)],
        out_specs=pl.BlockSpec((1, SWA_TQ, GROUP_WIDTH), lambda b, i: (b, i, 0)),
        out_shape=jax.ShapeDtypeStruct((B, L, GROUP_WIDTH), BF16),
        compiler_params=_cparams(("arbitrary", "arbitrary")),
        name="swa_attention",
    )(sink, hd["qa"], hd["ka"], hd["va"], hc["ka"], hc["va"])


GLOBAL_TQ = 1024
GLOBAL_TK = 2048
HEADS_PER_STEP = 2
FLASH_UNROLL = 4
MLA_UNROLL = FLASH_UNROLL


def _chunk_at(ref, lead, tk):
    return lambda i: ref[lead + (pl.ds(pl.multiple_of(i * tk, tk), tk), slice(None))]


def _mla_kernel(q_ref, kc_ref, vc_ref, k_ref, v_ref, o_ref, *, tk):
    n_chunks = k_ref.shape[2] // tk
    heads = range(HEADS_PER_STEP)
    qs = [q_ref[0, j] for j in heads]
    carries = _flash_start(qs, [kc_ref[0, j] for j in heads], [vc_ref[0, j] for j in heads])
    outs = _flash_finish(carries, qs, [_chunk_at(k_ref, (0, j), tk) for j in heads],
                         [_chunk_at(v_ref, (0, j), tk) for j in heads], n_chunks, MLA_V, MLA_UNROLL)
    o_ref[0] = jnp.concatenate(outs, axis=-1).astype(o_ref.dtype)


def _mla_attention(hd, hc):
    B, H, L, dk = hd["mq"].shape
    C = hc["mk"].shape[2]
    tq = min(GLOBAL_TQ, L)
    tk = min(GLOBAL_TK, L)
    hp = HEADS_PER_STEP
    whole = lambda b, h, i: (b, h, 0, 0)
    return pl.pallas_call(
        functools.partial(_mla_kernel, tk=tk),
        grid=(B, H // hp, L // tq),
        in_specs=[pl.BlockSpec((1, hp, tq, dk), lambda b, h, i: (b, h, i, 0)),
                  pl.BlockSpec((1, hp, C, dk), whole),
                  pl.BlockSpec((1, hp, C, V_EXT), whole),
                  pl.BlockSpec((1, hp, L, dk), whole),
                  pl.BlockSpec((1, hp, L, V_EXT), whole)],
        out_specs=pl.BlockSpec((1, tq, hp * MLA_V), lambda b, h, i: (b, i, h)),
        out_shape=jax.ShapeDtypeStruct((B, L, H * MLA_V), BF16),
        compiler_params=_cparams(("arbitrary", "arbitrary", "arbitrary")),
        name="mla_attention",
    )(hd["mq"], hc["mk"], hc["mv"], hd["mk"], hd["mv"])


def _diff_kernel(lam_ref, g_ref, q_ref, kc_ref, vc_ref, k_ref, v_ref, o_ref, *, tk, lam_init):
    n_chunks = k_ref.shape[2] // tk
    lam = _lambda_value(lam_ref, lam_init)
    branches = [(2 * h, 2 * h + 1) for h in range(HEADS_PER_STEP)]
    qs = [[q_ref[0, j] for j in br] for br in branches]
    starts = [_flash_start(qs[h], [kc_ref[0, j] for j in br], [vc_ref[0, h]] * 2) for h, br in enumerate(branches)]
    heads = []
    for h, br in enumerate(branches):
        o1, o2 = _flash_finish(starts[h], qs[h], [_chunk_at(k_ref, (0, j), tk) for j in br],
                               [_chunk_at(v_ref, (0, h), tk)] * 2, n_chunks, DIFF_V, FLASH_UNROLL)
        heads.append(_sub_ln(o1 - lam * o2, g_ref, lam_init))
    o_ref[0] = jnp.concatenate(heads, axis=-1).astype(o_ref.dtype)


def _diff_attention(lam_vecs, subln_g, hd, hc, lam_init):
    B, _, L, dk = hd["dq"].shape
    C = hc["dk"].shape[2]
    tq = min(GLOBAL_TQ, L)
    tk = min(GLOBAL_TK, L)
    hp = HEADS_PER_STEP
    whole = lambda b, h, i: (b, h, 0, 0)
    const = lambda b, h, i: (0, 0)
    return pl.pallas_call(
        functools.partial(_diff_kernel, tk=tk, lam_init=lam_init),
        grid=(B, N_HEADS // hp, L // tq),
        in_specs=[pl.BlockSpec(lam_vecs.shape, const),
                  pl.BlockSpec(subln_g.shape, const),
                  pl.BlockSpec((1, 2 * hp, tq, dk), lambda b, h, i: (b, h, i, 0)),
                  pl.BlockSpec((1, 2 * hp, C, dk), whole),
                  pl.BlockSpec((1, hp, C, V_EXT), whole),
                  pl.BlockSpec((1, 2 * hp, L, dk), whole),
                  pl.BlockSpec((1, hp, L, V_EXT), whole)],
        out_specs=pl.BlockSpec((1, tq, hp * DIFF_V), lambda b, h, i: (b, i, h)),
        out_shape=jax.ShapeDtypeStruct((B, L, N_HEADS * DIFF_V), BF16),
        compiler_params=_cparams(("arbitrary", "arbitrary", "arbitrary")),
        name="diff_attention",
    )(lam_vecs, subln_g, hd["dq"], hc["dk"], hc["dv"], hd["dk"], hd["dv"])


def _na_plan(rows):
    kh = min(NA_KH, rows)
    n_row_off = 2 * NA_KH - 1
    col = np.arange(GRID_W)
    col_start = np.clip(col - NA_KW // 2, 0, GRID_W - NA_KW)
    col_ok = (col[None, :] >= col_start[:, None]) & (col[None, :] < col_start[:, None] + NA_KW)
    col_off = col[None, :] - col[:, None] + (NA_KW - 1)
    col_onehot = ((col_off[None] == np.arange(2 * NA_KW - 1)[:, None, None]) & col_ok[None]).astype(np.float32)
    patterns, starts, ids = {}, [], []
    for blk in range(rows // NA_Q_ROWS):
        r0 = blk * NA_Q_ROWS
        ks = int(np.clip(r0 - kh // 2, 0, rows - NA_K_ROWS))
        q_row = r0 + np.arange(NA_Q_ROWS)
        k_row = ks + np.arange(NA_K_ROWS)
        r_start = np.clip(q_row - kh // 2, 0, rows - kh)
        row_ok = (k_row[None, :] >= r_start[:, None]) & (k_row[None, :] < r_start[:, None] + kh)
        row_off = k_row[None, :] - q_row[:, None] + (NA_KH - 1)
        sel = np.where(row_ok, row_off, n_row_off)
        key = tuple(int(v) for v in sel.reshape(-1))
        if key not in patterns:
            patterns[key] = (len(patterns), sel)
        starts.append(ks)
        ids.append(patterns[key][0])
    row_sel = np.stack([p[1] for p in sorted(patterns.values(), key=lambda p: p[0])])
    return np.asarray(starts, np.int32), np.asarray(ids, np.int32), row_sel, col_onehot, col_ok


def _na_kernel(ks_ref, pid_ref, q_ref, k_ref, v_ref, kc_ref, vc_ref, bias_ref, o_ref):
    del pid_ref
    k_len = NA_K_ROWS * GRID_W
    start = pl.multiple_of(ks_ref[pl.program_id(1)] * GRID_W, GRID_W)
    for h in range(N_HEADS):
        q = q_ref[0, h]
        kw = k_ref[0, h, pl.ds(start, k_len), :]
        vw = v_ref[0, h, pl.ds(start, k_len), :]
        s_win = _dot_nt(q, kw) + bias_ref[0, h]
        s_ctx = _dot_nt(q, kc_ref[0, h])
        (p_win, p_ctx), denom = _softmax_parts([s_win, s_ctx])
        o = _dot(p_win.astype(BF16), vw) + _dot(p_ctx.astype(BF16), vc_ref[0, h])
        o_ref[0, :, h * HEAD_DIM:(h + 1) * HEAD_DIM] = (o / denom).astype(o_ref.dtype)


def _na_attention(rpb, hd, hc):
    B, H, L, _ = hd["nq"].shape
    C = hc["nk"].shape[2]
    rows = L // GRID_W
    starts, ids, row_sel, col_onehot, col_ok = _na_plan(rows)
    q_len = NA_Q_ROWS * GRID_W
    k_len = NA_K_ROWS * GRID_W
    slab = jnp.einsum("hrj,jqk->hrqk", rpb * LOG2E, col_onehot, precision=lax.Precision.HIGHEST)
    slab = jnp.where(col_ok[None, None], slab, NEG_INF)
    slab = jnp.concatenate([slab, jnp.full_like(slab[:, :1], NEG_INF)], axis=1)
    bias = jnp.take(slab, jnp.asarray(row_sel), axis=1)
    bias = jnp.transpose(bias, (1, 0, 2, 4, 3, 5)).reshape(row_sel.shape[0], H, q_len, k_len)
    whole = lambda b, i, ks, pid: (b, 0, 0, 0)
    grid_spec = pltpu.PrefetchScalarGridSpec(
        num_scalar_prefetch=2,
        grid=(B, rows // NA_Q_ROWS),
        in_specs=[pl.BlockSpec((1, H, q_len, HEAD_DIM), lambda b, i, ks, pid: (b, 0, i, 0)),
                  pl.BlockSpec((1, H, L, HEAD_DIM), whole),
                  pl.BlockSpec((1, H, L, HEAD_DIM), whole),
                  pl.BlockSpec((1, H, C, HEAD_DIM), whole),
                  pl.BlockSpec((1, H, C, HEAD_DIM), whole),
                  pl.BlockSpec((1, H, q_len, k_len), lambda b, i, ks, pid: (pid[i], 0, 0, 0))],
        out_specs=pl.BlockSpec((1, q_len, GROUP_WIDTH), lambda b, i, ks, pid: (b, i, 0)))
    return pl.pallas_call(
        _na_kernel,
        grid_spec=grid_spec,
        out_shape=jax.ShapeDtypeStruct((B, L, GROUP_WIDTH), BF16),
        compiler_params=_cparams(("arbitrary", "arbitrary")),
        name="neighborhood_attention",
    )(jnp.asarray(starts), jnp.asarray(ids), hd["nq"], hd["nk"], hd["nv"], hc["nk"], hc["nv"], bias)


def _ctx_attn_kernel(sink_ref, lam_ref, g_ref, qa_ref, ka_ref, va_ref, mq_ref, mk_ref, mv_ref,
                     dq_ref, dk_ref, dv_ref, nq_ref, nk_ref, nv_ref,
                     ya_ref, yb_ref, yc_ref, yd_ref, *, lam_init):
    def attend(q, k, v, extra=None):
        (p,), denom = _softmax_parts([_dot_nt(q, k)], extra)
        return _dot(p.astype(BF16), v) / denom

    group = N_HEADS // SWA_KV_HEADS
    lam = _lambda_value(lam_ref, lam_init)
    for h in range(N_HEADS):
        lanes = slice(h * HEAD_DIM, (h + 1) * HEAD_DIM)
        ya_ref[0, :, lanes] = attend(qa_ref[0, h], ka_ref[0, h // group], va_ref[0, h // group],
                                     sink_ref[h] * LOG2E).astype(ya_ref.dtype)
        mv = mv_ref[0, h][:, :MLA_V]
        dv = dv_ref[0, h][:, :DIFF_V]
        yb_ref[0, :, lanes] = attend(mq_ref[0, h], mk_ref[0, h], mv).astype(yb_ref.dtype)
        o = (attend(dq_ref[0, 2 * h], dk_ref[0, 2 * h], dv)
             - lam * attend(dq_ref[0, 2 * h + 1], dk_ref[0, 2 * h + 1], dv))
        yc_ref[0, :, lanes] = _sub_ln(o, g_ref, lam_init).astype(yc_ref.dtype)
        yd_ref[0, :, lanes] = attend(nq_ref[0, h], nk_ref[0, h], nv_ref[0, h]).astype(yd_ref.dtype)


def _ctx_attention(sink, lam_vecs, subln_g, hc, lam_init):
    names = ("qa", "ka", "va", "mq", "mk", "mv", "dq", "dk", "dv", "nq", "nk", "nv")
    B, _, C, _ = hc["qa"].shape
    whole4 = lambda b: (b, 0, 0, 0)
    const = lambda b: (0, 0)
    in_specs = [pl.BlockSpec(memory_space=pltpu.SMEM),
                pl.BlockSpec(lam_vecs.shape, const), pl.BlockSpec(subln_g.shape, const)]
    in_specs += [pl.BlockSpec((1,) + hc[n].shape[1:], whole4) for n in names]
    tok = pl.BlockSpec((1, C, GROUP_WIDTH), lambda b: (b, 0, 0))
    return pl.pallas_call(
        functools.partial(_ctx_attn_kernel, lam_init=lam_init),
        grid=(B,),
        in_specs=in_specs,
        out_specs=[tok] * N_MIXERS,
        out_shape=[jax.ShapeDtypeStruct((B, C, GROUP_WIDTH), BF16)] * N_MIXERS,
        compiler_params=_cparams(("arbitrary",)),
        name="context_attention",
    )(sink, lam_vecs, subln_g, *[hc[n] for n in names])


def _route(r):
    lane = lax.broadcasted_iota(jnp.int32, r.shape, 1)
    lane_f = lane.astype(F32)
    big = float(ROUTER_LANES)
    is_grp = (lane >= N_EXPERTS) & (lane < N_EXPERTS + N_GROUPS)
    g_log = jnp.where(is_grp, r, NEG_INF)
    g_max = jnp.max(g_log, axis=-1, keepdims=True)
    g_val = 1.0 / jnp.sum(jnp.exp(g_log - g_max), axis=-1, keepdims=True)
    g_idx = jnp.min(jnp.where(g_log == g_max, lane_f, big), axis=-1, keepdims=True) - float(N_EXPERTS)
    lane_grp = lax.shift_right_logical(lane, int(math.log2(EXPERTS_PER_GROUP))).astype(F32)
    in_grp = (lane < N_EXPERTS) & (lane_grp == g_idx)
    e_log = jnp.where(in_grp, r, NEG_INF)
    e_max = jnp.max(e_log, axis=-1, keepdims=True)
    i1 = jnp.min(jnp.where(e_log == e_max, lane_f, big), axis=-1, keepdims=True)
    e_rest = jnp.where(lane_f == i1, NEG_INF, e_log)
    e_max2 = jnp.max(e_rest, axis=-1, keepdims=True)
    i2 = jnp.min(jnp.where(e_rest == e_max2, lane_f, big), axis=-1, keepdims=True)
    p2 = jnp.exp(e_max2 - e_max)
    w1 = 1.0 / (1.0 + p2)
    w2 = p2 / (1.0 + p2)
    gates = g_val * jnp.where(lane_f == i1, w1, jnp.where(lane_f == i2, w2, 0.0))
    return jnp.where(lane == GROUP_LANE, g_idx, gates)


def _split_bf16(v):
    hi = v.astype(BF16)
    return hi, (v - hi.astype(F32)).astype(BF16)


def _outproj_kernel(x_ref, a_ref, b_ref, c_ref, d_ref, wo_ref, g1_ref, sc2_ref, sh2_ref,
                    lng_ref, lnb_ref, wr_ref, br_ref, x1_ref, rec_ref, *, alpha):
    y = _dot(a_ref[0], wo_ref[0:GROUP_WIDTH, :])
    for i, m_ref in enumerate((b_ref, c_ref, d_ref), start=1):
        y += _dot(m_ref[0], wo_ref[i * GROUP_WIDTH:(i + 1) * GROUP_WIDTH, :])
    x1 = _layer_norm(alpha * x_ref[0] + g1_ref[0] * y, lng_ref[...], lnb_ref[...])
    x1_ref[0] = x1
    h2 = x1 * (1.0 + sc2_ref[0]) + sh2_ref[0]
    rec_ref[0, :, :D_MODEL] = h2
    h_hi, h_lo = _split_bf16(h2)
    r = (_dot(h_hi, wr_ref[0]) + _dot(h_lo, wr_ref[0]) + _dot(h_hi, wr_ref[1])) + br_ref[...]
    rec_ref[0, :, D_MODEL:] = _route(r)


def _output_projection(x, att, wo, g1, sc2, sh2, ln_g, ln_b, wr, br, alpha):
    B, N, D = x.shape
    tm = min(512, N)
    tok = lambda w: pl.BlockSpec((1, tm, w), lambda b, i: (b, i, 0))
    vec = pl.BlockSpec((1, 1, D), lambda b, i: (b, 0, 0))
    const = lambda a: pl.BlockSpec(a.shape, lambda b, i: (0,) * a.ndim)
    return pl.pallas_call(
        functools.partial(_outproj_kernel, alpha=alpha),
        grid=(B, N // tm),
        in_specs=[tok(D)] + [tok(GROUP_WIDTH)] * N_MIXERS + [const(wo), vec, vec, vec,
                  const(ln_g), const(ln_b), const(wr), const(br)],
        out_specs=[tok(D), tok(REC_W)],
        out_shape=[jax.ShapeDtypeStruct((B, N, D), F32), jax.ShapeDtypeStruct((B, N, REC_W), F32)],
        compiler_params=_cparams(("arbitrary", "arbitrary")),
        name="output_projection",
    )(x, att[0], att[1], att[2], att[3], wo, g1, sc2, sh2, ln_g, ln_b, wr, br)


MOE_TM = 1024
MOE_ROWS = 1024
MOE_STEPS = 4
MOE_EXPERTS_PER_STEP = EXPERTS_PER_GROUP // MOE_STEPS
GATHER_ROWS = 1024
GATHER_UNROLL = 8
REC_W = D_MODEL + ROUTER_LANES


def _issue_row_gather(idx_ref, src_ref, dst_ref, sem):
    n = dst_ref.shape[0]

    def row_copy(j):
        return pltpu.make_async_copy(src_ref.at[pl.ds(idx_ref[0, 0, j], 1), :], dst_ref.at[pl.ds(j, 1), :], sem)

    def issue(j, carry):
        row_copy(j).start()
        return carry

    lax.fori_loop(0, n, issue, 0, unroll=GATHER_UNROLL)
    pltpu.make_async_copy(src_ref.at[pl.ds(0, n), :], dst_ref, sem).wait()


def _gather_rows_kernel(idx_ref, src_ref, o_ref, sem):
    _issue_row_gather(idx_ref, src_ref, o_ref, sem)


def _gather_rows(src, idx):
    T, W = src.shape
    tg = min(GATHER_ROWS, T)
    return pl.pallas_call(
        _gather_rows_kernel,
        grid=(T // tg,),
        in_specs=[pl.BlockSpec((1, 1, tg), lambda i: (i, 0, 0), memory_space=pltpu.SMEM),
                  pl.BlockSpec(memory_space=pl.ANY)],
        out_specs=pl.BlockSpec((tg, W), lambda i: (i, 0)),
        out_shape=jax.ShapeDtypeStruct((T, W), src.dtype),
        scratch_shapes=[pltpu.SemaphoreType.DMA(())],
        compiler_params=_cparams(("arbitrary",)),
        name="gather_rows",
    )(idx.reshape(T // tg, 1, tg), src)


def _moe_ffn_kernel(tile_ref, group_ref, first_ref, valid_ref, rec_ref, wg_ref, wu_ref, wd_ref, f_ref):
    del tile_ref
    w = pl.program_id(0)
    step = pl.program_id(1)
    tm = f_ref.shape[0]

    @pl.when(valid_ref[w] != 0)
    def _():
        first = group_ref[w] * EXPERTS_PER_GROUP + step * MOE_EXPERTS_PER_STEP
        rb = min(MOE_ROWS, tm)
        lane = lax.broadcasted_iota(jnp.int32, (rb, ROUTER_LANES), 1)
        experts = range(MOE_EXPERTS_PER_STEP)
        wgs = [wg_ref[0, 0, e].astype(BF16) for e in experts]
        wus = [wu_ref[0, 0, e].astype(BF16) for e in experts]
        wds = [wd_ref[0, 0, e].astype(BF16) for e in experts]
        for r0 in range(0, tm, rb):
            rows = slice(r0, r0 + rb)
            h = rec_ref[rows, :D_MODEL].astype(BF16)
            gates = rec_ref[rows, D_MODEL:]
            y = None
            for e in experts:
                pre = _dot(h, wgs[e])
                hid = pre * jax.nn.sigmoid(pre) * _dot(h, wus[e])
                col = jnp.sum(jnp.where(lane == first + e, gates, 0.0), axis=-1, keepdims=True)
                part = _dot((hid * col).astype(BF16), wds[e])
                y = part if y is None else y + part
            starts_tile = (first_ref[w] != 0) & (step == 0)

            @pl.when(starts_tile)
            def _():
                f_ref[rows, :] = y

            @pl.when(jnp.logical_not(starts_tile))
            def _():
                f_ref[rows, :] += y


def _moe_ffn(rec, items, wg, wu, wd, layer):
    T = rec.shape[0]
    tm = min(MOE_TM, T)
    tile, group, first, valid = items
    n_items = tile.shape[0]
    wspec = lambda a: pl.BlockSpec((1, 1, MOE_EXPERTS_PER_STEP) + a.shape[3:],
                                   lambda w, s, t, g, f, v: (layer, g[w], s, 0, 0))
    grid_spec = pltpu.PrefetchScalarGridSpec(
        num_scalar_prefetch=4,
        grid=(n_items, MOE_STEPS),
        in_specs=[pl.BlockSpec((tm, REC_W), lambda w, s, t, g, f, v: (t[w], 0)), wspec(wg), wspec(wu), wspec(wd)],
        out_specs=pl.BlockSpec((tm, D_MODEL), lambda w, s, t, g, f, v: (t[w], 0)))
    return pl.pallas_call(
        _moe_ffn_kernel,
        grid_spec=grid_spec,
        out_shape=jax.ShapeDtypeStruct((T, D_MODEL), F32),
        compiler_params=_cparams(("arbitrary", "arbitrary")),
        name="moe_ffn",
    )(tile, group, first, valid, rec, wg, wu, wd)


def _moe_items_sorted(sorted_group, tm):
    T = sorted_group.shape[0]
    nt = T // tm
    g_lo = sorted_group[0::tm]
    g_hi = sorted_group[tm - 1::tm]
    per_tile = g_hi - g_lo + 1
    start = jnp.cumsum(per_tile) - per_tile
    n_items = nt + N_GROUPS - 1
    w = jnp.arange(n_items, dtype=jnp.int32)
    tile = jnp.clip(jnp.searchsorted(start, w, side="right") - 1, 0, nt - 1).astype(jnp.int32)
    valid = w < jnp.sum(per_tile)
    offset = w - start[tile]
    group = jnp.where(valid, g_lo[tile] + offset, g_hi[nt - 1]).astype(jnp.int32)
    return tile, group, (valid & (offset == 0)).astype(jnp.int32), valid.astype(jnp.int32)


def _moe_items_dense(T, tm):
    nt = T // tm
    w = np.arange(nt * N_GROUPS, dtype=np.int32)
    return (jnp.asarray(w // N_GROUPS), jnp.asarray(w % N_GROUPS), jnp.asarray((w % N_GROUPS == 0).astype(np.int32)),
            jnp.ones((nt * N_GROUPS,), jnp.int32))


def _ln2_kernel(idx_ref, f_ref, x1_ref, g2_ref, lng_ref, lnb_ref, o_ref, buf_ref, sem, *, alpha):
    _issue_row_gather(idx_ref, f_ref, buf_ref, sem)
    o_ref[0] = _layer_norm(alpha * x1_ref[0] + g2_ref[0] * buf_ref[...], lng_ref[...], lnb_ref[...])


def _ln2_unsort(f_rows, idx, x1, g2, ln_g, ln_b, alpha):
    B, N, D = x1.shape
    tg = min(GATHER_ROWS, N)
    nb = N // tg
    const = lambda a: pl.BlockSpec(a.shape, lambda b, i: (0, 0))
    return pl.pallas_call(
        functools.partial(_ln2_kernel, alpha=alpha),
        grid=(B, nb),
        in_specs=[pl.BlockSpec((1, 1, tg), lambda b, i: (b * nb + i, 0, 0), memory_space=pltpu.SMEM),
                  pl.BlockSpec(memory_space=pl.ANY),
                  pl.BlockSpec((1, tg, D), lambda b, i: (b, i, 0)),
                  pl.BlockSpec((1, 1, D), lambda b, i: (b, 0, 0)), const(ln_g), const(ln_b)],
        out_specs=pl.BlockSpec((1, tg, D), lambda b, i: (b, i, 0)),
        out_shape=jax.ShapeDtypeStruct((B, N, D), F32),
        scratch_shapes=[pltpu.VMEM((tg, D), F32), pltpu.SemaphoreType.DMA(())],
        compiler_params=_cparams(("arbitrary", "arbitrary")),
        name="ln2_unsort",
    )(idx.reshape(B * nb, 1, tg), f_rows, x1, g2, ln_g, ln_b)


def _moe(rec, x1, wg, wu, wd, layer, g2, ln_g, ln_b, alpha, sort):
    B, N, _ = rec.shape
    T = B * N
    tm = min(MOE_TM, T)
    flat = rec.reshape(T, REC_W)
    if sort:
        group = flat[:, D_MODEL + GROUP_LANE].astype(jnp.int32)
        order = jnp.argsort(group, stable=True).astype(jnp.int32)
        place = jnp.argsort(order).astype(jnp.int32)
        flat = _gather_rows(flat, order)
        items = _moe_items_sorted(group[order], tm)
    else:
        place = jnp.arange(T, dtype=jnp.int32)
        items = _moe_items_dense(T, tm)
    f_rows = _moe_ffn(flat, items, wg, wu, wd, layer)
    return _ln2_unsort(f_rows, place, x1, g2, ln_g, ln_b, alpha)


def _rot_cols(w, d):
    k, n = w.shape
    q = d // 4
    w4 = w.reshape(k, n // d, 4, q)
    return jnp.stack([-w4[:, :, 1], w4[:, :, 0], -w4[:, :, 3], w4[:, :, 2]], axis=2).reshape(k, n)


def _rope_tables(L):
    t = jnp.arange(L, dtype=jnp.int32)
    rows = (t // GRID_W).astype(F32)
    cols = (t % GRID_W).astype(F32)

    def cos_sin(d):
        q = d // 4
        inv = ROPE_BASE ** (-jnp.arange(q, dtype=F32) / q)
        ar = rows[:, None] * inv[None, :]
        ac = cols[:, None] * inv[None, :]
        return (jnp.concatenate([jnp.cos(ar), jnp.cos(ar), jnp.cos(ac), jnp.cos(ac)], -1),
                jnp.concatenate([jnp.sin(ar), jnp.sin(ar), jnp.sin(ac), jnp.sin(ac)], -1))

    c64, s64 = cos_sin(HEAD_DIM)
    c32, s32 = cos_sin(MLA_ROPE)
    ones = jnp.ones((L, MLA_NOPE), F32)
    zeros_n = jnp.zeros((L, MLA_NOPE), F32)
    zeros_p = jnp.zeros((L, MLA_QK_PAD - MLA_NOPE - MLA_ROPE), F32)
    return (jnp.tile(c64, (1, N_HEADS)), jnp.tile(s64, (1, N_HEADS)),
            jnp.tile(c32, (1, 2 * N_HEADS)), jnp.tile(s32, (1, 2 * N_HEADS)),
            jnp.concatenate([ones, c32, zeros_p], -1), jnp.concatenate([zeros_n, s32, zeros_p], -1))


def _identity_tables(C):
    one = jnp.ones((C, GROUP_WIDTH), F32)
    zero = jnp.zeros((C, GROUP_WIDTH), F32)
    pad = MLA_QK_PAD - MLA_NOPE - MLA_ROPE
    cosm = jnp.concatenate([jnp.ones((C, MLA_NOPE + MLA_ROPE), F32), jnp.zeros((C, pad), F32)], -1)
    return one, zero, one, zero, cosm, jnp.zeros((C, MLA_QK_PAD), F32)


def _fused_in_weight(w_in):
    d = w_in.shape[0]
    p = jnp.split(w_in, IN_CUTS, axis=1)
    z = lambda n: jnp.zeros((d, n), w_in.dtype)
    pad_r = MLA_QK_PAD - MLA_NOPE - MLA_ROPE
    parts = {"qa": p[0], "qa_r": _rot_cols(p[0], HEAD_DIM), "ka": p[1], "ka_r": _rot_cols(p[1], HEAD_DIM),
             "va": p[2], "mqr": p[3], "mkvr": p[4],
             "mkr": jnp.concatenate([z(MLA_NOPE), p[5], z(pad_r)], 1),
             "mkr_r": jnp.concatenate([z(MLA_NOPE), _rot_cols(p[5], MLA_ROPE), z(pad_r)], 1),
             "dq": p[6], "dq_r": _rot_cols(p[6], DIFF_QK), "dk": p[7], "dk_r": _rot_cols(p[7], DIFF_QK),
             "dv": p[8], "nq": p[9], "nk": p[10], "nv": p[11]}
    return jnp.concatenate([parts[n] for n, _ in _SEG_LAYOUT], axis=1).astype(BF16)


def _mla_up_weights(w_uq, w_ukv):
    rq = w_uq.shape[0]
    pad_r = MLA_QK_PAD - MLA_NOPE - MLA_ROPE
    wq = w_uq.reshape(rq, N_HEADS, MLA_NOPE + MLA_ROPE)
    zq = lambda n: jnp.zeros((rq, N_HEADS, n), w_uq.dtype)
    rope_rot = _rot_cols(wq[:, :, MLA_NOPE:].reshape(rq, -1), MLA_ROPE).reshape(rq, N_HEADS, MLA_ROPE)
    main = jnp.concatenate([wq, zq(pad_r)], -1).reshape(rq, -1)
    rot = jnp.concatenate([zq(MLA_NOPE), rope_rot, zq(pad_r)], -1).reshape(rq, -1)
    wuq_ext = jnp.concatenate([main, rot], 1).astype(BF16)
    rk = w_ukv.shape[0]
    wkv = w_ukv.reshape(rk, N_HEADS, MLA_NOPE + MLA_V)
    k_part = jnp.concatenate([wkv[:, :, :MLA_NOPE],
                              jnp.zeros((rk, N_HEADS, MLA_QK_PAD - MLA_NOPE), w_ukv.dtype)], -1)
    wukv_ext = jnp.concatenate([k_part.reshape(rk, -1), wkv[:, :, MLA_NOPE:].reshape(rk, -1)], 1).astype(BF16)
    return wuq_ext, wukv_ext


def kernel(x, c, ctx, c_ctx, w_mod, b_mod, w_in, attn_sink, mla_q_norm, w_uq, mla_kv_norm, w_ukv,
           lam_q1, lam_k1, lam_q2, lam_k2, diff_subln, na_rpb, w_out, ln1_g, ln1_b,
           w_group, b_group, w_router, b_router, w_gate, w_up, w_down, ln2_g, ln2_b):
    B, L, D = x.shape
    C = ctx.shape[1]
    depth = w_mod.shape[0]
    alpha = (2 * depth) ** 0.25
    assert D == D_MODEL and B + 1 <= 8 and L % (NA_Q_ROWS * GRID_W) == 0

    cvec = jnp.concatenate([c, c_ctx[None, :], jnp.zeros((8 - B - 1, D), F32)], axis=0)
    mod = _modulation(cvec, w_mod, b_mod)
    lat_tables = _rope_tables(L)
    ctx_tables = _identity_tables(C)
    row = lambda a: a.reshape(1, -1)

    xc = ctx
    for l in range(depth):
        need_ctx = l < depth - 1
        lam_init = 0.8 - 0.6 * math.exp(-0.3 * l)
        chunks = [mod[l, :, i * D:(i + 1) * D] for i in range(6)]
        sh1, sc1, g1, sh2, sc2, g2 = [m[:B, None, :] for m in chunks]
        sh1c, sc1c, g1c, sh2c, sc2c, g2c = [jnp.broadcast_to(m[B:B + 1, None, :], (B, 1, D)) for m in chunks]

        w_all = _fused_in_weight(w_in[l])
        wuq_ext, wukv_ext = _mla_up_weights(w_uq[l], w_ukv[l])
        proj = functools.partial(_input_projection, w_all=w_all, qng=row(mla_q_norm[l]), wuq=wuq_ext,
                                 kvng=row(mla_kv_norm[l]), wukv=wukv_ext)
        hd = proj(x, sc1, sh1, tables=lat_tables)
        hc = proj(xc, sc1c, sh1c, tables=ctx_tables)

        lam_vecs = jnp.stack([lam_q1[l], lam_k1[l], lam_q2[l], lam_k2[l]])
        subln_g = row(diff_subln[l])
        att = (_swa_attention(attn_sink[l], hd, hc), _mla_attention(hd, hc),
               _diff_attention(lam_vecs, subln_g, hd, hc, lam_init), _na_attention(na_rpb[l], hd, hc))

        wo = w_out[l].astype(BF16)
        pad = ROUTER_LANES - N_EXPERTS - N_GROUPS
        wr = jnp.stack(_split_bf16(jnp.concatenate([w_router[l], w_group[l], jnp.zeros((D, pad), F32)], axis=1)))
        br = row(jnp.concatenate([b_router[l], b_group[l], jnp.zeros((pad,), F32)]))
        post = functools.partial(_output_projection, wo=wo, ln_g=row(ln1_g[l]), ln_b=row(ln1_b[l]),
                                 wr=wr, br=br, alpha=alpha)
        ffn = functools.partial(_moe, wg=w_gate, wu=w_up, wd=w_down, layer=l,
                                ln_g=row(ln2_g[l]), ln_b=row(ln2_b[l]), alpha=alpha)

        x1, rec = post(x, att, g1=g1, sc2=sc2, sh2=sh2)
        x = ffn(rec, x1, g2=g2, sort=True)
        if need_ctx:
            att_c = _ctx_attention(attn_sink[l], lam_vecs, subln_g, hc, lam_init)
            xc1, rec_c = post(xc, att_c, g1=g1c, sc2=sc2c, sh2=sh2c)
            xc = ffn(rec_c, xc1, g2=g2c, sort=False)
    return x
```

```python
import functools
import math

import jax
import jax.numpy as jnp
import numpy as np
from jax import lax
from jax.experimental import pallas as pl
from jax.experimental.pallas import tpu as pltpu

F32 = jnp.float32
BF16 = jnp.bfloat16

D_MODEL = 1024
GRID_W = 64
HEAD_DIM = 64
N_HEADS = 4
N_MIXERS = 4
GROUP_WIDTH = N_HEADS * HEAD_DIM
SWA_KV_HEADS = 2
SWA_WINDOW = 128
SWA_BLOCK = 128
MLA_Q_RANK = 256
MLA_KV_RANK = 128
MLA_NOPE = 64
MLA_ROPE = 32
MLA_V = 64
MLA_QK_PAD = 128
V_EXT = 128
DIFF_QK = 32
DIFF_V = 64
NA_KH = 8
NA_KW = 16
NA_Q_ROWS = 4
NA_K_ROWS = NA_Q_ROWS + NA_KH
N_GROUPS = 4
EXPERTS_PER_GROUP = 8
N_EXPERTS = N_GROUPS * EXPERTS_PER_GROUP
EXPERT_HIDDEN = 256
ROUTER_LANES = 128
GROUP_LANE = N_EXPERTS
ROPE_BASE = 10000.0
NORM_EPS = 1e-5
NEG_INF = -1e30
LOG2E = math.log2(math.e)
SWA_SCALE = HEAD_DIM ** -0.5 * LOG2E
MLA_SCALE = (MLA_NOPE + MLA_ROPE) ** -0.5 * LOG2E
DIFF_SCALE = DIFF_QK ** -0.5 * LOG2E
NA_SCALE = HEAD_DIM ** -0.5 * LOG2E
IN_SPLITS = (GROUP_WIDTH, SWA_KV_HEADS * HEAD_DIM, SWA_KV_HEADS * HEAD_DIM,
             MLA_Q_RANK, MLA_KV_RANK, MLA_ROPE,
             N_HEADS * 2 * DIFF_QK, N_HEADS * 2 * DIFF_QK, N_HEADS * DIFF_V,
             GROUP_WIDTH, GROUP_WIDTH, GROUP_WIDTH)
IN_CUTS = tuple(int(v) for v in np.cumsum(IN_SPLITS)[:-1])

_SEG_LAYOUT = (("qa", 256), ("qa_r", 256), ("ka", 128), ("ka_r", 128), ("va", 128),
               ("mqr", 256), ("mkvr", 128), ("mkr", 128), ("mkr_r", 128),
               ("dq", 256), ("dq_r", 256), ("dk", 256), ("dk_r", 256), ("dv", 256),
               ("nq", 256), ("nk", 256), ("nv", 256))
_SEG = {}
_off = 0
for _name, _w in _SEG_LAYOUT:
    _SEG[_name] = (_off, _off + _w)
    _off += _w
W_ALL_COLS = _off

VMEM_LIMIT_BYTES = 56 * 1024 * 1024


def _cparams(sem):
    return pltpu.CompilerParams(dimension_semantics=sem, vmem_limit_bytes=VMEM_LIMIT_BYTES)


def _dot(a, b):
    return jnp.dot(a, b, preferred_element_type=F32)


def _dot_nt(a, b):
    return lax.dot_general(a, b, (((1,), (1,)), ((), ())), preferred_element_type=F32)


def _with_ones_lane(v):
    lane = lax.broadcasted_iota(jnp.int32, (v.shape[0], V_EXT - v.shape[1]), 1)
    return jnp.concatenate([v, (lane == 0).astype(v.dtype)], axis=-1)


def _rms(x):
    return x * lax.rsqrt(jnp.mean(x * x, axis=-1, keepdims=True) + NORM_EPS)


def _layer_norm(z, g, b):
    mu = jnp.mean(z, axis=-1, keepdims=True)
    zc = z - mu
    var = jnp.mean(zc * zc, axis=-1, keepdims=True)
    return zc * lax.rsqrt(var + NORM_EPS) * g + b


def _mod_kernel(c_ref, w_ref, b_ref, o_ref):
    c = c_ref[...]
    act = c * jax.nn.sigmoid(c)
    o_ref[0] = jnp.dot(act, w_ref[0], preferred_element_type=F32,
                       precision=lax.Precision.HIGHEST) + b_ref[0]


def _modulation(cvec, w_mod, b_mod):
    depth, d, n = w_mod.shape
    tn = 1024
    return pl.pallas_call(
        _mod_kernel,
        grid=(depth, n // tn),
        in_specs=[pl.BlockSpec((8, d), lambda l, j: (0, 0)),
                  pl.BlockSpec((1, d, tn), lambda l, j: (l, 0, j)),
                  pl.BlockSpec((1, 1, tn), lambda l, j: (l, 0, j))],
        out_specs=pl.BlockSpec((1, 8, tn), lambda l, j: (l, 0, j)),
        out_shape=jax.ShapeDtypeStruct((depth, 8, n), F32),
        compiler_params=_cparams(("arbitrary", "arbitrary")),
        name="modulation",
    )(cvec, w_mod, b_mod.reshape(depth, 1, n))


def _inproj_kernel(x_ref, sc_ref, sh_ref, w_ref, cos64_ref, sin64_ref, cos32_ref, sin32_ref,
                   cosm_ref, sinm_ref, qng_ref, wuq_ref, kvng_ref, wukv_ref,
                   qa_o, ka_o, va_o, mq_o, mk_o, mv_o, dq_o, dk_o, dv_o, nq_o, nk_o, nv_o):
    h = (x_ref[0] * (1.0 + sc_ref[0]) + sh_ref[0]).astype(BF16)

    def seg(name):
        a, b = _SEG[name]
        return _dot(h, w_ref[:, a:b])

    def split_heads(val, out_ref, n, width):
        for i in range(n):
            out_ref[0, i] = val[:, i * width:(i + 1) * width].astype(out_ref.dtype)

    cos64 = cos64_ref[...]
    sin64 = sin64_ref[...]
    cos32 = cos32_ref[...]
    sin32 = sin32_ref[...]
    cosm = cosm_ref[...]
    sinm = sinm_ref[...]

    qa = (seg("qa") * cos64 + seg("qa_r") * sin64) * SWA_SCALE
    split_heads(qa, qa_o, N_HEADS, HEAD_DIM)
    ka = seg("ka") * cos64[:, :128] + seg("ka_r") * sin64[:, :128]
    split_heads(ka, ka_o, SWA_KV_HEADS, HEAD_DIM)
    split_heads(seg("va"), va_o, SWA_KV_HEADS, HEAD_DIM)

    qn = (_rms(seg("mqr")) * qng_ref[...]).astype(BF16)
    uq = _dot(qn, wuq_ref[...])
    half = N_HEADS * MLA_QK_PAD
    for i in range(N_HEADS):
        a = i * MLA_QK_PAD
        mq = (uq[:, a:a + MLA_QK_PAD] * cosm + uq[:, half + a:half + a + MLA_QK_PAD] * sinm) * MLA_SCALE
        mq_o[0, i] = mq.astype(BF16)
    kvn = (_rms(seg("mkvr")) * kvng_ref[...]).astype(BF16)
    ukv = _dot(kvn, wukv_ref[...])
    k_rope = seg("mkr") * cosm + seg("mkr_r") * sinm
    for i in range(N_HEADS):
        a = i * MLA_QK_PAD
        mk_o[0, i] = (ukv[:, a:a + MLA_QK_PAD] + k_rope).astype(BF16)
        b = half + i * MLA_V
        mv_o[0, i] = _with_ones_lane(ukv[:, b:b + MLA_V]).astype(BF16)

    dq = (seg("dq") * cos32 + seg("dq_r") * sin32) * DIFF_SCALE
    split_heads(dq, dq_o, 2 * N_HEADS, DIFF_QK)
    dk = seg("dk") * cos32 + seg("dk_r") * sin32
    split_heads(dk, dk_o, 2 * N_HEADS, DIFF_QK)
    dv = seg("dv")
    for i in range(N_HEADS):
        dv_o[0, i] = _with_ones_lane(dv[:, i * DIFF_V:(i + 1) * DIFF_V]).astype(BF16)

    split_heads(seg("nq") * NA_SCALE, nq_o, N_HEADS, HEAD_DIM)
    split_heads(seg("nk"), nk_o, N_HEADS, HEAD_DIM)
    split_heads(seg("nv"), nv_o, N_HEADS, HEAD_DIM)


INPROJ_TM = 512

_HEAD_OUTS = (("qa", N_HEADS, HEAD_DIM), ("ka", SWA_KV_HEADS, HEAD_DIM), ("va", SWA_KV_HEADS, HEAD_DIM),
              ("mq", N_HEADS, MLA_QK_PAD), ("mk", N_HEADS, MLA_QK_PAD), ("mv", N_HEADS, V_EXT),
              ("dq", 2 * N_HEADS, DIFF_QK), ("dk", 2 * N_HEADS, DIFF_QK), ("dv", N_HEADS, V_EXT),
              ("nq", N_HEADS, HEAD_DIM), ("nk", N_HEADS, HEAD_DIM), ("nv", N_HEADS, HEAD_DIM))


def _input_projection(x, sc, sh, w_all, tables, qng, wuq, kvng, wukv):
    B, N, D = x.shape
    tm = min(INPROJ_TM, N)
    tok = lambda i, b: (b, i, 0)
    vec = lambda i, b: (b, 0, 0)
    tab = lambda i, b: (i, 0)
    const = lambda i, b: (0, 0)
    in_specs = [pl.BlockSpec((1, tm, D), tok),
                pl.BlockSpec((1, 1, D), vec), pl.BlockSpec((1, 1, D), vec),
                pl.BlockSpec(w_all.shape, const)]
    in_specs += [pl.BlockSpec((tm, t.shape[1]), tab) for t in tables]
    in_specs += [pl.BlockSpec(a.shape, const) for a in (qng, wuq, kvng, wukv)]
    out_specs = [pl.BlockSpec((1, n, tm, w), lambda i, b: (b, 0, i, 0)) for _, n, w in _HEAD_OUTS]
    out_shape = [jax.ShapeDtypeStruct((B, n, N, w), BF16) for _, n, w in _HEAD_OUTS]
    outs = pl.pallas_call(
        _inproj_kernel,
        grid=(N // tm, B),
        in_specs=in_specs, out_specs=out_specs, out_shape=out_shape,
        compiler_params=_cparams(("arbitrary", "arbitrary")),
        name="input_projection",
    )(x, sc, sh, w_all, *tables, qng, wuq, kvng, wukv)
    return {name: o for (name, _, _), o in zip(_HEAD_OUTS, outs)}


def _softmax_parts(scores, extra_logit=None):
    m = functools.reduce(jnp.maximum, [jnp.max(s, axis=-1, keepdims=True) for s in scores])
    if extra_logit is not None:
        m = jnp.maximum(m, extra_logit)
    ps = [jnp.exp2(s - m) for s in scores]
    denom = functools.reduce(jnp.add, [jnp.sum(p, axis=-1, keepdims=True) for p in ps])
    if extra_logit is not None:
        denom = denom + jnp.exp2(extra_logit - m)
    return ps, denom


def _flash_update(carry, q, k, v):
    m, acc = carry
    s = _dot_nt(q, k)
    m_new = jnp.maximum(m, jnp.max(s, axis=-1, keepdims=True))
    p = jnp.exp2(s - m_new)
    acc = jnp.exp2(m - m_new) * acc + _dot(p.astype(BF16), v)
    return m_new, acc


def _flash_start(qs, ctx_ks, ctx_vs):
    tq = qs[0].shape[0]
    init = (jnp.full((tq, 1), NEG_INF, F32), jnp.zeros((tq, V_EXT), F32))
    return tuple(_flash_update(init, q, k, v) for q, k, v in zip(qs, ctx_ks, ctx_vs))


def _flash_finish(carries, qs, k_ats, v_ats, n_chunks, dv, unroll):
    def body(i, carries):
        return tuple(_flash_update(cr, q, k_at(i), v_at(i))
                     for cr, q, k_at, v_at in zip(carries, qs, k_ats, v_ats))

    carries = lax.fori_loop(0, n_chunks, body, carries, unroll=min(unroll, n_chunks))
    return [acc[:, :dv] / acc[:, dv:dv + 1] for _, acc in carries]


def _lambda_value(lam_ref, lam_init):
    lv = lam_ref[...]
    return (jnp.exp(jnp.sum(lv[0:1] * lv[1:2], axis=-1, keepdims=True))
            - jnp.exp(jnp.sum(lv[2:3] * lv[3:4], axis=-1, keepdims=True)) + lam_init)


def _sub_ln(o, g_ref, lam_init):
    return _rms(o) * g_ref[...] * (1.0 - lam_init)


SWA_TQ = 2 * SWA_BLOCK
SWA_SPAN = SWA_TQ + 2 * SWA_WINDOW


def _swa_kernel(sink_ref, q_ref, k_ref, v_ref, kc_ref, vc_ref, o_ref):
    L = k_ref.shape[2]
    qb = pl.program_id(1)
    start = pl.multiple_of(jnp.clip(qb * SWA_TQ - SWA_WINDOW, 0, L - SWA_SPAN), SWA_BLOCK)
    group = N_HEADS // SWA_KV_HEADS
    rows = group * SWA_TQ
    n_keys = SWA_SPAN + kc_ref.shape[2]
    col = lax.broadcasted_iota(jnp.int32, (rows, n_keys), 1)
    row = lax.broadcasted_iota(jnp.int32, (rows, n_keys), 0)
    q_abs = qb * SWA_TQ + jnp.where(row >= SWA_TQ, row - SWA_TQ, row)
    allowed = (col >= SWA_SPAN) | (jnp.abs(start + col - q_abs) <= SWA_WINDOW)
    for hk in range(SWA_KV_HEADS):
        k_all = jnp.concatenate([k_ref[0, hk, pl.ds(start, SWA_SPAN), :], kc_ref[0, hk]], axis=0)
        v_all = jnp.concatenate([v_ref[0, hk, pl.ds(start, SWA_SPAN), :], vc_ref[0, hk]], axis=0)
        q = jnp.concatenate([q_ref[0, hk * group + g] for g in range(group)], axis=0)
        sink = jnp.concatenate([jnp.full((SWA_TQ, 1), sink_ref[hk * group + g], F32) for g in range(group)],
                               axis=0) * LOG2E
        (p,), denom = _softmax_parts([jnp.where(allowed, _dot_nt(q, k_all), NEG_INF)], sink)
        o = _dot(p.astype(BF16), v_all) / denom
        for g in range(group):
            h = hk * group + g
            o_ref[0, :, h * HEAD_DIM:(h + 1) * HEAD_DIM] = o[g * SWA_TQ:(g + 1) * SWA_TQ].astype(o_ref.dtype)


def _swa_attention(sink, hd, hc):
    B, _, L, _ = hd["qa"].shape
    C = hc["ka"].shape[2]
    assert L % SWA_TQ == 0 and L >= SWA_SPAN
    whole = lambda b, i: (b, 0, 0, 0)
    return pl.pallas_call(
        _swa_kernel,
        grid=(B, L // SWA_TQ),
        in_specs=[pl.BlockSpec(memory_space=pltpu.SMEM),
                  pl.BlockSpec((1, N_HEADS, SWA_TQ, HEAD_DIM), lambda b, i: (b, 0, i, 0)),
                  pl.BlockSpec((1, SWA_KV_HEADS, L, HEAD_DIM), whole),
                  pl.BlockSpec((1, SWA_KV_HEADS, L, HEAD_DIM), whole),
                  pl.BlockSpec((1, SWA_KV_HEADS, C, HEAD_DIM), whole),
                  pl.BlockSpec((1, SWA_KV_HEADS, C, HEAD_DIM), whole)],
        out_specs=pl.BlockSpec((1, SWA_TQ, GROUP_WIDTH), lambda b, i: (b, i, 0)),
        out_shape=jax.ShapeDtypeStruct((B, L, GROUP_WIDTH), BF16),
        compiler_params=_cparams(("arbitrary", "arbitrary")),
        name="swa_attention",
    )(sink, hd["qa"], hd["ka"], hd["va"], hc["ka"], hc["va"])


MLA_TQ = 1024
DIFF_TQ = 512
GLOBAL_TK = 2048
HEADS_PER_STEP = 2
FLASH_UNROLL = 4
MLA_UNROLL = FLASH_UNROLL


def _chunk_at(ref, lead, tk):
    return lambda i: ref[lead + (pl.ds(pl.multiple_of(i * tk, tk), tk), slice(None))]


def _mla_kernel(q_ref, kc_ref, vc_ref, k_ref, v_ref, o_ref, *, tk):
    n_chunks = k_ref.shape[2] // tk
    heads = range(HEADS_PER_STEP)
    qs = [q_ref[0, j] for j in heads]
    carries = _flash_start(qs, [kc_ref[0, j] for j in heads], [vc_ref[0, j] for j in heads])
    outs = _flash_finish(carries, qs, [_chunk_at(k_ref, (0, j), tk) for j in heads],
                         [_chunk_at(v_ref, (0, j), tk) for j in heads], n_chunks, MLA_V, MLA_UNROLL)
    o_ref[0] = jnp.concatenate(outs, axis=-1).astype(o_ref.dtype)


def _mla_attention(hd, hc):
    B, H, L, dk = hd["mq"].shape
    C = hc["mk"].shape[2]
    tq = min(MLA_TQ, L)
    tk = min(GLOBAL_TK, L)
    hp = HEADS_PER_STEP
    whole = lambda b, h, i: (b, h, 0, 0)
    return pl.pallas_call(
        functools.partial(_mla_kernel, tk=tk),
        grid=(B, H // hp, L // tq),
        in_specs=[pl.BlockSpec((1, hp, tq, dk), lambda b, h, i: (b, h, i, 0)),
                  pl.BlockSpec((1, hp, C, dk), whole),
                  pl.BlockSpec((1, hp, C, V_EXT), whole),
                  pl.BlockSpec((1, hp, L, dk), whole),
                  pl.BlockSpec((1, hp, L, V_EXT), whole)],
        out_specs=pl.BlockSpec((1, tq, hp * MLA_V), lambda b, h, i: (b, i, h)),
        out_shape=jax.ShapeDtypeStruct((B, L, H * MLA_V), BF16),
        compiler_params=_cparams(("arbitrary", "arbitrary", "arbitrary")),
        name="mla_attention",
    )(hd["mq"], hc["mk"], hc["mv"], hd["mk"], hd["mv"])


def _diff_kernel(lam_ref, g_ref, q_ref, kc_ref, vc_ref, k_ref, v_ref, o_ref, *, tk, lam_init):
    n_chunks = k_ref.shape[2] // tk
    lam = _lambda_value(lam_ref, lam_init)
    branches = [(2 * h, 2 * h + 1) for h in range(HEADS_PER_STEP)]
    qs = [[q_ref[0, j] for j in br] for br in branches]
    starts = [_flash_start(qs[h], [kc_ref[0, j] for j in br], [vc_ref[0, h]] * 2) for h, br in enumerate(branches)]
    heads = []
    for h, br in enumerate(branches):
        o1, o2 = _flash_finish(starts[h], qs[h], [_chunk_at(k_ref, (0, j), tk) for j in br],
                               [_chunk_at(v_ref, (0, h), tk)] * 2, n_chunks, DIFF_V, FLASH_UNROLL)
        heads.append(_sub_ln(o1 - lam * o2, g_ref, lam_init))
    o_ref[0] = jnp.concatenate(heads, axis=-1).astype(o_ref.dtype)


def _diff_attention(lam_vecs, subln_g, hd, hc, lam_init):
    B, _, L, dk = hd["dq"].shape
    C = hc["dk"].shape[2]
    tq = min(DIFF_TQ, L)
    tk = min(GLOBAL_TK, L)
    hp = HEADS_PER_STEP
    whole = lambda b, h, i: (b, h, 0, 0)
    const = lambda b, h, i: (0, 0)
    return pl.pallas_call(
        functools.partial(_diff_kernel, tk=tk, lam_init=lam_init),
        grid=(B, N_HEADS // hp, L // tq),
        in_specs=[pl.BlockSpec(lam_vecs.shape, const),
                  pl.BlockSpec(subln_g.shape, const),
                  pl.BlockSpec((1, 2 * hp, tq, dk), lambda b, h, i: (b, h, i, 0)),
                  pl.BlockSpec((1, 2 * hp, C, dk), whole),
                  pl.BlockSpec((1, hp, C, V_EXT), whole),
                  pl.BlockSpec((1, 2 * hp, L, dk), whole),
                  pl.BlockSpec((1, hp, L, V_EXT), whole)],
        out_specs=pl.BlockSpec((1, tq, hp * DIFF_V), lambda b, h, i: (b, i, h)),
        out_shape=jax.ShapeDtypeStruct((B, L, N_HEADS * DIFF_V), BF16),
        compiler_params=_cparams(("arbitrary", "arbitrary", "arbitrary")),
        name="diff_attention",
    )(lam_vecs, subln_g, hd["dq"], hc["dk"], hc["dv"], hd["dk"], hd["dv"])


def _na_plan(rows):
    kh = min(NA_KH, rows)
    n_row_off = 2 * NA_KH - 1
    col = np.arange(GRID_W)
    col_start = np.clip(col - NA_KW // 2, 0, GRID_W - NA_KW)
    col_ok = (col[None, :] >= col_start[:, None]) & (col[None, :] < col_start[:, None] + NA_KW)
    col_off = col[None, :] - col[:, None] + (NA_KW - 1)
    col_onehot = ((col_off[None] == np.arange(2 * NA_KW - 1)[:, None, None]) & col_ok[None]).astype(np.float32)
    patterns, starts, ids = {}, [], []
    for blk in range(rows // NA_Q_ROWS):
        r0 = blk * NA_Q_ROWS
        ks = int(np.clip(r0 - kh // 2, 0, rows - NA_K_ROWS))
        q_row = r0 + np.arange(NA_Q_ROWS)
        k_row = ks + np.arange(NA_K_ROWS)
        r_start = np.clip(q_row - kh // 2, 0, rows - kh)
        row_ok = (k_row[None, :] >= r_start[:, None]) & (k_row[None, :] < r_start[:, None] + kh)
        row_off = k_row[None, :] - q_row[:, None] + (NA_KH - 1)
        sel = np.where(row_ok, row_off, n_row_off)
        key = tuple(int(v) for v in sel.reshape(-1))
        if key not in patterns:
            patterns[key] = (len(patterns), sel)
        starts.append(ks)
        ids.append(patterns[key][0])
    row_sel = np.stack([p[1] for p in sorted(patterns.values(), key=lambda p: p[0])])
    return np.asarray(starts, np.int32), np.asarray(ids, np.int32), row_sel, col_onehot, col_ok


def _na_kernel(ks_ref, pid_ref, q_ref, k_ref, v_ref, kc_ref, vc_ref, bias_ref, o_ref):
    del pid_ref
    k_len = NA_K_ROWS * GRID_W
    start = pl.multiple_of(ks_ref[pl.program_id(1)] * GRID_W, GRID_W)
    for h in range(N_HEADS):
        q = q_ref[0, h]
        kw = k_ref[0, h, pl.ds(start, k_len), :]
        vw = v_ref[0, h, pl.ds(start, k_len), :]
        s_win = _dot_nt(q, kw) + bias_ref[0, h]
        s_ctx = _dot_nt(q, kc_ref[0, h])
        (p_win, p_ctx), denom = _softmax_parts([s_win, s_ctx])
        o = _dot(p_win.astype(BF16), vw) + _dot(p_ctx.astype(BF16), vc_ref[0, h])
        o_ref[0, :, h * HEAD_DIM:(h + 1) * HEAD_DIM] = (o / denom).astype(o_ref.dtype)


def _na_attention(rpb, hd, hc):
    B, H, L, _ = hd["nq"].shape
    C = hc["nk"].shape[2]
    rows = L // GRID_W
    starts, ids, row_sel, col_onehot, col_ok = _na_plan(rows)
    q_len = NA_Q_ROWS * GRID_W
    k_len = NA_K_ROWS * GRID_W
    slab = jnp.einsum("hrj,jqk->hrqk", rpb * LOG2E, col_onehot, precision=lax.Precision.HIGHEST)
    slab = jnp.where(col_ok[None, None], slab, NEG_INF)
    slab = jnp.concatenate([slab, jnp.full_like(slab[:, :1], NEG_INF)], axis=1)
    bias = jnp.take(slab, jnp.asarray(row_sel), axis=1)
    bias = jnp.transpose(bias, (1, 0, 2, 4, 3, 5)).reshape(row_sel.shape[0], H, q_len, k_len)
    whole = lambda b, i, ks, pid: (b, 0, 0, 0)
    grid_spec = pltpu.PrefetchScalarGridSpec(
        num_scalar_prefetch=2,
        grid=(B, rows // NA_Q_ROWS),
        in_specs=[pl.BlockSpec((1, H, q_len, HEAD_DIM), lambda b, i, ks, pid: (b, 0, i, 0)),
                  pl.BlockSpec((1, H, L, HEAD_DIM), whole),
                  pl.BlockSpec((1, H, L, HEAD_DIM), whole),
                  pl.BlockSpec((1, H, C, HEAD_DIM), whole),
                  pl.BlockSpec((1, H, C, HEAD_DIM), whole),
                  pl.BlockSpec((1, H, q_len, k_len), lambda b, i, ks, pid: (pid[i], 0, 0, 0))],
        out_specs=pl.BlockSpec((1, q_len, GROUP_WIDTH), lambda b, i, ks, pid: (b, i, 0)))
    return pl.pallas_call(
        _na_kernel,
        grid_spec=grid_spec,
        out_shape=jax.ShapeDtypeStruct((B, L, GROUP_WIDTH), BF16),
        compiler_params=_cparams(("arbitrary", "arbitrary")),
        name="neighborhood_attention",
    )(jnp.asarray(starts), jnp.asarray(ids), hd["nq"], hd["nk"], hd["nv"], hc["nk"], hc["nv"], bias)


def _ctx_attn_kernel(sink_ref, lam_ref, g_ref, qa_ref, ka_ref, va_ref, mq_ref, mk_ref, mv_ref,
                     dq_ref, dk_ref, dv_ref, nq_ref, nk_ref, nv_ref,
                     ya_ref, yb_ref, yc_ref, yd_ref, *, lam_init):
    def attend(q, k, v, extra=None):
        (p,), denom = _softmax_parts([_dot_nt(q, k)], extra)
        return _dot(p.astype(BF16), v) / denom

    group = N_HEADS // SWA_KV_HEADS
    lam = _lambda_value(lam_ref, lam_init)
    for h in range(N_HEADS):
        lanes = slice(h * HEAD_DIM, (h + 1) * HEAD_DIM)
        ya_ref[0, :, lanes] = attend(qa_ref[0, h], ka_ref[0, h // group], va_ref[0, h // group],
                                     sink_ref[h] * LOG2E).astype(ya_ref.dtype)
        mv = mv_ref[0, h][:, :MLA_V]
        dv = dv_ref[0, h][:, :DIFF_V]
        yb_ref[0, :, lanes] = attend(mq_ref[0, h], mk_ref[0, h], mv).astype(yb_ref.dtype)
        o = (attend(dq_ref[0, 2 * h], dk_ref[0, 2 * h], dv)
             - lam * attend(dq_ref[0, 2 * h + 1], dk_ref[0, 2 * h + 1], dv))
        yc_ref[0, :, lanes] = _sub_ln(o, g_ref, lam_init).astype(yc_ref.dtype)
        yd_ref[0, :, lanes] = attend(nq_ref[0, h], nk_ref[0, h], nv_ref[0, h]).astype(yd_ref.dtype)


def _ctx_attention(sink, lam_vecs, subln_g, hc, lam_init):
    names = ("qa", "ka", "va", "mq", "mk", "mv", "dq", "dk", "dv", "nq", "nk", "nv")
    B, _, C, _ = hc["qa"].shape
    whole4 = lambda b: (b, 0, 0, 0)
    const = lambda b: (0, 0)
    in_specs = [pl.BlockSpec(memory_space=pltpu.SMEM),
                pl.BlockSpec(lam_vecs.shape, const), pl.BlockSpec(subln_g.shape, const)]
    in_specs += [pl.BlockSpec((1,) + hc[n].shape[1:], whole4) for n in names]
    tok = pl.BlockSpec((1, C, GROUP_WIDTH), lambda b: (b, 0, 0))
    return pl.pallas_call(
        functools.partial(_ctx_attn_kernel, lam_init=lam_init),
        grid=(B,),
        in_specs=in_specs,
        out_specs=[tok] * N_MIXERS,
        out_shape=[jax.ShapeDtypeStruct((B, C, GROUP_WIDTH), BF16)] * N_MIXERS,
        compiler_params=_cparams(("arbitrary",)),
        name="context_attention",
    )(sink, lam_vecs, subln_g, *[hc[n] for n in names])


def _route(r):
    lane = lax.broadcasted_iota(jnp.int32, r.shape, 1)
    lane_f = lane.astype(F32)
    big = float(ROUTER_LANES)
    is_grp = (lane >= N_EXPERTS) & (lane < N_EXPERTS + N_GROUPS)
    g_log = jnp.where(is_grp, r, NEG_INF)
    g_max = jnp.max(g_log, axis=-1, keepdims=True)
    g_val = 1.0 / jnp.sum(jnp.exp(g_log - g_max), axis=-1, keepdims=True)
    g_idx = jnp.min(jnp.where(g_log == g_max, lane_f, big), axis=-1, keepdims=True) - float(N_EXPERTS)
    lane_grp = lax.shift_right_logical(lane, int(math.log2(EXPERTS_PER_GROUP))).astype(F32)
    in_grp = (lane < N_EXPERTS) & (lane_grp == g_idx)
    e_log = jnp.where(in_grp, r, NEG_INF)
    e_max = jnp.max(e_log, axis=-1, keepdims=True)
    i1 = jnp.min(jnp.where(e_log == e_max, lane_f, big), axis=-1, keepdims=True)
    e_rest = jnp.where(lane_f == i1, NEG_INF, e_log)
    e_max2 = jnp.max(e_rest, axis=-1, keepdims=True)
    i2 = jnp.min(jnp.where(e_rest == e_max2, lane_f, big), axis=-1, keepdims=True)
    p2 = jnp.exp(e_max2 - e_max)
    w1 = 1.0 / (1.0 + p2)
    w2 = p2 / (1.0 + p2)
    gates = g_val * jnp.where(lane_f == i1, w1, jnp.where(lane_f == i2, w2, 0.0))
    return jnp.where(lane == GROUP_LANE, g_idx, gates)


def _split_bf16(v):
    hi = v.astype(BF16)
    return hi, (v - hi.astype(F32)).astype(BF16)


def _outproj_kernel(x_ref, a_ref, b_ref, c_ref, d_ref, wo_ref, g1_ref, sc2_ref, sh2_ref,
                    lng_ref, lnb_ref, wr_ref, br_ref, x1_ref, rec_ref, *, alpha):
    y = _dot(a_ref[0], wo_ref[0:GROUP_WIDTH, :])
    for i, m_ref in enumerate((b_ref, c_ref, d_ref), start=1):
        y += _dot(m_ref[0], wo_ref[i * GROUP_WIDTH:(i + 1) * GROUP_WIDTH, :])
    x1 = _layer_norm(alpha * x_ref[0] + g1_ref[0] * y, lng_ref[...], lnb_ref[...])
    x1_ref[0] = x1
    h2 = x1 * (1.0 + sc2_ref[0]) + sh2_ref[0]
    rec_ref[0, :, :D_MODEL] = h2
    h_hi, h_lo = _split_bf16(h2)
    r = (_dot(h_hi, wr_ref[0]) + _dot(h_lo, wr_ref[0]) + _dot(h_hi, wr_ref[1])) + br_ref[...]
    rec_ref[0, :, D_MODEL:] = _route(r)


def _output_projection(x, att, wo, g1, sc2, sh2, ln_g, ln_b, wr, br, alpha):
    B, N, D = x.shape
    tm = min(512, N)
    tok = lambda w: pl.BlockSpec((1, tm, w), lambda b, i: (b, i, 0))
    vec = pl.BlockSpec((1, 1, D), lambda b, i: (b, 0, 0))
    const = lambda a: pl.BlockSpec(a.shape, lambda b, i: (0,) * a.ndim)
    return pl.pallas_call(
        functools.partial(_outproj_kernel, alpha=alpha),
        grid=(B, N // tm),
        in_specs=[tok(D)] + [tok(GROUP_WIDTH)] * N_MIXERS + [const(wo), vec, vec, vec,
                  const(ln_g), const(ln_b), const(wr), const(br)],
        out_specs=[tok(D), tok(REC_W)],
        out_shape=[jax.ShapeDtypeStruct((B, N, D), F32), jax.ShapeDtypeStruct((B, N, REC_W), F32)],
        compiler_params=_cparams(("arbitrary", "arbitrary")),
        name="output_projection",
    )(x, att[0], att[1], att[2], att[3], wo, g1, sc2, sh2, ln_g, ln_b, wr, br)


MOE_TM = 1024
MOE_ROWS = 1024
MOE_STEPS = 4
MOE_EXPERTS_PER_STEP = EXPERTS_PER_GROUP // MOE_STEPS
GATHER_ROWS = 1024
GATHER_UNROLL = 8
REC_W = D_MODEL + ROUTER_LANES


def _issue_row_gather(idx_ref, src_ref, dst_ref, sem):
    n = dst_ref.shape[0]

    def row_copy(j):
        return pltpu.make_async_copy(src_ref.at[pl.ds(idx_ref[0, 0, j], 1), :], dst_ref.at[pl.ds(j, 1), :], sem)

    def issue(j, carry):
        row_copy(j).start()
        return carry

    lax.fori_loop(0, n, issue, 0, unroll=GATHER_UNROLL)
    pltpu.make_async_copy(src_ref.at[pl.ds(0, n), :], dst_ref, sem).wait()


def _gather_rows_kernel(idx_ref, src_ref, o_ref, sem):
    _issue_row_gather(idx_ref, src_ref, o_ref, sem)


def _gather_rows(src, idx):
    T, W = src.shape
    tg = min(GATHER_ROWS, T)
    return pl.pallas_call(
        _gather_rows_kernel,
        grid=(T // tg,),
        in_specs=[pl.BlockSpec((1, 1, tg), lambda i: (i, 0, 0), memory_space=pltpu.SMEM),
                  pl.BlockSpec(memory_space=pl.ANY)],
        out_specs=pl.BlockSpec((tg, W), lambda i: (i, 0)),
        out_shape=jax.ShapeDtypeStruct((T, W), src.dtype),
        scratch_shapes=[pltpu.SemaphoreType.DMA(())],
        compiler_params=_cparams(("arbitrary",)),
        name="gather_rows",
    )(idx.reshape(T // tg, 1, tg), src)


def _moe_ffn_kernel(tile_ref, group_ref, first_ref, valid_ref, rec_ref, wg_ref, wu_ref, wd_ref, f_ref):
    del tile_ref
    w = pl.program_id(0)
    step = pl.program_id(1)
    tm = f_ref.shape[0]

    @pl.when(valid_ref[w] != 0)
    def _():
        first = group_ref[w] * EXPERTS_PER_GROUP + step * MOE_EXPERTS_PER_STEP
        rb = min(MOE_ROWS, tm)
        lane = lax.broadcasted_iota(jnp.int32, (rb, ROUTER_LANES), 1)
        experts = range(MOE_EXPERTS_PER_STEP)
        wgs = [wg_ref[0, 0, e].astype(BF16) for e in experts]
        wus = [wu_ref[0, 0, e].astype(BF16) for e in experts]
        wds = [wd_ref[0, 0, e].astype(BF16) for e in experts]
        for r0 in range(0, tm, rb):
            rows = slice(r0, r0 + rb)
            h = rec_ref[rows, :D_MODEL].astype(BF16)
            gates = rec_ref[rows, D_MODEL:]
            y = None
            for e in experts:
                pre = _dot(h, wgs[e])
                hid = pre * jax.nn.sigmoid(pre) * _dot(h, wus[e])
                col = jnp.sum(jnp.where(lane == first + e, gates, 0.0), axis=-1, keepdims=True)
                part = _dot((hid * col).astype(BF16), wds[e])
                y = part if y is None else y + part
            starts_tile = (first_ref[w] != 0) & (step == 0)

            @pl.when(starts_tile)
            def _():
                f_ref[rows, :] = y

            @pl.when(jnp.logical_not(starts_tile))
            def _():
                f_ref[rows, :] += y


def _moe_ffn(rec, items, wg, wu, wd, layer):
    T = rec.shape[0]
    tm = min(MOE_TM, T)
    tile, group, first, valid = items
    n_items = tile.shape[0]
    wspec = lambda a: pl.BlockSpec((1, 1, MOE_EXPERTS_PER_STEP) + a.shape[3:],
                                   lambda w, s, t, g, f, v: (layer, g[w], s, 0, 0))
    grid_spec = pltpu.PrefetchScalarGridSpec(
        num_scalar_prefetch=4,
        grid=(n_items, MOE_STEPS),
        in_specs=[pl.BlockSpec((tm, REC_W), lambda w, s, t, g, f, v: (t[w], 0)), wspec(wg), wspec(wu), wspec(wd)],
        out_specs=pl.BlockSpec((tm, D_MODEL), lambda w, s, t, g, f, v: (t[w], 0)))
    return pl.pallas_call(
        _moe_ffn_kernel,
        grid_spec=grid_spec,
        out_shape=jax.ShapeDtypeStruct((T, D_MODEL), F32),
        compiler_params=_cparams(("arbitrary", "arbitrary")),
        name="moe_ffn",
    )(tile, group, first, valid, rec, wg, wu, wd)


def _moe_items_sorted(sorted_group, tm):
    T = sorted_group.shape[0]
    nt = T // tm
    g_lo = sorted_group[0::tm]
    g_hi = sorted_group[tm - 1::tm]
    per_tile = g_hi - g_lo + 1
    start = jnp.cumsum(per_tile) - per_tile
    n_items = nt + N_GROUPS - 1
    w = jnp.arange(n_items, dtype=jnp.int32)
    tile = jnp.clip(jnp.searchsorted(start, w, side="right") - 1, 0, nt - 1).astype(jnp.int32)
    valid = w < jnp.sum(per_tile)
    offset = w - start[tile]
    group = jnp.where(valid, g_lo[tile] + offset, g_hi[nt - 1]).astype(jnp.int32)
    return tile, group, (valid & (offset == 0)).astype(jnp.int32), valid.astype(jnp.int32)


def _moe_items_dense(T, tm):
    nt = T // tm
    w = np.arange(nt * N_GROUPS, dtype=np.int32)
    return (jnp.asarray(w // N_GROUPS), jnp.asarray(w % N_GROUPS), jnp.asarray((w % N_GROUPS == 0).astype(np.int32)),
            jnp.ones((nt * N_GROUPS,), jnp.int32))


def _ln2_kernel(idx_ref, f_ref, x1_ref, g2_ref, lng_ref, lnb_ref, o_ref, buf_ref, sem, *, alpha):
    _issue_row_gather(idx_ref, f_ref, buf_ref, sem)
    o_ref[0] = _layer_norm(alpha * x1_ref[0] + g2_ref[0] * buf_ref[...], lng_ref[...], lnb_ref[...])


def _ln2_unsort(f_rows, idx, x1, g2, ln_g, ln_b, alpha):
    B, N, D = x1.shape
    tg = min(GATHER_ROWS, N)
    nb = N // tg
    const = lambda a: pl.BlockSpec(a.shape, lambda b, i: (0, 0))
    return pl.pallas_call(
        functools.partial(_ln2_kernel, alpha=alpha),
        grid=(B, nb),
        in_specs=[pl.BlockSpec((1, 1, tg), lambda b, i: (b * nb + i, 0, 0), memory_space=pltpu.SMEM),
                  pl.BlockSpec(memory_space=pl.ANY),
                  pl.BlockSpec((1, tg, D), lambda b, i: (b, i, 0)),
                  pl.BlockSpec((1, 1, D), lambda b, i: (b, 0, 0)), const(ln_g), const(ln_b)],
        out_specs=pl.BlockSpec((1, tg, D), lambda b, i: (b, i, 0)),
        out_shape=jax.ShapeDtypeStruct((B, N, D), F32),
        scratch_shapes=[pltpu.VMEM((tg, D), F32), pltpu.SemaphoreType.DMA(())],
        compiler_params=_cparams(("arbitrary", "arbitrary")),
        name="ln2_unsort",
    )(idx.reshape(B * nb, 1, tg), f_rows, x1, g2, ln_g, ln_b)


def _moe(rec, x1, wg, wu, wd, layer, g2, ln_g, ln_b, alpha, sort):
    B, N, _ = rec.shape
    T = B * N
    tm = min(MOE_TM, T)
    flat = rec.reshape(T, REC_W)
    if sort:
        group = flat[:, D_MODEL + GROUP_LANE].astype(jnp.int32)
        order = jnp.argsort(group, stable=True).astype(jnp.int32)
        place = jnp.argsort(order).astype(jnp.int32)
        flat = _gather_rows(flat, order)
        items = _moe_items_sorted(group[order], tm)
    else:
        place = jnp.arange(T, dtype=jnp.int32)
        items = _moe_items_dense(T, tm)
    f_rows = _moe_ffn(flat, items, wg, wu, wd, layer)
    return _ln2_unsort(f_rows, place, x1, g2, ln_g, ln_b, alpha)


def _rot_cols(w, d):
    k, n = w.shape
    q = d // 4
    w4 = w.reshape(k, n // d, 4, q)
    return jnp.stack([-w4[:, :, 1], w4[:, :, 0], -w4[:, :, 3], w4[:, :, 2]], axis=2).reshape(k, n)


def _rope_tables(L):
    t = jnp.arange(L, dtype=jnp.int32)
    rows = (t // GRID_W).astype(F32)
    cols = (t % GRID_W).astype(F32)

    def cos_sin(d):
        q = d // 4
        inv = ROPE_BASE ** (-jnp.arange(q, dtype=F32) / q)
        ar = rows[:, None] * inv[None, :]
        ac = cols[:, None] * inv[None, :]
        return (jnp.concatenate([jnp.cos(ar), jnp.cos(ar), jnp.cos(ac), jnp.cos(ac)], -1),
                jnp.concatenate([jnp.sin(ar), jnp.sin(ar), jnp.sin(ac), jnp.sin(ac)], -1))

    c64, s64 = cos_sin(HEAD_DIM)
    c32, s32 = cos_sin(MLA_ROPE)
    ones = jnp.ones((L, MLA_NOPE), F32)
    zeros_n = jnp.zeros((L, MLA_NOPE), F32)
    zeros_p = jnp.zeros((L, MLA_QK_PAD - MLA_NOPE - MLA_ROPE), F32)
    return (jnp.tile(c64, (1, N_HEADS)), jnp.tile(s64, (1, N_HEADS)),
            jnp.tile(c32, (1, 2 * N_HEADS)), jnp.tile(s32, (1, 2 * N_HEADS)),
            jnp.concatenate([ones, c32, zeros_p], -1), jnp.concatenate([zeros_n, s32, zeros_p], -1))


def _identity_tables(C):
    one = jnp.ones((C, GROUP_WIDTH), F32)
    zero = jnp.zeros((C, GROUP_WIDTH), F32)
    pad = MLA_QK_PAD - MLA_NOPE - MLA_ROPE
    cosm = jnp.concatenate([jnp.ones((C, MLA_NOPE + MLA_ROPE), F32), jnp.zeros((C, pad), F32)], -1)
    return one, zero, one, zero, cosm, jnp.zeros((C, MLA_QK_PAD), F32)


def _fused_in_weight(w_in):
    d = w_in.shape[0]
    p = jnp.split(w_in, IN_CUTS, axis=1)
    z = lambda n: jnp.zeros((d, n), w_in.dtype)
    pad_r = MLA_QK_PAD - MLA_NOPE - MLA_ROPE
    parts = {"qa": p[0], "qa_r": _rot_cols(p[0], HEAD_DIM), "ka": p[1], "ka_r": _rot_cols(p[1], HEAD_DIM),
             "va": p[2], "mqr": p[3], "mkvr": p[4],
             "mkr": jnp.concatenate([z(MLA_NOPE), p[5], z(pad_r)], 1),
             "mkr_r": jnp.concatenate([z(MLA_NOPE), _rot_cols(p[5], MLA_ROPE), z(pad_r)], 1),
             "dq": p[6], "dq_r": _rot_cols(p[6], DIFF_QK), "dk": p[7], "dk_r": _rot_cols(p[7], DIFF_QK),
             "dv": p[8], "nq": p[9], "nk": p[10], "nv": p[11]}
    return jnp.concatenate([parts[n] for n, _ in _SEG_LAYOUT], axis=1).astype(BF16)


def _mla_up_weights(w_uq, w_ukv):
    rq = w_uq.shape[0]
    pad_r = MLA_QK_PAD - MLA_NOPE - MLA_ROPE
    wq = w_uq.reshape(rq, N_HEADS, MLA_NOPE + MLA_ROPE)
    zq = lambda n: jnp.zeros((rq, N_HEADS, n), w_uq.dtype)
    rope_rot = _rot_cols(wq[:, :, MLA_NOPE:].reshape(rq, -1), MLA_ROPE).reshape(rq, N_HEADS, MLA_ROPE)
    main = jnp.concatenate([wq, zq(pad_r)], -1).reshape(rq, -1)
    rot = jnp.concatenate([zq(MLA_NOPE), rope_rot, zq(pad_r)], -1).reshape(rq, -1)
    wuq_ext = jnp.concatenate([main, rot], 1).astype(BF16)
    rk = w_ukv.shape[0]
    wkv = w_ukv.reshape(rk, N_HEADS, MLA_NOPE + MLA_V)
    k_part = jnp.concatenate([wkv[:, :, :MLA_NOPE],
                              jnp.zeros((rk, N_HEADS, MLA_QK_PAD - MLA_NOPE), w_ukv.dtype)], -1)
    wukv_ext = jnp.concatenate([k_part.reshape(rk, -1), wkv[:, :, MLA_NOPE:].reshape(rk, -1)], 1).astype(BF16)
    return wuq_ext, wukv_ext


def kernel(x, c, ctx, c_ctx, w_mod, b_mod, w_in, attn_sink, mla_q_norm, w_uq, mla_kv_norm, w_ukv,
           lam_q1, lam_k1, lam_q2, lam_k2, diff_subln, na_rpb, w_out, ln1_g, ln1_b,
           w_group, b_group, w_router, b_router, w_gate, w_up, w_down, ln2_g, ln2_b):
    B, L, D = x.shape
    C = ctx.shape[1]
    depth = w_mod.shape[0]
    alpha = (2 * depth) ** 0.25
    assert D == D_MODEL and B + 1 <= 8 and L % (NA_Q_ROWS * GRID_W) == 0

    cvec = jnp.concatenate([c, c_ctx[None, :], jnp.zeros((8 - B - 1, D), F32)], axis=0)
    mod = _modulation(cvec, w_mod, b_mod)
    lat_tables = _rope_tables(L)
    ctx_tables = _identity_tables(C)
    row = lambda a: a.reshape(1, -1)

    xc = ctx
    for l in range(depth):
        need_ctx = l < depth - 1
        lam_init = 0.8 - 0.6 * math.exp(-0.3 * l)
        chunks = [mod[l, :, i * D:(i + 1) * D] for i in range(6)]
        sh1, sc1, g1, sh2, sc2, g2 = [m[:B, None, :] for m in chunks]
        sh1c, sc1c, g1c, sh2c, sc2c, g2c = [jnp.broadcast_to(m[B:B + 1, None, :], (B, 1, D)) for m in chunks]

        w_all = _fused_in_weight(w_in[l])
        wuq_ext, wukv_ext = _mla_up_weights(w_uq[l], w_ukv[l])
        proj = functools.partial(_input_projection, w_all=w_all, qng=row(mla_q_norm[l]), wuq=wuq_ext,
                                 kvng=row(mla_kv_norm[l]), wukv=wukv_ext)
        hd = proj(x, sc1, sh1, tables=lat_tables)
        hc = proj(xc, sc1c, sh1c, tables=ctx_tables)

        lam_vecs = jnp.stack([lam_q1[l], lam_k1[l], lam_q2[l], lam_k2[l]])
        subln_g = row(diff_subln[l])
        att = (_swa_attention(attn_sink[l], hd, hc), _mla_attention(hd, hc),
               _diff_attention(lam_vecs, subln_g, hd, hc, lam_init), _na_attention(na_rpb[l], hd, hc))

        wo = w_out[l].astype(BF16)
        pad = ROUTER_LANES - N_EXPERTS - N_GROUPS
        wr = jnp.stack(_split_bf16(jnp.concatenate([w_router[l], w_group[l], jnp.zeros((D, pad), F32)], axis=1)))
        br = row(jnp.concatenate([b_router[l], b_group[l], jnp.zeros((pad,), F32)]))
        post = functools.partial(_output_projection, wo=wo, ln_g=row(ln1_g[l]), ln_b=row(ln1_b[l]),
                                 wr=wr, br=br, alpha=alpha)
        ffn = functools.partial(_moe, wg=w_gate, wu=w_up, wd=w_down, layer=l,
                                ln_g=row(ln2_g[l]), ln_b=row(ln2_b[l]), alpha=alpha)

        x1, rec = post(x, att, g1=g1, sc2=sc2, sh2=sh2)
        x = ffn(rec, x1, g2=g2, sort=True)
        if need_ctx:
            att_c = _ctx_attention(attn_sink[l], lam_vecs, subln_g, hc, lam_init)
            xc1, rec_c = post(xc, att_c, g1=g1c, sc2=sc2c, sh2=sh2c)
            xc = ffn(rec_c, xc1, g2=g2c, sort=False)
    return x
```

```python
import functools
import math

import jax
import jax.numpy as jnp
import numpy as np
from jax import lax
from jax.experimental import pallas as pl
from jax.experimental.pallas import tpu as pltpu

F32 = jnp.float32
BF16 = jnp.bfloat16

D_MODEL = 1024
GRID_W = 64
HEAD_DIM = 64
N_HEADS = 4
N_MIXERS = 4
GROUP_WIDTH = N_HEADS * HEAD_DIM
SWA_KV_HEADS = 2
SWA_WINDOW = 128
SWA_BLOCK = 128
MLA_Q_RANK = 256
MLA_KV_RANK = 128
MLA_NOPE = 64
MLA_ROPE = 32
MLA_V = 64
MLA_QK_PAD = 128
V_EXT = 128
DIFF_QK = 32
DIFF_V = 64
NA_KH = 8
NA_KW = 16
NA_Q_ROWS = 4
NA_K_ROWS = NA_Q_ROWS + NA_KH
N_GROUPS = 4
EXPERTS_PER_GROUP = 8
N_EXPERTS = N_GROUPS * EXPERTS_PER_GROUP
EXPERT_HIDDEN = 256
ROUTER_LANES = 128
GROUP_LANE = N_EXPERTS
ROPE_BASE = 10000.0
NORM_EPS = 1e-5
NEG_INF = -1e30
LOG2E = math.log2(math.e)
SWA_SCALE = HEAD_DIM ** -0.5 * LOG2E
MLA_SCALE = (MLA_NOPE + MLA_ROPE) ** -0.5 * LOG2E
DIFF_SCALE = DIFF_QK ** -0.5 * LOG2E
NA_SCALE = HEAD_DIM ** -0.5 * LOG2E
IN_SPLITS = (GROUP_WIDTH, SWA_KV_HEADS * HEAD_DIM, SWA_KV_HEADS * HEAD_DIM,
             MLA_Q_RANK, MLA_KV_RANK, MLA_ROPE,
             N_HEADS * 2 * DIFF_QK, N_HEADS * 2 * DIFF_QK, N_HEADS * DIFF_V,
             GROUP_WIDTH, GROUP_WIDTH, GROUP_WIDTH)
IN_CUTS = tuple(int(v) for v in np.cumsum(IN_SPLITS)[:-1])

_SEG_LAYOUT = (("qa", 256), ("qa_r", 256), ("ka", 128), ("ka_r", 128), ("va", 128),
               ("mqr", 256), ("mkvr", 128), ("mkr", 128), ("mkr_r", 128),
               ("dq", 256), ("dq_r", 256), ("dk", 256), ("dk_r", 256), ("dv", 256),
               ("nq", 256), ("nk", 256), ("nv", 256))
_SEG = {}
_off = 0
for _name, _w in _SEG_LAYOUT:
    _SEG[_name] = (_off, _off + _w)
    _off += _w
W_ALL_COLS = _off

VMEM_LIMIT_BYTES = 56 * 1024 * 1024


def _cparams(sem):
    return pltpu.CompilerParams(dimension_semantics=sem, vmem_limit_bytes=VMEM_LIMIT_BYTES)


def _dot(a, b):
    return jnp.dot(a, b, preferred_element_type=F32)


def _dot_nt(a, b):
    return lax.dot_general(a, b, (((1,), (1,)), ((), ())), preferred_element_type=F32)


def _with_ones_lane(v):
    lane = lax.broadcasted_iota(jnp.int32, (v.shape[0], V_EXT - v.shape[1]), 1)
    return jnp.concatenate([v, (lane == 0).astype(v.dtype)], axis=-1)


def _rms(x):
    return x * lax.rsqrt(jnp.mean(x * x, axis=-1, keepdims=True) + NORM_EPS)


def _layer_norm(z, g, b):
    mu = jnp.mean(z, axis=-1, keepdims=True)
    zc = z - mu
    var = jnp.mean(zc * zc, axis=-1, keepdims=True)
    return zc * lax.rsqrt(var + NORM_EPS) * g + b


def _mod_kernel(c_ref, w_ref, b_ref, o_ref):
    c = c_ref[...]
    act = c * jax.nn.sigmoid(c)
    o_ref[0] = jnp.dot(act, w_ref[0], preferred_element_type=F32,
                       precision=lax.Precision.HIGHEST) + b_ref[0]


def _modulation(cvec, w_mod, b_mod):
    depth, d, n = w_mod.shape
    tn = 1024
    return pl.pallas_call(
        _mod_kernel,
        grid=(depth, n // tn),
        in_specs=[pl.BlockSpec((8, d), lambda l, j: (0, 0)),
                  pl.BlockSpec((1, d, tn), lambda l, j: (l, 0, j)),
                  pl.BlockSpec((1, 1, tn), lambda l, j: (l, 0, j))],
        out_specs=pl.BlockSpec((1, 8, tn), lambda l, j: (l, 0, j)),
        out_shape=jax.ShapeDtypeStruct((depth, 8, n), F32),
        compiler_params=_cparams(("arbitrary", "arbitrary")),
        name="modulation",
    )(cvec, w_mod, b_mod.reshape(depth, 1, n))


def _inproj_kernel(x_ref, sc_ref, sh_ref, w_ref, cos64_ref, sin64_ref, cos32_ref, sin32_ref,
                   cosm_ref, sinm_ref, qng_ref, wuq_ref, kvng_ref, wukv_ref,
                   qa_o, ka_o, va_o, mq_o, mk_o, mv_o, dq_o, dk_o, dv_o, nq_o, nk_o, nv_o):
    h = (x_ref[0] * (1.0 + sc_ref[0]) + sh_ref[0]).astype(BF16)

    def seg(name):
        a, b = _SEG[name]
        return _dot(h, w_ref[:, a:b])

    def split_heads(val, out_ref, n, width):
        for i in range(n):
            out_ref[0, i] = val[:, i * width:(i + 1) * width].astype(out_ref.dtype)

    cos64 = cos64_ref[...]
    sin64 = sin64_ref[...]
    cos32 = cos32_ref[...]
    sin32 = sin32_ref[...]
    cosm = cosm_ref[...]
    sinm = sinm_ref[...]

    qa = (seg("qa") * cos64 + seg("qa_r") * sin64) * SWA_SCALE
    split_heads(qa, qa_o, N_HEADS, HEAD_DIM)
    ka = seg("ka") * cos64[:, :128] + seg("ka_r") * sin64[:, :128]
    split_heads(ka, ka_o, SWA_KV_HEADS, HEAD_DIM)
    split_heads(seg("va"), va_o, SWA_KV_HEADS, HEAD_DIM)

    qn = (_rms(seg("mqr")) * qng_ref[...]).astype(BF16)
    uq = _dot(qn, wuq_ref[...])
    half = N_HEADS * MLA_QK_PAD
    for i in range(N_HEADS):
        a = i * MLA_QK_PAD
        mq = (uq[:, a:a + MLA_QK_PAD] * cosm + uq[:, half + a:half + a + MLA_QK_PAD] * sinm) * MLA_SCALE
        mq_o[0, i] = mq.astype(BF16)
    kvn = (_rms(seg("mkvr")) * kvng_ref[...]).astype(BF16)
    ukv = _dot(kvn, wukv_ref[...])
    k_rope = seg("mkr") * cosm + seg("mkr_r") * sinm
    for i in range(N_HEADS):
        a = i * MLA_QK_PAD
        mk_o[0, i] = (ukv[:, a:a + MLA_QK_PAD] + k_rope).astype(BF16)
        b = half + i * MLA_V
        mv_o[0, i] = _with_ones_lane(ukv[:, b:b + MLA_V]).astype(BF16)

    dq = (seg("dq") * cos32 + seg("dq_r") * sin32) * DIFF_SCALE
    split_heads(dq, dq_o, 2 * N_HEADS, DIFF_QK)
    dk = seg("dk") * cos32 + seg("dk_r") * sin32
    split_heads(dk, dk_o, 2 * N_HEADS, DIFF_QK)
    dv = seg("dv")
    for i in range(N_HEADS):
        dv_o[0, i] = _with_ones_lane(dv[:, i * DIFF_V:(i + 1) * DIFF_V]).astype(BF16)

    split_heads(seg("nq") * NA_SCALE, nq_o, N_HEADS, HEAD_DIM)
    split_heads(seg("nk"), nk_o, N_HEADS, HEAD_DIM)
    split_heads(seg("nv"), nv_o, N_HEADS, HEAD_DIM)


INPROJ_TM = 512

_HEAD_OUTS = (("qa", N_HEADS, HEAD_DIM), ("ka", SWA_KV_HEADS, HEAD_DIM), ("va", SWA_KV_HEADS, HEAD_DIM),
              ("mq", N_HEADS, MLA_QK_PAD), ("mk", N_HEADS, MLA_QK_PAD), ("mv", N_HEADS, V_EXT),
              ("dq", 2 * N_HEADS, DIFF_QK), ("dk", 2 * N_HEADS, DIFF_QK), ("dv", N_HEADS, V_EXT),
              ("nq", N_HEADS, HEAD_DIM), ("nk", N_HEADS, HEAD_DIM), ("nv", N_HEADS, HEAD_DIM))


def _input_projection(x, sc, sh, w_all, tables, qng, wuq, kvng, wukv):
    B, N, D = x.shape
    tm = min(INPROJ_TM, N)
    tok = lambda i, b: (b, i, 0)
    vec = lambda i, b: (b, 0, 0)
    tab = lambda i, b: (i, 0)
    const = lambda i, b: (0, 0)
    in_specs = [pl.BlockSpec((1, tm, D), tok),
                pl.BlockSpec((1, 1, D), vec), pl.BlockSpec((1, 1, D), vec),
                pl.BlockSpec(w_all.shape, const)]
    in_specs += [pl.BlockSpec((tm, t.shape[1]), tab) for t in tables]
    in_specs += [pl.BlockSpec(a.shape, const) for a in (qng, wuq, kvng, wukv)]
    out_specs = [pl.BlockSpec((1, n, tm, w), lambda i, b: (b, 0, i, 0)) for _, n, w in _HEAD_OUTS]
    out_shape = [jax.ShapeDtypeStruct((B, n, N, w), BF16) for _, n, w in _HEAD_OUTS]
    outs = pl.pallas_call(
        _inproj_kernel,
        grid=(N // tm, B),
        in_specs=in_specs, out_specs=out_specs, out_shape=out_shape,
        compiler_params=_cparams(("arbitrary", "arbitrary")),
        name="input_projection",
    )(x, sc, sh, w_all, *tables, qng, wuq, kvng, wukv)
    return {name: o for (name, _, _), o in zip(_HEAD_OUTS, outs)}


def _softmax_parts(scores, extra_logit=None):
    m = functools.reduce(jnp.maximum, [jnp.max(s, axis=-1, keepdims=True) for s in scores])
    if extra_logit is not None:
        m = jnp.maximum(m, extra_logit)
    ps = [jnp.exp2(s - m) for s in scores]
    denom = functools.reduce(jnp.add, [jnp.sum(p, axis=-1, keepdims=True) for p in ps])
    if extra_logit is not None:
        denom = denom + jnp.exp2(extra_logit - m)
    return ps, denom


def _flash_update(carry, q, k, v):
    m, acc = carry
    s = _dot_nt(q, k)
    m_new = jnp.maximum(m, jnp.max(s, axis=-1, keepdims=True))
    p = jnp.exp2(s - m_new)
    acc = jnp.exp2(m - m_new) * acc + _dot(p.astype(BF16), v)
    return m_new, acc


def _flash_start(qs, ctx_ks, ctx_vs):
    tq = qs[0].shape[0]
    init = (jnp.full((tq, 1), NEG_INF, F32), jnp.zeros((tq, V_EXT), F32))
    return tuple(_flash_update(init, q, k, v) for q, k, v in zip(qs, ctx_ks, ctx_vs))


def _flash_finish(carries, qs, k_ats, v_ats, n_chunks, dv, unroll):
    def body(i, carries):
        return tuple(_flash_update(cr, q, k_at(i), v_at(i))
                     for cr, q, k_at, v_at in zip(carries, qs, k_ats, v_ats))

    carries = lax.fori_loop(0, n_chunks, body, carries, unroll=min(unroll, n_chunks))
    return [acc[:, :dv] / acc[:, dv:dv + 1] for _, acc in carries]


def _lambda_value(lam_ref, lam_init):
    lv = lam_ref[...]
    return (jnp.exp(jnp.sum(lv[0:1] * lv[1:2], axis=-1, keepdims=True))
            - jnp.exp(jnp.sum(lv[2:3] * lv[3:4], axis=-1, keepdims=True)) + lam_init)


def _sub_ln(o, g_ref, lam_init):
    return _rms(o) * g_ref[...] * (1.0 - lam_init)


SWA_TQ = 2 * SWA_BLOCK
SWA_SPAN = SWA_TQ + 2 * SWA_WINDOW


def _swa_kernel(sink_ref, q_ref, k_ref, v_ref, kc_ref, vc_ref, o_ref):
    L = k_ref.shape[2]
    qb = pl.program_id(1)
    start = pl.multiple_of(jnp.clip(qb * SWA_TQ - SWA_WINDOW, 0, L - SWA_SPAN), SWA_BLOCK)
    group = N_HEADS // SWA_KV_HEADS
    rows = group * SWA_TQ
    n_keys = SWA_SPAN + kc_ref.shape[2]
    col = lax.broadcasted_iota(jnp.int32, (rows, n_keys), 1)
    row = lax.broadcasted_iota(jnp.int32, (rows, n_keys), 0)
    q_abs = qb * SWA_TQ + jnp.where(row >= SWA_TQ, row - SWA_TQ, row)
    allowed = (col >= SWA_SPAN) | (jnp.abs(start + col - q_abs) <= SWA_WINDOW)
    for hk in range(SWA_KV_HEADS):
        k_all = jnp.concatenate([k_ref[0, hk, pl.ds(start, SWA_SPAN), :], kc_ref[0, hk]], axis=0)
        v_all = jnp.concatenate([v_ref[0, hk, pl.ds(start, SWA_SPAN), :], vc_ref[0, hk]], axis=0)
        q = jnp.concatenate([q_ref[0, hk * group + g] for g in range(group)], axis=0)
        sink = jnp.concatenate([jnp.full((SWA_TQ, 1), sink_ref[hk * group + g], F32) for g in range(group)],
                               axis=0) * LOG2E
        (p,), denom = _softmax_parts([jnp.where(allowed, _dot_nt(q, k_all), NEG_INF)], sink)
        o = _dot(p.astype(BF16), v_all) / denom
        for g in range(group):
            h = hk * group + g
            o_ref[0, :, h * HEAD_DIM:(h + 1) * HEAD_DIM] = o[g * SWA_TQ:(g + 1) * SWA_TQ].astype(o_ref.dtype)


def _swa_attention(sink, hd, hc):
    B, _, L, _ = hd["qa"].shape
    C = hc["ka"].shape[2]
    assert L % SWA_TQ == 0 and L >= SWA_SPAN
    whole = lambda b, i: (b, 0, 0, 0)
    return pl.pallas_call(
        _swa_kernel,
        grid=(B, L // SWA_TQ),
        in_specs=[pl.BlockSpec(memory_space=pltpu.SMEM),
                  pl.BlockSpec((1, N_HEADS, SWA_TQ, HEAD_DIM), lambda b, i: (b, 0, i, 0)),
                  pl.BlockSpec((1, SWA_KV_HEADS, L, HEAD_DIM), whole),
                  pl.BlockSpec((1, SWA_KV_HEADS, L, HEAD_DIM), whole),
                  pl.BlockSpec((1, SWA_KV_HEADS, C, HEAD_DIM), whole),
                  pl.BlockSpec((1, SWA_KV_HEADS, C, HEAD_DIM), whole)],
        out_specs=pl.BlockSpec((1, SWA_TQ, GROUP_WIDTH), lambda b, i: (b, i, 0)),
        out_shape=jax.ShapeDtypeStruct((B, L, GROUP_WIDTH), BF16),
        compiler_params=_cparams(("arbitrary", "arbitrary")),
        name="swa_attention",
    )(sink, hd["qa"], hd["ka"], hd["va"], hc["ka"], hc["va"])


MLA_TQ = 1024
DIFF_TQ = 512
GLOBAL_TK = 2048
HEADS_PER_STEP = 2
FLASH_UNROLL = 4
MLA_UNROLL = FLASH_UNROLL


def _chunk_at(ref, lead, tk):
    return lambda i: ref[lead + (pl.ds(pl.multiple_of(i * tk, tk), tk), slice(None))]


def _mla_kernel(q_ref, kc_ref, vc_ref, k_ref, v_ref, o_ref, *, tk):
    n_chunks = k_ref.shape[2] // tk
    heads = range(HEADS_PER_STEP)
    qs = [q_ref[0, j] for j in heads]
    carries = _flash_start(qs, [kc_ref[0, j] for j in heads], [vc_ref[0, j] for j in heads])
    outs = _flash_finish(carries, qs, [_chunk_at(k_ref, (0, j), tk) for j in heads],
                         [_chunk_at(v_ref, (0, j), tk) for j in heads], n_chunks, MLA_V, MLA_UNROLL)
    o_ref[0] = jnp.concatenate(outs, axis=-1).astype(o_ref.dtype)


def _mla_attention(hd, hc):
    B, H, L, dk = hd["mq"].shape
    C = hc["mk"].shape[2]
    tq = min(MLA_TQ, L)
    tk = min(GLOBAL_TK, L)
    hp = HEADS_PER_STEP
    whole = lambda b, h, i: (b, h, 0, 0)
    return pl.pallas_call(
        functools.partial(_mla_kernel, tk=tk),
        grid=(B, H // hp, L // tq),
        in_specs=[pl.BlockSpec((1, hp, tq, dk), lambda b, h, i: (b, h, i, 0)),
                  pl.BlockSpec((1, hp, C, dk), whole),
                  pl.BlockSpec((1, hp, C, V_EXT), whole),
                  pl.BlockSpec((1, hp, L, dk), whole),
                  pl.BlockSpec((1, hp, L, V_EXT), whole)],
        out_specs=pl.BlockSpec((1, tq, hp * MLA_V), lambda b, h, i: (b, i, h)),
        out_shape=jax.ShapeDtypeStruct((B, L, H * MLA_V), BF16),
        compiler_params=_cparams(("arbitrary", "arbitrary", "arbitrary")),
        name="mla_attention",
    )(hd["mq"], hc["mk"], hc["mv"], hd["mk"], hd["mv"])


def _diff_kernel(lam_ref, g_ref, q_ref, kc_ref, vc_ref, k_ref, v_ref, o_ref, *, tk, lam_init):
    n_chunks = k_ref.shape[2] // tk
    lam = _lambda_value(lam_ref, lam_init)
    branches = [(2 * h, 2 * h + 1) for h in range(HEADS_PER_STEP)]
    qs = [[q_ref[0, j] for j in br] for br in branches]
    starts = [_flash_start(qs[h], [kc_ref[0, j] for j in br], [vc_ref[0, h]] * 2) for h, br in enumerate(branches)]
    heads = []
    for h, br in enumerate(branches):
        o1, o2 = _flash_finish(starts[h], qs[h], [_chunk_at(k_ref, (0, j), tk) for j in br],
                               [_chunk_at(v_ref, (0, h), tk)] * 2, n_chunks, DIFF_V, FLASH_UNROLL)
        heads.append(_sub_ln(o1 - lam * o2, g_ref, lam_init))
    o_ref[0] = jnp.concatenate(heads, axis=-1).astype(o_ref.dtype)


def _diff_attention(lam_vecs, subln_g, hd, hc, lam_init):
    B, _, L, dk = hd["dq"].shape
    C = hc["dk"].shape[2]
    tq = min(DIFF_TQ, L)
    tk = min(GLOBAL_TK, L)
    hp = HEADS_PER_STEP
    whole = lambda b, h, i: (b, h, 0, 0)
    const = lambda b, h, i: (0, 0)
    return pl.pallas_call(
        functools.partial(_diff_kernel, tk=tk, lam_init=lam_init),
        grid=(B, N_HEADS // hp, L // tq),
        in_specs=[pl.BlockSpec(lam_vecs.shape, const),
                  pl.BlockSpec(subln_g.shape, const),
                  pl.BlockSpec((1, 2 * hp, tq, dk), lambda b, h, i: (b, h, i, 0)),
                  pl.BlockSpec((1, 2 * hp, C, dk), whole),
                  pl.BlockSpec((1, hp, C, V_EXT), whole),
                  pl.BlockSpec((1, 2 * hp, L, dk), whole),
                  pl.BlockSpec((1, hp, L, V_EXT), whole)],
        out_specs=pl.BlockSpec((1, tq, hp * DIFF_V), lambda b, h, i: (b, i, h)),
        out_shape=jax.ShapeDtypeStruct((B, L, N_HEADS * DIFF_V), BF16),
        compiler_params=_cparams(("arbitrary", "arbitrary", "arbitrary")),
        name="diff_attention",
    )(lam_vecs, subln_g, hd["dq"], hc["dk"], hc["dv"], hd["dk"], hd["dv"])


def _na_plan(rows):
    kh = min(NA_KH, rows)
    n_row_off = 2 * NA_KH - 1
    col = np.arange(GRID_W)
    col_start = np.clip(col - NA_KW // 2, 0, GRID_W - NA_KW)
    col_ok = (col[None, :] >= col_start[:, None]) & (col[None, :] < col_start[:, None] + NA_KW)
    col_off = col[None, :] - col[:, None] + (NA_KW - 1)
    col_onehot = ((col_off[None] == np.arange(2 * NA_KW - 1)[:, None, None]) & col_ok[None]).astype(np.float32)
    patterns, starts, ids = {}, [], []
    for blk in range(rows // NA_Q_ROWS):
        r0 = blk * NA_Q_ROWS
        ks = int(np.clip(r0 - kh // 2, 0, rows - NA_K_ROWS))
        q_row = r0 + np.arange(NA_Q_ROWS)
        k_row = ks + np.arange(NA_K_ROWS)
        r_start = np.clip(q_row - kh // 2, 0, rows - kh)
        row_ok = (k_row[None, :] >= r_start[:, None]) & (k_row[None, :] < r_start[:, None] + kh)
        row_off = k_row[None, :] - q_row[:, None] + (NA_KH - 1)
        sel = np.where(row_ok, row_off, n_row_off)
        key = tuple(int(v) for v in sel.reshape(-1))
        if key not in patterns:
            patterns[key] = (len(patterns), sel)
        starts.append(ks)
        ids.append(patterns[key][0])
    row_sel = np.stack([p[1] for p in sorted(patterns.values(), key=lambda p: p[0])])
    return np.asarray(starts, np.int32), np.asarray(ids, np.int32), row_sel, col_onehot, col_ok


def _na_kernel(ks_ref, pid_ref, q_ref, k_ref, v_ref, kc_ref, vc_ref, bias_ref, o_ref):
    del pid_ref
    k_len = NA_K_ROWS * GRID_W
    start = pl.multiple_of(ks_ref[pl.program_id(1)] * GRID_W, GRID_W)
    for h in range(N_HEADS):
        q = q_ref[0, h]
        kw = k_ref[0, h, pl.ds(start, k_len), :]
        vw = v_ref[0, h, pl.ds(start, k_len), :]
        s_win = _dot_nt(q, kw) + bias_ref[0, h]
        s_ctx = _dot_nt(q, kc_ref[0, h])
        (p_win, p_ctx), denom = _softmax_parts([s_win, s_ctx])
        o = _dot(p_win.astype(BF16), vw) + _dot(p_ctx.astype(BF16), vc_ref[0, h])
        o_ref[0, :, h * HEAD_DIM:(h + 1) * HEAD_DIM] = (o / denom).astype(o_ref.dtype)


def _na_attention(rpb, hd, hc):
    B, H, L, _ = hd["nq"].shape
    C = hc["nk"].shape[2]
    rows = L // GRID_W
    starts, ids, row_sel, col_onehot, col_ok = _na_plan(rows)
    q_len = NA_Q_ROWS * GRID_W
    k_len = NA_K_ROWS * GRID_W
    slab = jnp.einsum("hrj,jqk->hrqk", rpb * LOG2E, col_onehot, precision=lax.Precision.HIGHEST)
    slab = jnp.where(col_ok[None, None], slab, NEG_INF)
    slab = jnp.concatenate([slab, jnp.full_like(slab[:, :1], NEG_INF)], axis=1)
    bias = jnp.take(slab, jnp.asarray(row_sel), axis=1)
    bias = jnp.transpose(bias, (1, 0, 2, 4, 3, 5)).reshape(row_sel.shape[0], H, q_len, k_len)
    whole = lambda b, i, ks, pid: (b, 0, 0, 0)
    grid_spec = pltpu.PrefetchScalarGridSpec(
        num_scalar_prefetch=2,
        grid=(B, rows // NA_Q_ROWS),
        in_specs=[pl.BlockSpec((1, H, q_len, HEAD_DIM), lambda b, i, ks, pid: (b, 0, i, 0)),
                  pl.BlockSpec((1, H, L, HEAD_DIM), whole),
                  pl.BlockSpec((1, H, L, HEAD_DIM), whole),
                  pl.BlockSpec((1, H, C, HEAD_DIM), whole),
                  pl.BlockSpec((1, H, C, HEAD_DIM), whole),
                  pl.BlockSpec((1, H, q_len, k_len), lambda b, i, ks, pid: (pid[i], 0, 0, 0))],
        out_specs=pl.BlockSpec((1, q_len, GROUP_WIDTH), lambda b, i, ks, pid: (b, i, 0)))
    return pl.pallas_call(
        _na_kernel,
        grid_spec=grid_spec,
        out_shape=jax.ShapeDtypeStruct((B, L, GROUP_WIDTH), BF16),
        compiler_params=_cparams(("arbitrary", "arbitrary")),
        name="neighborhood_attention",
    )(jnp.asarray(starts), jnp.asarray(ids), hd["nq"], hd["nk"], hd["nv"], hc["nk"], hc["nv"], bias)


def _ctx_attn_kernel(sink_ref, lam_ref, g_ref, qa_ref, ka_ref, va_ref, mq_ref, mk_ref, mv_ref,
                     dq_ref, dk_ref, dv_ref, nq_ref, nk_ref, nv_ref,
                     ya_ref, yb_ref, yc_ref, yd_ref, *, lam_init):
    def attend(q, k, v, extra=None):
        (p,), denom = _softmax_parts([_dot_nt(q, k)], extra)
        return _dot(p.astype(BF16), v) / denom

    group = N_HEADS // SWA_KV_HEADS
    lam = _lambda_value(lam_ref, lam_init)
    for h in range(N_HEADS):
        lanes = slice(h * HEAD_DIM, (h + 1) * HEAD_DIM)
        ya_ref[0, :, lanes] = attend(qa_ref[0, h], ka_ref[0, h // group], va_ref[0, h // group],
                                     sink_ref[h] * LOG2E).astype(ya_ref.dtype)
        mv = mv_ref[0, h][:, :MLA_V]
        dv = dv_ref[0, h][:, :DIFF_V]
        yb_ref[0, :, lanes] = attend(mq_ref[0, h], mk_ref[0, h], mv).astype(yb_ref.dtype)
        o = (attend(dq_ref[0, 2 * h], dk_ref[0, 2 * h], dv)
             - lam * attend(dq_ref[0, 2 * h + 1], dk_ref[0, 2 * h + 1], dv))
        yc_ref[0, :, lanes] = _sub_ln(o, g_ref, lam_init).astype(yc_ref.dtype)
        yd_ref[0, :, lanes] = attend(nq_ref[0, h], nk_ref[0, h], nv_ref[0, h]).astype(yd_ref.dtype)


def _ctx_attention(sink, lam_vecs, subln_g, hc, lam_init):
    names = ("qa", "ka", "va", "mq", "mk", "mv", "dq", "dk", "dv", "nq", "nk", "nv")
    B, _, C, _ = hc["qa"].shape
    whole4 = lambda b: (b, 0, 0, 0)
    const = lambda b: (0, 0)
    in_specs = [pl.BlockSpec(memory_space=pltpu.SMEM),
                pl.BlockSpec(lam_vecs.shape, const), pl.BlockSpec(subln_g.shape, const)]
    in_specs += [pl.BlockSpec((1,) + hc[n].shape[1:], whole4) for n in names]
    tok = pl.BlockSpec((1, C, GROUP_WIDTH), lambda b: (b, 0, 0))
    return pl.pallas_call(
        functools.partial(_ctx_attn_kernel, lam_init=lam_init),
        grid=(B,),
        in_specs=in_specs,
        out_specs=[tok] * N_MIXERS,
        out_shape=[jax.ShapeDtypeStruct((B, C, GROUP_WIDTH), BF16)] * N_MIXERS,
        compiler_params=_cparams(("arbitrary",)),
        name="context_attention",
    )(sink, lam_vecs, subln_g, *[hc[n] for n in names])


def _route(r):
    lane = lax.broadcasted_iota(jnp.int32, r.shape, 1)
    lane_f = lane.astype(F32)
    big = float(ROUTER_LANES)
    is_grp = (lane >= N_EXPERTS) & (lane < N_EXPERTS + N_GROUPS)
    g_log = jnp.where(is_grp, r, NEG_INF)
    g_max = jnp.max(g_log, axis=-1, keepdims=True)
    g_val = 1.0 / jnp.sum(jnp.exp(g_log - g_max), axis=-1, keepdims=True)
    g_idx = jnp.min(jnp.where(g_log == g_max, lane_f, big), axis=-1, keepdims=True) - float(N_EXPERTS)
    lane_grp = lax.shift_right_logical(lane, int(math.log2(EXPERTS_PER_GROUP))).astype(F32)
    in_grp = (lane < N_EXPERTS) & (lane_grp == g_idx)
    e_log = jnp.where(in_grp, r, NEG_INF)
    e_max = jnp.max(e_log, axis=-1, keepdims=True)
    i1 = jnp.min(jnp.where(e_log == e_max, lane_f, big), axis=-1, keepdims=True)
    e_rest = jnp.where(lane_f == i1, NEG_INF, e_log)
    e_max2 = jnp.max(e_rest, axis=-1, keepdims=True)
    i2 = jnp.min(jnp.where(e_rest == e_max2, lane_f, big), axis=-1, keepdims=True)
    p2 = jnp.exp(e_max2 - e_max)
    w1 = 1.0 / (1.0 + p2)
    w2 = p2 / (1.0 + p2)
    gates = g_val * jnp.where(lane_f == i1, w1, jnp.where(lane_f == i2, w2, 0.0))
    return jnp.where(lane == GROUP_LANE, g_idx, gates)


def _split_bf16(v):
    hi = v.astype(BF16)
    return hi, (v - hi.astype(F32)).astype(BF16)


def _outproj_kernel(x_ref, a_ref, b_ref, c_ref, d_ref, wo_ref, g1_ref, sc2_ref, sh2_ref,
                    lng_ref, lnb_ref, wr_ref, br_ref, x1_ref, rec_ref, *, alpha):
    y = _dot(a_ref[0], wo_ref[0:GROUP_WIDTH, :])
    for i, m_ref in enumerate((b_ref, c_ref, d_ref), start=1):
        y += _dot(m_ref[0], wo_ref[i * GROUP_WIDTH:(i + 1) * GROUP_WIDTH, :])
    x1 = _layer_norm(alpha * x_ref[0] + g1_ref[0] * y, lng_ref[...], lnb_ref[...])
    x1_ref[0] = x1
    h2 = x1 * (1.0 + sc2_ref[0]) + sh2_ref[0]
    rec_ref[0, :, :D_MODEL] = h2
    h_hi, h_lo = _split_bf16(h2)
    r = (_dot(h_hi, wr_ref[0]) + _dot(h_lo, wr_ref[0]) + _dot(h_hi, wr_ref[1])) + br_ref[...]
    rec_ref[0, :, D_MODEL:] = _route(r)


def _output_projection(x, att, wo, g1, sc2, sh2, ln_g, ln_b, wr, br, alpha):
    B, N, D = x.shape
    tm = min(512, N)
    tok = lambda w: pl.BlockSpec((1, tm, w), lambda b, i: (b, i, 0))
    vec = pl.BlockSpec((1, 1, D), lambda b, i: (b, 0, 0))
    const = lambda a: pl.BlockSpec(a.shape, lambda b, i: (0,) * a.ndim)
    return pl.pallas_call(
        functools.partial(_outproj_kernel, alpha=alpha),
        grid=(B, N // tm),
        in_specs=[tok(D)] + [tok(GROUP_WIDTH)] * N_MIXERS + [const(wo), vec, vec, vec,
                  const(ln_g), const(ln_b), const(wr), const(br)],
        out_specs=[tok(D), tok(REC_W)],
        out_shape=[jax.ShapeDtypeStruct((B, N, D), F32), jax.ShapeDtypeStruct((B, N, REC_W), F32)],
        compiler_params=_cparams(("arbitrary", "arbitrary")),
        name="output_projection",
    )(x, att[0], att[1], att[2], att[3], wo, g1, sc2, sh2, ln_g, ln_b, wr, br)


MOE_TM = 1024
MOE_ROWS = 1024
MOE_STEPS = 4
MOE_EXPERTS_PER_STEP = EXPERTS_PER_GROUP // MOE_STEPS
GATHER_ROWS = 1024
GATHER_UNROLL = 8
REC_W = D_MODEL + ROUTER_LANES


def _issue_row_gather(idx_ref, src_ref, dst_ref, sem):
    n = dst_ref.shape[0]

    def row_copy(j):
        return pltpu.make_async_copy(src_ref.at[pl.ds(idx_ref[0, 0, j], 1), :], dst_ref.at[pl.ds(j, 1), :], sem)

    def issue(i, carry):
        row_copy(2 * i).start(priority=0)
        row_copy(2 * i + 1).start(priority=1)
        return carry

    lax.fori_loop(0, n // 2, issue, 0, unroll=GATHER_UNROLL // 2)
    pltpu.make_async_copy(src_ref.at[pl.ds(0, n), :], dst_ref, sem).wait()


def _gather_rows_kernel(idx_ref, src_ref, o_ref, sem):
    _issue_row_gather(idx_ref, src_ref, o_ref, sem)


def _gather_rows(src, idx):
    T, W = src.shape
    tg = min(GATHER_ROWS, T)
    return pl.pallas_call(
        _gather_rows_kernel,
        grid=(T // tg,),
        in_specs=[pl.BlockSpec((1, 1, tg), lambda i: (i, 0, 0), memory_space=pltpu.SMEM),
                  pl.BlockSpec(memory_space=pl.ANY)],
        out_specs=pl.BlockSpec((tg, W), lambda i: (i, 0)),
        out_shape=jax.ShapeDtypeStruct((T, W), src.dtype),
        scratch_shapes=[pltpu.SemaphoreType.DMA(())],
        compiler_params=_cparams(("arbitrary",)),
        name="gather_rows",
    )(idx.reshape(T // tg, 1, tg), src)


def _moe_ffn_kernel(tile_ref, group_ref, first_ref, valid_ref, rec_ref, wg_ref, wu_ref, wd_ref, f_ref):
    del tile_ref
    w = pl.program_id(0)
    step = pl.program_id(1)
    tm = f_ref.shape[0]

    @pl.when(valid_ref[w] != 0)
    def _():
        first = group_ref[w] * EXPERTS_PER_GROUP + step * MOE_EXPERTS_PER_STEP
        rb = min(MOE_ROWS, tm)
        lane = lax.broadcasted_iota(jnp.int32, (rb, ROUTER_LANES), 1)
        experts = range(MOE_EXPERTS_PER_STEP)
        wgs = [wg_ref[0, 0, e].astype(BF16) for e in experts]
        wus = [wu_ref[0, 0, e].astype(BF16) for e in experts]
        wds = [wd_ref[0, 0, e].astype(BF16) for e in experts]
        for r0 in range(0, tm, rb):
            rows = slice(r0, r0 + rb)
            h = rec_ref[rows, :D_MODEL].astype(BF16)
            gates = rec_ref[rows, D_MODEL:]
            y = None
            for e in experts:
                pre = _dot(h, wgs[e])
                hid = pre * jax.nn.sigmoid(pre) * _dot(h, wus[e])
                col = jnp.sum(jnp.where(lane == first + e, gates, 0.0), axis=-1, keepdims=True)
                part = _dot((hid * col).astype(BF16), wds[e])
                y = part if y is None else y + part
            starts_tile = (first_ref[w] != 0) & (step == 0)

            @pl.when(starts_tile)
            def _():
                f_ref[rows, :] = y

            @pl.when(jnp.logical_not(starts_tile))
            def _():
                f_ref[rows, :] += y


def _moe_ffn(rec, items, wg, wu, wd, layer):
    T = rec.shape[0]
    tm = min(MOE_TM, T)
    tile, group, first, valid = items
    n_items = tile.shape[0]
    wspec = lambda a: pl.BlockSpec((1, 1, MOE_EXPERTS_PER_STEP) + a.shape[3:],
                                   lambda w, s, t, g, f, v: (layer, g[w], s, 0, 0))
    grid_spec = pltpu.PrefetchScalarGridSpec(
        num_scalar_prefetch=4,
        grid=(n_items, MOE_STEPS),
        in_specs=[pl.BlockSpec((tm, REC_W), lambda w, s, t, g, f, v: (t[w], 0)), wspec(wg), wspec(wu), wspec(wd)],
        out_specs=pl.BlockSpec((tm, D_MODEL), lambda w, s, t, g, f, v: (t[w], 0)))
    return pl.pallas_call(
        _moe_ffn_kernel,
        grid_spec=grid_spec,
        out_shape=jax.ShapeDtypeStruct((T, D_MODEL), F32),
        compiler_params=_cparams(("arbitrary", "arbitrary")),
        name="moe_ffn",
    )(tile, group, first, valid, rec, wg, wu, wd)


def _moe_items_sorted(sorted_group, tm):
    T = sorted_group.shape[0]
    nt = T // tm
    g_lo = sorted_group[0::tm]
    g_hi = sorted_group[tm - 1::tm]
    per_tile = g_hi - g_lo + 1
    start = jnp.cumsum(per_tile) - per_tile
    n_items = nt + N_GROUPS - 1
    w = jnp.arange(n_items, dtype=jnp.int32)
    tile = jnp.clip(jnp.searchsorted(start, w, side="right") - 1, 0, nt - 1).astype(jnp.int32)
    valid = w < jnp.sum(per_tile)
    offset = w - start[tile]
    group = jnp.where(valid, g_lo[tile] + offset, g_hi[nt - 1]).astype(jnp.int32)
    return tile, group, (valid & (offset == 0)).astype(jnp.int32), valid.astype(jnp.int32)


def _moe_items_dense(T, tm):
    nt = T // tm
    w = np.arange(nt * N_GROUPS, dtype=np.int32)
    return (jnp.asarray(w // N_GROUPS), jnp.asarray(w % N_GROUPS), jnp.asarray((w % N_GROUPS == 0).astype(np.int32)),
            jnp.ones((nt * N_GROUPS,), jnp.int32))


def _ln2_kernel(idx_ref, f_ref, x1_ref, g2_ref, lng_ref, lnb_ref, o_ref, buf_ref, sem, *, alpha):
    _issue_row_gather(idx_ref, f_ref, buf_ref, sem)
    o_ref[0] = _layer_norm(alpha * x1_ref[0] + g2_ref[0] * buf_ref[...], lng_ref[...], lnb_ref[...])


def _ln2_unsort(f_rows, idx, x1, g2, ln_g, ln_b, alpha):
    B, N, D = x1.shape
    tg = min(GATHER_ROWS, N)
    nb = N // tg
    const = lambda a: pl.BlockSpec(a.shape, lambda b, i: (0, 0))
    return pl.pallas_call(
        functools.partial(_ln2_kernel, alpha=alpha),
        grid=(B, nb),
        in_specs=[pl.BlockSpec((1, 1, tg), lambda b, i: (b * nb + i, 0, 0), memory_space=pltpu.SMEM),
                  pl.BlockSpec(memory_space=pl.ANY),
                  pl.BlockSpec((1, tg, D), lambda b, i: (b, i, 0)),
                  pl.BlockSpec((1, 1, D), lambda b, i: (b, 0, 0)), const(ln_g), const(ln_b)],
        out_specs=pl.BlockSpec((1, tg, D), lambda b, i: (b, i, 0)),
        out_shape=jax.ShapeDtypeStruct((B, N, D), F32),
        scratch_shapes=[pltpu.VMEM((tg, D), F32), pltpu.SemaphoreType.DMA(())],
        compiler_params=_cparams(("arbitrary", "arbitrary")),
        name="ln2_unsort",
    )(idx.reshape(B * nb, 1, tg), f_rows, x1, g2, ln_g, ln_b)


def _moe(rec, x1, wg, wu, wd, layer, g2, ln_g, ln_b, alpha, sort):
    B, N, _ = rec.shape
    T = B * N
    tm = min(MOE_TM, T)
    flat = rec.reshape(T, REC_W)
    if sort:
        group = flat[:, D_MODEL + GROUP_LANE].astype(jnp.int32)
        order = jnp.argsort(group, stable=True).astype(jnp.int32)
        place = jnp.argsort(order).astype(jnp.int32)
        flat = _gather_rows(flat, order)
        items = _moe_items_sorted(group[order], tm)
    else:
        place = jnp.arange(T, dtype=jnp.int32)
        items = _moe_items_dense(T, tm)
    f_rows = _moe_ffn(flat, items, wg, wu, wd, layer)
    return _ln2_unsort(f_rows, place, x1, g2, ln_g, ln_b, alpha)


def _rot_cols(w, d):
    k, n = w.shape
    q = d // 4
    w4 = w.reshape(k, n // d, 4, q)
    return jnp.stack([-w4[:, :, 1], w4[:, :, 0], -w4[:, :, 3], w4[:, :, 2]], axis=2).reshape(k, n)


def _rope_tables(L):
    t = jnp.arange(L, dtype=jnp.int32)
    rows = (t // GRID_W).astype(F32)
    cols = (t % GRID_W).astype(F32)

    def cos_sin(d):
        q = d // 4
        inv = ROPE_BASE ** (-jnp.arange(q, dtype=F32) / q)
        ar = rows[:, None] * inv[None, :]
        ac = cols[:, None] * inv[None, :]
        return (jnp.concatenate([jnp.cos(ar), jnp.cos(ar), jnp.cos(ac), jnp.cos(ac)], -1),
                jnp.concatenate([jnp.sin(ar), jnp.sin(ar), jnp.sin(ac), jnp.sin(ac)], -1))

    c64, s64 = cos_sin(HEAD_DIM)
    c32, s32 = cos_sin(MLA_ROPE)
    ones = jnp.ones((L, MLA_NOPE), F32)
    zeros_n = jnp.zeros((L, MLA_NOPE), F32)
    zeros_p = jnp.zeros((L, MLA_QK_PAD - MLA_NOPE - MLA_ROPE), F32)
    return (jnp.tile(c64, (1, N_HEADS)), jnp.tile(s64, (1, N_HEADS)),
            jnp.tile(c32, (1, 2 * N_HEADS)), jnp.tile(s32, (1, 2 * N_HEADS)),
            jnp.concatenate([ones, c32, zeros_p], -1), jnp.concatenate([zeros_n, s32, zeros_p], -1))


def _identity_tables(C):
    one = jnp.ones((C, GROUP_WIDTH), F32)
    zero = jnp.zeros((C, GROUP_WIDTH), F32)
    pad = MLA_QK_PAD - MLA_NOPE - MLA_ROPE
    cosm = jnp.concatenate([jnp.ones((C, MLA_NOPE + MLA_ROPE), F32), jnp.zeros((C, pad), F32)], -1)
    return one, zero, one, zero, cosm, jnp.zeros((C, MLA_QK_PAD), F32)


def _fused_in_weight(w_in):
    d = w_in.shape[0]
    p = jnp.split(w_in, IN_CUTS, axis=1)
    z = lambda n: jnp.zeros((d, n), w_in.dtype)
    pad_r = MLA_QK_PAD - MLA_NOPE - MLA_ROPE
    parts = {"qa": p[0], "qa_r": _rot_cols(p[0], HEAD_DIM), "ka": p[1], "ka_r": _rot_cols(p[1], HEAD_DIM),
             "va": p[2], "mqr": p[3], "mkvr": p[4],
             "mkr": jnp.concatenate([z(MLA_NOPE), p[5], z(pad_r)], 1),
             "mkr_r": jnp.concatenate([z(MLA_NOPE), _rot_cols(p[5], MLA_ROPE), z(pad_r)], 1),
             "dq": p[6], "dq_r": _rot_cols(p[6], DIFF_QK), "dk": p[7], "dk_r": _rot_cols(p[7], DIFF_QK),
             "dv": p[8], "nq": p[9], "nk": p[10], "nv": p[11]}
    return jnp.concatenate([parts[n] for n, _ in _SEG_LAYOUT], axis=1).astype(BF16)


def _mla_up_weights(w_uq, w_ukv):
    rq = w_uq.shape[0]
    pad_r = MLA_QK_PAD - MLA_NOPE - MLA_ROPE
    wq = w_uq.reshape(rq, N_HEADS, MLA_NOPE + MLA_ROPE)
    zq = lambda n: jnp.zeros((rq, N_HEADS, n), w_uq.dtype)
    rope_rot = _rot_cols(wq[:, :, MLA_NOPE:].reshape(rq, -1), MLA_ROPE).reshape(rq, N_HEADS, MLA_ROPE)
    main = jnp.concatenate([wq, zq(pad_r)], -1).reshape(rq, -1)
    rot = jnp.concatenate([zq(MLA_NOPE), rope_rot, zq(pad_r)], -1).reshape(rq, -1)
    wuq_ext = jnp.concatenate([main, rot], 1).astype(BF16)
    rk = w_ukv.shape[0]
    wkv = w_ukv.reshape(rk, N_HEADS, MLA_NOPE + MLA_V)
    k_part = jnp.concatenate([wkv[:, :, :MLA_NOPE],
                              jnp.zeros((rk, N_HEADS, MLA_QK_PAD - MLA_NOPE), w_ukv.dtype)], -1)
    wukv_ext = jnp.concatenate([k_part.reshape(rk, -1), wkv[:, :, MLA_NOPE:].reshape(rk, -1)], 1).astype(BF16)
    return wuq_ext, wukv_ext


def kernel(x, c, ctx, c_ctx, w_mod, b_mod, w_in, attn_sink, mla_q_norm, w_uq, mla_kv_norm, w_ukv,
           lam_q1, lam_k1, lam_q2, lam_k2, diff_subln, na_rpb, w_out, ln1_g, ln1_b,
           w_group, b_group, w_router, b_router, w_gate, w_up, w_down, ln2_g, ln2_b):
    B, L, D = x.shape
    C = ctx.shape[1]
    depth = w_mod.shape[0]
    alpha = (2 * depth) ** 0.25
    assert D == D_MODEL and B + 1 <= 8 and L % (NA_Q_ROWS * GRID_W) == 0

    cvec = jnp.concatenate([c, c_ctx[None, :], jnp.zeros((8 - B - 1, D), F32)], axis=0)
    mod = _modulation(cvec, w_mod, b_mod)
    lat_tables = _rope_tables(L)
    ctx_tables = _identity_tables(C)
    row = lambda a: a.reshape(1, -1)

    xc = ctx
    for l in range(depth):
        need_ctx = l < depth - 1
        lam_init = 0.8 - 0.6 * math.exp(-0.3 * l)
        chunks = [mod[l, :, i * D:(i + 1) * D] for i in range(6)]
        sh1, sc1, g1, sh2, sc2, g2 = [m[:B, None, :] for m in chunks]
        sh1c, sc1c, g1c, sh2c, sc2c, g2c = [jnp.broadcast_to(m[B:B + 1, None, :], (B, 1, D)) for m in chunks]

        w_all = _fused_in_weight(w_in[l])
        wuq_ext, wukv_ext = _mla_up_weights(w_uq[l], w_ukv[l])
        proj = functools.partial(_input_projection, w_all=w_all, qng=row(mla_q_norm[l]), wuq=wuq_ext,
                                 kvng=row(mla_kv_norm[l]), wukv=wukv_ext)
        hd = proj(x, sc1, sh1, tables=lat_tables)
        hc = proj(xc, sc1c, sh1c, tables=ctx_tables)

        lam_vecs = jnp.stack([lam_q1[l], lam_k1[l], lam_q2[l], lam_k2[l]])
        subln_g = row(diff_subln[l])
        att = (_swa_attention(attn_sink[l], hd, hc), _mla_attention(hd, hc),
               _diff_attention(lam_vecs, subln_g, hd, hc, lam_init), _na_attention(na_rpb[l], hd, hc))

        wo = w_out[l].astype(BF16)
        pad = ROUTER_LANES - N_EXPERTS - N_GROUPS
        wr = jnp.stack(_split_bf16(jnp.concatenate([w_router[l], w_group[l], jnp.zeros((D, pad), F32)], axis=1)))
        br = row(jnp.concatenate([b_router[l], b_group[l], jnp.zeros((pad,), F32)]))
        post = functools.partial(_output_projection, wo=wo, ln_g=row(ln1_g[l]), ln_b=row(ln1_b[l]),
                                 wr=wr, br=br, alpha=alpha)
        ffn = functools.partial(_moe, wg=w_gate, wu=w_up, wd=w_down, layer=l,
                                ln_g=row(ln2_g[l]), ln_b=row(ln2_b[l]), alpha=alpha)

        x1, rec = post(x, att, g1=g1, sc2=sc2, sh2=sh2)
        x = ffn(rec, x1, g2=g2, sort=True)
        if need_ctx:
            att_c = _ctx_attention(attn_sink[l], lam_vecs, subln_g, hc, lam_init)
            xc1, rec_c = post(xc, att_c, g1=g1c, sc2=sc2c, sh2=sh2c)
            xc = ffn(rec_c, xc1, g2=g2c, sort=False)
    return x
```
